```python
import jax, jax.numpy as jnp
from jax import lax
import numpy as np

D_MODEL = 2048
BATCH = 8
SEQ = 4096
DEPTH = 4

N_MEM = 256
N_MIXERS = 3
N_A_LAYERS = (DEPTH + 2) // 3
N_B_LAYERS = (DEPTH + 1) // 3
N_C_LAYERS = DEPTH // 3
SHORT_CONV = 3
CHUNK = 128
GMLP_GROUPS = 8
GMLP_HIDDEN = D_MODEL
GMLP_GROUP_DIM = GMLP_HIDDEN // GMLP_GROUPS
CONF_CONV = 31
XA_HEADS = 4
XA_HEAD_DIM = D_MODEL // XA_HEADS
D_FF = ((8 * D_MODEL + 3 * 256 - 1) // (3 * 256)) * 256
EPS = 1e-6

kernel_name = 'hybrid_interleaved_conv_gmlp_conformer_xattn'


def _rmsnorm(x, g):
    xf = x.astype(jnp.float32)
    y = xf * lax.rsqrt(jnp.mean(xf * xf, axis=-1, keepdims=True) + EPS)
    return (y * g.astype(jnp.float32)).astype(x.dtype)


def _layernorm(x, g, b):
    xf = x.astype(jnp.float32)
    mu = jnp.mean(xf, axis=-1, keepdims=True)
    var = jnp.mean(jnp.square(xf - mu), axis=-1, keepdims=True)
    y = (xf - mu) * lax.rsqrt(var + EPS)
    return (y * g.astype(jnp.float32) + b.astype(jnp.float32)).astype(x.dtype)


def _causal_dwconv(x, w):
    k = w.shape[0]
    return lax.conv_general_dilated(
        x, w[:, None, :].astype(x.dtype), window_strides=(1,),
        padding=[(k - 1, 0)], dimension_numbers=('NWC', 'WIO', 'NWC'),
        feature_group_count=x.shape[-1])


def _mixer_short_conv(h, w_in, conv_w, w_out):
    bcz = h @ w_in
    b_gate, c_gate, z = jnp.split(bcz, 3, axis=-1)
    y = _causal_dwconv(c_gate * z, conv_w)
    return (b_gate * y) @ w_out


def _mixer_chunked_gmlp(h, w_in, v_g, v_b, w_s, s_bias, w_out):
    bsz, seq, _ = h.shape
    uv = jax.nn.gelu(h @ w_in)
    u, v = jnp.split(uv, 2, axis=-1)
    v = _layernorm(v, v_g, v_b)
    v = v.reshape(bsz, seq // CHUNK, CHUNK, GMLP_GROUPS, GMLP_GROUP_DIM)
    mask = jnp.tril(jnp.ones((CHUNK, CHUNK), dtype=bool))
    ws = jnp.where(mask[None], w_s, jnp.zeros((), w_s.dtype))
    sv = jnp.einsum('gts,bnsgc->bntgc', ws, v)
    sv = sv + s_bias.T[None, None, :, :, None]
    gated = u * sv.reshape(bsz, seq, GMLP_HIDDEN)
    return gated @ w_out


def _mixer_conformer_conv(h, w_in, conv_w, conv_b, ln_g, ln_b, w_out):
    ag = h @ w_in
    a, g = jnp.split(ag, 2, axis=-1)
    y = a * jax.nn.sigmoid(g)
    y = _causal_dwconv(y, conv_w) + conv_b
    y = _layernorm(y, ln_g, ln_b)
    y = jax.nn.silu(y)
    return y @ w_out


def _cross_attention(h, mem_n, wq, wkv, wo):
    bsz, seq, _ = h.shape
    q = (h @ wq).reshape(bsz, seq, XA_HEADS, XA_HEAD_DIM)
    kv = mem_n @ wkv
    k, v = jnp.split(kv, 2, axis=-1)
    k = k.reshape(bsz, N_MEM, XA_HEADS, XA_HEAD_DIM)
    v = v.reshape(bsz, N_MEM, XA_HEADS, XA_HEAD_DIM)
    scale = XA_HEAD_DIM ** -0.5
    s = jnp.einsum('bshd,bmhd->bhsm', q, k).astype(jnp.float32) * scale
    p = jax.nn.softmax(s, axis=-1).astype(v.dtype)
    o = jnp.einsum('bhsm,bmhd->bshd', p, v).reshape(bsz, seq, D_MODEL)
    return o @ wo


def _swiglu(h, w_gu, w_down):
    gu = h @ w_gu
    gate, up = jnp.split(gu, 2, axis=-1)
    return (jax.nn.silu(gate) * up) @ w_down


def _fwd_setup_inputs(seed: int = 0) -> dict:
    key = jax.random.key(seed)
    ks = jax.random.split(key, 32)

    def nrm(k, shape, scale):
        return jax.random.normal(k, shape, jnp.float32) * scale

    def gain(k, shape):
        return 1.0 + 0.05 * jax.random.normal(k, shape, jnp.float32)

    d = D_MODEL
    return {
        'x': nrm(ks[0], (BATCH, SEQ, d), 1.0),
        'mem': nrm(ks[1], (BATCH, N_MEM, d), 1.0),
        'mix_norm': gain(ks[2], (DEPTH, 2, d)),
        'xa_norm': gain(ks[3], (DEPTH, 3, d)),
        'xa_wq': nrm(ks[4], (DEPTH, d, d), d ** -0.5),
        'xa_wkv': nrm(ks[5], (DEPTH, d, 2 * d), d ** -0.5),
        'xa_wo': nrm(ks[6], (DEPTH, d, d), d ** -0.5),
        'ffn_norm': gain(ks[7], (DEPTH, 2, d)),
        'ffn_w_gu': nrm(ks[8], (DEPTH, d, 2 * D_FF), d ** -0.5),
        'ffn_w_down': nrm(ks[9], (DEPTH, D_FF, d), D_FF ** -0.5),
        'a_w_in': nrm(ks[10], (N_A_LAYERS, d, 3 * d), d ** -0.5),
        'a_conv_w': nrm(ks[11], (N_A_LAYERS, SHORT_CONV, d), SHORT_CONV ** -0.5),
        'a_w_out': nrm(ks[12], (N_A_LAYERS, d, d), d ** -0.5),
        'b_w_in': nrm(ks[13], (N_B_LAYERS, d, 2 * GMLP_HIDDEN), d ** -0.5),
        'b_v_g': gain(ks[14], (N_B_LAYERS, GMLP_HIDDEN)),
        'b_v_b': nrm(ks[15], (N_B_LAYERS, GMLP_HIDDEN), 0.02),
        'b_w_s': nrm(ks[16], (N_B_LAYERS, GMLP_GROUPS, CHUNK, CHUNK), CHUNK ** -0.5),
        'b_s_bias': gain(ks[17], (N_B_LAYERS, GMLP_GROUPS, CHUNK)),
        'b_w_out': nrm(ks[18], (N_B_LAYERS, GMLP_HIDDEN, d), GMLP_HIDDEN ** -0.5),
        'c_w_in': nrm(ks[19], (N_C_LAYERS, d, 2 * d), d ** -0.5),
        'c_conv_w': nrm(ks[20], (N_C_LAYERS, CONF_CONV, d), CONF_CONV ** -0.5),
        'c_conv_b': nrm(ks[21], (N_C_LAYERS, d), 0.02),
        'c_ln_g': gain(ks[22], (N_C_LAYERS, d)),
        'c_ln_b': nrm(ks[23], (N_C_LAYERS, d), 0.02),
        'c_w_out': nrm(ks[24], (N_C_LAYERS, d, d), d ** -0.5),
    }


def _fwd_reference(x, mem, mix_norm, xa_norm, xa_wq, xa_wkv, xa_wo, ffn_norm,
              ffn_w_gu, ffn_w_down, a_w_in, a_conv_w, a_w_out,
              b_w_in, b_v_g, b_v_b, b_w_s, b_s_bias, b_w_out,
              c_w_in, c_conv_w, c_conv_b, c_ln_g, c_ln_b, c_w_out):
    for i in range(DEPTH):
        kind = i % N_MIXERS
        slot = i // N_MIXERS
        h = _rmsnorm(x, mix_norm[i, 0])
        if kind == 0:
            y = _mixer_short_conv(h, a_w_in[slot], a_conv_w[slot], a_w_out[slot])
        elif kind == 1:
            y = _mixer_chunked_gmlp(h, b_w_in[slot], b_v_g[slot], b_v_b[slot],
                                    b_w_s[slot], b_s_bias[slot], b_w_out[slot])
        else:
            y = _mixer_conformer_conv(h, c_w_in[slot], c_conv_w[slot], c_conv_b[slot],
                                      c_ln_g[slot], c_ln_b[slot], c_w_out[slot])
        x = x + _rmsnorm(y, mix_norm[i, 1])
        h = _rmsnorm(x, xa_norm[i, 0])
        mem_n = _rmsnorm(mem, xa_norm[i, 2])
        y = _cross_attention(h, mem_n, xa_wq[i], xa_wkv[i], xa_wo[i])
        x = x + _rmsnorm(y, xa_norm[i, 1])
        h = _rmsnorm(x, ffn_norm[i, 0])
        y = _swiglu(h, ffn_w_gu[i], ffn_w_down[i])
        x = x + _rmsnorm(y, ffn_norm[i, 1])
    return x


import jax as _jax
import jax.numpy as _jnp

TWIN_FORMAT = 'train_step'
FWD_PARAMS = ['x', 'mem', 'mix_norm', 'xa_norm', 'xa_wq', 'xa_wkv', 'xa_wo', 'ffn_norm', 'ffn_w_gu', 'ffn_w_down', 'a_w_in', 'a_conv_w', 'a_w_out', 'b_w_in', 'b_v_g', 'b_v_b', 'b_w_s', 'b_s_bias', 'b_w_out', 'c_w_in', 'c_conv_w', 'c_conv_b', 'c_ln_g', 'c_ln_b', 'c_w_out']
TWIN_WEIGHTS = ['mix_norm', 'xa_norm', 'xa_wq', 'xa_wkv', 'xa_wo', 'ffn_norm', 'ffn_w_gu', 'ffn_w_down', 'a_w_in', 'a_conv_w', 'a_w_out', 'b_w_in', 'b_v_g', 'b_v_b', 'b_w_s', 'b_s_bias', 'b_w_out', 'c_w_in', 'c_conv_w', 'c_conv_b', 'c_ln_g', 'c_ln_b', 'c_w_out']
TWIN_DIFF_INPUT = 'x'
TWIN_INPUTS = ['x', 'mem', 'mix_norm', 'xa_norm', 'xa_wq', 'xa_wkv', 'xa_wo', 'ffn_norm', 'ffn_w_gu', 'ffn_w_down', 'a_w_in', 'a_conv_w', 'a_w_out', 'b_w_in', 'b_v_g', 'b_v_b', 'b_w_s', 'b_s_bias', 'b_w_out', 'c_w_in', 'c_conv_w', 'c_conv_b', 'c_ln_g', 'c_ln_b', 'c_w_out', 'loss_target', 'm_mix_norm', 'm_xa_norm', 'm_xa_wq', 'm_xa_wkv', 'm_xa_wo', 'm_ffn_norm', 'm_ffn_w_gu', 'm_ffn_w_down', 'm_a_w_in', 'm_a_conv_w', 'm_a_w_out', 'm_b_w_in', 'm_b_v_g', 'm_b_v_b', 'm_b_w_s', 'm_b_s_bias', 'm_b_w_out', 'm_c_w_in', 'm_c_conv_w', 'm_c_conv_b', 'm_c_ln_g', 'm_c_ln_b', 'm_c_w_out', 'v_mix_norm', 'v_xa_norm', 'v_xa_wq', 'v_xa_wkv', 'v_xa_wo', 'v_ffn_norm', 'v_ffn_w_gu', 'v_ffn_w_down', 'v_a_w_in', 'v_a_conv_w', 'v_a_w_out', 'v_b_w_in', 'v_b_v_g', 'v_b_v_b', 'v_b_w_s', 'v_b_s_bias', 'v_b_w_out', 'v_c_w_in', 'v_c_conv_w', 'v_c_conv_b', 'v_c_ln_g', 'v_c_ln_b', 'v_c_w_out']
TWIN_OUTPUTS = ['loss', 'grad_x', 'grad_mix_norm', 'grad_xa_norm', 'grad_xa_wq', 'grad_xa_wkv', 'grad_xa_wo', 'grad_ffn_norm', 'grad_ffn_w_gu', 'grad_ffn_w_down', 'grad_a_w_in', 'grad_a_conv_w', 'grad_a_w_out', 'grad_b_w_in', 'grad_b_v_g', 'grad_b_v_b', 'grad_b_w_s', 'grad_b_s_bias', 'grad_b_w_out', 'grad_c_w_in', 'grad_c_conv_w', 'grad_c_conv_b', 'grad_c_ln_g', 'grad_c_ln_b', 'grad_c_w_out', 'delta_mix_norm', 'delta_xa_norm', 'delta_xa_wq', 'delta_xa_wkv', 'delta_xa_wo', 'delta_ffn_norm', 'delta_ffn_w_gu', 'delta_ffn_w_down', 'delta_a_w_in', 'delta_a_conv_w', 'delta_a_w_out', 'delta_b_w_in', 'delta_b_v_g', 'delta_b_v_b', 'delta_b_w_s', 'delta_b_s_bias', 'delta_b_w_out', 'delta_c_w_in', 'delta_c_conv_w', 'delta_c_conv_b', 'delta_c_ln_g', 'delta_c_ln_b', 'delta_c_w_out', 'new_m_mix_norm', 'new_m_xa_norm', 'new_m_xa_wq', 'new_m_xa_wkv', 'new_m_xa_wo', 'new_m_ffn_norm', 'new_m_ffn_w_gu', 'new_m_ffn_w_down', 'new_m_a_w_in', 'new_m_a_conv_w', 'new_m_a_w_out', 'new_m_b_w_in', 'new_m_b_v_g', 'new_m_b_v_b', 'new_m_b_w_s', 'new_m_b_s_bias', 'new_m_b_w_out', 'new_m_c_w_in', 'new_m_c_conv_w', 'new_m_c_conv_b', 'new_m_c_ln_g', 'new_m_c_ln_b', 'new_m_c_w_out', 'new_v_mix_norm', 'new_v_xa_norm', 'new_v_xa_wq', 'new_v_xa_wkv', 'new_v_xa_wo', 'new_v_ffn_norm', 'new_v_ffn_w_gu', 'new_v_ffn_w_down', 'new_v_a_w_in', 'new_v_a_conv_w', 'new_v_a_w_out', 'new_v_b_w_in', 'new_v_b_v_g', 'new_v_b_v_b', 'new_v_b_w_s', 'new_v_b_s_bias', 'new_v_b_w_out', 'new_v_c_w_in', 'new_v_c_conv_w', 'new_v_c_conv_b', 'new_v_c_ln_g', 'new_v_c_ln_b', 'new_v_c_w_out']
TWIN_LEAF_KINDS = {'loss': 'loss', 'grad_x': 'grad_x', 'grad_mix_norm': 'grad_w', 'grad_xa_norm': 'grad_w', 'grad_xa_wq': 'grad_w', 'grad_xa_wkv': 'grad_w', 'grad_xa_wo': 'grad_w', 'grad_ffn_norm': 'grad_w', 'grad_ffn_w_gu': 'grad_w', 'grad_ffn_w_down': 'grad_w', 'grad_a_w_in': 'grad_w', 'grad_a_conv_w': 'grad_w', 'grad_a_w_out': 'grad_w', 'grad_b_w_in': 'grad_w', 'grad_b_v_g': 'grad_w', 'grad_b_v_b': 'grad_w', 'grad_b_w_s': 'grad_w', 'grad_b_s_bias': 'grad_w', 'grad_b_w_out': 'grad_w', 'grad_c_w_in': 'grad_w', 'grad_c_conv_w': 'grad_w', 'grad_c_conv_b': 'grad_w', 'grad_c_ln_g': 'grad_w', 'grad_c_ln_b': 'grad_w', 'grad_c_w_out': 'grad_w', 'delta_mix_norm': 'delta_w', 'delta_xa_norm': 'delta_w', 'delta_xa_wq': 'delta_w', 'delta_xa_wkv': 'delta_w', 'delta_xa_wo': 'delta_w', 'delta_ffn_norm': 'delta_w', 'delta_ffn_w_gu': 'delta_w', 'delta_ffn_w_down': 'delta_w', 'delta_a_w_in': 'delta_w', 'delta_a_conv_w': 'delta_w', 'delta_a_w_out': 'delta_w', 'delta_b_w_in': 'delta_w', 'delta_b_v_g': 'delta_w', 'delta_b_v_b': 'delta_w', 'delta_b_w_s': 'delta_w', 'delta_b_s_bias': 'delta_w', 'delta_b_w_out': 'delta_w', 'delta_c_w_in': 'delta_w', 'delta_c_conv_w': 'delta_w', 'delta_c_conv_b': 'delta_w', 'delta_c_ln_g': 'delta_w', 'delta_c_ln_b': 'delta_w', 'delta_c_w_out': 'delta_w', 'new_m_mix_norm': 'new_m', 'new_m_xa_norm': 'new_m', 'new_m_xa_wq': 'new_m', 'new_m_xa_wkv': 'new_m', 'new_m_xa_wo': 'new_m', 'new_m_ffn_norm': 'new_m', 'new_m_ffn_w_gu': 'new_m', 'new_m_ffn_w_down': 'new_m', 'new_m_a_w_in': 'new_m', 'new_m_a_conv_w': 'new_m', 'new_m_a_w_out': 'new_m', 'new_m_b_w_in': 'new_m', 'new_m_b_v_g': 'new_m', 'new_m_b_v_b': 'new_m', 'new_m_b_w_s': 'new_m', 'new_m_b_s_bias': 'new_m', 'new_m_b_w_out': 'new_m', 'new_m_c_w_in': 'new_m', 'new_m_c_conv_w': 'new_m', 'new_m_c_conv_b': 'new_m', 'new_m_c_ln_g': 'new_m', 'new_m_c_ln_b': 'new_m', 'new_m_c_w_out': 'new_m', 'new_v_mix_norm': 'new_v', 'new_v_xa_norm': 'new_v', 'new_v_xa_wq': 'new_v', 'new_v_xa_wkv': 'new_v', 'new_v_xa_wo': 'new_v', 'new_v_ffn_norm': 'new_v', 'new_v_ffn_w_gu': 'new_v', 'new_v_ffn_w_down': 'new_v', 'new_v_a_w_in': 'new_v', 'new_v_a_conv_w': 'new_v', 'new_v_a_w_out': 'new_v', 'new_v_b_w_in': 'new_v', 'new_v_b_v_g': 'new_v', 'new_v_b_v_b': 'new_v', 'new_v_b_w_s': 'new_v', 'new_v_b_s_bias': 'new_v', 'new_v_b_w_out': 'new_v', 'new_v_c_w_in': 'new_v', 'new_v_c_conv_w': 'new_v', 'new_v_c_conv_b': 'new_v', 'new_v_c_ln_g': 'new_v', 'new_v_c_ln_b': 'new_v', 'new_v_c_w_out': 'new_v'}


def _forward(args):
    return _fwd_reference(*[args[k] for k in FWD_PARAMS])


def _output_shape():
    out = _jax.eval_shape(lambda: _forward(_fwd_setup_inputs(0)))
    return out.shape, out.dtype

N_MICROBATCH = 1
ADAM_LR = 0.001
ADAM_B1 = 0.9
ADAM_B2 = 0.999
ADAM_EPS = 1e-08
ADAM_WD = 0.01
ADAM_STEP = 10
PER_EXAMPLE_BATCH_AXIS = {'x': 0, 'mem': 0, 'loss_target': 0}
SHARED_INPUTS = []
_WEIGHT_DTYPES = {'mix_norm': _jnp.float32, 'xa_norm': _jnp.float32, 'xa_wq': _jnp.float32, 'xa_wkv': _jnp.float32, 'xa_wo': _jnp.float32, 'ffn_norm': _jnp.float32, 'ffn_w_gu': _jnp.float32, 'ffn_w_down': _jnp.float32, 'a_w_in': _jnp.float32, 'a_conv_w': _jnp.float32, 'a_w_out': _jnp.float32, 'b_w_in': _jnp.float32, 'b_v_g': _jnp.float32, 'b_v_b': _jnp.float32, 'b_w_s': _jnp.float32, 'b_s_bias': _jnp.float32, 'b_w_out': _jnp.float32, 'c_w_in': _jnp.float32, 'c_conv_w': _jnp.float32, 'c_conv_b': _jnp.float32, 'c_ln_g': _jnp.float32, 'c_ln_b': _jnp.float32, 'c_w_out': _jnp.float32}
MOMENT_SCALE = {'mix_norm': 1.102229e+01, 'xa_norm': 1.132460e+01, 'xa_wq': 1.913140e+00, 'xa_wkv': 5.343001e+00, 'xa_wo': 7.539460e+00, 'ffn_norm': 1.132494e+01, 'ffn_w_gu': 1.051124e+00, 'ffn_w_down': 2.063246e+00, 'a_w_in': 9.790998e-01, 'a_conv_w': 9.966304e-01, 'a_w_out': 1.000201e+00, 'b_w_in': 1.511203e+00, 'b_v_g': 4.166164e-01, 'b_v_b': 4.490557e-01, 'b_w_s': 5.374097e-01, 'b_s_bias': 7.643488e-01, 'b_w_out': 5.442653e+00, 'c_w_in': 2.498645e+00, 'c_conv_w': 3.746022e+00, 'c_conv_b': 1.931632e+01, 'c_ln_g': 8.458372e+00, 'c_ln_b': 1.134570e+01, 'c_w_out': 5.932939e+00}


def _to_microbatches(a, axis):
    t = _jnp.moveaxis(a, axis, 0)
    t = t.reshape((N_MICROBATCH, t.shape[0] // N_MICROBATCH) + t.shape[1:])
    return _jnp.moveaxis(t, 1, axis + 1)


def setup_inputs(seed: int = 0) -> dict:
    inp = _fwd_setup_inputs(seed)
    key = _jax.random.fold_in(_jax.random.key(seed), 7919)
    shape, _ = _output_shape()
    out = dict(inp)
    out["loss_target"] = _jax.random.normal(_jax.random.fold_in(key, 0), shape, _jnp.float32)
    for i, name in enumerate(TWIN_WEIGHTS):
        w = inp[name].astype(_jnp.float32)
        if MOMENT_SCALE is None:
            s = _jnp.sqrt(_jnp.mean(_jnp.square(w)) + 1e-30)
        else:
            s = MOMENT_SCALE[name]
        km, kv = _jax.random.split(_jax.random.fold_in(key, i + 1))
        out[name] = w
        out["m_" + name] = s * _jax.random.normal(km, w.shape, _jnp.float32)
        out["v_" + name] = (s * s) * _jax.random.uniform(kv, w.shape, _jnp.float32, 0.5, 1.5)
    if N_MICROBATCH > 1:
        for name, axis in PER_EXAMPLE_BATCH_AXIS.items():
            out[name] = _to_microbatches(out[name], axis)
    return {'x': out['x'], 'mem': out['mem'], 'mix_norm': out['mix_norm'], 'xa_norm': out['xa_norm'], 'xa_wq': out['xa_wq'], 'xa_wkv': out['xa_wkv'], 'xa_wo': out['xa_wo'], 'ffn_norm': out['ffn_norm'], 'ffn_w_gu': out['ffn_w_gu'], 'ffn_w_down': out['ffn_w_down'], 'a_w_in': out['a_w_in'], 'a_conv_w': out['a_conv_w'], 'a_w_out': out['a_w_out'], 'b_w_in': out['b_w_in'], 'b_v_g': out['b_v_g'], 'b_v_b': out['b_v_b'], 'b_w_s': out['b_w_s'], 'b_s_bias': out['b_s_bias'], 'b_w_out': out['b_w_out'], 'c_w_in': out['c_w_in'], 'c_conv_w': out['c_conv_w'], 'c_conv_b': out['c_conv_b'], 'c_ln_g': out['c_ln_g'], 'c_ln_b': out['c_ln_b'], 'c_w_out': out['c_w_out'], 'loss_target': out['loss_target'], 'm_mix_norm': out['m_mix_norm'], 'm_xa_norm': out['m_xa_norm'], 'm_xa_wq': out['m_xa_wq'], 'm_xa_wkv': out['m_xa_wkv'], 'm_xa_wo': out['m_xa_wo'], 'm_ffn_norm': out['m_ffn_norm'], 'm_ffn_w_gu': out['m_ffn_w_gu'], 'm_ffn_w_down': out['m_ffn_w_down'], 'm_a_w_in': out['m_a_w_in'], 'm_a_conv_w': out['m_a_conv_w'], 'm_a_w_out': out['m_a_w_out'], 'm_b_w_in': out['m_b_w_in'], 'm_b_v_g': out['m_b_v_g'], 'm_b_v_b': out['m_b_v_b'], 'm_b_w_s': out['m_b_w_s'], 'm_b_s_bias': out['m_b_s_bias'], 'm_b_w_out': out['m_b_w_out'], 'm_c_w_in': out['m_c_w_in'], 'm_c_conv_w': out['m_c_conv_w'], 'm_c_conv_b': out['m_c_conv_b'], 'm_c_ln_g': out['m_c_ln_g'], 'm_c_ln_b': out['m_c_ln_b'], 'm_c_w_out': out['m_c_w_out'], 'v_mix_norm': out['v_mix_norm'], 'v_xa_norm': out['v_xa_norm'], 'v_xa_wq': out['v_xa_wq'], 'v_xa_wkv': out['v_xa_wkv'], 'v_xa_wo': out['v_xa_wo'], 'v_ffn_norm': out['v_ffn_norm'], 'v_ffn_w_gu': out['v_ffn_w_gu'], 'v_ffn_w_down': out['v_ffn_w_down'], 'v_a_w_in': out['v_a_w_in'], 'v_a_conv_w': out['v_a_conv_w'], 'v_a_w_out': out['v_a_w_out'], 'v_b_w_in': out['v_b_w_in'], 'v_b_v_g': out['v_b_v_g'], 'v_b_v_b': out['v_b_v_b'], 'v_b_w_s': out['v_b_w_s'], 'v_b_s_bias': out['v_b_s_bias'], 'v_b_w_out': out['v_b_w_out'], 'v_c_w_in': out['v_c_w_in'], 'v_c_conv_w': out['v_c_conv_w'], 'v_c_conv_b': out['v_c_conv_b'], 'v_c_ln_g': out['v_c_ln_g'], 'v_c_ln_b': out['v_c_ln_b'], 'v_c_w_out': out['v_c_w_out']}


def _loss(weights, diff, rest, loss_target):
    with _jax.named_scope("forward"):
        args = {**rest, TWIN_DIFF_INPUT: diff, **{k: w.astype(_WEIGHT_DTYPES[k]) for k, w in weights.items()}}
        y = _forward(args)
    with _jax.named_scope("loss_head"):
        err = _jnp.square(y.astype(_jnp.float32) - loss_target)
        return 0.5 * _jnp.sum(_jnp.mean(err, axis=-1)) if err.ndim else 0.5 * err


def _adamw(w, g, m, v):
    m = ADAM_B1 * m + (1.0 - ADAM_B1) * g
    v = ADAM_B2 * v + (1.0 - ADAM_B2) * _jnp.square(g)
    m_hat = m / (1.0 - ADAM_B1 ** ADAM_STEP)
    v_hat = v / (1.0 - ADAM_B2 ** ADAM_STEP)
    delta = -ADAM_LR * (m_hat / (_jnp.sqrt(v_hat) + ADAM_EPS) + ADAM_WD * w)
    return delta, m, v


def reference(x, mem, mix_norm, xa_norm, xa_wq, xa_wkv, xa_wo, ffn_norm, ffn_w_gu, ffn_w_down, a_w_in, a_conv_w, a_w_out, b_w_in, b_v_g, b_v_b, b_w_s, b_s_bias, b_w_out, c_w_in, c_conv_w, c_conv_b, c_ln_g, c_ln_b, c_w_out, loss_target, m_mix_norm, m_xa_norm, m_xa_wq, m_xa_wkv, m_xa_wo, m_ffn_norm, m_ffn_w_gu, m_ffn_w_down, m_a_w_in, m_a_conv_w, m_a_w_out, m_b_w_in, m_b_v_g, m_b_v_b, m_b_w_s, m_b_s_bias, m_b_w_out, m_c_w_in, m_c_conv_w, m_c_conv_b, m_c_ln_g, m_c_ln_b, m_c_w_out, v_mix_norm, v_xa_norm, v_xa_wq, v_xa_wkv, v_xa_wo, v_ffn_norm, v_ffn_w_gu, v_ffn_w_down, v_a_w_in, v_a_conv_w, v_a_w_out, v_b_w_in, v_b_v_g, v_b_v_b, v_b_w_s, v_b_s_bias, v_b_w_out, v_c_w_in, v_c_conv_w, v_c_conv_b, v_c_ln_g, v_c_ln_b, v_c_w_out):
    given = dict(x=x, mem=mem, mix_norm=mix_norm, xa_norm=xa_norm, xa_wq=xa_wq, xa_wkv=xa_wkv, xa_wo=xa_wo, ffn_norm=ffn_norm, ffn_w_gu=ffn_w_gu, ffn_w_down=ffn_w_down, a_w_in=a_w_in, a_conv_w=a_conv_w, a_w_out=a_w_out, b_w_in=b_w_in, b_v_g=b_v_g, b_v_b=b_v_b, b_w_s=b_w_s, b_s_bias=b_s_bias, b_w_out=b_w_out, c_w_in=c_w_in, c_conv_w=c_conv_w, c_conv_b=c_conv_b, c_ln_g=c_ln_g, c_ln_b=c_ln_b, c_w_out=c_w_out, loss_target=loss_target, m_mix_norm=m_mix_norm, m_xa_norm=m_xa_norm, m_xa_wq=m_xa_wq, m_xa_wkv=m_xa_wkv, m_xa_wo=m_xa_wo, m_ffn_norm=m_ffn_norm, m_ffn_w_gu=m_ffn_w_gu, m_ffn_w_down=m_ffn_w_down, m_a_w_in=m_a_w_in, m_a_conv_w=m_a_conv_w, m_a_w_out=m_a_w_out, m_b_w_in=m_b_w_in, m_b_v_g=m_b_v_g, m_b_v_b=m_b_v_b, m_b_w_s=m_b_w_s, m_b_s_bias=m_b_s_bias, m_b_w_out=m_b_w_out, m_c_w_in=m_c_w_in, m_c_conv_w=m_c_conv_w, m_c_conv_b=m_c_conv_b, m_c_ln_g=m_c_ln_g, m_c_ln_b=m_c_ln_b, m_c_w_out=m_c_w_out, v_mix_norm=v_mix_norm, v_xa_norm=v_xa_norm, v_xa_wq=v_xa_wq, v_xa_wkv=v_xa_wkv, v_xa_wo=v_xa_wo, v_ffn_norm=v_ffn_norm, v_ffn_w_gu=v_ffn_w_gu, v_ffn_w_down=v_ffn_w_down, v_a_w_in=v_a_w_in, v_a_conv_w=v_a_conv_w, v_a_w_out=v_a_w_out, v_b_w_in=v_b_w_in, v_b_v_g=v_b_v_g, v_b_v_b=v_b_v_b, v_b_w_s=v_b_w_s, v_b_s_bias=v_b_s_bias, v_b_w_out=v_b_w_out, v_c_w_in=v_c_w_in, v_c_conv_w=v_c_conv_w, v_c_conv_b=v_c_conv_b, v_c_ln_g=v_c_ln_g, v_c_ln_b=v_c_ln_b, v_c_w_out=v_c_w_out)
    weights = {n: given[n] for n in TWIN_WEIGHTS}
    shared = {n: given[n] for n in SHARED_INPUTS}
    per_example = {n: given[n] for n in ['x', 'mem']}
    grad_fn = _jax.value_and_grad(_loss, argnums=(0, 1))

    def one_microbatch(ex, loss_target):
        ex = dict(ex)
        diff = ex.pop(TWIN_DIFF_INPUT)
        return grad_fn(weights, diff, {**shared, **ex}, loss_target)

    if N_MICROBATCH == 1:
        loss, (grad_w, grad_x) = one_microbatch(per_example, given["loss_target"])
    else:
        def body(carry, xs):
            loss_sum, grad_sum = carry
            l_k, (gw_k, gx_k) = one_microbatch(xs[0], xs[1])
            with _jax.named_scope("update"):
                return (loss_sum + l_k, _jax.tree.map(_jnp.add, grad_sum, gw_k)), gx_k

        init = (_jnp.zeros((), _jnp.float32), _jax.tree.map(_jnp.zeros_like, weights))
        (loss, grad_w), grad_x = _jax.lax.scan(body, init, (per_example, given["loss_target"]))
    with _jax.named_scope("update"):
        delta_w, new_m, new_v = {}, {}, {}
        for n in TWIN_WEIGHTS:
            delta_w[n], new_m[n], new_v[n] = _adamw(weights[n], grad_w[n], given["m_" + n], given["v_" + n])
    return (loss, grad_x, *[grad_w[n] for n in TWIN_WEIGHTS], *[delta_w[n] for n in TWIN_WEIGHTS],
            *[new_m[n] for n in TWIN_WEIGHTS], *[new_v[n] for n in TWIN_WEIGHTS])
```

```python
import functools

import jax
import jax.numpy as jnp
from jax import lax
from jax.experimental import pallas as pl
from jax.experimental.pallas import tpu as pltpu

F32 = jnp.float32
BF16 = jnp.bfloat16
EPS = 1e-6
XA_HEADS = 4
CHUNK = 128
GMLP_GROUPS = 8
ADAM_LR, ADAM_B1, ADAM_B2, ADAM_EPS, ADAM_WD, ADAM_STEP = 0.001, 0.9, 0.999, 1e-08, 0.01, 10
VMEM_LIMIT_V7X = 48 * 1024 * 1024
HBM = pl.BlockSpec(memory_space=pltpu.HBM)
MESH = pl.DeviceIdType.MESH
N_CHIPS = 4
BIG_KINDS = {"xa_wq": "row", "xa_wkv": "col", "xa_wo": "row", "ffn_w_gu": "col", "ffn_w_down": "row",
             "a_w_in": "col", "a_w_out": "row", "b_w_in": "col", "b_w_out": "row", "c_w_in": "col", "c_w_out": "row"}
SMALL_SHARDED = ["mix_norm", "xa_norm", "ffn_norm", "a_conv_w", "c_conv_w", "c_conv_b", "c_ln_g", "c_ln_b"]
SMALL_REPL = ["b_v_g", "b_v_b", "b_w_s", "b_s_bias"]
WEIGHTS = ["mix_norm", "xa_norm", "xa_wq", "xa_wkv", "xa_wo", "ffn_norm", "ffn_w_gu", "ffn_w_down", "a_w_in", "a_conv_w",
           "a_w_out", "b_w_in", "b_v_g", "b_v_b", "b_w_s", "b_s_bias", "b_w_out", "c_w_in", "c_conv_w", "c_conv_b",
           "c_ln_g", "c_ln_b", "c_w_out"]


def _params(*sem):
    return pltpu.CompilerParams(dimension_semantics=sem, vmem_limit_bytes=VMEM_LIMIT_V7X)


def _tile(n, cands=(1024, 512, 256, 128)):
    for c in cands:
        if n % c == 0:
            return c
    return n


_DIMS = {"nn": (((1,), (0,)), ((), ())), "nt": (((1,), (1,)), ((), ())), "tn": (((0,), (0,)), ((), ()))}


def _mm(mode, a, b, out_dtype, name, *, bl=None, a_parts=1, b_parts=1, o_parts=1, into=None, into_l=0):
    bshape = b.shape[1:] if bl is not None else b.shape
    if mode == "nn":
        mo, c = a.shape
        no = bshape[1]
    elif mode == "nt":
        mo, c = (a.shape[1], a.shape[0] * a.shape[2]) if a_parts > 1 else a.shape
        no = bshape[0]
    else:
        c, mo = a.shape
        no = b.shape[0] * b.shape[2] if b_parts > 1 else bshape[1]
    tmo = _tile(mo)
    tno = _tile(no // max(o_parts, b_parts))
    tc = _tile(c // a_parts)
    nk = c // tc
    nkp = nk // a_parts
    njp = (no // tno) // max(o_parts, b_parts)
    lead = (None,) if bl is not None else ()
    lidx = (bl,) if bl is not None else ()

    if mode == "nn":
        a_spec = pl.BlockSpec((tmo, tc), lambda i, j, k: (i, k))
        b_spec = pl.BlockSpec(lead + (tc, tno), lambda i, j, k: lidx + (k, j))
    elif mode == "nt":
        if a_parts > 1:
            a_spec = pl.BlockSpec((None, tmo, tc), lambda i, j, k: (k // nkp, i, k % nkp))
        else:
            a_spec = pl.BlockSpec((tmo, tc), lambda i, j, k: (i, k))
        b_spec = pl.BlockSpec(lead + (tno, tc), lambda i, j, k: lidx + (j, k))
    else:
        a_spec = pl.BlockSpec((tc, tmo), lambda i, j, k: (k, i))
        if b_parts > 1:
            b_spec = pl.BlockSpec((None, tc, tno), lambda i, j, k: (j // njp, k, j % njp))
        else:
            b_spec = pl.BlockSpec((tc, tno), lambda i, j, k: (k, j))

    in_specs = [a_spec, b_spec]
    args = [a, b]
    aliases = {}
    if into is not None:
        out_shape = jax.ShapeDtypeStruct(into.shape, into.dtype)
        out_spec = pl.BlockSpec((None, tmo, tno), lambda i, j, k: (into_l, i, j))
        in_specs.append(HBM)
        args.append(into)
        aliases = {2: 0}
    elif o_parts > 1:
        out_shape = jax.ShapeDtypeStruct((o_parts, mo, no // o_parts), out_dtype)
        out_spec = pl.BlockSpec((None, tmo, tno), lambda i, j, k: (j // njp, i, j % njp))
    else:
        out_shape = jax.ShapeDtypeStruct((mo, no), out_dtype)
        out_spec = pl.BlockSpec((tmo, tno), lambda i, j, k: (i, j))
    dims = _DIMS[mode]

    def body(a_ref, b_ref, *rest):
        o_ref, acc = rest[-2], rest[-1]
        k = pl.program_id(2)

        @pl.when(k == 0)
        def _():
            acc[...] = jnp.zeros_like(acc)

        acc[...] += lax.dot_general(a_ref[...], b_ref[...], dims, preferred_element_type=F32)

        @pl.when(k == nk - 1)
        def _():
            o_ref[...] = acc[...].astype(o_ref.dtype)

    return pl.pallas_call(
        body, name=name, out_shape=out_shape, grid=(mo // tmo, no // tno, nk), in_specs=in_specs, out_specs=out_spec,
        scratch_shapes=[pltpu.VMEM((tmo, tno), F32)], input_output_aliases=aliases,
        compiler_params=_params("parallel", "parallel", "arbitrary"))(*args)


def _ew(fn, ins, out_dtypes, name):
    rows, cols = ins[0].shape
    tr = rows
    for cand in (512, 256, 128, 64, 32, 16):
        if rows % cand == 0 and cand * cols * 4 <= (1 << 20):
            tr = cand
            break
    spec = pl.BlockSpec((tr, cols), lambda i: (i, 0))
    n_in = len(ins)

    def body(*refs):
        outs = fn(*[r[...] for r in refs[:n_in]])
        for o_ref, o in zip(refs[n_in:], outs):
            o_ref[...] = o.astype(o_ref.dtype)

    return pl.pallas_call(
        body, name=name, out_shape=[jax.ShapeDtypeStruct((rows, cols), d) for d in out_dtypes], grid=(rows // tr,),
        in_specs=[spec] * n_in, out_specs=[spec] * len(out_dtypes), compiler_params=_params("parallel"))(*ins)


def _adamw_fn(w, g, m, v):
    m = ADAM_B1 * m + (1.0 - ADAM_B1) * g
    v = ADAM_B2 * v + (1.0 - ADAM_B2) * (g * g)
    m_hat = m / (1.0 - ADAM_B1 ** ADAM_STEP)
    v_hat = v / (1.0 - ADAM_B2 ** ADAM_STEP)
    delta = -ADAM_LR * (m_hat / (jnp.sqrt(v_hat) + ADAM_EPS) + ADAM_WD * w)
    return delta, m, v


def _adamw(w, g, m, v, name):
    shape = w.shape
    cols = shape[-1]
    flat = [t.reshape(-1, cols) for t in (w, g, m, v)]
    outs = _ew(_adamw_fn, flat, [F32] * 3, name)
    return [o.reshape(shape) for o in outs]


def _row_tile(s):
    return _tile(s, (256, 128, 64, 32, 16, 8))


def _rms_fwd(x, g, name):
    s, d = x.shape
    r = _row_tile(s)

    def body(x_ref, g_ref, o_ref):
        xv = x_ref[...]
        o_ref[...] = (xv * lax.rsqrt(jnp.mean(xv * xv, axis=-1, keepdims=True) + EPS) * g_ref[...]).astype(BF16)

    return pl.pallas_call(
        body, name=name, out_shape=jax.ShapeDtypeStruct((s, d), BF16), grid=(s // r,),
        in_specs=[pl.BlockSpec((r, d), lambda i: (i, 0)), pl.BlockSpec((1, d), lambda i: (0, 0))],
        out_specs=pl.BlockSpec((r, d), lambda i: (i, 0)), compiler_params=_params("parallel"))(x, g)


def _res_rms_fwd(x, y, g, name):
    s, d = x.shape
    r = _row_tile(s)

    def body(x_ref, y_ref, g_ref, o_ref):
        yv = y_ref[...]
        o_ref[...] = x_ref[...] + yv * lax.rsqrt(jnp.mean(yv * yv, axis=-1, keepdims=True) + EPS) * g_ref[...]

    row = pl.BlockSpec((r, d), lambda i: (i, 0))
    return pl.pallas_call(
        body, name=name, out_shape=jax.ShapeDtypeStruct((s, d), F32), grid=(s // r,),
        in_specs=[row, row, pl.BlockSpec((1, d), lambda i: (0, 0))], out_specs=row,
        compiler_params=_params("parallel"))(x, y, g)


def _rms_bwd(x, g, dy, resid, out_dtype, name):
    s, d = x.shape
    r = _row_tile(s)
    has_res = resid is not None

    def body(*refs):
        x_ref, g_ref, dy_ref = refs[:3]
        dx_ref, dg_ref = refs[-2:]
        i = pl.program_id(0)
        xv = x_ref[...]
        dyv = dy_ref[...].astype(F32)
        rstd = lax.rsqrt(jnp.mean(xv * xv, axis=-1, keepdims=True) + EPS)
        n = xv * rstd
        dn = dyv * g_ref[...]
        dx = rstd * (dn - n * jnp.mean(dn * n, axis=-1, keepdims=True))
        if has_res:
            dx = dx + refs[3][...]
        dx_ref[...] = dx.astype(dx_ref.dtype)
        part = jnp.sum(dyv * n, axis=0, keepdims=True)

        @pl.when(i == 0)
        def _():
            dg_ref[...] = part

        @pl.when(i > 0)
        def _():
            dg_ref[...] += part

    row = pl.BlockSpec((r, d), lambda i: (i, 0))
    vec = pl.BlockSpec((1, d), lambda i: (0, 0))
    ins = [x, g, dy] + ([resid] if has_res else [])
    return pl.pallas_call(
        body, name=name, out_shape=[jax.ShapeDtypeStruct((s, d), out_dtype), jax.ShapeDtypeStruct((1, d), F32)],
        grid=(s // r,), in_specs=[row, vec, row] + ([row] if has_res else []), out_specs=[row, vec],
        compiler_params=_params("arbitrary"))(*ins)


def _loss(y, t, name):
    s, d = y.shape
    r = _row_tile(s)

    def body(y_ref, t_ref, l_ref, dy_ref):
        i = pl.program_id(0)
        e = y_ref[...] - t_ref[...]
        dy_ref[...] = e * (1.0 / d)
        part = jnp.full((8, 128), 0.5 * jnp.sum(jnp.mean(e * e, axis=-1, keepdims=True)), F32)

        @pl.when(i == 0)
        def _():
            l_ref[...] = part

        @pl.when(i > 0)
        def _():
            l_ref[...] += part

    row = pl.BlockSpec((r, d), lambda i: (i, 0))
    return pl.pallas_call(
        body, name=name, out_shape=[jax.ShapeDtypeStruct((8, 128), F32), jax.ShapeDtypeStruct((s, d), F32)],
        grid=(s // r,), in_specs=[row, row], out_specs=[pl.BlockSpec((8, 128), lambda i: (0, 0)), row],
        compiler_params=_params("arbitrary"))(y, t)


def _rows(xv, a, m, cache):
    r = a % 8
    q = a - r
    if r == 0:
        return xv[q:q + m]
    if r not in cache:
        cache[r] = pltpu.roll(xv, xv.shape[0] - r, 0)
    return cache[r][q:q + m]


def _conv_taps(xv, w, k_w, halo, m, flip):
    cache = {}
    acc = None
    for k in range(k_w):
        a = (k_w - 1 - k) if flip else (halo + k - (k_w - 1))
        term = w[k:k + 1, :] * _rows(xv, a, m, cache)
        acc = term if acc is None else acc + term
    return acc


def _conv_wgrad(dw_ref, dyv, xv, k_w, halo, m):
    cache = {}
    for k in range(k_w):
        xs = _rows(xv, halo + k - (k_w - 1), m, cache)
        dw_ref[pl.ds(k, 1), :] += jnp.sum(dyv * xs, axis=0, keepdims=True)


def _conv_tiles(s, dp, halo):
    r = _tile(s, (256, 128))
    cw = _tile(dp, (256, 128))
    return r, cw, r // halo


A_HALO = 8


def _a_mid_fwd(bcz3, w, name):
    _, s, d = bcz3.shape
    r, cw, rh = _conv_tiles(s, d, A_HALO)
    k_w = w.shape[0]

    def body(m_ref, h_ref, w_ref, o_ref):
        i = pl.program_id(0)
        cz = m_ref[1].astype(F32) * m_ref[2].astype(F32)
        hcz = h_ref[1].astype(F32) * h_ref[2].astype(F32)
        hcz = jnp.where(i == 0, 0.0, hcz)
        xv = jnp.concatenate([hcz, cz], axis=0)
        y = _conv_taps(xv, w_ref[...], k_w, A_HALO, r, False)
        o_ref[...] = (m_ref[0].astype(F32) * y).astype(BF16)

    return pl.pallas_call(
        body, name=name, out_shape=jax.ShapeDtypeStruct((s, d), BF16), grid=(s // r, d // cw),
        in_specs=[pl.BlockSpec((3, r, cw), lambda i, j: (0, i, j)),
                  pl.BlockSpec((3, A_HALO, cw), lambda i, j: (0, jnp.maximum(i * rh - 1, 0), j)),
                  pl.BlockSpec((k_w, cw), lambda i, j: (0, j))],
        out_specs=pl.BlockSpec((r, cw), lambda i, j: (i, j)), compiler_params=_params("parallel", "parallel"))(bcz3, bcz3, w)


def _a_mid_bwd(bcz3, dgated, w, name):
    _, s, d = bcz3.shape
    r, cw, rh = _conv_tiles(s, d, A_HALO)
    k_w = w.shape[0]
    ni = s // r
    last_h = s // A_HALO - 1

    def body(m_ref, hp_ref, hn_ref, dg_ref, dgn_ref, w_ref, o_ref, dw_ref):
        i = pl.program_id(1)
        wv = w_ref[...]
        b = m_ref[0].astype(F32)
        c = m_ref[1].astype(F32)
        z = m_ref[2].astype(F32)
        hcz = jnp.where(i == 0, 0.0, hp_ref[1].astype(F32) * hp_ref[2].astype(F32))
        xv = jnp.concatenate([hcz, c * z], axis=0)
        y = _conv_taps(xv, wv, k_w, A_HALO, r, False)
        dg = dg_ref[...].astype(F32)
        dy = dg * b
        dyn = jnp.where(i == ni - 1, 0.0, dgn_ref[...].astype(F32) * hn_ref[0].astype(F32))
        dcz = _conv_taps(jnp.concatenate([dy, dyn], axis=0), wv, k_w, A_HALO, r, True)
        o_ref[0] = (dg * y).astype(BF16)
        o_ref[1] = (dcz * z).astype(BF16)
        o_ref[2] = (dcz * c).astype(BF16)

        @pl.when(i == 0)
        def _():
            dw_ref[...] = jnp.zeros_like(dw_ref)

        _conv_wgrad(dw_ref, dy, xv, k_w, A_HALO, r)

    return pl.pallas_call(
        body, name=name, out_shape=[jax.ShapeDtypeStruct((3, s, d), BF16), jax.ShapeDtypeStruct((k_w, d), F32)],
        grid=(d // cw, ni),
        in_specs=[pl.BlockSpec((3, r, cw), lambda j, i: (0, i, j)),
                  pl.BlockSpec((3, A_HALO, cw), lambda j, i: (0, jnp.maximum(i * rh - 1, 0), j)),
                  pl.BlockSpec((3, A_HALO, cw), lambda j, i: (0, jnp.minimum((i + 1) * rh, last_h), j)),
                  pl.BlockSpec((r, cw), lambda j, i: (i, j)),
                  pl.BlockSpec((A_HALO, cw), lambda j, i: (jnp.minimum((i + 1) * rh, last_h), j)),
                  pl.BlockSpec((k_w, cw), lambda j, i: (0, j))],
        out_specs=[pl.BlockSpec((3, r, cw), lambda j, i: (0, i, j)), pl.BlockSpec((k_w, cw), lambda j, i: (0, j))],
        compiler_params=_params("parallel", "arbitrary"))(bcz3, bcz3, bcz3, dgated, dgated, w)


C_HALO = 32


def _c_conv_fwd(ag3, w, bias, name):
    _, s, d = ag3.shape
    r, cw, rh = _conv_tiles(s, d, C_HALO)
    k_w = w.shape[0]

    def body(m_ref, h_ref, w_ref, b_ref, o_ref):
        i = pl.program_id(0)
        y1 = m_ref[0].astype(F32) * jax.nn.sigmoid(m_ref[1].astype(F32))
        h1 = jnp.where(i == 0, 0.0, h_ref[0].astype(F32) * jax.nn.sigmoid(h_ref[1].astype(F32)))
        xv = jnp.concatenate([h1, y1], axis=0)
        o_ref[...] = _conv_taps(xv, w_ref[...], k_w, C_HALO, r, False) + b_ref[...]

    return pl.pallas_call(
        body, name=name, out_shape=jax.ShapeDtypeStruct((s, d), F32), grid=(s // r, d // cw),
        in_specs=[pl.BlockSpec((2, r, cw), lambda i, j: (0, i, j)),
                  pl.BlockSpec((2, C_HALO, cw), lambda i, j: (0, jnp.maximum(i * rh - 1, 0), j)),
                  pl.BlockSpec((k_w, cw), lambda i, j: (0, j)), pl.BlockSpec((1, cw), lambda i, j: (0, j))],
        out_specs=pl.BlockSpec((r, cw), lambda i, j: (i, j)),
        compiler_params=_params("parallel", "parallel"))(ag3, ag3, w, bias)


def _c_conv_bwd(ag3, dy2, w, name):
    _, s, d = ag3.shape
    r, cw, rh = _conv_tiles(s, d, C_HALO)
    k_w = w.shape[0]
    ni = s // r
    last_h = s // C_HALO - 1

    def body(m_ref, hp_ref, dy_ref, dyn_ref, w_ref, o_ref, dw_ref, db_ref):
        i = pl.program_id(1)
        wv = w_ref[...]
        a = m_ref[0].astype(F32)
        sg = jax.nn.sigmoid(m_ref[1].astype(F32))
        h1 = jnp.where(i == 0, 0.0, hp_ref[0].astype(F32) * jax.nn.sigmoid(hp_ref[1].astype(F32)))
        xv = jnp.concatenate([h1, a * sg], axis=0)
        dy = dy_ref[...]
        dyn = jnp.where(i == ni - 1, 0.0, dyn_ref[...])
        dy1 = _conv_taps(jnp.concatenate([dy, dyn], axis=0), wv, k_w, C_HALO, r, True)
        o_ref[0] = (dy1 * sg).astype(BF16)
        o_ref[1] = (dy1 * a * sg * (1.0 - sg)).astype(BF16)

        @pl.when(i == 0)
        def _():
            dw_ref[...] = jnp.zeros_like(dw_ref)
            db_ref[...] = jnp.zeros_like(db_ref)

        db_ref[...] += jnp.sum(dy, axis=0, keepdims=True)
        _conv_wgrad(dw_ref, dy, xv, k_w, C_HALO, r)

    return pl.pallas_call(
        body, name=name,
        out_shape=[jax.ShapeDtypeStruct((2, s, d), BF16), jax.ShapeDtypeStruct((k_w, d), F32),
                   jax.ShapeDtypeStruct((1, d), F32)],
        grid=(d // cw, ni),
        in_specs=[pl.BlockSpec((2, r, cw), lambda j, i: (0, i, j)),
                  pl.BlockSpec((2, C_HALO, cw), lambda j, i: (0, jnp.maximum(i * rh - 1, 0), j)),
                  pl.BlockSpec((r, cw), lambda j, i: (i, j)),
                  pl.BlockSpec((C_HALO, cw), lambda j, i: (jnp.minimum((i + 1) * rh, last_h), j)),
                  pl.BlockSpec((k_w, cw), lambda j, i: (0, j))],
        out_specs=[pl.BlockSpec((2, r, cw), lambda j, i: (0, i, j)), pl.BlockSpec((k_w, cw), lambda j, i: (0, j)),
                   pl.BlockSpec((1, cw), lambda j, i: (0, j))],
        compiler_params=_params("parallel", "arbitrary"))(ag3, ag3, dy2, dy2, w)


def _ln_stats(v):
    mu = jnp.mean(v, axis=-1, keepdims=True)
    vc = v - mu
    rstd = lax.rsqrt(jnp.mean(vc * vc, axis=-1, keepdims=True) + EPS)
    return vc * rstd, rstd


def _ln_bwd(dn, g, xh, rstd):
    dxh = dn * g
    return rstd * (dxh - jnp.mean(dxh, axis=-1, keepdims=True) - xh * jnp.mean(dxh * xh, axis=-1, keepdims=True))


def _c_ln_fwd(y2, g, b, name):
    s, d = y2.shape
    r = _row_tile(s)

    def body(y_ref, g_ref, b_ref, o_ref):
        xh, _ = _ln_stats(y_ref[...])
        y3 = xh * g_ref[...] + b_ref[...]
        o_ref[...] = (y3 * jax.nn.sigmoid(y3)).astype(BF16)

    row = pl.BlockSpec((r, d), lambda i: (i, 0))
    vec = pl.BlockSpec((1, d), lambda i: (0, 0))
    return pl.pallas_call(
        body, name=name, out_shape=jax.ShapeDtypeStruct((s, d), BF16), grid=(s // r,), in_specs=[row, vec, vec],
        out_specs=row, compiler_params=_params("parallel"))(y2, g, b)


def _c_ln_bwd(y2, dout, g, b, name):
    s, d = y2.shape
    r = _row_tile(s)

    def body(y_ref, do_ref, g_ref, b_ref, dy_ref, dg_ref, db_ref):
        i = pl.program_id(0)
        xh, rstd = _ln_stats(y_ref[...])
        gv = g_ref[...]
        y3 = xh * gv + b_ref[...]
        sg = jax.nn.sigmoid(y3)
        dy3 = do_ref[...].astype(F32) * (sg + y3 * sg * (1.0 - sg))
        dy_ref[...] = _ln_bwd(dy3, gv, xh, rstd)

        @pl.when(i == 0)
        def _():
            dg_ref[...] = jnp.zeros_like(dg_ref)
            db_ref[...] = jnp.zeros_like(db_ref)

        dg_ref[...] += jnp.sum(dy3 * xh, axis=0, keepdims=True)
        db_ref[...] += jnp.sum(dy3, axis=0, keepdims=True)

    row = pl.BlockSpec((r, d), lambda i: (i, 0))
    vec = pl.BlockSpec((1, d), lambda i: (0, 0))
    return pl.pallas_call(
        body, name=name,
        out_shape=[jax.ShapeDtypeStruct((s, d), F32), jax.ShapeDtypeStruct((1, d), F32), jax.ShapeDtypeStruct((1, d), F32)],
        grid=(s // r,), in_specs=[row, row, vec, vec], out_specs=[row, vec, vec],
        compiler_params=_params("arbitrary"))(y2, dout, g, b)


_GELU_C = 0.7978845608028654
_GELU_A = 0.044715


def _gelu(x):
    return 0.5 * x * (1.0 + jnp.tanh(_GELU_C * (x + _GELU_A * x * x * x)))


def _gelu_grad(x):
    t = jnp.tanh(_GELU_C * (x + _GELU_A * x * x * x))
    return 0.5 * (1.0 + t) + 0.5 * x * (1.0 - t * t) * _GELU_C * (1.0 + 3.0 * _GELU_A * x * x)


def _b_mid_fwd(uv3, vg, vb, ws_m, sbt, name):
    _, s, h = uv3.shape
    g_n, t, _ = ws_m.shape
    gd = h // g_n

    def body(uv_ref, vg_ref, vb_ref, ws_ref, sb_ref, o_ref):
        u = _gelu(uv_ref[0].astype(F32))
        xh, _ = _ln_stats(_gelu(uv_ref[1].astype(F32)))
        vn = (xh * vg_ref[...] + vb_ref[...]).astype(BF16)
        for g in range(g_n):
            sl = slice(g * gd, (g + 1) * gd)
            sv = jnp.dot(ws_ref[g], vn[:, sl], preferred_element_type=F32) + sb_ref[:, g:g + 1]
            o_ref[:, sl] = (u[:, sl] * sv).astype(BF16)

    vec = pl.BlockSpec((1, h), lambda i: (0, 0))
    return pl.pallas_call(
        body, name=name, out_shape=jax.ShapeDtypeStruct((s, h), BF16), grid=(s // t,),
        in_specs=[pl.BlockSpec((2, t, h), lambda i: (0, i, 0)), vec, vec,
                  pl.BlockSpec((g_n, t, t), lambda i: (0, 0, 0)), pl.BlockSpec((t, 128), lambda i: (0, 0))],
        out_specs=pl.BlockSpec((t, h), lambda i: (i, 0)), compiler_params=_params("parallel"))(uv3, vg, vb, ws_m, sbt)


def _b_mid_bwd(uv3, dgated, vg, vb, ws_m, sbt, name):
    _, s, h = uv3.shape
    g_n, t, _ = ws_m.shape
    gd = h // g_n

    def body(uv_ref, dg_ref, vg_ref, vb_ref, ws_ref, sb_ref, o_ref, dvg_ref, dvb_ref, dws_ref, dsb_ref, dvn_ref):
        i = pl.program_id(0)

        @pl.when(i == 0)
        def _():
            dvg_ref[...] = jnp.zeros_like(dvg_ref)
            dvb_ref[...] = jnp.zeros_like(dvb_ref)
            dws_ref[...] = jnp.zeros_like(dws_ref)
            dsb_ref[...] = jnp.zeros_like(dsb_ref)

        upre = uv_ref[0].astype(F32)
        vpre = uv_ref[1].astype(F32)
        u = _gelu(upre)
        xh, rstd = _ln_stats(_gelu(vpre))
        gv = vg_ref[...]
        vn = (xh * gv + vb_ref[...]).astype(BF16)
        causal = lax.broadcasted_iota(jnp.int32, (t, t), 0) >= lax.broadcasted_iota(jnp.int32, (t, t), 1)
        lane = lax.broadcasted_iota(jnp.int32, (t, 128), 1)
        for g in range(g_n):
            sl = slice(g * gd, (g + 1) * gd)
            wsg = ws_ref[g]
            sv = jnp.dot(wsg, vn[:, sl], preferred_element_type=F32) + sb_ref[:, g:g + 1]
            dg = dg_ref[:, sl].astype(F32)
            o_ref[0, :, sl] = (dg * sv * _gelu_grad(upre[:, sl])).astype(BF16)
            dsv = dg * u[:, sl]
            dsvb = dsv.astype(BF16)
            dsb_ref[...] += jnp.where(lane == g, jnp.sum(dsv, axis=1, keepdims=True), 0.0)
            dws = lax.dot_general(dsvb, vn[:, sl], _DIMS["nt"], preferred_element_type=F32)
            dws_ref[g] += jnp.where(causal, dws, 0.0)
            dvn_ref[:, sl] = lax.dot_general(wsg, dsvb, _DIMS["tn"], preferred_element_type=F32)
        dvn = dvn_ref[...]
        dvg_ref[...] += jnp.sum(dvn * xh, axis=0, keepdims=True)
        dvb_ref[...] += jnp.sum(dvn, axis=0, keepdims=True)
        o_ref[1] = (_ln_bwd(dvn, gv, xh, rstd) * _gelu_grad(vpre)).astype(BF16)

    vec = pl.BlockSpec((1, h), lambda i: (0, 0))
    return pl.pallas_call(
        body, name=name,
        out_shape=[jax.ShapeDtypeStruct((2, s, h), BF16), jax.ShapeDtypeStruct((1, h), F32), jax.ShapeDtypeStruct((1, h), F32),
                   jax.ShapeDtypeStruct((g_n, t, t), F32), jax.ShapeDtypeStruct((t, 128), F32)],
        grid=(s // t,),
        in_specs=[pl.BlockSpec((2, t, h), lambda i: (0, i, 0)), pl.BlockSpec((t, h), lambda i: (i, 0)), vec, vec,
                  pl.BlockSpec((g_n, t, t), lambda i: (0, 0, 0)), pl.BlockSpec((t, 128), lambda i: (0, 0))],
        out_specs=[pl.BlockSpec((2, t, h), lambda i: (0, i, 0)), vec, vec,
                   pl.BlockSpec((g_n, t, t), lambda i: (0, 0, 0)), pl.BlockSpec((t, 128), lambda i: (0, 0))],
        scratch_shapes=[pltpu.VMEM((t, h), F32)],
        compiler_params=_params("arbitrary"))(uv3, dgated, vg, vb, ws_m, sbt)


def _softmax_rows(sc):
    e = jnp.exp(sc - jnp.max(sc, axis=-1, keepdims=True))
    return e / jnp.sum(e, axis=-1, keepdims=True)


def _attn_fwd(q, kv3, name):
    s, d = q.shape
    m = kv3.shape[1]
    dh = d // XA_HEADS
    scale = dh ** -0.5
    r = _row_tile(s)

    def body(q_ref, kv_ref, o_ref):
        for hd in range(XA_HEADS):
            sl = slice(hd * dh, (hd + 1) * dh)
            sc = lax.dot_general(q_ref[:, sl], kv_ref[0, :, sl], _DIMS["nt"], preferred_element_type=F32) * scale
            p = _softmax_rows(sc).astype(BF16)
            o_ref[:, sl] = jnp.dot(p, kv_ref[1, :, sl], preferred_element_type=F32).astype(BF16)

    return pl.pallas_call(
        body, name=name, out_shape=jax.ShapeDtypeStruct((s, d), BF16), grid=(s // r,),
        in_specs=[pl.BlockSpec((r, d), lambda i: (i, 0)), pl.BlockSpec((2, m, d), lambda i: (0, 0, 0))],
        out_specs=pl.BlockSpec((r, d), lambda i: (i, 0)), compiler_params=_params("parallel"))(q, kv3)


def _attn_bwd(q, kv3, do, name):
    s, d = q.shape
    m = kv3.shape[1]
    dh = d // XA_HEADS
    scale = dh ** -0.5
    r = _row_tile(s)

    def body(q_ref, kv_ref, do_ref, dq_ref, dkv_ref):
        i = pl.program_id(0)

        @pl.when(i == 0)
        def _():
            dkv_ref[...] = jnp.zeros_like(dkv_ref)

        for hd in range(XA_HEADS):
            sl = slice(hd * dh, (hd + 1) * dh)
            qh = q_ref[:, sl]
            kh = kv_ref[0, :, sl]
            doh = do_ref[:, sl]
            sc = lax.dot_general(qh, kh, _DIMS["nt"], preferred_element_type=F32) * scale
            p = _softmax_rows(sc)
            pb = p.astype(BF16)
            dkv_ref[1, :, sl] += lax.dot_general(pb, doh, _DIMS["tn"], preferred_element_type=F32)
            dp = lax.dot_general(doh, kv_ref[1, :, sl], _DIMS["nt"], preferred_element_type=F32)
            ds = (p * (dp - jnp.sum(dp * p, axis=-1, keepdims=True)) * scale).astype(BF16)
            dq_ref[:, sl] = jnp.dot(ds, kh, preferred_element_type=F32).astype(BF16)
            dkv_ref[0, :, sl] += lax.dot_general(ds, qh, _DIMS["tn"], preferred_element_type=F32)

    row = pl.BlockSpec((r, d), lambda i: (i, 0))
    kvs = pl.BlockSpec((2, m, d), lambda i: (0, 0, 0))
    return pl.pallas_call(
        body, name=name, out_shape=[jax.ShapeDtypeStruct((s, d), BF16), jax.ShapeDtypeStruct((2, m, d), F32)],
        grid=(s // r,), in_specs=[row, kvs, row], out_specs=[row, kvs], compiler_params=_params("arbitrary"))(q, kv3, do)


def _swiglu_fwd(gu3, name):
    _, s, f = gu3.shape
    r = _row_tile(s)
    cw = _tile(f, (512, 256, 128))

    def body(gu_ref, o_ref):
        gate = gu_ref[0].astype(F32)
        o_ref[...] = (gate * jax.nn.sigmoid(gate) * gu_ref[1].astype(F32)).astype(BF16)

    return pl.pallas_call(
        body, name=name, out_shape=jax.ShapeDtypeStruct((s, f), BF16), grid=(s // r, f // cw),
        in_specs=[pl.BlockSpec((2, r, cw), lambda i, j: (0, i, j))], out_specs=pl.BlockSpec((r, cw), lambda i, j: (i, j)),
        compiler_params=_params("parallel", "parallel"))(gu3)


def _swiglu_bwd(gu3, dact, name):
    _, s, f = gu3.shape
    r = _row_tile(s)
    cw = _tile(f, (512, 256, 128))

    def body(gu_ref, da_ref, o_ref):
        gate = gu_ref[0].astype(F32)
        up = gu_ref[1].astype(F32)
        da = da_ref[...].astype(F32)
        sg = jax.nn.sigmoid(gate)
        o_ref[0] = (da * up * (sg + gate * sg * (1.0 - sg))).astype(BF16)
        o_ref[1] = (da * gate * sg).astype(BF16)

    return pl.pallas_call(
        body, name=name, out_shape=jax.ShapeDtypeStruct((2, s, f), BF16), grid=(s // r, f // cw),
        in_specs=[pl.BlockSpec((2, r, cw), lambda i, j: (0, i, j)), pl.BlockSpec((r, cw), lambda i, j: (i, j))],
        out_specs=pl.BlockSpec((2, r, cw), lambda i, j: (0, i, j)),
        compiler_params=_params("parallel", "parallel"))(gu3, dact)


def _ids():
    x, y, c = lax.axis_index("x"), lax.axis_index("y"), lax.axis_index("c")
    return x, y, c, 2 * x + y


def _chip_peers(x, y):
    return [(d - 1, 2 * (x ^ (d >> 1)) + (y ^ (d & 1)), x ^ (d >> 1), y ^ (d & 1)) for d in (1, 2, 3)]


def _remote(src, dst, ssem, rsem, dev):
    return pltpu.make_async_remote_copy(src_ref=src, dst_ref=dst, send_sem=ssem, recv_sem=rsem, device_id=dev,
                                        device_id_type=MESH)


def _gview(ref, kind, j, cc):
    _, k, n = ref.shape
    if kind == "row":
        return ref.at[:, pl.ds(j * (k // N_CHIPS) + cc * (k // (2 * N_CHIPS)), k // (2 * N_CHIPS)), :]
    return ref.at[:, pl.ds(cc * (k // 2), k // 2), pl.ds(j * (n // N_CHIPS), n // N_CHIPS)]


def _sview(ref, cc):
    r = ref.shape[1]
    return ref.at[:, pl.ds(cc * (r // 2), r // 2), :]


def _comm_call(body, name, ins, out_shapes, n_sems):
    return pl.pallas_call(
        body, name=name, out_shape=out_shapes, in_specs=[HBM] * len(ins), out_specs=[HBM] * len(out_shapes),
        scratch_shapes=[pltpu.SemaphoreType.DMA((n,)) for n in n_sems],
        compiler_params=pltpu.CompilerParams(has_side_effects=True))(*ins)


def _ag_weight(shard, kind, name):
    l, r, n = shard.shape
    full = (l, r * N_CHIPS, n) if kind == "row" else (l, r, n * N_CHIPS)

    def body(s_ref, o_ref, ssem, rsem, fsem, frsem, lsem):
        x, y, c, me = _ids()
        sib = (x, y, 1 - c)
        peers = _chip_peers(x, y)
        local = [pltpu.make_async_copy(_sview(s_ref, cc), _gview(o_ref, kind, me, cc), lsem.at[cc]) for cc in (0, 1)]
        for cp in local:
            cp.start()
        sends = [_remote(_sview(s_ref, c), _gview(o_ref, kind, me, c), ssem.at[d], rsem.at[d], (px, py, c))
                 for d, _, px, py in peers]
        for cp in sends:
            cp.start()
        passed = []
        for d, pj, px, py in peers:
            piece = _gview(o_ref, kind, pj, c)
            _remote(piece, piece, ssem.at[d], rsem.at[d], (px, py, c)).wait_recv()
            fw = _remote(piece, piece, fsem.at[d], frsem.at[d], sib)
            fw.start()
            passed.append(fw)
        for d, pj, _, _ in peers:
            piece = _gview(o_ref, kind, pj, 1 - c)
            _remote(piece, piece, fsem.at[d], frsem.at[d], sib).wait_recv()
        for cp in sends + passed:
            cp.wait_send()
        for cp in local:
            cp.wait()

    return _comm_call(body, name, [shard], [jax.ShapeDtypeStruct(full, shard.dtype)], (3, 3, 3, 3, 2))[0]


def _rs1(g_full, kind, name):
    l, k, n = g_full.shape
    piece = (l, k // (2 * N_CHIPS), n) if kind == "row" else (l, k // 2, n // N_CHIPS)
    out = jax.ShapeDtypeStruct((N_CHIPS,) + piece, g_full.dtype)

    def body(g_ref, own_ref, got_ref, ssem, rsem, lsem):
        x, y, c, _ = _ids()
        sib = (x, y, 1 - c)
        local = [pltpu.make_async_copy(_gview(g_ref, kind, j, c), own_ref.at[j], lsem.at[j]) for j in range(N_CHIPS)]
        sends = [_remote(_gview(g_ref, kind, j, 1 - c), got_ref.at[j], ssem.at[j], rsem.at[j], sib) for j in range(N_CHIPS)]
        for cp in local + sends:
            cp.start()
        for cp in sends:
            cp.wait()
        for cp in local:
            cp.wait()

    return _comm_call(body, name, [g_full], [out, out], (4, 4, 4))


def _rs2(p, name):
    def body(p_ref, got_ref, ssem, rsem, lsem):
        x, y, c, me = _ids()
        local = pltpu.make_async_copy(p_ref.at[me], got_ref.at[me], lsem.at[0])
        local.start()
        sends = [_remote(p_ref.at[pj], got_ref.at[me], ssem.at[d], rsem.at[d], (px, py, c)) for d, pj, px, py in _chip_peers(x, y)]
        for cp in sends:
            cp.start()
        for d, pj, px, py in _chip_peers(x, y):
            _remote(p_ref.at[pj], got_ref.at[pj], ssem.at[d], rsem.at[d], (px, py, c)).wait_recv()
        for cp in sends:
            cp.wait_send()
        local.wait()

    return _comm_call(body, name, [p], [jax.ShapeDtypeStruct(p.shape, p.dtype)], (3, 3, 1))[0]


def _rs3(f_half, name):
    l, pr, pc = f_half.shape

    def body(f_ref, o_ref, ssem, rsem, lsem):
        x, y, c, _ = _ids()
        local = pltpu.make_async_copy(f_ref, _sview(o_ref, c), lsem.at[0])
        local.start()
        send = _remote(f_ref, _sview(o_ref, c), ssem.at[0], rsem.at[0], (x, y, 1 - c))
        send.start()
        _remote(f_ref, _sview(o_ref, 1 - c), ssem.at[0], rsem.at[0], (x, y, 1 - c)).wait_recv()
        send.wait_send()
        local.wait()

    return _comm_call(body, name, [f_half], [jax.ShapeDtypeStruct((l, 2 * pr, pc), f_half.dtype)], (1, 1, 1))[0]


def _ag_small(sp, name):
    def body(s_ref, o_ref, ssem, rsem, lsem):
        x, y, c, me = _ids()
        local = pltpu.make_async_copy(s_ref, o_ref.at[me], lsem.at[0])
        local.start()
        sends = [_remote(s_ref, o_ref.at[me], ssem.at[d], rsem.at[d], (px, py, c)) for d, _, px, py in _chip_peers(x, y)]
        for cp in sends:
            cp.start()
        for d, pj, px, py in _chip_peers(x, y):
            _remote(s_ref, o_ref.at[pj], ssem.at[d], rsem.at[d], (px, py, c)).wait_recv()
        for cp in sends:
            cp.wait_send()
        local.wait()

    return _comm_call(body, name, [sp], [jax.ShapeDtypeStruct((N_CHIPS,) + sp.shape, sp.dtype)], (3, 3, 1))[0]


def _gather8(g, name):
    def body(g_ref, o_ref, ssem, rsem, lsem):
        x, y, c, _ = _ids()
        me = 4 * x + 2 * y + c
        local = pltpu.make_async_copy(g_ref, o_ref.at[me], lsem.at[0])
        local.start()
        peers = [(d - 1, x ^ (d >> 2), y ^ ((d >> 1) & 1), c ^ (d & 1)) for d in range(1, 8)]
        sends = [_remote(g_ref, o_ref.at[me], ssem.at[d], rsem.at[d], (px, py, pc)) for d, px, py, pc in peers]
        for cp in sends:
            cp.start()
        for d, px, py, pc in peers:
            _remote(g_ref, o_ref.at[4 * px + 2 * py + pc], ssem.at[d], rsem.at[d], (px, py, pc)).wait_recv()
        for cp in sends:
            cp.wait_send()
        local.wait()

    return _comm_call(body, name, [g], [jax.ShapeDtypeStruct((8,) + g.shape, g.dtype)], (7, 7, 1))[0]


def _sum_slots(a, out_dtype, name):
    n = a.shape[0]
    shape = a.shape[1:]
    cols = shape[-1]
    a3 = a.reshape(n, -1, cols)
    rows = a3.shape[1]
    tr = rows
    for cand in (512, 256, 128, 64, 32, 16):
        if rows % cand == 0 and cand * cols * 4 <= (1 << 20):
            tr = cand
            break

    def body(a_ref, o_ref):
        acc = a_ref[0].astype(F32)
        for j in range(1, n):
            acc = acc + a_ref[j].astype(F32)
        o_ref[...] = acc.astype(o_ref.dtype)

    out = pl.pallas_call(
        body, name=name, out_shape=jax.ShapeDtypeStruct((rows, cols), out_dtype), grid=(rows // tr,),
        in_specs=[pl.BlockSpec((n, tr, cols), lambda i: (0, i, 0))], out_specs=pl.BlockSpec((tr, cols), lambda i: (i, 0)),
        compiler_params=_params("parallel"))(a3)
    return out.reshape(shape)


def _reduce_scatter(g_full, kind, tag):
    own, got = _rs1(g_full, kind, "rs1_" + tag)
    shape = own.shape
    cols = shape[-1]
    p = _ew(lambda a, b: (a.astype(F32) + b.astype(F32),), [own.reshape(-1, cols), got.reshape(-1, cols)], [BF16],
            "rs_add1_" + tag)[0].reshape(shape)
    r2 = _rs2(p, "rs2_" + tag)
    f_half = _sum_slots(r2, F32, "rs_add2_" + tag)
    return _rs3(f_half, "rs3_" + tag)


def kernel(x, mem, mix_norm, xa_norm, xa_wq, xa_wkv, xa_wo, ffn_norm, ffn_w_gu, ffn_w_down, a_w_in, a_conv_w, a_w_out, b_w_in, b_v_g, b_v_b, b_w_s, b_s_bias, b_w_out, c_w_in, c_conv_w, c_conv_b, c_ln_g, c_ln_b, c_w_out, loss_target, m_mix_norm, m_xa_norm, m_xa_wq, m_xa_wkv, m_xa_wo, m_ffn_norm, m_ffn_w_gu, m_ffn_w_down, m_a_w_in, m_a_conv_w, m_a_w_out, m_b_w_in, m_b_v_g, m_b_v_b, m_b_w_s, m_b_s_bias, m_b_w_out, m_c_w_in, m_c_conv_w, m_c_conv_b, m_c_ln_g, m_c_ln_b, m_c_w_out, v_mix_norm, v_xa_norm, v_xa_wq, v_xa_wkv, v_xa_wo, v_ffn_norm, v_ffn_w_gu, v_ffn_w_down, v_a_w_in, v_a_conv_w, v_a_w_out, v_b_w_in, v_b_v_g, v_b_v_b, v_b_w_s, v_b_s_bias, v_b_w_out, v_c_w_in, v_c_conv_w, v_c_conv_b, v_c_ln_g, v_c_ln_b, v_c_w_out):
    given = dict(locals())
    w = {n: given[n] for n in WEIGHTS}
    depth = mix_norm.shape[0]
    s, d = x.shape[1], x.shape[2]
    n_mem = mem.shape[1]
    ds = d // N_CHIPS
    xin = x.reshape(s, d)
    memv = mem.reshape(n_mem, d)
    target = loss_target.reshape(s, d)
    me = 2 * lax.axis_index("x") + lax.axis_index("y")

    wg = {n: _ag_weight(w[n].astype(BF16), kind, "ag_" + n) for n, kind in BIG_KINDS.items()}
    small_rows = [w[n].reshape(-1, ds) for n in SMALL_SHARDED]
    counts = [t.shape[0] for t in small_rows]
    pad = (-sum(counts)) % 8
    packed = jnp.concatenate(small_rows + [jnp.zeros((pad, ds), F32)], axis=0)
    gathered = _ag_small(packed, "ag_small")
    gathered = jnp.transpose(gathered, (1, 0, 2)).reshape(-1, d)
    full, off = {}, 0
    for n, cnt in zip(SMALL_SHARDED, counts):
        full[n] = gathered[off:off + cnt].reshape(w[n].shape[:-1] + (d,))
        off += cnt
    t_chunk = b_w_s.shape[-1]
    tril = jnp.tril(jnp.ones((t_chunk, t_chunk), dtype=bool))

    def vec(a):
        return a.reshape(1, -1)

    def b_params(slot):
        ws_m = jnp.where(tril[None], b_w_s[slot], 0.0).astype(BF16)
        sbt = jnp.zeros((t_chunk, 128), F32).at[:, :b_s_bias.shape[1]].set(b_s_bias[slot].T)
        return vec(b_v_g[slot]), vec(b_v_b[slot]), ws_m, sbt

    saved = []
    xc = xin
    for i in range(depth):
        kind, slot = i % 3, i // 3
        t = f"{i}"
        sv = {"x0": xc}
        h = _rms_fwd(xc, vec(full["mix_norm"][i, 0]), "rms_mix_" + t)
        sv["h1"] = h
        if kind == 0:
            pre = _mm("nn", h, wg["a_w_in"], BF16, "a_in_" + t, bl=slot, o_parts=3)
            mid = _a_mid_fwd(pre, full["a_conv_w"][slot], "a_mid_" + t)
            y = _mm("nn", mid, wg["a_w_out"], F32, "a_out_" + t, bl=slot)
        elif kind == 1:
            pre = _mm("nn", h, wg["b_w_in"], BF16, "b_in_" + t, bl=slot, o_parts=2)
            mid = _b_mid_fwd(pre, *b_params(slot), "b_mid_" + t)
            y = _mm("nn", mid, wg["b_w_out"], F32, "b_out_" + t, bl=slot)
        else:
            pre = _mm("nn", h, wg["c_w_in"], BF16, "c_in_" + t, bl=slot, o_parts=2)
            y2 = _c_conv_fwd(pre, full["c_conv_w"][slot], vec(full["c_conv_b"][slot]), "c_conv_" + t)
            sv["cy2"] = y2
            mid = _c_ln_fwd(y2, vec(full["c_ln_g"][slot]), vec(full["c_ln_b"][slot]), "c_ln_" + t)
            y = _mm("nn", mid, wg["c_w_out"], F32, "c_out_" + t, bl=slot)
        sv.update(pre=pre, mid=mid, y1=y)
        xc = _res_rms_fwd(xc, y, vec(full["mix_norm"][i, 1]), "res_mix_" + t)

        sv["x1"] = xc
        h = _rms_fwd(xc, vec(full["xa_norm"][i, 0]), "rms_xa_" + t)
        mem_n = _rms_fwd(memv, vec(full["xa_norm"][i, 2]), "rms_mem_" + t)
        q = _mm("nn", h, wg["xa_wq"], BF16, "xa_q_" + t, bl=i)
        kv3 = _mm("nn", mem_n, wg["xa_wkv"], BF16, "xa_kv_" + t, bl=i, o_parts=2)
        o = _attn_fwd(q, kv3, "attn_" + t)
        y = _mm("nn", o, wg["xa_wo"], F32, "xa_o_" + t, bl=i)
        sv.update(h2=h, mem_n=mem_n, q=q, kv3=kv3, o=o, y2=y)
        xc = _res_rms_fwd(xc, y, vec(full["xa_norm"][i, 1]), "res_xa_" + t)

        sv["x2"] = xc
        h = _rms_fwd(xc, vec(full["ffn_norm"][i, 0]), "rms_ffn_" + t)
        gu3 = _mm("nn", h, wg["ffn_w_gu"], BF16, "ffn_gu_" + t, bl=i, o_parts=2)
        act = _swiglu_fwd(gu3, "swiglu_" + t)
        y = _mm("nn", act, wg["ffn_w_down"], F32, "ffn_down_" + t, bl=i)
        sv.update(h3=h, gu3=gu3, act=act, y3=y)
        xc = _res_rms_fwd(xc, y, vec(full["ffn_norm"][i, 1]), "res_ffn_" + t)
        saved.append(sv)

    loss_blk, dx = _loss(xc, target, "loss")
    loss = lax.psum(loss_blk[0, 0], ("x", "y", "c"))

    gbuf = {n: lax.empty(wg[n].shape, BF16) for n in BIG_KINDS}
    gsmall = {n: [None] * full[n].shape[0] for n in ("mix_norm", "xa_norm", "ffn_norm", "a_conv_w", "c_conv_w", "c_conv_b",
                                                      "c_ln_g", "c_ln_b")}
    grepl = {}

    def wgrad(name, l, a, dy, tag, b_parts=1):
        gbuf[name] = _mm("tn", a, dy, BF16, "wg_" + tag, b_parts=b_parts, into=gbuf[name], into_l=l)

    for i in reversed(range(depth)):
        kind, slot = i % 3, i // 3
        t = f"{i}"
        sv = saved[i]
        dy, dg_post = _rms_bwd(sv["y3"], vec(full["ffn_norm"][i, 1]), dx, None, BF16, "rmsb_ffn_post_" + t)
        wgrad("ffn_w_down", i, sv["act"], dy, "ffn_down_" + t)
        dact = _mm("nt", dy, wg["ffn_w_down"], F32, "dg_ffn_down_" + t, bl=i)
        dgu3 = _swiglu_bwd(sv["gu3"], dact, "swiglu_b_" + t)
        wgrad("ffn_w_gu", i, sv["h3"], dgu3, "ffn_gu_" + t, b_parts=2)
        dh = _mm("nt", dgu3, wg["ffn_w_gu"], F32, "dg_ffn_gu_" + t, bl=i, a_parts=2)
        dx, dg_pre = _rms_bwd(sv["x2"], vec(full["ffn_norm"][i, 0]), dh, dx, F32, "rmsb_ffn_pre_" + t)
        gsmall["ffn_norm"][i] = jnp.concatenate([dg_pre, dg_post], axis=0)
        dy, dg_post = _rms_bwd(sv["y2"], vec(full["xa_norm"][i, 1]), dx, None, BF16, "rmsb_xa_post_" + t)
        wgrad("xa_wo", i, sv["o"], dy, "xa_o_" + t)
        do = _mm("nt", dy, wg["xa_wo"], BF16, "dg_xa_o_" + t, bl=i)
        dq, dkv3 = _attn_bwd(sv["q"], sv["kv3"], do, "attn_b_" + t)
        wgrad("xa_wq", i, sv["h2"], dq, "xa_q_" + t)
        dh = _mm("nt", dq, wg["xa_wq"], F32, "dg_xa_q_" + t, bl=i)
        dkv3 = dkv3.astype(BF16)
        wgrad("xa_wkv", i, sv["mem_n"], dkv3, "xa_kv_" + t, b_parts=2)
        dmem_n = _mm("nt", dkv3, wg["xa_wkv"], F32, "dg_xa_kv_" + t, bl=i, a_parts=2)
        _, dg_mem = _rms_bwd(memv, vec(full["xa_norm"][i, 2]), dmem_n, None, F32, "rmsb_mem_" + t)
        dx, dg_pre = _rms_bwd(sv["x1"], vec(full["xa_norm"][i, 0]), dh, dx, F32, "rmsb_xa_pre_" + t)
        gsmall["xa_norm"][i] = jnp.concatenate([dg_pre, dg_post, dg_mem], axis=0)
        dy, dg_post = _rms_bwd(sv["y1"], vec(full["mix_norm"][i, 1]), dx, None, BF16, "rmsb_mix_post_" + t)
        if kind == 0:
            wgrad("a_w_out", slot, sv["mid"], dy, "a_out_" + t)
            dmid = _mm("nt", dy, wg["a_w_out"], F32, "dg_a_out_" + t, bl=slot)
            dpre, dcw = _a_mid_bwd(sv["pre"], dmid, full["a_conv_w"][slot], "a_mid_b_" + t)
            gsmall["a_conv_w"][slot] = dcw
            wgrad("a_w_in", slot, sv["h1"], dpre, "a_in_" + t, b_parts=3)
            dh = _mm("nt", dpre, wg["a_w_in"], F32, "dg_a_in_" + t, bl=slot, a_parts=3)
        elif kind == 1:
            wgrad("b_w_out", slot, sv["mid"], dy, "b_out_" + t)
            dmid = _mm("nt", dy, wg["b_w_out"], F32, "dg_b_out_" + t, bl=slot)
            dpre, dvg, dvb, dws, dsbt = _b_mid_bwd(sv["pre"], dmid, *b_params(slot), "b_mid_b_" + t)
            grepl[slot] = (dvg, dvb, dws, dsbt[:, :b_s_bias.shape[1]].T)
            wgrad("b_w_in", slot, sv["h1"], dpre, "b_in_" + t, b_parts=2)
            dh = _mm("nt", dpre, wg["b_w_in"], F32, "dg_b_in_" + t, bl=slot, a_parts=2)
        else:
            wgrad("c_w_out", slot, sv["mid"], dy, "c_out_" + t)
            dmid = _mm("nt", dy, wg["c_w_out"], F32, "dg_c_out_" + t, bl=slot)
            dy2, dlg, dlb = _c_ln_bwd(sv["cy2"], dmid, vec(full["c_ln_g"][slot]), vec(full["c_ln_b"][slot]), "c_ln_b_" + t)
            dpre, dcw, dcb = _c_conv_bwd(sv["pre"], dy2, full["c_conv_w"][slot], "c_conv_b_" + t)
            gsmall["c_conv_w"][slot], gsmall["c_conv_b"][slot] = dcw, dcb
            gsmall["c_ln_g"][slot], gsmall["c_ln_b"][slot] = dlg, dlb
            wgrad("c_w_in", slot, sv["h1"], dpre, "c_in_" + t, b_parts=2)
            dh = _mm("nt", dpre, wg["c_w_in"], F32, "dg_c_in_" + t, bl=slot, a_parts=2)
        dx, dg_pre = _rms_bwd(sv["x0"], vec(full["mix_norm"][i, 0]), dh, dx, F32, "rmsb_mix_pre_" + t)
        gsmall["mix_norm"][i] = jnp.concatenate([dg_pre, dg_post], axis=0)
    grad_x = dx.reshape(x.shape)

    grads = {n: _reduce_scatter(gbuf[n], kind, n).reshape(w[n].shape) for n, kind in BIG_KINDS.items()}
    small_g = [jnp.concatenate(gsmall[n], axis=0).reshape(-1, d) for n in SMALL_SHARDED]
    n_b = b_v_g.shape[0]
    repl_g = [jnp.concatenate([grepl[sl][k] for sl in range(n_b)], axis=0) for k in range(4)]
    repl_rows = []
    for g_arr in repl_g:
        flat = g_arr.reshape(-1)
        flat = jnp.concatenate([flat, jnp.zeros(((-flat.shape[0]) % d,), F32)])
        repl_rows.append(flat.reshape(-1, d))
    rows_all = small_g + repl_rows
    n_rows = sum(t.shape[0] for t in rows_all)
    rows_all.append(jnp.zeros(((-n_rows) % 8, d), F32))
    total = _sum_slots(_gather8(jnp.concatenate(rows_all, axis=0), "gather_small_grads"), F32, "sum_small_grads")
    off = 0
    for n, cnt in zip(SMALL_SHARDED, counts):
        blk = lax.dynamic_slice_in_dim(total[off:off + cnt], me * ds, ds, axis=1)
        grads[n] = blk.reshape(w[n].shape)
        off += cnt
    for n, g_arr in zip(SMALL_REPL, repl_g):
        cnt = -(-g_arr.size // d)
        grads[n] = total[off:off + cnt].reshape(-1)[:g_arr.size].reshape(w[n].shape)
        off += cnt

    delta, new_m, new_v = {}, {}, {}
    for n in WEIGHTS:
        delta[n], new_m[n], new_v[n] = _adamw(w[n], grads[n], given["m_" + n], given["v_" + n], "adamw_" + n)
    return (loss, grad_x, *[grads[n] for n in WEIGHTS], *[delta[n] for n in WEIGHTS], *[new_m[n] for n in WEIGHTS],
            *[new_v[n] for n in WEIGHTS])
```

```python
import functools

import jax
import jax.numpy as jnp
from jax import lax
from jax.experimental import pallas as pl
from jax.experimental.pallas import tpu as pltpu

F32 = jnp.float32
BF16 = jnp.bfloat16
EPS = 1e-6
XA_HEADS = 4
CHUNK = 128
GMLP_GROUPS = 8
ADAM_LR, ADAM_B1, ADAM_B2, ADAM_EPS, ADAM_WD, ADAM_STEP = 0.001, 0.9, 0.999, 1e-08, 0.01, 10
VMEM_LIMIT_V7X = 48 * 1024 * 1024
HBM = pl.BlockSpec(memory_space=pltpu.HBM)
MESH = pl.DeviceIdType.MESH
N_CHIPS = 4
BIG_KINDS = {"xa_wq": "row", "xa_wkv": "col", "xa_wo": "row", "ffn_w_gu": "col", "ffn_w_down": "row",
             "a_w_in": "col", "a_w_out": "row", "b_w_in": "col", "b_w_out": "row", "c_w_in": "col", "c_w_out": "row"}
SMALL_SHARDED = ["mix_norm", "xa_norm", "ffn_norm", "a_conv_w", "c_conv_w", "c_conv_b", "c_ln_g", "c_ln_b"]
SMALL_REPL = ["b_v_g", "b_v_b", "b_w_s", "b_s_bias"]
WEIGHTS = ["mix_norm", "xa_norm", "xa_wq", "xa_wkv", "xa_wo", "ffn_norm", "ffn_w_gu", "ffn_w_down", "a_w_in", "a_conv_w",
           "a_w_out", "b_w_in", "b_v_g", "b_v_b", "b_w_s", "b_s_bias", "b_w_out", "c_w_in", "c_conv_w", "c_conv_b",
           "c_ln_g", "c_ln_b", "c_w_out"]


def _params(*sem):
    return pltpu.CompilerParams(dimension_semantics=sem, vmem_limit_bytes=VMEM_LIMIT_V7X)


def _tile(n, cands=(1024, 512, 256, 128)):
    for c in cands:
        if n % c == 0:
            return c
    return n


_DIMS = {"nn": (((1,), (0,)), ((), ())), "nt": (((1,), (1,)), ((), ())), "tn": (((0,), (0,)), ((), ()))}


def _mm(mode, a, b, out_dtype, name, *, bl=None, a_parts=1, b_parts=1, o_parts=1, into=None, into_l=0):
    bshape = b.shape[1:] if bl is not None else b.shape
    if mode == "nn":
        mo, c = a.shape
        no = bshape[1]
    elif mode == "nt":
        mo, c = (a.shape[1], a.shape[0] * a.shape[2]) if a_parts > 1 else a.shape
        no = bshape[0]
    else:
        c, mo = a.shape
        no = b.shape[0] * b.shape[2] if b_parts > 1 else bshape[1]
    tmo = _tile(mo)
    tno = _tile(no // max(o_parts, b_parts))
    tc = _tile(c // a_parts)
    nk = c // tc
    nkp = nk // a_parts
    njp = (no // tno) // max(o_parts, b_parts)
    lead = (None,) if bl is not None else ()
    lidx = (bl,) if bl is not None else ()

    if mode == "nn":
        a_spec = pl.BlockSpec((tmo, tc), lambda i, j, k: (i, k))
        b_spec = pl.BlockSpec(lead + (tc, tno), lambda i, j, k: lidx + (k, j))
    elif mode == "nt":
        if a_parts > 1:
            a_spec = pl.BlockSpec((None, tmo, tc), lambda i, j, k: (k // nkp, i, k % nkp))
        else:
            a_spec = pl.BlockSpec((tmo, tc), lambda i, j, k: (i, k))
        b_spec = pl.BlockSpec(lead + (tno, tc), lambda i, j, k: lidx + (j, k))
    else:
        a_spec = pl.BlockSpec((tc, tmo), lambda i, j, k: (k, i))
        if b_parts > 1:
            b_spec = pl.BlockSpec((None, tc, tno), lambda i, j, k: (j // njp, k, j % njp))
        else:
            b_spec = pl.BlockSpec((tc, tno), lambda i, j, k: (k, j))

    in_specs = [a_spec, b_spec]
    args = [a, b]
    aliases = {}
    if into is not None:
        out_shape = jax.ShapeDtypeStruct(into.shape, into.dtype)
        out_spec = pl.BlockSpec((None, tmo, tno), lambda i, j, k: (into_l, i, j))
        in_specs.append(HBM)
        args.append(into)
        aliases = {2: 0}
    elif o_parts > 1:
        out_shape = jax.ShapeDtypeStruct((o_parts, mo, no // o_parts), out_dtype)
        out_spec = pl.BlockSpec((None, tmo, tno), lambda i, j, k: (j // njp, i, j % njp))
    else:
        out_shape = jax.ShapeDtypeStruct((mo, no), out_dtype)
        out_spec = pl.BlockSpec((tmo, tno), lambda i, j, k: (i, j))
    dims = _DIMS[mode]

    def body(a_ref, b_ref, *rest):
        o_ref, acc = rest[-2], rest[-1]
        k = pl.program_id(2)

        @pl.when(k == 0)
        def _():
            acc[...] = jnp.zeros_like(acc)

        acc[...] += lax.dot_general(a_ref[...], b_ref[...], dims, preferred_element_type=F32)

        @pl.when(k == nk - 1)
        def _():
            o_ref[...] = acc[...].astype(o_ref.dtype)

    return pl.pallas_call(
        body, name=name, out_shape=out_shape, grid=(mo // tmo, no // tno, nk), in_specs=in_specs, out_specs=out_spec,
        scratch_shapes=[pltpu.VMEM((tmo, tno), F32)], input_output_aliases=aliases,
        compiler_params=_params("parallel", "parallel", "arbitrary"))(*args)


def _ew(fn, ins, out_dtypes, name):
    rows, cols = ins[0].shape
    tr = rows
    for cand in (512, 256, 128, 64, 32, 16):
        if rows % cand == 0 and cand * cols * 4 <= (1 << 20):
            tr = cand
            break
    spec = pl.BlockSpec((tr, cols), lambda i: (i, 0))
    n_in = len(ins)

    def body(*refs):
        outs = fn(*[r[...] for r in refs[:n_in]])
        for o_ref, o in zip(refs[n_in:], outs):
            o_ref[...] = o.astype(o_ref.dtype)

    return pl.pallas_call(
        body, name=name, out_shape=[jax.ShapeDtypeStruct((rows, cols), d) for d in out_dtypes], grid=(rows // tr,),
        in_specs=[spec] * n_in, out_specs=[spec] * len(out_dtypes), compiler_params=_params("parallel"))(*ins)


def _adamw_fn(w, g, m, v):
    m = ADAM_B1 * m + (1.0 - ADAM_B1) * g
    v = ADAM_B2 * v + (1.0 - ADAM_B2) * (g * g)
    m_hat = m / (1.0 - ADAM_B1 ** ADAM_STEP)
    v_hat = v / (1.0 - ADAM_B2 ** ADAM_STEP)
    delta = -ADAM_LR * (m_hat / (jnp.sqrt(v_hat) + ADAM_EPS) + ADAM_WD * w)
    return delta, m, v


def _adamw(w, g, m, v, name):
    shape = w.shape
    cols = shape[-1]
    flat = [t.reshape(-1, cols) for t in (w, g, m, v)]
    outs = _ew(_adamw_fn, flat, [F32] * 3, name)
    return [o.reshape(shape) for o in outs]


def _row_tile(s):
    return _tile(s, (256, 128, 64, 32, 16, 8))


def _rms_fwd(x, g, name):
    s, d = x.shape
    r = _row_tile(s)

    def body(x_ref, g_ref, o_ref):
        xv = x_ref[...]
        o_ref[...] = (xv * lax.rsqrt(jnp.mean(xv * xv, axis=-1, keepdims=True) + EPS) * g_ref[...]).astype(BF16)

    return pl.pallas_call(
        body, name=name, out_shape=jax.ShapeDtypeStruct((s, d), BF16), grid=(s // r,),
        in_specs=[pl.BlockSpec((r, d), lambda i: (i, 0)), pl.BlockSpec((1, d), lambda i: (0, 0))],
        out_specs=pl.BlockSpec((r, d), lambda i: (i, 0)), compiler_params=_params("parallel"))(x, g)


def _res_rms_fwd(x, y, g, name):
    s, d = x.shape
    r = _row_tile(s)

    def body(x_ref, y_ref, g_ref, o_ref):
        yv = y_ref[...]
        o_ref[...] = x_ref[...] + yv * lax.rsqrt(jnp.mean(yv * yv, axis=-1, keepdims=True) + EPS) * g_ref[...]

    row = pl.BlockSpec((r, d), lambda i: (i, 0))
    return pl.pallas_call(
        body, name=name, out_shape=jax.ShapeDtypeStruct((s, d), F32), grid=(s // r,),
        in_specs=[row, row, pl.BlockSpec((1, d), lambda i: (0, 0))], out_specs=row,
        compiler_params=_params("parallel"))(x, y, g)


def _rms_bwd(x, g, dy, resid, out_dtype, name):
    s, d = x.shape
    r = _row_tile(s)
    has_res = resid is not None

    def body(*refs):
        x_ref, g_ref, dy_ref = refs[:3]
        dx_ref, dg_ref = refs[-2:]
        i = pl.program_id(0)
        xv = x_ref[...]
        dyv = dy_ref[...].astype(F32)
        rstd = lax.rsqrt(jnp.mean(xv * xv, axis=-1, keepdims=True) + EPS)
        n = xv * rstd
        dn = dyv * g_ref[...]
        dx = rstd * (dn - n * jnp.mean(dn * n, axis=-1, keepdims=True))
        if has_res:
            dx = dx + refs[3][...]
        dx_ref[...] = dx.astype(dx_ref.dtype)
        part = jnp.sum(dyv * n, axis=0, keepdims=True)

        @pl.when(i == 0)
        def _():
            dg_ref[...] = part

        @pl.when(i > 0)
        def _():
            dg_ref[...] += part

    row = pl.BlockSpec((r, d), lambda i: (i, 0))
    vec = pl.BlockSpec((1, d), lambda i: (0, 0))
    ins = [x, g, dy] + ([resid] if has_res else [])
    return pl.pallas_call(
        body, name=name, out_shape=[jax.ShapeDtypeStruct((s, d), out_dtype), jax.ShapeDtypeStruct((1, d), F32)],
        grid=(s // r,), in_specs=[row, vec, row] + ([row] if has_res else []), out_specs=[row, vec],
        compiler_params=_params("arbitrary"))(*ins)


def _loss(y, t, name):
    s, d = y.shape
    r = _row_tile(s)

    def body(y_ref, t_ref, l_ref, dy_ref):
        i = pl.program_id(0)
        e = y_ref[...] - t_ref[...]
        dy_ref[...] = e * (1.0 / d)
        part = jnp.full((8, 128), 0.5 * jnp.sum(jnp.mean(e * e, axis=-1, keepdims=True)), F32)

        @pl.when(i == 0)
        def _():
            l_ref[...] = part

        @pl.when(i > 0)
        def _():
            l_ref[...] += part

    row = pl.BlockSpec((r, d), lambda i: (i, 0))
    return pl.pallas_call(
        body, name=name, out_shape=[jax.ShapeDtypeStruct((8, 128), F32), jax.ShapeDtypeStruct((s, d), F32)],
        grid=(s // r,), in_specs=[row, row], out_specs=[pl.BlockSpec((8, 128), lambda i: (0, 0)), row],
        compiler_params=_params("arbitrary"))(y, t)


def _rows(xv, a, m, cache):
    r = a % 8
    q = a - r
    if r == 0:
        return xv[q:q + m]
    if r not in cache:
        cache[r] = pltpu.roll(xv, xv.shape[0] - r, 0)
    return cache[r][q:q + m]


def _conv_taps(xv, w, k_w, halo, m, flip):
    cache = {}
    acc = None
    for k in range(k_w):
        a = (k_w - 1 - k) if flip else (halo + k - (k_w - 1))
        term = w[k:k + 1, :] * _rows(xv, a, m, cache)
        acc = term if acc is None else acc + term
    return acc


def _conv_wgrad(dw_ref, dyv, xv, k_w, halo, m):
    cache = {}
    for k in range(k_w):
        xs = _rows(xv, halo + k - (k_w - 1), m, cache)
        dw_ref[pl.ds(k, 1), :] += jnp.sum(dyv * xs, axis=0, keepdims=True)


def _conv_tiles(s, dp, halo):
    r = _tile(s, (256, 128))
    cw = _tile(dp, (256, 128))
    return r, cw, r // halo


A_HALO = 8


def _a_mid_fwd(bcz3, w, name):
    _, s, d = bcz3.shape
    r, cw, rh = _conv_tiles(s, d, A_HALO)
    k_w = w.shape[0]

    def body(m_ref, h_ref, w_ref, o_ref):
        i = pl.program_id(0)
        cz = m_ref[1].astype(F32) * m_ref[2].astype(F32)
        hcz = h_ref[1].astype(F32) * h_ref[2].astype(F32)
        hcz = jnp.where(i == 0, 0.0, hcz)
        xv = jnp.concatenate([hcz, cz], axis=0)
        y = _conv_taps(xv, w_ref[...], k_w, A_HALO, r, False)
        o_ref[...] = (m_ref[0].astype(F32) * y).astype(BF16)

    return pl.pallas_call(
        body, name=name, out_shape=jax.ShapeDtypeStruct((s, d), BF16), grid=(s // r, d // cw),
        in_specs=[pl.BlockSpec((3, r, cw), lambda i, j: (0, i, j)),
                  pl.BlockSpec((3, A_HALO, cw), lambda i, j: (0, jnp.maximum(i * rh - 1, 0), j)),
                  pl.BlockSpec((k_w, cw), lambda i, j: (0, j))],
        out_specs=pl.BlockSpec((r, cw), lambda i, j: (i, j)), compiler_params=_params("parallel", "parallel"))(bcz3, bcz3, w)


def _a_mid_bwd(bcz3, dgated, w, name):
    _, s, d = bcz3.shape
    r, cw, rh = _conv_tiles(s, d, A_HALO)
    k_w = w.shape[0]
    ni = s // r
    last_h = s // A_HALO - 1

    def body(m_ref, hp_ref, hn_ref, dg_ref, dgn_ref, w_ref, o_ref, dw_ref):
        i = pl.program_id(1)
        wv = w_ref[...]
        b = m_ref[0].astype(F32)
        c = m_ref[1].astype(F32)
        z = m_ref[2].astype(F32)
        hcz = jnp.where(i == 0, 0.0, hp_ref[1].astype(F32) * hp_ref[2].astype(F32))
        xv = jnp.concatenate([hcz, c * z], axis=0)
        y = _conv_taps(xv, wv, k_w, A_HALO, r, False)
        dg = dg_ref[...].astype(F32)
        dy = dg * b
        dyn = jnp.where(i == ni - 1, 0.0, dgn_ref[...].astype(F32) * hn_ref[0].astype(F32))
        dcz = _conv_taps(jnp.concatenate([dy, dyn], axis=0), wv, k_w, A_HALO, r, True)
        o_ref[0] = (dg * y).astype(BF16)
        o_ref[1] = (dcz * z).astype(BF16)
        o_ref[2] = (dcz * c).astype(BF16)

        @pl.when(i == 0)
        def _():
            dw_ref[...] = jnp.zeros_like(dw_ref)

        _conv_wgrad(dw_ref, dy, xv, k_w, A_HALO, r)

    return pl.pallas_call(
        body, name=name, out_shape=[jax.ShapeDtypeStruct((3, s, d), BF16), jax.ShapeDtypeStruct((k_w, d), F32)],
        grid=(d // cw, ni),
        in_specs=[pl.BlockSpec((3, r, cw), lambda j, i: (0, i, j)),
                  pl.BlockSpec((3, A_HALO, cw), lambda j, i: (0, jnp.maximum(i * rh - 1, 0), j)),
                  pl.BlockSpec((3, A_HALO, cw), lambda j, i: (0, jnp.minimum((i + 1) * rh, last_h), j)),
                  pl.BlockSpec((r, cw), lambda j, i: (i, j)),
                  pl.BlockSpec((A_HALO, cw), lambda j, i: (jnp.minimum((i + 1) * rh, last_h), j)),
                  pl.BlockSpec((k_w, cw), lambda j, i: (0, j))],
        out_specs=[pl.BlockSpec((3, r, cw), lambda j, i: (0, i, j)), pl.BlockSpec((k_w, cw), lambda j, i: (0, j))],
        compiler_params=_params("parallel", "arbitrary"))(bcz3, bcz3, bcz3, dgated, dgated, w)


C_HALO = 32


def _c_conv_fwd(ag3, w, bias, name):
    _, s, d = ag3.shape
    r, cw, rh = _conv_tiles(s, d, C_HALO)
    k_w = w.shape[0]

    def body(m_ref, h_ref, w_ref, b_ref, o_ref):
        i = pl.program_id(0)
        y1 = m_ref[0].astype(F32) * jax.nn.sigmoid(m_ref[1].astype(F32))
        h1 = jnp.where(i == 0, 0.0, h_ref[0].astype(F32) * jax.nn.sigmoid(h_ref[1].astype(F32)))
        xv = jnp.concatenate([h1, y1], axis=0)
        o_ref[...] = _conv_taps(xv, w_ref[...], k_w, C_HALO, r, False) + b_ref[...]

    return pl.pallas_call(
        body, name=name, out_shape=jax.ShapeDtypeStruct((s, d), F32), grid=(s // r, d // cw),
        in_specs=[pl.BlockSpec((2, r, cw), lambda i, j: (0, i, j)),
                  pl.BlockSpec((2, C_HALO, cw), lambda i, j: (0, jnp.maximum(i * rh - 1, 0), j)),
                  pl.BlockSpec((k_w, cw), lambda i, j: (0, j)), pl.BlockSpec((1, cw), lambda i, j: (0, j))],
        out_specs=pl.BlockSpec((r, cw), lambda i, j: (i, j)),
        compiler_params=_params("parallel", "parallel"))(ag3, ag3, w, bias)


def _c_conv_bwd(ag3, dy2, w, name):
    _, s, d = ag3.shape
    r, cw, rh = _conv_tiles(s, d, C_HALO)
    k_w = w.shape[0]
    ni = s // r
    last_h = s // C_HALO - 1

    def body(m_ref, hp_ref, dy_ref, dyn_ref, w_ref, o_ref, dw_ref, db_ref):
        i = pl.program_id(1)
        wv = w_ref[...]
        a = m_ref[0].astype(F32)
        sg = jax.nn.sigmoid(m_ref[1].astype(F32))
        h1 = jnp.where(i == 0, 0.0, hp_ref[0].astype(F32) * jax.nn.sigmoid(hp_ref[1].astype(F32)))
        xv = jnp.concatenate([h1, a * sg], axis=0)
        dy = dy_ref[...]
        dyn = jnp.where(i == ni - 1, 0.0, dyn_ref[...])
        dy1 = _conv_taps(jnp.concatenate([dy, dyn], axis=0), wv, k_w, C_HALO, r, True)
        o_ref[0] = (dy1 * sg).astype(BF16)
        o_ref[1] = (dy1 * a * sg * (1.0 - sg)).astype(BF16)

        @pl.when(i == 0)
        def _():
            dw_ref[...] = jnp.zeros_like(dw_ref)
            db_ref[...] = jnp.zeros_like(db_ref)

        db_ref[...] += jnp.sum(dy, axis=0, keepdims=True)
        _conv_wgrad(dw_ref, dy, xv, k_w, C_HALO, r)

    return pl.pallas_call(
        body, name=name,
        out_shape=[jax.ShapeDtypeStruct((2, s, d), BF16), jax.ShapeDtypeStruct((k_w, d), F32),
                   jax.ShapeDtypeStruct((1, d), F32)],
        grid=(d // cw, ni),
        in_specs=[pl.BlockSpec((2, r, cw), lambda j, i: (0, i, j)),
                  pl.BlockSpec((2, C_HALO, cw), lambda j, i: (0, jnp.maximum(i * rh - 1, 0), j)),
                  pl.BlockSpec((r, cw), lambda j, i: (i, j)),
                  pl.BlockSpec((C_HALO, cw), lambda j, i: (jnp.minimum((i + 1) * rh, last_h), j)),
                  pl.BlockSpec((k_w, cw), lambda j, i: (0, j))],
        out_specs=[pl.BlockSpec((2, r, cw), lambda j, i: (0, i, j)), pl.BlockSpec((k_w, cw), lambda j, i: (0, j)),
                   pl.BlockSpec((1, cw), lambda j, i: (0, j))],
        compiler_params=_params("parallel", "arbitrary"))(ag3, ag3, dy2, dy2, w)


def _ln_stats(v):
    mu = jnp.mean(v, axis=-1, keepdims=True)
    vc = v - mu
    rstd = lax.rsqrt(jnp.mean(vc * vc, axis=-1, keepdims=True) + EPS)
    return vc * rstd, rstd


def _ln_bwd(dn, g, xh, rstd):
    dxh = dn * g
    return rstd * (dxh - jnp.mean(dxh, axis=-1, keepdims=True) - xh * jnp.mean(dxh * xh, axis=-1, keepdims=True))


def _c_ln_fwd(y2, g, b, name):
    s, d = y2.shape
    r = _row_tile(s)

    def body(y_ref, g_ref, b_ref, o_ref):
        xh, _ = _ln_stats(y_ref[...])
        y3 = xh * g_ref[...] + b_ref[...]
        o_ref[...] = (y3 * jax.nn.sigmoid(y3)).astype(BF16)

    row = pl.BlockSpec((r, d), lambda i: (i, 0))
    vec = pl.BlockSpec((1, d), lambda i: (0, 0))
    return pl.pallas_call(
        body, name=name, out_shape=jax.ShapeDtypeStruct((s, d), BF16), grid=(s // r,), in_specs=[row, vec, vec],
        out_specs=row, compiler_params=_params("parallel"))(y2, g, b)


def _c_ln_bwd(y2, dout, g, b, name):
    s, d = y2.shape
    r = _row_tile(s)

    def body(y_ref, do_ref, g_ref, b_ref, dy_ref, dg_ref, db_ref):
        i = pl.program_id(0)
        xh, rstd = _ln_stats(y_ref[...])
        gv = g_ref[...]
        y3 = xh * gv + b_ref[...]
        sg = jax.nn.sigmoid(y3)
        dy3 = do_ref[...].astype(F32) * (sg + y3 * sg * (1.0 - sg))
        dy_ref[...] = _ln_bwd(dy3, gv, xh, rstd)

        @pl.when(i == 0)
        def _():
            dg_ref[...] = jnp.zeros_like(dg_ref)
            db_ref[...] = jnp.zeros_like(db_ref)

        dg_ref[...] += jnp.sum(dy3 * xh, axis=0, keepdims=True)
        db_ref[...] += jnp.sum(dy3, axis=0, keepdims=True)

    row = pl.BlockSpec((r, d), lambda i: (i, 0))
    vec = pl.BlockSpec((1, d), lambda i: (0, 0))
    return pl.pallas_call(
        body, name=name,
        out_shape=[jax.ShapeDtypeStruct((s, d), F32), jax.ShapeDtypeStruct((1, d), F32), jax.ShapeDtypeStruct((1, d), F32)],
        grid=(s // r,), in_specs=[row, row, vec, vec], out_specs=[row, vec, vec],
        compiler_params=_params("arbitrary"))(y2, dout, g, b)


_GELU_C = 0.7978845608028654
_GELU_A = 0.044715


def _gelu(x):
    return 0.5 * x * (1.0 + jnp.tanh(_GELU_C * (x + _GELU_A * x * x * x)))


def _gelu_grad(x):
    t = jnp.tanh(_GELU_C * (x + _GELU_A * x * x * x))
    return 0.5 * (1.0 + t) + 0.5 * x * (1.0 - t * t) * _GELU_C * (1.0 + 3.0 * _GELU_A * x * x)


def _b_mid_fwd(uv3, vg, vb, ws_m, sbt, name):
    _, s, h = uv3.shape
    g_n, t, _ = ws_m.shape
    gd = h // g_n

    def body(uv_ref, vg_ref, vb_ref, ws_ref, sb_ref, o_ref):
        u = _gelu(uv_ref[0].astype(F32))
        xh, _ = _ln_stats(_gelu(uv_ref[1].astype(F32)))
        vn = (xh * vg_ref[...] + vb_ref[...]).astype(BF16)
        for g in range(g_n):
            sl = slice(g * gd, (g + 1) * gd)
            sv = jnp.dot(ws_ref[g], vn[:, sl], preferred_element_type=F32) + sb_ref[:, g:g + 1]
            o_ref[:, sl] = (u[:, sl] * sv).astype(BF16)

    vec = pl.BlockSpec((1, h), lambda i: (0, 0))
    return pl.pallas_call(
        body, name=name, out_shape=jax.ShapeDtypeStruct((s, h), BF16), grid=(s // t,),
        in_specs=[pl.BlockSpec((2, t, h), lambda i: (0, i, 0)), vec, vec,
                  pl.BlockSpec((g_n, t, t), lambda i: (0, 0, 0)), pl.BlockSpec((t, 128), lambda i: (0, 0))],
        out_specs=pl.BlockSpec((t, h), lambda i: (i, 0)), compiler_params=_params("parallel"))(uv3, vg, vb, ws_m, sbt)


def _b_mid_bwd(uv3, dgated, vg, vb, ws_m, sbt, name):
    _, s, h = uv3.shape
    g_n, t, _ = ws_m.shape
    gd = h // g_n

    def body(uv_ref, dg_ref, vg_ref, vb_ref, ws_ref, sb_ref, o_ref, dvg_ref, dvb_ref, dws_ref, dsb_ref, dvn_ref):
        i = pl.program_id(0)

        @pl.when(i == 0)
        def _():
            dvg_ref[...] = jnp.zeros_like(dvg_ref)
            dvb_ref[...] = jnp.zeros_like(dvb_ref)
            dws_ref[...] = jnp.zeros_like(dws_ref)
            dsb_ref[...] = jnp.zeros_like(dsb_ref)

        upre = uv_ref[0].astype(F32)
        vpre = uv_ref[1].astype(F32)
        u = _gelu(upre)
        xh, rstd = _ln_stats(_gelu(vpre))
        gv = vg_ref[...]
        vn = (xh * gv + vb_ref[...]).astype(BF16)
        causal = lax.broadcasted_iota(jnp.int32, (t, t), 0) >= lax.broadcasted_iota(jnp.int32, (t, t), 1)
        lane = lax.broadcasted_iota(jnp.int32, (t, 128), 1)
        for g in range(g_n):
            sl = slice(g * gd, (g + 1) * gd)
            wsg = ws_ref[g]
            sv = jnp.dot(wsg, vn[:, sl], preferred_element_type=F32) + sb_ref[:, g:g + 1]
            dg = dg_ref[:, sl].astype(F32)
            o_ref[0, :, sl] = (dg * sv * _gelu_grad(upre[:, sl])).astype(BF16)
            dsv = dg * u[:, sl]
            dsvb = dsv.astype(BF16)
            dsb_ref[...] += jnp.where(lane == g, jnp.sum(dsv, axis=1, keepdims=True), 0.0)
            dws = lax.dot_general(dsvb, vn[:, sl], _DIMS["nt"], preferred_element_type=F32)
            dws_ref[g] += jnp.where(causal, dws, 0.0)
            dvn_ref[:, sl] = lax.dot_general(wsg, dsvb, _DIMS["tn"], preferred_element_type=F32)
        dvn = dvn_ref[...]
        dvg_ref[...] += jnp.sum(dvn * xh, axis=0, keepdims=True)
        dvb_ref[...] += jnp.sum(dvn, axis=0, keepdims=True)
        o_ref[1] = (_ln_bwd(dvn, gv, xh, rstd) * _gelu_grad(vpre)).astype(BF16)

    vec = pl.BlockSpec((1, h), lambda i: (0, 0))
    return pl.pallas_call(
        body, name=name,
        out_shape=[jax.ShapeDtypeStruct((2, s, h), BF16), jax.ShapeDtypeStruct((1, h), F32), jax.ShapeDtypeStruct((1, h), F32),
                   jax.ShapeDtypeStruct((g_n, t, t), F32), jax.ShapeDtypeStruct((t, 128), F32)],
        grid=(s // t,),
        in_specs=[pl.BlockSpec((2, t, h), lambda i: (0, i, 0)), pl.BlockSpec((t, h), lambda i: (i, 0)), vec, vec,
                  pl.BlockSpec((g_n, t, t), lambda i: (0, 0, 0)), pl.BlockSpec((t, 128), lambda i: (0, 0))],
        out_specs=[pl.BlockSpec((2, t, h), lambda i: (0, i, 0)), vec, vec,
                   pl.BlockSpec((g_n, t, t), lambda i: (0, 0, 0)), pl.BlockSpec((t, 128), lambda i: (0, 0))],
        scratch_shapes=[pltpu.VMEM((t, h), F32)],
        compiler_params=_params("arbitrary"))(uv3, dgated, vg, vb, ws_m, sbt)


def _softmax_rows(sc):
    e = jnp.exp(sc - jnp.max(sc, axis=-1, keepdims=True))
    return e / jnp.sum(e, axis=-1, keepdims=True)


def _attn_fwd(q, kv3, name):
    s, d = q.shape
    m = kv3.shape[1]
    dh = d // XA_HEADS
    scale = dh ** -0.5
    r = _row_tile(s)

    def body(q_ref, kv_ref, o_ref):
        for hd in range(XA_HEADS):
            sl = slice(hd * dh, (hd + 1) * dh)
            sc = lax.dot_general(q_ref[:, sl], kv_ref[0, :, sl], _DIMS["nt"], preferred_element_type=F32) * scale
            p = _softmax_rows(sc).astype(BF16)
            o_ref[:, sl] = jnp.dot(p, kv_ref[1, :, sl], preferred_element_type=F32).astype(BF16)

    return pl.pallas_call(
        body, name=name, out_shape=jax.ShapeDtypeStruct((s, d), BF16), grid=(s // r,),
        in_specs=[pl.BlockSpec((r, d), lambda i: (i, 0)), pl.BlockSpec((2, m, d), lambda i: (0, 0, 0))],
        out_specs=pl.BlockSpec((r, d), lambda i: (i, 0)), compiler_params=_params("parallel"))(q, kv3)


def _attn_bwd(q, kv3, do, name):
    s, d = q.shape
    m = kv3.shape[1]
    dh = d // XA_HEADS
    scale = dh ** -0.5
    r = _row_tile(s)

    def body(q_ref, kv_ref, do_ref, dq_ref, dkv_ref):
        i = pl.program_id(0)

        @pl.when(i == 0)
        def _():
            dkv_ref[...] = jnp.zeros_like(dkv_ref)

        for hd in range(XA_HEADS):
            sl = slice(hd * dh, (hd + 1) * dh)
            qh = q_ref[:, sl]
            kh = kv_ref[0, :, sl]
            doh = do_ref[:, sl]
            sc = lax.dot_general(qh, kh, _DIMS["nt"], preferred_element_type=F32) * scale
            p = _softmax_rows(sc)
            pb = p.astype(BF16)
            dkv_ref[1, :, sl] += lax.dot_general(pb, doh, _DIMS["tn"], preferred_element_type=F32)
            dp = lax.dot_general(doh, kv_ref[1, :, sl], _DIMS["nt"], preferred_element_type=F32)
            ds = (p * (dp - jnp.sum(dp * p, axis=-1, keepdims=True)) * scale).astype(BF16)
            dq_ref[:, sl] = jnp.dot(ds, kh, preferred_element_type=F32).astype(BF16)
            dkv_ref[0, :, sl] += lax.dot_general(ds, qh, _DIMS["tn"], preferred_element_type=F32)

    row = pl.BlockSpec((r, d), lambda i: (i, 0))
    kvs = pl.BlockSpec((2, m, d), lambda i: (0, 0, 0))
    return pl.pallas_call(
        body, name=name, out_shape=[jax.ShapeDtypeStruct((s, d), BF16), jax.ShapeDtypeStruct((2, m, d), F32)],
        grid=(s // r,), in_specs=[row, kvs, row], out_specs=[row, kvs], compiler_params=_params("arbitrary"))(q, kv3, do)


def _swiglu_fwd(gu3, name):
    _, s, f = gu3.shape
    r = _row_tile(s)
    cw = _tile(f, (512, 256, 128))

    def body(gu_ref, o_ref):
        gate = gu_ref[0].astype(F32)
        o_ref[...] = (gate * jax.nn.sigmoid(gate) * gu_ref[1].astype(F32)).astype(BF16)

    return pl.pallas_call(
        body, name=name, out_shape=jax.ShapeDtypeStruct((s, f), BF16), grid=(s // r, f // cw),
        in_specs=[pl.BlockSpec((2, r, cw), lambda i, j: (0, i, j))], out_specs=pl.BlockSpec((r, cw), lambda i, j: (i, j)),
        compiler_params=_params("parallel", "parallel"))(gu3)


def _swiglu_bwd(gu3, dact, name):
    _, s, f = gu3.shape
    r = _row_tile(s)
    cw = _tile(f, (512, 256, 128))

    def body(gu_ref, da_ref, o_ref):
        gate = gu_ref[0].astype(F32)
        up = gu_ref[1].astype(F32)
        da = da_ref[...].astype(F32)
        sg = jax.nn.sigmoid(gate)
        o_ref[0] = (da * up * (sg + gate * sg * (1.0 - sg))).astype(BF16)
        o_ref[1] = (da * gate * sg).astype(BF16)

    return pl.pallas_call(
        body, name=name, out_shape=jax.ShapeDtypeStruct((2, s, f), BF16), grid=(s // r, f // cw),
        in_specs=[pl.BlockSpec((2, r, cw), lambda i, j: (0, i, j)), pl.BlockSpec((r, cw), lambda i, j: (i, j))],
        out_specs=pl.BlockSpec((2, r, cw), lambda i, j: (0, i, j)),
        compiler_params=_params("parallel", "parallel"))(gu3, dact)


def _ids():
    x, y, c = lax.axis_index("x"), lax.axis_index("y"), lax.axis_index("c")
    return x, y, c, 2 * x + y


def _chip_peers(x, y):
    return [(d - 1, 2 * (x ^ (d >> 1)) + (y ^ (d & 1)), x ^ (d >> 1), y ^ (d & 1)) for d in (1, 2, 3)]


def _remote(src, dst, ssem, rsem, dev):
    return pltpu.make_async_remote_copy(src_ref=src, dst_ref=dst, send_sem=ssem, recv_sem=rsem, device_id=dev,
                                        device_id_type=MESH)


def _gview(ref, kind, j, cc):
    _, k, n = ref.shape
    if kind == "row":
        return ref.at[:, pl.ds(j * (k // N_CHIPS) + cc * (k // (2 * N_CHIPS)), k // (2 * N_CHIPS)), :]
    return ref.at[:, pl.ds(cc * (k // 2), k // 2), pl.ds(j * (n // N_CHIPS), n // N_CHIPS)]


def _sview(ref, cc):
    r = ref.shape[1]
    return ref.at[:, pl.ds(cc * (r // 2), r // 2), :]


def _comm_call(body, name, ins, out_shapes, n_sems, aliases=None):
    return pl.pallas_call(
        body, name=name, out_shape=out_shapes, in_specs=[HBM] * len(ins), out_specs=[HBM] * len(out_shapes),
        scratch_shapes=[pltpu.SemaphoreType.DMA((n,)) for n in n_sems], input_output_aliases=aliases or {},
        compiler_params=pltpu.CompilerParams(has_side_effects=True))(*ins)


def _mesh_scalars():
    x, y, c = lax.axis_index("x"), lax.axis_index("y"), lax.axis_index("c")
    return jnp.stack([2 * x + y, c]).astype(jnp.int32)


def _slab_rows(rows, cols, itemsize=4):
    for cand in (512, 256, 128, 64, 32, 16):
        if rows % cand == 0 and cand * cols * itemsize <= (2 << 20):
            return cand
    return rows


def _ag_place(shard, kind, name):
    l, r, n = shard.shape
    full = (l, r * N_CHIPS, n) if kind == "row" else (l, r, n * N_CHIPS)
    tr = _slab_rows(r, n)
    nt = r // tr
    if kind == "row":
        out_spec = pl.BlockSpec((None, tr, n), lambda li, t, s: (li, s[0] * nt + t, 0))
    else:
        out_spec = pl.BlockSpec((None, tr, n), lambda li, t, s: (li, t, s[0]))

    def body(s_ref, i_ref, o_ref):
        o_ref[...] = i_ref[...].astype(BF16)

    return pl.pallas_call(
        body, name=name, out_shape=jax.ShapeDtypeStruct(full, BF16),
        grid_spec=pltpu.PrefetchScalarGridSpec(
            num_scalar_prefetch=1, grid=(l, nt), in_specs=[pl.BlockSpec((None, tr, n), lambda li, t, s: (li, t, 0))],
            out_specs=out_spec),
        compiler_params=_params("parallel", "parallel"))(_mesh_scalars(), shard)


def _ag_weight(placed, kind, name):
    def body(i_ref, o_ref, ssem, rsem, fsem, frsem):
        del i_ref
        x, y, c, me = _ids()
        sib = (x, y, 1 - c)
        peers = _chip_peers(x, y)
        mine = _gview(o_ref, kind, me, c)
        sends = [_remote(mine, mine, ssem.at[d], rsem.at[d], (px, py, c)) for d, _, px, py in peers]
        for cp in sends:
            cp.start()
        passed = []
        for d, pj, px, py in peers:
            piece = _gview(o_ref, kind, pj, c)
            _remote(piece, piece, ssem.at[d], rsem.at[d], (px, py, c)).wait_recv()
            fw = _remote(piece, piece, fsem.at[d], frsem.at[d], sib)
            fw.start()
            passed.append(fw)
        for d, pj, _, _ in peers:
            piece = _gview(o_ref, kind, pj, 1 - c)
            _remote(piece, piece, fsem.at[d], frsem.at[d], sib).wait_recv()
        for cp in sends + passed:
            cp.wait_send()

    return _comm_call(body, name, [placed], [jax.ShapeDtypeStruct(placed.shape, placed.dtype)], (3, 3, 3, 3), {0: 0})[0]


def _rs1(g_full, kind, name):
    l, k, n = g_full.shape
    piece = (l, k // (2 * N_CHIPS), n) if kind == "row" else (l, k // 2, n // N_CHIPS)

    def body(g_ref, got_ref, ssem, rsem):
        x, y, c, _ = _ids()
        sends = [_remote(_gview(g_ref, kind, j, 1 - c), got_ref.at[j], ssem.at[j], rsem.at[j], (x, y, 1 - c))
                 for j in range(N_CHIPS)]
        for cp in sends:
            cp.start()
        for cp in sends:
            cp.wait()

    return _comm_call(body, name, [g_full], [jax.ShapeDtypeStruct((N_CHIPS,) + piece, g_full.dtype)], (4, 4))[0]


def _rs_add1(g_full, got, kind, name):
    l, k, n = g_full.shape
    _, _, pr, pc = got.shape
    tr = _slab_rows(pr, pc, 2)
    nt = pr // tr
    if kind == "row":
        g_spec = pl.BlockSpec((None, tr, n), lambda j, li, t, s: (li, (2 * j + s[1]) * nt + t, 0))
    else:
        g_spec = pl.BlockSpec((None, tr, pc), lambda j, li, t, s: (li, s[1] * nt + t, j))
    slot = pl.BlockSpec((None, None, tr, pc), lambda j, li, t, s: (j, li, t, 0))

    def body(s_ref, g_ref, got_ref, o_ref):
        o_ref[...] = (g_ref[...].astype(F32) + got_ref[...].astype(F32)).astype(BF16)

    return pl.pallas_call(
        body, name=name, out_shape=jax.ShapeDtypeStruct(got.shape, BF16),
        grid_spec=pltpu.PrefetchScalarGridSpec(num_scalar_prefetch=1, grid=(N_CHIPS, l, nt), in_specs=[g_spec, slot],
                                               out_specs=slot),
        compiler_params=_params("parallel", "parallel", "parallel"))(_mesh_scalars(), g_full, got)


def _rs2(p, name):
    def body(p_ref, got_ref, ssem, rsem):
        x, y, c, me = _ids()
        sends = [_remote(p_ref.at[pj], got_ref.at[me], ssem.at[d], rsem.at[d], (px, py, c)) for d, pj, px, py in _chip_peers(x, y)]
        for cp in sends:
            cp.start()
        for d, pj, px, py in _chip_peers(x, y):
            _remote(p_ref.at[pj], got_ref.at[pj], ssem.at[d], rsem.at[d], (px, py, c)).wait_recv()
        for cp in sends:
            cp.wait_send()

    return _comm_call(body, name, [p], [jax.ShapeDtypeStruct(p.shape, p.dtype)], (3, 3))[0]


def _rs_add2(p, got, name):
    _, l, pr, pc = p.shape
    tr = _slab_rows(pr, pc)
    nt = pr // tr

    def slot(d):
        return pl.BlockSpec((None, None, tr, pc), lambda li, t, s: (s[0] ^ d, li, t, 0))

    def body(s_ref, p_ref, g1_ref, g2_ref, g3_ref, o_ref):
        o_ref[...] = (p_ref[...].astype(F32) + g1_ref[...].astype(F32) + g2_ref[...].astype(F32) + g3_ref[...].astype(F32))

    return pl.pallas_call(
        body, name=name, out_shape=jax.ShapeDtypeStruct((l, 2 * pr, pc), F32),
        grid_spec=pltpu.PrefetchScalarGridSpec(
            num_scalar_prefetch=1, grid=(l, nt), in_specs=[slot(0), slot(1), slot(2), slot(3)],
            out_specs=pl.BlockSpec((None, tr, pc), lambda li, t, s: (li, s[1] * nt + t, 0))),
        compiler_params=_params("parallel", "parallel"))(_mesh_scalars(), p, got, got, got)


def _rs3(shard, name):
    def body(i_ref, o_ref, ssem, rsem):
        del i_ref
        x, y, c, _ = _ids()
        send = _remote(_sview(o_ref, c), _sview(o_ref, c), ssem.at[0], rsem.at[0], (x, y, 1 - c))
        send.start()
        _remote(_sview(o_ref, 1 - c), _sview(o_ref, 1 - c), ssem.at[0], rsem.at[0], (x, y, 1 - c)).wait_recv()
        send.wait_send()

    return _comm_call(body, name, [shard], [jax.ShapeDtypeStruct(shard.shape, shard.dtype)], (1, 1), {0: 0})[0]


def _ag_small(sp, name):
    def body(s_ref, o_ref, ssem, rsem, lsem):
        x, y, c, me = _ids()
        local = pltpu.make_async_copy(s_ref, o_ref.at[me], lsem.at[0])
        local.start()
        sends = [_remote(s_ref, o_ref.at[me], ssem.at[d], rsem.at[d], (px, py, c)) for d, _, px, py in _chip_peers(x, y)]
        for cp in sends:
            cp.start()
        for d, pj, px, py in _chip_peers(x, y):
            _remote(s_ref, o_ref.at[pj], ssem.at[d], rsem.at[d], (px, py, c)).wait_recv()
        for cp in sends:
            cp.wait_send()
        local.wait()

    return _comm_call(body, name, [sp], [jax.ShapeDtypeStruct((N_CHIPS,) + sp.shape, sp.dtype)], (3, 3, 1))[0]


def _gather8(g, name):
    def body(g_ref, o_ref, ssem, rsem, lsem):
        x, y, c, _ = _ids()
        me = 4 * x + 2 * y + c
        local = pltpu.make_async_copy(g_ref, o_ref.at[me], lsem.at[0])
        local.start()
        peers = [(d - 1, x ^ (d >> 2), y ^ ((d >> 1) & 1), c ^ (d & 1)) for d in range(1, 8)]
        sends = [_remote(g_ref, o_ref.at[me], ssem.at[d], rsem.at[d], (px, py, pc)) for d, px, py, pc in peers]
        for cp in sends:
            cp.start()
        for d, px, py, pc in peers:
            _remote(g_ref, o_ref.at[4 * px + 2 * py + pc], ssem.at[d], rsem.at[d], (px, py, pc)).wait_recv()
        for cp in sends:
            cp.wait_send()
        local.wait()

    return _comm_call(body, name, [g], [jax.ShapeDtypeStruct((8,) + g.shape, g.dtype)], (7, 7, 1))[0]


def _sum_slots(a, out_dtype, name):
    n = a.shape[0]
    shape = a.shape[1:]
    cols = shape[-1]
    a3 = a.reshape(n, -1, cols)
    rows = a3.shape[1]
    tr = rows
    for cand in (512, 256, 128, 64, 32, 16):
        if rows % cand == 0 and cand * cols * 4 <= (1 << 20):
            tr = cand
            break

    def body(a_ref, o_ref):
        acc = a_ref[0].astype(F32)
        for j in range(1, n):
            acc = acc + a_ref[j].astype(F32)
        o_ref[...] = acc.astype(o_ref.dtype)

    out = pl.pallas_call(
        body, name=name, out_shape=jax.ShapeDtypeStruct((rows, cols), out_dtype), grid=(rows // tr,),
        in_specs=[pl.BlockSpec((n, tr, cols), lambda i: (0, i, 0))], out_specs=pl.BlockSpec((tr, cols), lambda i: (i, 0)),
        compiler_params=_params("parallel"))(a3)
    return out.reshape(shape)


def _reduce_scatter(g_full, kind, tag):
    p = _rs_add1(g_full, _rs1(g_full, kind, "rs1_" + tag), kind, "rs_add1_" + tag)
    return _rs3(_rs_add2(p, _rs2(p, "rs2_" + tag), "rs_add2_" + tag), "rs3_" + tag)


def kernel(x, mem, mix_norm, xa_norm, xa_wq, xa_wkv, xa_wo, ffn_norm, ffn_w_gu, ffn_w_down, a_w_in, a_conv_w, a_w_out, b_w_in, b_v_g, b_v_b, b_w_s, b_s_bias, b_w_out, c_w_in, c_conv_w, c_conv_b, c_ln_g, c_ln_b, c_w_out, loss_target, m_mix_norm, m_xa_norm, m_xa_wq, m_xa_wkv, m_xa_wo, m_ffn_norm, m_ffn_w_gu, m_ffn_w_down, m_a_w_in, m_a_conv_w, m_a_w_out, m_b_w_in, m_b_v_g, m_b_v_b, m_b_w_s, m_b_s_bias, m_b_w_out, m_c_w_in, m_c_conv_w, m_c_conv_b, m_c_ln_g, m_c_ln_b, m_c_w_out, v_mix_norm, v_xa_norm, v_xa_wq, v_xa_wkv, v_xa_wo, v_ffn_norm, v_ffn_w_gu, v_ffn_w_down, v_a_w_in, v_a_conv_w, v_a_w_out, v_b_w_in, v_b_v_g, v_b_v_b, v_b_w_s, v_b_s_bias, v_b_w_out, v_c_w_in, v_c_conv_w, v_c_conv_b, v_c_ln_g, v_c_ln_b, v_c_w_out):
    given = dict(locals())
    w = {n: given[n] for n in WEIGHTS}
    depth = mix_norm.shape[0]
    s, d = x.shape[1], x.shape[2]
    n_mem = mem.shape[1]
    ds = d // N_CHIPS
    xin = x.reshape(s, d)
    memv = mem.reshape(n_mem, d)
    target = loss_target.reshape(s, d)
    me = 2 * lax.axis_index("x") + lax.axis_index("y")

    wg = {n: _ag_weight(_ag_place(w[n], kind, "ag_place_" + n), kind, "ag_" + n) for n, kind in BIG_KINDS.items()}

    def pad8(t):
        return jnp.pad(t, ((0, (-t.shape[0]) % 8), (0, 0)))

    small_rows = [w[n].reshape(-1, ds) for n in SMALL_SHARDED]
    counts = [t.shape[0] for t in small_rows]
    gathered = _ag_small(jnp.concatenate([pad8(t) for t in small_rows], axis=0), "ag_small")
    gathered = jnp.transpose(gathered, (1, 0, 2)).reshape(-1, d)
    full, off = {}, 0
    for n, cnt in zip(SMALL_SHARDED, counts):
        full[n] = gathered[off:off + cnt].reshape(w[n].shape[:-1] + (d,))
        off += cnt + (-cnt) % 8
    t_chunk = b_w_s.shape[-1]
    tril = jnp.tril(jnp.ones((t_chunk, t_chunk), dtype=bool))

    def vec(a):
        return a.reshape(1, -1)

    def b_params(slot):
        ws_m = jnp.where(tril[None], b_w_s[slot], 0.0).astype(BF16)
        sbt = jnp.zeros((t_chunk, 128), F32).at[:, :b_s_bias.shape[1]].set(b_s_bias[slot].T)
        return vec(b_v_g[slot]), vec(b_v_b[slot]), ws_m, sbt

    saved = []
    xc = xin
    for i in range(depth):
        kind, slot = i % 3, i // 3
        t = f"{i}"
        sv = {"x0": xc}
        h = _rms_fwd(xc, vec(full["mix_norm"][i, 0]), "rms_mix_" + t)
        sv["h1"] = h
        if kind == 0:
            pre = _mm("nn", h, wg["a_w_in"], BF16, "a_in_" + t, bl=slot, o_parts=3)
            mid = _a_mid_fwd(pre, full["a_conv_w"][slot], "a_mid_" + t)
            y = _mm("nn", mid, wg["a_w_out"], F32, "a_out_" + t, bl=slot)
        elif kind == 1:
            pre = _mm("nn", h, wg["b_w_in"], BF16, "b_in_" + t, bl=slot, o_parts=2)
            mid = _b_mid_fwd(pre, *b_params(slot), "b_mid_" + t)
            y = _mm("nn", mid, wg["b_w_out"], F32, "b_out_" + t, bl=slot)
        else:
            pre = _mm("nn", h, wg["c_w_in"], BF16, "c_in_" + t, bl=slot, o_parts=2)
            y2 = _c_conv_fwd(pre, full["c_conv_w"][slot], vec(full["c_conv_b"][slot]), "c_conv_" + t)
            sv["cy2"] = y2
            mid = _c_ln_fwd(y2, vec(full["c_ln_g"][slot]), vec(full["c_ln_b"][slot]), "c_ln_" + t)
            y = _mm("nn", mid, wg["c_w_out"], F32, "c_out_" + t, bl=slot)
        sv.update(pre=pre, mid=mid, y1=y)
        xc = _res_rms_fwd(xc, y, vec(full["mix_norm"][i, 1]), "res_mix_" + t)

        sv["x1"] = xc
        h = _rms_fwd(xc, vec(full["xa_norm"][i, 0]), "rms_xa_" + t)
        mem_n = _rms_fwd(memv, vec(full["xa_norm"][i, 2]), "rms_mem_" + t)
        q = _mm("nn", h, wg["xa_wq"], BF16, "xa_q_" + t, bl=i)
        kv3 = _mm("nn", mem_n, wg["xa_wkv"], BF16, "xa_kv_" + t, bl=i, o_parts=2)
        o = _attn_fwd(q, kv3, "attn_" + t)
        y = _mm("nn", o, wg["xa_wo"], F32, "xa_o_" + t, bl=i)
        sv.update(h2=h, mem_n=mem_n, q=q, kv3=kv3, o=o, y2=y)
        xc = _res_rms_fwd(xc, y, vec(full["xa_norm"][i, 1]), "res_xa_" + t)

        sv["x2"] = xc
        h = _rms_fwd(xc, vec(full["ffn_norm"][i, 0]), "rms_ffn_" + t)
        gu3 = _mm("nn", h, wg["ffn_w_gu"], BF16, "ffn_gu_" + t, bl=i, o_parts=2)
        act = _swiglu_fwd(gu3, "swiglu_" + t)
        y = _mm("nn", act, wg["ffn_w_down"], F32, "ffn_down_" + t, bl=i)
        sv.update(h3=h, gu3=gu3, act=act, y3=y)
        xc = _res_rms_fwd(xc, y, vec(full["ffn_norm"][i, 1]), "res_ffn_" + t)
        saved.append(sv)

    loss_blk, dx = _loss(xc, target, "loss")
    loss = lax.psum(loss_blk[0, 0], ("x", "y", "c"))

    gbuf = {n: lax.empty(wg[n].shape, BF16) for n in BIG_KINDS}
    gsmall = {n: [None] * full[n].shape[0] for n in ("mix_norm", "xa_norm", "ffn_norm", "a_conv_w", "c_conv_w", "c_conv_b",
                                                      "c_ln_g", "c_ln_b")}
    grepl = {}

    def wgrad(name, l, a, dy, tag, b_parts=1):
        gbuf[name] = _mm("tn", a, dy, BF16, "wg_" + tag, b_parts=b_parts, into=gbuf[name], into_l=l)

    for i in reversed(range(depth)):
        kind, slot = i % 3, i // 3
        t = f"{i}"
        sv = saved[i]
        dy, dg_post = _rms_bwd(sv["y3"], vec(full["ffn_norm"][i, 1]), dx, None, BF16, "rmsb_ffn_post_" + t)
        wgrad("ffn_w_down", i, sv["act"], dy, "ffn_down_" + t)
        dact = _mm("nt", dy, wg["ffn_w_down"], F32, "dg_ffn_down_" + t, bl=i)
        dgu3 = _swiglu_bwd(sv["gu3"], dact, "swiglu_b_" + t)
        wgrad("ffn_w_gu", i, sv["h3"], dgu3, "ffn_gu_" + t, b_parts=2)
        dh = _mm("nt", dgu3, wg["ffn_w_gu"], F32, "dg_ffn_gu_" + t, bl=i, a_parts=2)
        dx, dg_pre = _rms_bwd(sv["x2"], vec(full["ffn_norm"][i, 0]), dh, dx, F32, "rmsb_ffn_pre_" + t)
        gsmall["ffn_norm"][i] = jnp.concatenate([dg_pre, dg_post], axis=0)
        dy, dg_post = _rms_bwd(sv["y2"], vec(full["xa_norm"][i, 1]), dx, None, BF16, "rmsb_xa_post_" + t)
        wgrad("xa_wo", i, sv["o"], dy, "xa_o_" + t)
        do = _mm("nt", dy, wg["xa_wo"], BF16, "dg_xa_o_" + t, bl=i)
        dq, dkv3 = _attn_bwd(sv["q"], sv["kv3"], do, "attn_b_" + t)
        wgrad("xa_wq", i, sv["h2"], dq, "xa_q_" + t)
        dh = _mm("nt", dq, wg["xa_wq"], F32, "dg_xa_q_" + t, bl=i)
        dkv3 = dkv3.astype(BF16)
        wgrad("xa_wkv", i, sv["mem_n"], dkv3, "xa_kv_" + t, b_parts=2)
        dmem_n = _mm("nt", dkv3, wg["xa_wkv"], F32, "dg_xa_kv_" + t, bl=i, a_parts=2)
        _, dg_mem = _rms_bwd(memv, vec(full["xa_norm"][i, 2]), dmem_n, None, F32, "rmsb_mem_" + t)
        dx, dg_pre = _rms_bwd(sv["x1"], vec(full["xa_norm"][i, 0]), dh, dx, F32, "rmsb_xa_pre_" + t)
        gsmall["xa_norm"][i] = jnp.concatenate([dg_pre, dg_post, dg_mem], axis=0)
        dy, dg_post = _rms_bwd(sv["y1"], vec(full["mix_norm"][i, 1]), dx, None, BF16, "rmsb_mix_post_" + t)
        if kind == 0:
            wgrad("a_w_out", slot, sv["mid"], dy, "a_out_" + t)
            dmid = _mm("nt", dy, wg["a_w_out"], F32, "dg_a_out_" + t, bl=slot)
            dpre, dcw = _a_mid_bwd(sv["pre"], dmid, full["a_conv_w"][slot], "a_mid_b_" + t)
            gsmall["a_conv_w"][slot] = dcw
            wgrad("a_w_in", slot, sv["h1"], dpre, "a_in_" + t, b_parts=3)
            dh = _mm("nt", dpre, wg["a_w_in"], F32, "dg_a_in_" + t, bl=slot, a_parts=3)
        elif kind == 1:
            wgrad("b_w_out", slot, sv["mid"], dy, "b_out_" + t)
            dmid = _mm("nt", dy, wg["b_w_out"], F32, "dg_b_out_" + t, bl=slot)
            dpre, dvg, dvb, dws, dsbt = _b_mid_bwd(sv["pre"], dmid, *b_params(slot), "b_mid_b_" + t)
            grepl[slot] = (dvg, dvb, dws, dsbt[:, :b_s_bias.shape[1]].T)
            wgrad("b_w_in", slot, sv["h1"], dpre, "b_in_" + t, b_parts=2)
            dh = _mm("nt", dpre, wg["b_w_in"], F32, "dg_b_in_" + t, bl=slot, a_parts=2)
        else:
            wgrad("c_w_out", slot, sv["mid"], dy, "c_out_" + t)
            dmid = _mm("nt", dy, wg["c_w_out"], F32, "dg_c_out_" + t, bl=slot)
            dy2, dlg, dlb = _c_ln_bwd(sv["cy2"], dmid, vec(full["c_ln_g"][slot]), vec(full["c_ln_b"][slot]), "c_ln_b_" + t)
            dpre, dcw, dcb = _c_conv_bwd(sv["pre"], dy2, full["c_conv_w"][slot], "c_conv_b_" + t)
            gsmall["c_conv_w"][slot], gsmall["c_conv_b"][slot] = dcw, dcb
            gsmall["c_ln_g"][slot], gsmall["c_ln_b"][slot] = dlg, dlb
            wgrad("c_w_in", slot, sv["h1"], dpre, "c_in_" + t, b_parts=2)
            dh = _mm("nt", dpre, wg["c_w_in"], F32, "dg_c_in_" + t, bl=slot, a_parts=2)
        dx, dg_pre = _rms_bwd(sv["x0"], vec(full["mix_norm"][i, 0]), dh, dx, F32, "rmsb_mix_pre_" + t)
        gsmall["mix_norm"][i] = jnp.concatenate([dg_pre, dg_post], axis=0)
    grad_x = dx.reshape(x.shape)

    grads = {n: _reduce_scatter(gbuf[n], kind, n).reshape(w[n].shape) for n, kind in BIG_KINDS.items()}
    small_g = [jnp.concatenate(gsmall[n], axis=0).reshape(-1, d) for n in SMALL_SHARDED]
    n_b = b_v_g.shape[0]
    repl_g = [jnp.concatenate([grepl[sl][k] for sl in range(n_b)], axis=0) for k in range(4)]
    repl_rows = []
    for g_arr in repl_g:
        flat = g_arr.reshape(-1)
        flat = jnp.concatenate([flat, jnp.zeros(((-flat.shape[0]) % d,), F32)])
        repl_rows.append(flat.reshape(-1, d))
    rows_all = [pad8(t) for t in small_g + repl_rows]
    total = _sum_slots(_gather8(jnp.concatenate(rows_all, axis=0), "gather_small_grads"), F32, "sum_small_grads")
    off = 0
    for n, cnt in zip(SMALL_SHARDED, counts):
        blk = lax.dynamic_slice_in_dim(total[off:off + cnt], me * ds, ds, axis=1)
        grads[n] = blk.reshape(w[n].shape)
        off += cnt + (-cnt) % 8
    for n, g_arr in zip(SMALL_REPL, repl_g):
        cnt = -(-g_arr.size // d)
        grads[n] = total[off:off + cnt].reshape(-1)[:g_arr.size].reshape(w[n].shape)
        off += cnt + (-cnt) % 8

    delta, new_m, new_v = {}, {}, {}
    for n in WEIGHTS:
        delta[n], new_m[n], new_v[n] = _adamw(w[n], grads[n], given["m_" + n], given["v_" + n], "adamw_" + n)
    return (loss, grad_x, *[grads[n] for n in WEIGHTS], *[delta[n] for n in WEIGHTS], *[new_m[n] for n in WEIGHTS],
            *[new_v[n] for n in WEIGHTS])
```

```python
import functools

import jax
import jax.numpy as jnp
from jax import lax
from jax.experimental import pallas as pl
from jax.experimental.pallas import tpu as pltpu

F32 = jnp.float32
BF16 = jnp.bfloat16
EPS = 1e-6
XA_HEADS = 4
CHUNK = 128
GMLP_GROUPS = 8
ADAM_LR, ADAM_B1, ADAM_B2, ADAM_EPS, ADAM_WD, ADAM_STEP = 0.001, 0.9, 0.999, 1e-08, 0.01, 10
VMEM_LIMIT_V7X = 48 * 1024 * 1024
HBM = pl.BlockSpec(memory_space=pltpu.HBM)
MESH = pl.DeviceIdType.MESH
N_CHIPS = 4
BIG_KINDS = {"xa_wq": "row", "xa_wkv": "col", "xa_wo": "row", "ffn_w_gu": "col", "ffn_w_down": "row",
             "a_w_in": "col", "a_w_out": "row", "b_w_in": "col", "b_w_out": "row", "c_w_in": "col", "c_w_out": "row"}
SMALL_SHARDED = ["mix_norm", "xa_norm", "ffn_norm", "a_conv_w", "c_conv_w", "c_conv_b", "c_ln_g", "c_ln_b"]
SMALL_REPL = ["b_v_g", "b_v_b", "b_w_s", "b_s_bias"]
WEIGHTS = ["mix_norm", "xa_norm", "xa_wq", "xa_wkv", "xa_wo", "ffn_norm", "ffn_w_gu", "ffn_w_down", "a_w_in", "a_conv_w",
           "a_w_out", "b_w_in", "b_v_g", "b_v_b", "b_w_s", "b_s_bias", "b_w_out", "c_w_in", "c_conv_w", "c_conv_b",
           "c_ln_g", "c_ln_b", "c_w_out"]


def _params(*sem):
    return pltpu.CompilerParams(dimension_semantics=sem, vmem_limit_bytes=VMEM_LIMIT_V7X)


def _tile(n, cands=(1024, 512, 256, 128)):
    for c in cands:
        if n % c == 0:
            return c
    return n


def _div_tile(n, cap):
    best = None
    for t in range(128, min(n, cap) + 1, 128):
        if n % t == 0:
            best = t
    return best or n


MM_OUT_TILE_CAP = 1408
MM_K_TILE_CAP = 2816

_DIMS = {"nn": (((1,), (0,)), ((), ())), "nt": (((1,), (1,)), ((), ())), "tn": (((0,), (0,)), ((), ()))}


def _mm(mode, a, b, out_dtype, name, *, bl=None, a_parts=1, b_parts=1, o_parts=1, into=None, into_l=0):
    if isinstance(b, list):
        b, bl = b[bl], 0
    bshape = b.shape[1:] if bl is not None else b.shape
    if mode == "nn":
        mo, c = a.shape
        no = bshape[1]
    elif mode == "nt":
        mo, c = (a.shape[1], a.shape[0] * a.shape[2]) if a_parts > 1 else a.shape
        no = bshape[0]
    else:
        c, mo = a.shape
        no = b.shape[0] * b.shape[2] if b_parts > 1 else bshape[1]
    tmo = _div_tile(mo, MM_OUT_TILE_CAP)
    tno = _div_tile(no // max(o_parts, b_parts), MM_OUT_TILE_CAP)
    tc = _div_tile(c // a_parts, MM_K_TILE_CAP)
    nk = c // tc
    nkp = nk // a_parts
    njp = (no // tno) // max(o_parts, b_parts)
    lead = (None,) if bl is not None else ()
    lidx = (bl,) if bl is not None else ()

    if mode == "nn":
        a_spec = pl.BlockSpec((tmo, tc), lambda i, j, k: (i, k))
        b_spec = pl.BlockSpec(lead + (tc, tno), lambda i, j, k: lidx + (k, j))
    elif mode == "nt":
        if a_parts > 1:
            a_spec = pl.BlockSpec((None, tmo, tc), lambda i, j, k: (k // nkp, i, k % nkp))
        else:
            a_spec = pl.BlockSpec((tmo, tc), lambda i, j, k: (i, k))
        b_spec = pl.BlockSpec(lead + (tno, tc), lambda i, j, k: lidx + (j, k))
    else:
        a_spec = pl.BlockSpec((tc, tmo), lambda i, j, k: (k, i))
        if b_parts > 1:
            b_spec = pl.BlockSpec((None, tc, tno), lambda i, j, k: (j // njp, k, j % njp))
        else:
            b_spec = pl.BlockSpec((tc, tno), lambda i, j, k: (k, j))

    in_specs = [a_spec, b_spec]
    args = [a, b]
    aliases = {}
    if into is not None:
        out_shape = jax.ShapeDtypeStruct(into.shape, into.dtype)
        out_spec = pl.BlockSpec((None, tmo, tno), lambda i, j, k: (into_l, i, j))
        in_specs.append(HBM)
        args.append(into)
        aliases = {2: 0}
    elif o_parts > 1:
        out_shape = jax.ShapeDtypeStruct((o_parts, mo, no // o_parts), out_dtype)
        out_spec = pl.BlockSpec((None, tmo, tno), lambda i, j, k: (j // njp, i, j % njp))
    else:
        out_shape = jax.ShapeDtypeStruct((mo, no), out_dtype)
        out_spec = pl.BlockSpec((tmo, tno), lambda i, j, k: (i, j))
    dims = _DIMS[mode]

    def body(a_ref, b_ref, *rest):
        if nk == 1:
            o_ref = rest[-1]
            o_ref[...] = lax.dot_general(a_ref[...], b_ref[...], dims, preferred_element_type=F32).astype(o_ref.dtype)
            return
        o_ref, acc = rest[-2], rest[-1]
        k = pl.program_id(2)
        part = lax.dot_general(a_ref[...], b_ref[...], dims, preferred_element_type=F32)

        @pl.when(k == 0)
        def _():
            acc[...] = part

        @pl.when(jnp.logical_and(k > 0, k < nk - 1))
        def _():
            acc[...] += part

        @pl.when(k == nk - 1)
        def _():
            o_ref[...] = (acc[...] + part).astype(o_ref.dtype)

    return pl.pallas_call(
        body, name=name, out_shape=out_shape, grid=(mo // tmo, no // tno, nk), in_specs=in_specs, out_specs=out_spec,
        scratch_shapes=[pltpu.VMEM((tmo, tno), F32)] if nk > 1 else [], input_output_aliases=aliases,
        compiler_params=_params("parallel", "parallel", "arbitrary"))(*args)


def _ew(fn, ins, out_dtypes, name):
    rows, cols = ins[0].shape
    tr = rows
    for cand in (512, 256, 128, 64, 32, 16):
        if rows % cand == 0 and cand * cols * 4 <= (1 << 20):
            tr = cand
            break
    spec = pl.BlockSpec((tr, cols), lambda i: (i, 0))
    n_in = len(ins)

    def body(*refs):
        outs = fn(*[r[...] for r in refs[:n_in]])
        for o_ref, o in zip(refs[n_in:], outs):
            o_ref[...] = o.astype(o_ref.dtype)

    return pl.pallas_call(
        body, name=name, out_shape=[jax.ShapeDtypeStruct((rows, cols), d) for d in out_dtypes], grid=(rows // tr,),
        in_specs=[spec] * n_in, out_specs=[spec] * len(out_dtypes), compiler_params=_params("parallel"))(*ins)


def _adamw_fn(w, g, m, v):
    m = ADAM_B1 * m + (1.0 - ADAM_B1) * g
    v = ADAM_B2 * v + (1.0 - ADAM_B2) * (g * g)
    m_hat = m / (1.0 - ADAM_B1 ** ADAM_STEP)
    v_hat = v / (1.0 - ADAM_B2 ** ADAM_STEP)
    delta = -ADAM_LR * (m_hat / (jnp.sqrt(v_hat) + ADAM_EPS) + ADAM_WD * w)
    return delta, m, v


def _adamw(w, g, m, v, name):
    shape = w.shape
    cols = shape[-1]
    flat = [t.reshape(-1, cols) for t in (w, g, m, v)]
    outs = _ew(_adamw_fn, flat, [F32] * 3, name)
    return [o.reshape(shape) for o in outs]


def _row_tile(s):
    return _tile(s, (256, 128, 64, 32, 16, 8))


def _rms_fwd(x, g, name):
    s, d = x.shape
    r = _row_tile(s)

    def body(x_ref, g_ref, o_ref):
        xv = x_ref[...]
        o_ref[...] = (xv * lax.rsqrt(jnp.mean(xv * xv, axis=-1, keepdims=True) + EPS) * g_ref[...]).astype(BF16)

    return pl.pallas_call(
        body, name=name, out_shape=jax.ShapeDtypeStruct((s, d), BF16), grid=(s // r,),
        in_specs=[pl.BlockSpec((r, d), lambda i: (i, 0)), pl.BlockSpec((1, d), lambda i: (0, 0))],
        out_specs=pl.BlockSpec((r, d), lambda i: (i, 0)), compiler_params=_params("parallel"))(x, g)


def _res_rms_fwd(x, y, g, name):
    s, d = x.shape
    r = _row_tile(s)

    def body(x_ref, y_ref, g_ref, o_ref):
        yv = y_ref[...]
        o_ref[...] = x_ref[...] + yv * lax.rsqrt(jnp.mean(yv * yv, axis=-1, keepdims=True) + EPS) * g_ref[...]

    row = pl.BlockSpec((r, d), lambda i: (i, 0))
    return pl.pallas_call(
        body, name=name, out_shape=jax.ShapeDtypeStruct((s, d), F32), grid=(s // r,),
        in_specs=[row, row, pl.BlockSpec((1, d), lambda i: (0, 0))], out_specs=row,
        compiler_params=_params("parallel"))(x, y, g)


def _rms_bwd(x, g, dy, resid, out_dtype, name):
    s, d = x.shape
    r = _row_tile(s)
    has_res = resid is not None

    def body(*refs):
        x_ref, g_ref, dy_ref = refs[:3]
        dx_ref, dg_ref = refs[-2:]
        i = pl.program_id(0)
        xv = x_ref[...]
        dyv = dy_ref[...].astype(F32)
        rstd = lax.rsqrt(jnp.mean(xv * xv, axis=-1, keepdims=True) + EPS)
        n = xv * rstd
        dn = dyv * g_ref[...]
        dx = rstd * (dn - n * jnp.mean(dn * n, axis=-1, keepdims=True))
        if has_res:
            dx = dx + refs[3][...]
        dx_ref[...] = dx.astype(dx_ref.dtype)
        part = jnp.sum(dyv * n, axis=0, keepdims=True)

        @pl.when(i == 0)
        def _():
            dg_ref[...] = part

        @pl.when(i > 0)
        def _():
            dg_ref[...] += part

    row = pl.BlockSpec((r, d), lambda i: (i, 0))
    vec = pl.BlockSpec((1, d), lambda i: (0, 0))
    ins = [x, g, dy] + ([resid] if has_res else [])
    return pl.pallas_call(
        body, name=name, out_shape=[jax.ShapeDtypeStruct((s, d), out_dtype), jax.ShapeDtypeStruct((1, d), F32)],
        grid=(s // r,), in_specs=[row, vec, row] + ([row] if has_res else []), out_specs=[row, vec],
        compiler_params=_params("arbitrary"))(*ins)


def _loss(y, t, name):
    s, d = y.shape
    r = _row_tile(s)

    def body(y_ref, t_ref, l_ref, dy_ref):
        i = pl.program_id(0)
        e = y_ref[...] - t_ref[...]
        dy_ref[...] = e * (1.0 / d)
        part = jnp.full((8, 128), 0.5 * jnp.sum(jnp.mean(e * e, axis=-1, keepdims=True)), F32)

        @pl.when(i == 0)
        def _():
            l_ref[...] = part

        @pl.when(i > 0)
        def _():
            l_ref[...] += part

    row = pl.BlockSpec((r, d), lambda i: (i, 0))
    return pl.pallas_call(
        body, name=name, out_shape=[jax.ShapeDtypeStruct((8, 128), F32), jax.ShapeDtypeStruct((s, d), F32)],
        grid=(s // r,), in_specs=[row, row], out_specs=[pl.BlockSpec((8, 128), lambda i: (0, 0)), row],
        compiler_params=_params("arbitrary"))(y, t)


def _rows(xv, a, m, cache):
    r = a % 8
    q = a - r
    if r == 0:
        return xv[q:q + m]
    if r not in cache:
        cache[r] = pltpu.roll(xv, xv.shape[0] - r, 0)
    return cache[r][q:q + m]


def _conv_taps(xv, w, k_w, halo, m, flip):
    cache = {}
    acc = None
    for k in range(k_w):
        a = (k_w - 1 - k) if flip else (halo + k - (k_w - 1))
        term = w[k:k + 1, :] * _rows(xv, a, m, cache)
        acc = term if acc is None else acc + term
    return acc


def _conv_wgrad(dw_ref, dyv, xv, k_w, halo, m):
    cache = {}
    for k in range(k_w):
        xs = _rows(xv, halo + k - (k_w - 1), m, cache)
        dw_ref[pl.ds(k, 1), :] += jnp.sum(dyv * xs, axis=0, keepdims=True)


def _conv_tiles(s, dp, halo):
    r = _tile(s, (256, 128))
    cw = _tile(dp, (256, 128))
    return r, cw, r // halo


A_HALO = 8


def _a_mid_fwd(bcz3, w, name):
    _, s, d = bcz3.shape
    r, cw, rh = _conv_tiles(s, d, A_HALO)
    k_w = w.shape[0]

    def body(m_ref, h_ref, w_ref, o_ref):
        i = pl.program_id(0)
        cz = m_ref[1].astype(F32) * m_ref[2].astype(F32)
        hcz = h_ref[1].astype(F32) * h_ref[2].astype(F32)
        hcz = jnp.where(i == 0, 0.0, hcz)
        xv = jnp.concatenate([hcz, cz], axis=0)
        y = _conv_taps(xv, w_ref[...], k_w, A_HALO, r, False)
        o_ref[...] = (m_ref[0].astype(F32) * y).astype(BF16)

    return pl.pallas_call(
        body, name=name, out_shape=jax.ShapeDtypeStruct((s, d), BF16), grid=(s // r, d // cw),
        in_specs=[pl.BlockSpec((3, r, cw), lambda i, j: (0, i, j)),
                  pl.BlockSpec((3, A_HALO, cw), lambda i, j: (0, jnp.maximum(i * rh - 1, 0), j)),
                  pl.BlockSpec((k_w, cw), lambda i, j: (0, j))],
        out_specs=pl.BlockSpec((r, cw), lambda i, j: (i, j)), compiler_params=_params("parallel", "parallel"))(bcz3, bcz3, w)


def _a_mid_bwd(bcz3, dgated, w, name):
    _, s, d = bcz3.shape
    r, cw, rh = _conv_tiles(s, d, A_HALO)
    k_w = w.shape[0]
    ni = s // r
    last_h = s // A_HALO - 1

    def body(m_ref, hp_ref, hn_ref, dg_ref, dgn_ref, w_ref, o_ref, dw_ref):
        i = pl.program_id(1)
        wv = w_ref[...]
        b = m_ref[0].astype(F32)
        c = m_ref[1].astype(F32)
        z = m_ref[2].astype(F32)
        hcz = jnp.where(i == 0, 0.0, hp_ref[1].astype(F32) * hp_ref[2].astype(F32))
        xv = jnp.concatenate([hcz, c * z], axis=0)
        y = _conv_taps(xv, wv, k_w, A_HALO, r, False)
        dg = dg_ref[...].astype(F32)
        dy = dg * b
        dyn = jnp.where(i == ni - 1, 0.0, dgn_ref[...].astype(F32) * hn_ref[0].astype(F32))
        dcz = _conv_taps(jnp.concatenate([dy, dyn], axis=0), wv, k_w, A_HALO, r, True)
        o_ref[0] = (dg * y).astype(BF16)
        o_ref[1] = (dcz * z).astype(BF16)
        o_ref[2] = (dcz * c).astype(BF16)

        @pl.when(i == 0)
        def _():
            dw_ref[...] = jnp.zeros_like(dw_ref)

        _conv_wgrad(dw_ref, dy, xv, k_w, A_HALO, r)

    return pl.pallas_call(
        body, name=name, out_shape=[jax.ShapeDtypeStruct((3, s, d), BF16), jax.ShapeDtypeStruct((k_w, d), F32)],
        grid=(d // cw, ni),
        in_specs=[pl.BlockSpec((3, r, cw), lambda j, i: (0, i, j)),
                  pl.BlockSpec((3, A_HALO, cw), lambda j, i: (0, jnp.maximum(i * rh - 1, 0), j)),
                  pl.BlockSpec((3, A_HALO, cw), lambda j, i: (0, jnp.minimum((i + 1) * rh, last_h), j)),
                  pl.BlockSpec((r, cw), lambda j, i: (i, j)),
                  pl.BlockSpec((A_HALO, cw), lambda j, i: (jnp.minimum((i + 1) * rh, last_h), j)),
                  pl.BlockSpec((k_w, cw), lambda j, i: (0, j))],
        out_specs=[pl.BlockSpec((3, r, cw), lambda j, i: (0, i, j)), pl.BlockSpec((k_w, cw), lambda j, i: (0, j))],
        compiler_params=_params("parallel", "arbitrary"))(bcz3, bcz3, bcz3, dgated, dgated, w)


C_HALO = 32


def _c_conv_fwd(ag3, w, bias, name):
    _, s, d = ag3.shape
    r, cw, rh = _conv_tiles(s, d, C_HALO)
    k_w = w.shape[0]

    def body(m_ref, h_ref, w_ref, b_ref, o_ref):
        i = pl.program_id(0)
        y1 = m_ref[0].astype(F32) * jax.nn.sigmoid(m_ref[1].astype(F32))
        h1 = jnp.where(i == 0, 0.0, h_ref[0].astype(F32) * jax.nn.sigmoid(h_ref[1].astype(F32)))
        xv = jnp.concatenate([h1, y1], axis=0)
        o_ref[...] = _conv_taps(xv, w_ref[...], k_w, C_HALO, r, False) + b_ref[...]

    return pl.pallas_call(
        body, name=name, out_shape=jax.ShapeDtypeStruct((s, d), F32), grid=(s // r, d // cw),
        in_specs=[pl.BlockSpec((2, r, cw), lambda i, j: (0, i, j)),
                  pl.BlockSpec((2, C_HALO, cw), lambda i, j: (0, jnp.maximum(i * rh - 1, 0), j)),
                  pl.BlockSpec((k_w, cw), lambda i, j: (0, j)), pl.BlockSpec((1, cw), lambda i, j: (0, j))],
        out_specs=pl.BlockSpec((r, cw), lambda i, j: (i, j)),
        compiler_params=_params("parallel", "parallel"))(ag3, ag3, w, bias)


def _c_conv_bwd(ag3, dy2, w, name):
    _, s, d = ag3.shape
    r, cw, rh = _conv_tiles(s, d, C_HALO)
    k_w = w.shape[0]
    ni = s // r
    last_h = s // C_HALO - 1

    def body(m_ref, hp_ref, dy_ref, dyn_ref, w_ref, o_ref, dw_ref, db_ref):
        i = pl.program_id(1)
        wv = w_ref[...]
        a = m_ref[0].astype(F32)
        sg = jax.nn.sigmoid(m_ref[1].astype(F32))
        h1 = jnp.where(i == 0, 0.0, hp_ref[0].astype(F32) * jax.nn.sigmoid(hp_ref[1].astype(F32)))
        xv = jnp.concatenate([h1, a * sg], axis=0)
        dy = dy_ref[...]
        dyn = jnp.where(i == ni - 1, 0.0, dyn_ref[...])
        dy1 = _conv_taps(jnp.concatenate([dy, dyn], axis=0), wv, k_w, C_HALO, r, True)
        o_ref[0] = (dy1 * sg).astype(BF16)
        o_ref[1] = (dy1 * a * sg * (1.0 - sg)).astype(BF16)

        @pl.when(i == 0)
        def _():
            dw_ref[...] = jnp.zeros_like(dw_ref)
            db_ref[...] = jnp.zeros_like(db_ref)

        db_ref[...] += jnp.sum(dy, axis=0, keepdims=True)
        _conv_wgrad(dw_ref, dy, xv, k_w, C_HALO, r)

    return pl.pallas_call(
        body, name=name,
        out_shape=[jax.ShapeDtypeStruct((2, s, d), BF16), jax.ShapeDtypeStruct((k_w, d), F32),
                   jax.ShapeDtypeStruct((1, d), F32)],
        grid=(d // cw, ni),
        in_specs=[pl.BlockSpec((2, r, cw), lambda j, i: (0, i, j)),
                  pl.BlockSpec((2, C_HALO, cw), lambda j, i: (0, jnp.maximum(i * rh - 1, 0), j)),
                  pl.BlockSpec((r, cw), lambda j, i: (i, j)),
                  pl.BlockSpec((C_HALO, cw), lambda j, i: (jnp.minimum((i + 1) * rh, last_h), j)),
                  pl.BlockSpec((k_w, cw), lambda j, i: (0, j))],
        out_specs=[pl.BlockSpec((2, r, cw), lambda j, i: (0, i, j)), pl.BlockSpec((k_w, cw), lambda j, i: (0, j)),
                   pl.BlockSpec((1, cw), lambda j, i: (0, j))],
        compiler_params=_params("parallel", "arbitrary"))(ag3, ag3, dy2, dy2, w)


def _ln_stats(v):
    mu = jnp.mean(v, axis=-1, keepdims=True)
    vc = v - mu
    rstd = lax.rsqrt(jnp.mean(vc * vc, axis=-1, keepdims=True) + EPS)
    return vc * rstd, rstd


def _ln_bwd(dn, g, xh, rstd):
    dxh = dn * g
    return rstd * (dxh - jnp.mean(dxh, axis=-1, keepdims=True) - xh * jnp.mean(dxh * xh, axis=-1, keepdims=True))


def _c_ln_fwd(y2, g, b, name):
    s, d = y2.shape
    r = _row_tile(s)

    def body(y_ref, g_ref, b_ref, o_ref):
        xh, _ = _ln_stats(y_ref[...])
        y3 = xh * g_ref[...] + b_ref[...]
        o_ref[...] = (y3 * jax.nn.sigmoid(y3)).astype(BF16)

    row = pl.BlockSpec((r, d), lambda i: (i, 0))
    vec = pl.BlockSpec((1, d), lambda i: (0, 0))
    return pl.pallas_call(
        body, name=name, out_shape=jax.ShapeDtypeStruct((s, d), BF16), grid=(s // r,), in_specs=[row, vec, vec],
        out_specs=row, compiler_params=_params("parallel"))(y2, g, b)


def _c_ln_bwd(y2, dout, g, b, name):
    s, d = y2.shape
    r = _row_tile(s)

    def body(y_ref, do_ref, g_ref, b_ref, dy_ref, dg_ref, db_ref):
        i = pl.program_id(0)
        xh, rstd = _ln_stats(y_ref[...])
        gv = g_ref[...]
        y3 = xh * gv + b_ref[...]
        sg = jax.nn.sigmoid(y3)
        dy3 = do_ref[...].astype(F32) * (sg + y3 * sg * (1.0 - sg))
        dy_ref[...] = _ln_bwd(dy3, gv, xh, rstd)

        @pl.when(i == 0)
        def _():
            dg_ref[...] = jnp.zeros_like(dg_ref)
            db_ref[...] = jnp.zeros_like(db_ref)

        dg_ref[...] += jnp.sum(dy3 * xh, axis=0, keepdims=True)
        db_ref[...] += jnp.sum(dy3, axis=0, keepdims=True)

    row = pl.BlockSpec((r, d), lambda i: (i, 0))
    vec = pl.BlockSpec((1, d), lambda i: (0, 0))
    return pl.pallas_call(
        body, name=name,
        out_shape=[jax.ShapeDtypeStruct((s, d), F32), jax.ShapeDtypeStruct((1, d), F32), jax.ShapeDtypeStruct((1, d), F32)],
        grid=(s // r,), in_specs=[row, row, vec, vec], out_specs=[row, vec, vec],
        compiler_params=_params("arbitrary"))(y2, dout, g, b)


_GELU_C = 0.7978845608028654
_GELU_A = 0.044715


def _gelu(x):
    return 0.5 * x * (1.0 + jnp.tanh(_GELU_C * (x + _GELU_A * x * x * x)))


def _gelu_grad(x):
    t = jnp.tanh(_GELU_C * (x + _GELU_A * x * x * x))
    return 0.5 * (1.0 + t) + 0.5 * x * (1.0 - t * t) * _GELU_C * (1.0 + 3.0 * _GELU_A * x * x)


def _b_mid_fwd(uv3, vg, vb, ws_m, sbt, name):
    _, s, h = uv3.shape
    g_n, t, _ = ws_m.shape
    gd = h // g_n

    def body(uv_ref, vg_ref, vb_ref, ws_ref, sb_ref, o_ref):
        u = _gelu(uv_ref[0].astype(F32))
        xh, _ = _ln_stats(_gelu(uv_ref[1].astype(F32)))
        vn = (xh * vg_ref[...] + vb_ref[...]).astype(BF16)
        for g in range(g_n):
            sl = slice(g * gd, (g + 1) * gd)
            sv = jnp.dot(ws_ref[g], vn[:, sl], preferred_element_type=F32) + sb_ref[:, g:g + 1]
            o_ref[:, sl] = (u[:, sl] * sv).astype(BF16)

    vec = pl.BlockSpec((1, h), lambda i: (0, 0))
    return pl.pallas_call(
        body, name=name, out_shape=jax.ShapeDtypeStruct((s, h), BF16), grid=(s // t,),
        in_specs=[pl.BlockSpec((2, t, h), lambda i: (0, i, 0)), vec, vec,
                  pl.BlockSpec((g_n, t, t), lambda i: (0, 0, 0)), pl.BlockSpec((t, 128), lambda i: (0, 0))],
        out_specs=pl.BlockSpec((t, h), lambda i: (i, 0)), compiler_params=_params("parallel"))(uv3, vg, vb, ws_m, sbt)


def _b_mid_bwd(uv3, dgated, vg, vb, ws_m, sbt, name):
    _, s, h = uv3.shape
    g_n, t, _ = ws_m.shape
    gd = h // g_n

    def body(uv_ref, dg_ref, vg_ref, vb_ref, ws_ref, sb_ref, o_ref, dvg_ref, dvb_ref, dws_ref, dsb_ref, dvn_ref):
        i = pl.program_id(0)

        @pl.when(i == 0)
        def _():
            dvg_ref[...] = jnp.zeros_like(dvg_ref)
            dvb_ref[...] = jnp.zeros_like(dvb_ref)
            dws_ref[...] = jnp.zeros_like(dws_ref)
            dsb_ref[...] = jnp.zeros_like(dsb_ref)

        upre = uv_ref[0].astype(F32)
        vpre = uv_ref[1].astype(F32)
        u = _gelu(upre)
        xh, rstd = _ln_stats(_gelu(vpre))
        gv = vg_ref[...]
        vn = (xh * gv + vb_ref[...]).astype(BF16)
        causal = lax.broadcasted_iota(jnp.int32, (t, t), 0) >= lax.broadcasted_iota(jnp.int32, (t, t), 1)
        lane = lax.broadcasted_iota(jnp.int32, (t, 128), 1)
        for g in range(g_n):
            sl = slice(g * gd, (g + 1) * gd)
            wsg = ws_ref[g]
            sv = jnp.dot(wsg, vn[:, sl], preferred_element_type=F32) + sb_ref[:, g:g + 1]
            dg = dg_ref[:, sl].astype(F32)
            o_ref[0, :, sl] = (dg * sv * _gelu_grad(upre[:, sl])).astype(BF16)
            dsv = dg * u[:, sl]
            dsvb = dsv.astype(BF16)
            dsb_ref[...] += jnp.where(lane == g, jnp.sum(dsv, axis=1, keepdims=True), 0.0)
            dws = lax.dot_general(dsvb, vn[:, sl], _DIMS["nt"], preferred_element_type=F32)
            dws_ref[g] += jnp.where(causal, dws, 0.0)
            dvn_ref[:, sl] = lax.dot_general(wsg, dsvb, _DIMS["tn"], preferred_element_type=F32)
        dvn = dvn_ref[...]
        dvg_ref[...] += jnp.sum(dvn * xh, axis=0, keepdims=True)
        dvb_ref[...] += jnp.sum(dvn, axis=0, keepdims=True)
        o_ref[1] = (_ln_bwd(dvn, gv, xh, rstd) * _gelu_grad(vpre)).astype(BF16)

    vec = pl.BlockSpec((1, h), lambda i: (0, 0))
    return pl.pallas_call(
        body, name=name,
        out_shape=[jax.ShapeDtypeStruct((2, s, h), BF16), jax.ShapeDtypeStruct((1, h), F32), jax.ShapeDtypeStruct((1, h), F32),
                   jax.ShapeDtypeStruct((g_n, t, t), F32), jax.ShapeDtypeStruct((t, 128), F32)],
        grid=(s // t,),
        in_specs=[pl.BlockSpec((2, t, h), lambda i: (0, i, 0)), pl.BlockSpec((t, h), lambda i: (i, 0)), vec, vec,
                  pl.BlockSpec((g_n, t, t), lambda i: (0, 0, 0)), pl.BlockSpec((t, 128), lambda i: (0, 0))],
        out_specs=[pl.BlockSpec((2, t, h), lambda i: (0, i, 0)), vec, vec,
                   pl.BlockSpec((g_n, t, t), lambda i: (0, 0, 0)), pl.BlockSpec((t, 128), lambda i: (0, 0))],
        scratch_shapes=[pltpu.VMEM((t, h), F32)],
        compiler_params=_params("arbitrary"))(uv3, dgated, vg, vb, ws_m, sbt)


def _softmax_rows(sc):
    e = jnp.exp(sc - jnp.max(sc, axis=-1, keepdims=True))
    return e / jnp.sum(e, axis=-1, keepdims=True)


def _attn_fwd(q, kv3, name):
    s, d = q.shape
    m = kv3.shape[1]
    dh = d // XA_HEADS
    scale = dh ** -0.5
    r = _row_tile(s)

    def body(q_ref, kv_ref, o_ref):
        for hd in range(XA_HEADS):
            sl = slice(hd * dh, (hd + 1) * dh)
            sc = lax.dot_general(q_ref[:, sl], kv_ref[0, :, sl], _DIMS["nt"], preferred_element_type=F32) * scale
            p = _softmax_rows(sc).astype(BF16)
            o_ref[:, sl] = jnp.dot(p, kv_ref[1, :, sl], preferred_element_type=F32).astype(BF16)

    return pl.pallas_call(
        body, name=name, out_shape=jax.ShapeDtypeStruct((s, d), BF16), grid=(s // r,),
        in_specs=[pl.BlockSpec((r, d), lambda i: (i, 0)), pl.BlockSpec((2, m, d), lambda i: (0, 0, 0))],
        out_specs=pl.BlockSpec((r, d), lambda i: (i, 0)), compiler_params=_params("parallel"))(q, kv3)


def _attn_bwd(q, kv3, do, name):
    s, d = q.shape
    m = kv3.shape[1]
    dh = d // XA_HEADS
    scale = dh ** -0.5
    r = _row_tile(s)

    def body(q_ref, kv_ref, do_ref, dq_ref, dkv_ref):
        i = pl.program_id(0)

        @pl.when(i == 0)
        def _():
            dkv_ref[...] = jnp.zeros_like(dkv_ref)

        for hd in range(XA_HEADS):
            sl = slice(hd * dh, (hd + 1) * dh)
            qh = q_ref[:, sl]
            kh = kv_ref[0, :, sl]
            doh = do_ref[:, sl]
            sc = lax.dot_general(qh, kh, _DIMS["nt"], preferred_element_type=F32) * scale
            p = _softmax_rows(sc)
            pb = p.astype(BF16)
            dkv_ref[1, :, sl] += lax.dot_general(pb, doh, _DIMS["tn"], preferred_element_type=F32)
            dp = lax.dot_general(doh, kv_ref[1, :, sl], _DIMS["nt"], preferred_element_type=F32)
            ds = (p * (dp - jnp.sum(dp * p, axis=-1, keepdims=True)) * scale).astype(BF16)
            dq_ref[:, sl] = jnp.dot(ds, kh, preferred_element_type=F32).astype(BF16)
            dkv_ref[0, :, sl] += lax.dot_general(ds, qh, _DIMS["tn"], preferred_element_type=F32)

    row = pl.BlockSpec((r, d), lambda i: (i, 0))
    kvs = pl.BlockSpec((2, m, d), lambda i: (0, 0, 0))
    return pl.pallas_call(
        body, name=name, out_shape=[jax.ShapeDtypeStruct((s, d), BF16), jax.ShapeDtypeStruct((2, m, d), F32)],
        grid=(s // r,), in_specs=[row, kvs, row], out_specs=[row, kvs], compiler_params=_params("arbitrary"))(q, kv3, do)


def _swiglu_fwd(gu3, name):
    _, s, f = gu3.shape
    r = _row_tile(s)
    cw = _tile(f, (512, 256, 128))

    def body(gu_ref, o_ref):
        gate = gu_ref[0].astype(F32)
        o_ref[...] = (gate * jax.nn.sigmoid(gate) * gu_ref[1].astype(F32)).astype(BF16)

    return pl.pallas_call(
        body, name=name, out_shape=jax.ShapeDtypeStruct((s, f), BF16), grid=(s // r, f // cw),
        in_specs=[pl.BlockSpec((2, r, cw), lambda i, j: (0, i, j))], out_specs=pl.BlockSpec((r, cw), lambda i, j: (i, j)),
        compiler_params=_params("parallel", "parallel"))(gu3)


def _swiglu_bwd(gu3, dact, name):
    _, s, f = gu3.shape
    r = _row_tile(s)
    cw = _tile(f, (512, 256, 128))

    def body(gu_ref, da_ref, o_ref):
        gate = gu_ref[0].astype(F32)
        up = gu_ref[1].astype(F32)
        da = da_ref[...].astype(F32)
        sg = jax.nn.sigmoid(gate)
        o_ref[0] = (da * up * (sg + gate * sg * (1.0 - sg))).astype(BF16)
        o_ref[1] = (da * gate * sg).astype(BF16)

    return pl.pallas_call(
        body, name=name, out_shape=jax.ShapeDtypeStruct((2, s, f), BF16), grid=(s // r, f // cw),
        in_specs=[pl.BlockSpec((2, r, cw), lambda i, j: (0, i, j)), pl.BlockSpec((r, cw), lambda i, j: (i, j))],
        out_specs=pl.BlockSpec((2, r, cw), lambda i, j: (0, i, j)),
        compiler_params=_params("parallel", "parallel"))(gu3, dact)


def _ids():
    x, y, c = lax.axis_index("x"), lax.axis_index("y"), lax.axis_index("c")
    return x, y, c, 2 * x + y


def _chip_peers(x, y):
    return [(d - 1, 2 * (x ^ (d >> 1)) + (y ^ (d & 1)), x ^ (d >> 1), y ^ (d & 1)) for d in (1, 2, 3)]


def _remote(src, dst, ssem, rsem, dev):
    return pltpu.make_async_remote_copy(src_ref=src, dst_ref=dst, send_sem=ssem, recv_sem=rsem, device_id=dev,
                                        device_id_type=MESH)


def _gview(ref, kind, j, cc):
    _, k, n = ref.shape
    if kind == "row":
        return ref.at[:, pl.ds(j * (k // N_CHIPS) + cc * (k // (2 * N_CHIPS)), k // (2 * N_CHIPS)), :]
    return ref.at[:, pl.ds(cc * (k // 2), k // 2), pl.ds(j * (n // N_CHIPS), n // N_CHIPS)]


def _sview(ref, cc):
    r = ref.shape[1]
    return ref.at[:, pl.ds(cc * (r // 2), r // 2), :]


def _comm_call(body, name, ins, out_shapes, n_sems, aliases=None):
    return pl.pallas_call(
        body, name=name, out_shape=out_shapes, in_specs=[HBM] * len(ins), out_specs=[HBM] * len(out_shapes),
        scratch_shapes=[pltpu.SemaphoreType.DMA((n,)) for n in n_sems], input_output_aliases=aliases or {},
        compiler_params=pltpu.CompilerParams(has_side_effects=True))(*ins)


def _mesh_scalars():
    x, y, c = lax.axis_index("x"), lax.axis_index("y"), lax.axis_index("c")
    return jnp.stack([2 * x + y, c]).astype(jnp.int32)


def _slab_rows(rows, cols, itemsize=4):
    for cand in (512, 256, 128, 64, 32, 16):
        if rows % cand == 0 and cand * cols * itemsize <= (2 << 20):
            return cand
    return rows


def _ag_place(shard, layer, kind, name):
    _, r, n = shard.shape
    full = (1, r * N_CHIPS, n) if kind == "row" else (1, r, n * N_CHIPS)
    tr = _slab_rows(r, n)
    nt = r // tr
    if kind == "row":
        out_spec = pl.BlockSpec((None, tr, n), lambda t, s: (0, s[0] * nt + t, 0))
    else:
        out_spec = pl.BlockSpec((None, tr, n), lambda t, s: (0, t, s[0]))

    def body(s_ref, i_ref, o_ref):
        o_ref[...] = i_ref[...].astype(BF16)

    return pl.pallas_call(
        body, name=name, out_shape=jax.ShapeDtypeStruct(full, BF16),
        grid_spec=pltpu.PrefetchScalarGridSpec(
            num_scalar_prefetch=1, grid=(nt,), in_specs=[pl.BlockSpec((None, tr, n), lambda t, s: (layer, t, 0))],
            out_specs=out_spec),
        compiler_params=_params("parallel"))(_mesh_scalars(), shard)


SEM = pl.BlockSpec(memory_space=pltpu.SEMAPHORE)
ANY = pl.BlockSpec(memory_space=pl.ANY)
DATAFLOW = pltpu.SideEffectType.DATAFLOW_SIDE_EFFECTING


def _in_hbm(arrs):
    return [pltpu.with_memory_space_constraint(a, pltpu.HBM) for a in arrs]


def _ag_start(bufs, kinds, name):
    n = len(bufs)

    def body(*refs):
        ssem, rsem, token = refs[n], refs[n + 1], refs[-1]
        x, y, c, me = _ids()
        for t in range(n):
            mine = _gview(refs[t], kinds[t], me, c)
            for d, _, px, py in _chip_peers(x, y):
                _remote(mine, mine, ssem.at[3 * t + d], rsem.at[3 * t + d], (px, py, c)).start()
        token[...] = jnp.zeros_like(token)

    outs = pl.pallas_call(
        body, name=name,
        out_shape=(pltpu.SemaphoreType.DMA((3 * n,)), pltpu.SemaphoreType.DMA((3 * n,)),
                   *[pltpu.HBM(b.shape, b.dtype) for b in bufs], jax.ShapeDtypeStruct((8, 128), F32)),
        in_specs=[HBM] * n, out_specs=(SEM, SEM, *[HBM] * n, pl.BlockSpec(memory_space=pltpu.VMEM)),
        input_output_aliases={t: 2 + t for t in range(n)},
        compiler_params=pltpu.CompilerParams(has_side_effects=DATAFLOW))(*_in_hbm(bufs))
    return outs[0], outs[1], list(outs[2:2 + n]), outs[-1]


def _ag_wait(ssem, rsem, bufs, kinds, after, name):
    n = len(bufs)

    def body(*refs):
        ssem_ref, rsem_ref = refs[n], refs[n + 1]
        x, y, c, me = _ids()
        for t in range(n):
            mine = _gview(refs[t], kinds[t], me, c)
            for d, pj, px, py in _chip_peers(x, y):
                theirs = _gview(refs[t], kinds[t], pj, c)
                _remote(mine, mine, ssem_ref.at[3 * t + d], rsem_ref.at[3 * t + d], (px, py, c)).wait_send()
                _remote(theirs, theirs, ssem_ref.at[3 * t + d], rsem_ref.at[3 * t + d], (px, py, c)).wait_recv()

    outs = pl.pallas_call(
        body, name=name, out_shape=[pltpu.HBM(b.shape, b.dtype) for b in bufs],
        in_specs=[HBM] * n + [SEM, SEM, ANY], out_specs=[HBM] * n, input_output_aliases={t: t for t in range(n)},
        compiler_params=pltpu.CompilerParams(has_side_effects=DATAFLOW))(*bufs, ssem, rsem, after)
    return list(outs)


def _ag_forward(bufs, kinds, name):
    n = len(bufs)

    def body(*refs):
        outs = refs[n:2 * n]
        ssem, rsem = refs[2 * n], refs[2 * n + 1]
        x, y, c, _ = _ids()
        sib = (x, y, 1 - c)
        sends = []
        for t in range(n):
            for d, pj, _, _ in _chip_peers(x, y):
                piece = _gview(outs[t], kinds[t], pj, c)
                sends.append(_remote(piece, piece, ssem.at[3 * t + d], rsem.at[3 * t + d], sib))
        for cp in sends:
            cp.start()
        for t in range(n):
            for d, pj, _, _ in _chip_peers(x, y):
                piece = _gview(outs[t], kinds[t], pj, 1 - c)
                _remote(piece, piece, ssem.at[3 * t + d], rsem.at[3 * t + d], sib).wait_recv()
        for cp in sends:
            cp.wait_send()

    return _comm_call(body, name, bufs, [jax.ShapeDtypeStruct(b.shape, b.dtype) for b in bufs], (3 * n, 3 * n),
                      {t: t for t in range(n)})


def _rs1(g_fulls, kinds, name):
    n = len(g_fulls)
    outs = []
    for g, kind in zip(g_fulls, kinds):
        l, k, nn = g.shape
        piece = (l, k // (2 * N_CHIPS), nn) if kind == "row" else (l, k // 2, nn // N_CHIPS)
        outs.append(jax.ShapeDtypeStruct((N_CHIPS,) + piece, g.dtype))

    def body(*refs):
        ssem, rsem = refs[2 * n], refs[2 * n + 1]
        x, y, c, _ = _ids()
        sends = [_remote(_gview(refs[t], kinds[t], j, 1 - c), refs[n + t].at[j], ssem.at[4 * t + j], rsem.at[4 * t + j],
                         (x, y, 1 - c)) for t in range(n) for j in range(N_CHIPS)]
        for cp in sends:
            cp.start()
        for cp in sends:
            cp.wait()

    return _comm_call(body, name, g_fulls, outs, (4 * n, 4 * n))


def _rs_add1(g_full, got, kind, name):
    l, k, n = g_full.shape
    _, _, pr, pc = got.shape
    tr = _slab_rows(pr, pc, 2)
    nt = pr // tr
    if kind == "row":
        g_spec = pl.BlockSpec((None, tr, n), lambda j, li, t, s: (li, (2 * j + s[1]) * nt + t, 0))
    else:
        g_spec = pl.BlockSpec((None, tr, pc), lambda j, li, t, s: (li, s[1] * nt + t, j))
    slot = pl.BlockSpec((None, None, tr, pc), lambda j, li, t, s: (j, li, t, 0))

    def body(s_ref, g_ref, got_ref, o_ref):
        o_ref[...] = (g_ref[...].astype(F32) + got_ref[...].astype(F32)).astype(BF16)

    return pl.pallas_call(
        body, name=name, out_shape=jax.ShapeDtypeStruct(got.shape, BF16),
        grid_spec=pltpu.PrefetchScalarGridSpec(num_scalar_prefetch=1, grid=(N_CHIPS, l, nt), in_specs=[g_spec, slot],
                                               out_specs=slot),
        compiler_params=_params("parallel", "parallel", "parallel"))(_mesh_scalars(), g_full, got)


def _rs2_start(ps, name):
    n = len(ps)
    lands = [lax.empty(p.shape, p.dtype) for p in ps]

    def body(*refs):
        ssem, rsem, token = refs[2 * n], refs[2 * n + 1], refs[-1]
        x, y, c, me = _ids()
        for t in range(n):
            for d, pj, px, py in _chip_peers(x, y):
                _remote(refs[t].at[pj], refs[n + t].at[me], ssem.at[3 * t + d], rsem.at[3 * t + d], (px, py, c)).start()
        token[...] = jnp.zeros_like(token)

    outs = pl.pallas_call(
        body, name=name,
        out_shape=(pltpu.SemaphoreType.DMA((3 * n,)), pltpu.SemaphoreType.DMA((3 * n,)),
                   *[pltpu.HBM(p.shape, p.dtype) for p in ps + lands], jax.ShapeDtypeStruct((8, 128), F32)),
        in_specs=[HBM] * (2 * n), out_specs=(SEM, SEM, *[HBM] * (2 * n), pl.BlockSpec(memory_space=pltpu.VMEM)),
        input_output_aliases={t: 2 + t for t in range(2 * n)},
        compiler_params=pltpu.CompilerParams(has_side_effects=DATAFLOW))(*_in_hbm(ps + lands))
    return outs[0], outs[1], list(outs[2:2 + n]), list(outs[2 + n:2 + 2 * n]), outs[-1]


def _rs2_wait(ssem, rsem, ps, lands, after, name):
    n = len(ps)

    def body(*refs):
        ssem_ref, rsem_ref = refs[2 * n], refs[2 * n + 1]
        x, y, c, me = _ids()
        for t in range(n):
            for d, pj, px, py in _chip_peers(x, y):
                _remote(refs[t].at[pj], refs[n + t].at[me], ssem_ref.at[3 * t + d], rsem_ref.at[3 * t + d], (px, py, c)).wait_send()
                _remote(refs[t].at[pj], refs[n + t].at[pj], ssem_ref.at[3 * t + d], rsem_ref.at[3 * t + d], (px, py, c)).wait_recv()

    outs = pl.pallas_call(
        body, name=name, out_shape=[pltpu.HBM(p.shape, p.dtype) for p in ps + lands],
        in_specs=[HBM] * (2 * n) + [SEM, SEM, ANY], out_specs=[HBM] * (2 * n),
        input_output_aliases={t: t for t in range(2 * n)},
        compiler_params=pltpu.CompilerParams(has_side_effects=DATAFLOW))(*ps, *lands, ssem, rsem, after)
    return list(outs[:n]), list(outs[n:])


def _rs_add2(p, got, into, layer, name):
    _, _, pr, pc = p.shape
    tr = _slab_rows(pr, pc)
    nt = pr // tr

    def slot(d):
        return pl.BlockSpec((None, None, tr, pc), lambda t, s: (s[0] ^ d, 0, t, 0))

    def body(s_ref, p_ref, g1_ref, g2_ref, g3_ref, i_ref, o_ref):
        o_ref[...] = (p_ref[...].astype(F32) + g1_ref[...].astype(F32) + g2_ref[...].astype(F32) + g3_ref[...].astype(F32))

    return pl.pallas_call(
        body, name=name, out_shape=jax.ShapeDtypeStruct(into.shape, F32),
        grid_spec=pltpu.PrefetchScalarGridSpec(
            num_scalar_prefetch=1, grid=(nt,), in_specs=[slot(0), slot(1), slot(2), slot(3), HBM],
            out_specs=pl.BlockSpec((None, tr, pc), lambda t, s: (layer, s[1] * nt + t, 0))),
        input_output_aliases={5: 0},
        compiler_params=_params("parallel"))(_mesh_scalars(), p, got, got, got, into)


def _rs3(shards, layers, name):
    n = len(shards)

    def body(*refs):
        outs = refs[n:2 * n]
        ssem, rsem = refs[2 * n], refs[2 * n + 1]
        x, y, c, _ = _ids()
        sib = (x, y, 1 - c)

        def half(t, cc):
            return _sview(outs[t].at[pl.ds(layers[t], 1)], cc)

        sends = [_remote(half(t, c), half(t, c), ssem.at[t], rsem.at[t], sib) for t in range(n)]
        for cp in sends:
            cp.start()
        for t in range(n):
            _remote(half(t, 1 - c), half(t, 1 - c), ssem.at[t], rsem.at[t], sib).wait_recv()
        for cp in sends:
            cp.wait_send()

    return _comm_call(body, name, shards, [jax.ShapeDtypeStruct(s.shape, s.dtype) for s in shards], (n, n),
                      {t: t for t in range(n)})


def _ag_small(sp, name):
    def body(s_ref, o_ref, ssem, rsem, lsem):
        x, y, c, me = _ids()
        local = pltpu.make_async_copy(s_ref, o_ref.at[me], lsem.at[0])
        local.start()
        sends = [_remote(s_ref, o_ref.at[me], ssem.at[d], rsem.at[d], (px, py, c)) for d, _, px, py in _chip_peers(x, y)]
        for cp in sends:
            cp.start()
        for d, pj, px, py in _chip_peers(x, y):
            _remote(s_ref, o_ref.at[pj], ssem.at[d], rsem.at[d], (px, py, c)).wait_recv()
        for cp in sends:
            cp.wait_send()
        local.wait()

    return _comm_call(body, name, [sp], [jax.ShapeDtypeStruct((N_CHIPS,) + sp.shape, sp.dtype)], (3, 3, 1))[0]


def _gather8(g, name):
    def body(g_ref, o_ref, ssem, rsem, lsem):
        x, y, c, _ = _ids()
        me = 4 * x + 2 * y + c
        local = pltpu.make_async_copy(g_ref, o_ref.at[me], lsem.at[0])
        local.start()
        peers = [(d - 1, x ^ (d >> 2), y ^ ((d >> 1) & 1), c ^ (d & 1)) for d in range(1, 8)]
        sends = [_remote(g_ref, o_ref.at[me], ssem.at[d], rsem.at[d], (px, py, pc)) for d, px, py, pc in peers]
        for cp in sends:
            cp.start()
        for d, px, py, pc in peers:
            _remote(g_ref, o_ref.at[4 * px + 2 * py + pc], ssem.at[d], rsem.at[d], (px, py, pc)).wait_recv()
        for cp in sends:
            cp.wait_send()
        local.wait()

    return _comm_call(body, name, [g], [jax.ShapeDtypeStruct((8,) + g.shape, g.dtype)], (7, 7, 1))[0]


def _sum_slots(a, out_dtype, name):
    n = a.shape[0]
    shape = a.shape[1:]
    cols = shape[-1]
    a3 = a.reshape(n, -1, cols)
    rows = a3.shape[1]
    tr = rows
    for cand in (512, 256, 128, 64, 32, 16):
        if rows % cand == 0 and cand * cols * 4 <= (1 << 20):
            tr = cand
            break

    def body(a_ref, o_ref):
        acc = a_ref[0].astype(F32)
        for j in range(1, n):
            acc = acc + a_ref[j].astype(F32)
        o_ref[...] = acc.astype(o_ref.dtype)

    out = pl.pallas_call(
        body, name=name, out_shape=jax.ShapeDtypeStruct((rows, cols), out_dtype), grid=(rows // tr,),
        in_specs=[pl.BlockSpec((n, tr, cols), lambda i: (0, i, 0))], out_specs=pl.BlockSpec((tr, cols), lambda i: (i, 0)),
        compiler_params=_params("parallel"))(a3)
    return out.reshape(shape)


def kernel(x, mem, mix_norm, xa_norm, xa_wq, xa_wkv, xa_wo, ffn_norm, ffn_w_gu, ffn_w_down, a_w_in, a_conv_w, a_w_out, b_w_in, b_v_g, b_v_b, b_w_s, b_s_bias, b_w_out, c_w_in, c_conv_w, c_conv_b, c_ln_g, c_ln_b, c_w_out, loss_target, m_mix_norm, m_xa_norm, m_xa_wq, m_xa_wkv, m_xa_wo, m_ffn_norm, m_ffn_w_gu, m_ffn_w_down, m_a_w_in, m_a_conv_w, m_a_w_out, m_b_w_in, m_b_v_g, m_b_v_b, m_b_w_s, m_b_s_bias, m_b_w_out, m_c_w_in, m_c_conv_w, m_c_conv_b, m_c_ln_g, m_c_ln_b, m_c_w_out, v_mix_norm, v_xa_norm, v_xa_wq, v_xa_wkv, v_xa_wo, v_ffn_norm, v_ffn_w_gu, v_ffn_w_down, v_a_w_in, v_a_conv_w, v_a_w_out, v_b_w_in, v_b_v_g, v_b_v_b, v_b_w_s, v_b_s_bias, v_b_w_out, v_c_w_in, v_c_conv_w, v_c_conv_b, v_c_ln_g, v_c_ln_b, v_c_w_out):
    given = dict(locals())
    w = {n: given[n] for n in WEIGHTS}
    depth = mix_norm.shape[0]
    s, d = x.shape[1], x.shape[2]
    n_mem = mem.shape[1]
    ds = d // N_CHIPS
    xin = x.reshape(s, d)
    memv = mem.reshape(n_mem, d)
    target = loss_target.reshape(s, d)
    me = 2 * lax.axis_index("x") + lax.axis_index("y")

    placed = {(n, l): _ag_place(w[n], l, kind, f"ag_place_{n}_{l}") for n, kind in BIG_KINDS.items()
              for l in range(w[n].shape[0])}
    wg = {n: [None] * w[n].shape[0] for n in BIG_KINDS}

    def group(i):
        mixer = "abc"[i % 3]
        return [(mixer + "_w_in", i // 3), (mixer + "_w_out", i // 3), ("xa_wq", i), ("xa_wkv", i), ("xa_wo", i),
                ("ffn_w_gu", i), ("ffn_w_down", i)]

    def ag_begin(i):
        keys = group(i)
        kinds = [BIG_KINDS[n] for n, _ in keys]
        ssem, rsem, bufs, token = _ag_start([placed[k] for k in keys], kinds, f"ag_start_{i}")
        return keys, kinds, ssem, rsem, bufs, token

    def ag_end(state, after, i):
        keys, kinds, ssem, rsem, bufs, _ = state
        bufs = _ag_forward(_ag_wait(ssem, rsem, bufs, kinds, after, f"ag_wait_{i}"), kinds, f"ag_fwd_{i}")
        for (n, l), buf in zip(keys, bufs):
            wg[n][l] = buf

    def pad8(t):
        return jnp.pad(t, ((0, (-t.shape[0]) % 8), (0, 0)))

    small_rows = [w[n].reshape(-1, ds) for n in SMALL_SHARDED]
    counts = [t.shape[0] for t in small_rows]
    gathered = _ag_small(jnp.concatenate([pad8(t) for t in small_rows], axis=0), "ag_small")
    gathered = jnp.transpose(gathered, (1, 0, 2)).reshape(-1, d)
    full, off = {}, 0
    for n, cnt in zip(SMALL_SHARDED, counts):
        full[n] = gathered[off:off + cnt].reshape(w[n].shape[:-1] + (d,))
        off += cnt + (-cnt) % 8
    t_chunk = b_w_s.shape[-1]
    tril = jnp.tril(jnp.ones((t_chunk, t_chunk), dtype=bool))

    def vec(a):
        return a.reshape(1, -1)

    def b_params(slot):
        ws_m = jnp.where(tril[None], b_w_s[slot], 0.0).astype(BF16)
        sbt = jnp.zeros((t_chunk, 128), F32).at[:, :b_s_bias.shape[1]].set(b_s_bias[slot].T)
        return vec(b_v_g[slot]), vec(b_v_b[slot]), ws_m, sbt

    saved = []
    xc = xin
    state = ag_begin(0)
    ag_end(state, state[-1], 0)
    for i in range(depth):
        kind, slot = i % 3, i // 3
        t = f"{i}"
        if i + 1 < depth:
            state = ag_begin(i + 1)
            xc, _ = lax.optimization_barrier((xc, state[-1]))
        sv = {"x0": xc}
        h = _rms_fwd(xc, vec(full["mix_norm"][i, 0]), "rms_mix_" + t)
        sv["h1"] = h
        if kind == 0:
            pre = _mm("nn", h, wg["a_w_in"], BF16, "a_in_" + t, bl=slot, o_parts=3)
            mid = _a_mid_fwd(pre, full["a_conv_w"][slot], "a_mid_" + t)
            y = _mm("nn", mid, wg["a_w_out"], F32, "a_out_" + t, bl=slot)
        elif kind == 1:
            pre = _mm("nn", h, wg["b_w_in"], BF16, "b_in_" + t, bl=slot, o_parts=2)
            mid = _b_mid_fwd(pre, *b_params(slot), "b_mid_" + t)
            y = _mm("nn", mid, wg["b_w_out"], F32, "b_out_" + t, bl=slot)
        else:
            pre = _mm("nn", h, wg["c_w_in"], BF16, "c_in_" + t, bl=slot, o_parts=2)
            y2 = _c_conv_fwd(pre, full["c_conv_w"][slot], vec(full["c_conv_b"][slot]), "c_conv_" + t)
            sv["cy2"] = y2
            mid = _c_ln_fwd(y2, vec(full["c_ln_g"][slot]), vec(full["c_ln_b"][slot]), "c_ln_" + t)
            y = _mm("nn", mid, wg["c_w_out"], F32, "c_out_" + t, bl=slot)
        sv.update(pre=pre, mid=mid, y1=y)
        xc = _res_rms_fwd(xc, y, vec(full["mix_norm"][i, 1]), "res_mix_" + t)

        sv["x1"] = xc
        h = _rms_fwd(xc, vec(full["xa_norm"][i, 0]), "rms_xa_" + t)
        mem_n = _rms_fwd(memv, vec(full["xa_norm"][i, 2]), "rms_mem_" + t)
        q = _mm("nn", h, wg["xa_wq"], BF16, "xa_q_" + t, bl=i)
        kv3 = _mm("nn", mem_n, wg["xa_wkv"], BF16, "xa_kv_" + t, bl=i, o_parts=2)
        o = _attn_fwd(q, kv3, "attn_" + t)
        y = _mm("nn", o, wg["xa_wo"], F32, "xa_o_" + t, bl=i)
        sv.update(h2=h, mem_n=mem_n, q=q, kv3=kv3, o=o, y2=y)
        xc = _res_rms_fwd(xc, y, vec(full["xa_norm"][i, 1]), "res_xa_" + t)

        sv["x2"] = xc
        h = _rms_fwd(xc, vec(full["ffn_norm"][i, 0]), "rms_ffn_" + t)
        gu3 = _mm("nn", h, wg["ffn_w_gu"], BF16, "ffn_gu_" + t, bl=i, o_parts=2)
        act = _swiglu_fwd(gu3, "swiglu_" + t)
        y = _mm("nn", act, wg["ffn_w_down"], F32, "ffn_down_" + t, bl=i)
        sv.update(h3=h, gu3=gu3, act=act, y3=y)
        xc = _res_rms_fwd(xc, y, vec(full["ffn_norm"][i, 1]), "res_ffn_" + t)
        saved.append(sv)
        if i + 1 < depth:
            ag_end(state, xc, i + 1)

    loss_blk, dx = _loss(xc, target, "loss")
    loss = lax.psum(loss_blk[0, 0], ("x", "y", "c"))

    gbuf = {}
    gfin = {n: lax.empty(w[n].shape, F32) for n in BIG_KINDS}
    gsmall = {n: [None] * full[n].shape[0] for n in ("mix_norm", "xa_norm", "ffn_norm", "a_conv_w", "c_conv_w", "c_conv_b",
                                                      "c_ln_g", "c_ln_b")}
    grepl = {}

    def wgrad(name, l, a, dy, tag, b_parts=1):
        g2 = _mm("tn", a, dy, BF16, "wg_" + tag, b_parts=b_parts)
        gbuf[name, l] = g2.reshape((1,) + g2.shape)

    def rs_begin(i):
        keys = group(i)
        kinds = [BIG_KINDS[n] for n, _ in keys]
        gots = _rs1([gbuf[k] for k in keys], kinds, f"rs1_{i}")
        ps = [_rs_add1(gbuf[k], got, kind, f"rs_add1_{k[0]}_{k[1]}") for k, got, kind in zip(keys, gots, kinds)]
        ssem, rsem, ps, lands, token = _rs2_start(ps, f"rs2_start_{i}")
        return keys, ssem, rsem, ps, lands, token

    def rs_end(state, after, i):
        keys, ssem, rsem, ps, lands, _ = state
        ps, lands = _rs2_wait(ssem, rsem, ps, lands, after, f"rs2_wait_{i}")
        for (n, l), p, land in zip(keys, ps, lands):
            gfin[n] = _rs_add2(p, land, gfin[n], l, f"rs_add2_{n}_{l}")
        outs = _rs3([gfin[n] for n, _ in keys], [l for _, l in keys], f"rs3_{i}")
        for (n, _), o in zip(keys, outs):
            gfin[n] = o

    rs_state = None

    for i in reversed(range(depth)):
        kind, slot = i % 3, i // 3
        t = f"{i}"
        sv = saved[i]
        dy, dg_post = _rms_bwd(sv["y3"], vec(full["ffn_norm"][i, 1]), dx, None, BF16, "rmsb_ffn_post_" + t)
        wgrad("ffn_w_down", i, sv["act"], dy, "ffn_down_" + t)
        dact = _mm("nt", dy, wg["ffn_w_down"], F32, "dg_ffn_down_" + t, bl=i)
        dgu3 = _swiglu_bwd(sv["gu3"], dact, "swiglu_b_" + t)
        wgrad("ffn_w_gu", i, sv["h3"], dgu3, "ffn_gu_" + t, b_parts=2)
        dh = _mm("nt", dgu3, wg["ffn_w_gu"], F32, "dg_ffn_gu_" + t, bl=i, a_parts=2)
        dx, dg_pre = _rms_bwd(sv["x2"], vec(full["ffn_norm"][i, 0]), dh, dx, F32, "rmsb_ffn_pre_" + t)
        gsmall["ffn_norm"][i] = jnp.concatenate([dg_pre, dg_post], axis=0)
        dy, dg_post = _rms_bwd(sv["y2"], vec(full["xa_norm"][i, 1]), dx, None, BF16, "rmsb_xa_post_" + t)
        wgrad("xa_wo", i, sv["o"], dy, "xa_o_" + t)
        do = _mm("nt", dy, wg["xa_wo"], BF16, "dg_xa_o_" + t, bl=i)
        dq, dkv3 = _attn_bwd(sv["q"], sv["kv3"], do, "attn_b_" + t)
        wgrad("xa_wq", i, sv["h2"], dq, "xa_q_" + t)
        dh = _mm("nt", dq, wg["xa_wq"], F32, "dg_xa_q_" + t, bl=i)
        dkv3 = dkv3.astype(BF16)
        wgrad("xa_wkv", i, sv["mem_n"], dkv3, "xa_kv_" + t, b_parts=2)
        dmem_n = _mm("nt", dkv3, wg["xa_wkv"], F32, "dg_xa_kv_" + t, bl=i, a_parts=2)
        _, dg_mem = _rms_bwd(memv, vec(full["xa_norm"][i, 2]), dmem_n, None, F32, "rmsb_mem_" + t)
        dx, dg_pre = _rms_bwd(sv["x1"], vec(full["xa_norm"][i, 0]), dh, dx, F32, "rmsb_xa_pre_" + t)
        gsmall["xa_norm"][i] = jnp.concatenate([dg_pre, dg_post, dg_mem], axis=0)
        dy, dg_post = _rms_bwd(sv["y1"], vec(full["mix_norm"][i, 1]), dx, None, BF16, "rmsb_mix_post_" + t)
        if kind == 0:
            wgrad("a_w_out", slot, sv["mid"], dy, "a_out_" + t)
            dmid = _mm("nt", dy, wg["a_w_out"], F32, "dg_a_out_" + t, bl=slot)
            dpre, dcw = _a_mid_bwd(sv["pre"], dmid, full["a_conv_w"][slot], "a_mid_b_" + t)
            gsmall["a_conv_w"][slot] = dcw
            wgrad("a_w_in", slot, sv["h1"], dpre, "a_in_" + t, b_parts=3)
            dh = _mm("nt", dpre, wg["a_w_in"], F32, "dg_a_in_" + t, bl=slot, a_parts=3)
        elif kind == 1:
            wgrad("b_w_out", slot, sv["mid"], dy, "b_out_" + t)
            dmid = _mm("nt", dy, wg["b_w_out"], F32, "dg_b_out_" + t, bl=slot)
            dpre, dvg, dvb, dws, dsbt = _b_mid_bwd(sv["pre"], dmid, *b_params(slot), "b_mid_b_" + t)
            grepl[slot] = (dvg, dvb, dws, dsbt[:, :b_s_bias.shape[1]].T)
            wgrad("b_w_in", slot, sv["h1"], dpre, "b_in_" + t, b_parts=2)
            dh = _mm("nt", dpre, wg["b_w_in"], F32, "dg_b_in_" + t, bl=slot, a_parts=2)
        else:
            wgrad("c_w_out", slot, sv["mid"], dy, "c_out_" + t)
            dmid = _mm("nt", dy, wg["c_w_out"], F32, "dg_c_out_" + t, bl=slot)
            dy2, dlg, dlb = _c_ln_bwd(sv["cy2"], dmid, vec(full["c_ln_g"][slot]), vec(full["c_ln_b"][slot]), "c_ln_b_" + t)
            dpre, dcw, dcb = _c_conv_bwd(sv["pre"], dy2, full["c_conv_w"][slot], "c_conv_b_" + t)
            gsmall["c_conv_w"][slot], gsmall["c_conv_b"][slot] = dcw, dcb
            gsmall["c_ln_g"][slot], gsmall["c_ln_b"][slot] = dlg, dlb
            wgrad("c_w_in", slot, sv["h1"], dpre, "c_in_" + t, b_parts=2)
            dh = _mm("nt", dpre, wg["c_w_in"], F32, "dg_c_in_" + t, bl=slot, a_parts=2)
        dx, dg_pre = _rms_bwd(sv["x0"], vec(full["mix_norm"][i, 0]), dh, dx, F32, "rmsb_mix_pre_" + t)
        gsmall["mix_norm"][i] = jnp.concatenate([dg_pre, dg_post], axis=0)
        if rs_state is not None:
            rs_end(rs_state, dx, i + 1)
        rs_state = rs_begin(i)
        dx, _ = lax.optimization_barrier((dx, rs_state[-1]))
    rs_end(rs_state, rs_state[-1], 0)
    grad_x = dx.reshape(x.shape)

    grads = dict(gfin)
    small_g = [jnp.concatenate(gsmall[n], axis=0).reshape(-1, d) for n in SMALL_SHARDED]
    n_b = b_v_g.shape[0]
    repl_g = [jnp.concatenate([grepl[sl][k] for sl in range(n_b)], axis=0) for k in range(4)]
    repl_rows = []
    for g_arr in repl_g:
        flat = g_arr.reshape(-1)
        flat = jnp.concatenate([flat, jnp.zeros(((-flat.shape[0]) % d,), F32)])
        repl_rows.append(flat.reshape(-1, d))
    rows_all = [pad8(t) for t in small_g + repl_rows]
    total = _sum_slots(_gather8(jnp.concatenate(rows_all, axis=0), "gather_small_grads"), F32, "sum_small_grads")
    off = 0
    for n, cnt in zip(SMALL_SHARDED, counts):
        blk = lax.dynamic_slice_in_dim(total[off:off + cnt], me * ds, ds, axis=1)
        grads[n] = blk.reshape(w[n].shape)
        off += cnt + (-cnt) % 8
    for n, g_arr in zip(SMALL_REPL, repl_g):
        cnt = -(-g_arr.size // d)
        grads[n] = total[off:off + cnt].reshape(-1)[:g_arr.size].reshape(w[n].shape)
        off += cnt + (-cnt) % 8

    delta, new_m, new_v = {}, {}, {}
    for n in WEIGHTS:
        delta[n], new_m[n], new_v[n] = _adamw(w[n], grads[n], given["m_" + n], given["v_" + n], "adamw_" + n)
    return (loss, grad_x, *[grads[n] for n in WEIGHTS], *[delta[n] for n in WEIGHTS], *[new_m[n] for n in WEIGHTS],
            *[new_v[n] for n in WEIGHTS])
```

```python
import functools

import jax
import jax.numpy as jnp
from jax import lax
from jax.experimental import pallas as pl
from jax.experimental.pallas import tpu as pltpu

F32 = jnp.float32
BF16 = jnp.bfloat16
EPS = 1e-6
XA_HEADS = 4
CHUNK = 128
GMLP_GROUPS = 8
ADAM_LR, ADAM_B1, ADAM_B2, ADAM_EPS, ADAM_WD, ADAM_STEP = 0.001, 0.9, 0.999, 1e-08, 0.01, 10
VMEM_LIMIT_V7X = 48 * 1024 * 1024
HBM = pl.BlockSpec(memory_space=pltpu.HBM)
MESH = pl.DeviceIdType.MESH
N_CHIPS = 4
BIG_KINDS = {"xa_wq": "row", "xa_wkv": "col", "xa_wo": "row", "ffn_w_gu": "col", "ffn_w_down": "row",
             "a_w_in": "col", "a_w_out": "row", "b_w_in": "col", "b_w_out": "row", "c_w_in": "col", "c_w_out": "row"}
SMALL_SHARDED = ["mix_norm", "xa_norm", "ffn_norm", "a_conv_w", "c_conv_w", "c_conv_b", "c_ln_g", "c_ln_b"]
SMALL_REPL = ["b_v_g", "b_v_b", "b_w_s", "b_s_bias"]
WEIGHTS = ["mix_norm", "xa_norm", "xa_wq", "xa_wkv", "xa_wo", "ffn_norm", "ffn_w_gu", "ffn_w_down", "a_w_in", "a_conv_w",
           "a_w_out", "b_w_in", "b_v_g", "b_v_b", "b_w_s", "b_s_bias", "b_w_out", "c_w_in", "c_conv_w", "c_conv_b",
           "c_ln_g", "c_ln_b", "c_w_out"]


def _params(*sem):
    return pltpu.CompilerParams(dimension_semantics=sem, vmem_limit_bytes=VMEM_LIMIT_V7X)


def _tile(n, cands=(1024, 512, 256, 128)):
    for c in cands:
        if n % c == 0:
            return c
    return n


def _div_tile(n, cap):
    best = None
    for t in range(128, min(n, cap) + 1, 128):
        if n % t == 0:
            best = t
    return best or n


MM_OUT_TILE_CAP = 1408
MM_K_TILE_CAP = 2816

_DIMS = {"nn": (((1,), (0,)), ((), ())), "nt": (((1,), (1,)), ((), ())), "tn": (((0,), (0,)), ((), ()))}


def _mm(mode, a, b, out_dtype, name, *, bl=None, a_parts=1, b_parts=1, o_parts=1, into=None, into_l=0):
    if isinstance(b, list):
        b, bl = b[bl], 0
    bshape = b.shape[1:] if bl is not None else b.shape
    if mode == "nn":
        mo, c = a.shape
        no = bshape[1]
    elif mode == "nt":
        mo, c = (a.shape[1], a.shape[0] * a.shape[2]) if a_parts > 1 else a.shape
        no = bshape[0]
    else:
        c, mo = a.shape
        no = b.shape[0] * b.shape[2] if b_parts > 1 else bshape[1]
    tmo = _div_tile(mo, MM_OUT_TILE_CAP)
    tno = _div_tile(no // max(o_parts, b_parts), MM_OUT_TILE_CAP)
    tc = _div_tile(c // a_parts, MM_K_TILE_CAP)
    nk = c // tc
    nkp = nk // a_parts
    njp = (no // tno) // max(o_parts, b_parts)
    lead = (None,) if bl is not None else ()
    lidx = (bl,) if bl is not None else ()

    if mode == "nn":
        a_spec = pl.BlockSpec((tmo, tc), lambda i, j, k: (i, k))
        b_spec = pl.BlockSpec(lead + (tc, tno), lambda i, j, k: lidx + (k, j))
    elif mode == "nt":
        if a_parts > 1:
            a_spec = pl.BlockSpec((None, tmo, tc), lambda i, j, k: (k // nkp, i, k % nkp))
        else:
            a_spec = pl.BlockSpec((tmo, tc), lambda i, j, k: (i, k))
        b_spec = pl.BlockSpec(lead + (tno, tc), lambda i, j, k: lidx + (j, k))
    else:
        a_spec = pl.BlockSpec((tc, tmo), lambda i, j, k: (k, i))
        if b_parts > 1:
            b_spec = pl.BlockSpec((None, tc, tno), lambda i, j, k: (j // njp, k, j % njp))
        else:
            b_spec = pl.BlockSpec((tc, tno), lambda i, j, k: (k, j))

    in_specs = [a_spec, b_spec]
    args = [a, b]
    aliases = {}
    if into is not None:
        out_shape = jax.ShapeDtypeStruct(into.shape, into.dtype)
        out_spec = pl.BlockSpec((None, tmo, tno), lambda i, j, k: (into_l, i, j))
        in_specs.append(HBM)
        args.append(into)
        aliases = {2: 0}
    elif o_parts > 1:
        out_shape = jax.ShapeDtypeStruct((o_parts, mo, no // o_parts), out_dtype)
        out_spec = pl.BlockSpec((None, tmo, tno), lambda i, j, k: (j // njp, i, j % njp))
    else:
        out_shape = jax.ShapeDtypeStruct((mo, no), out_dtype)
        out_spec = pl.BlockSpec((tmo, tno), lambda i, j, k: (i, j))
    dims = _DIMS[mode]

    def body(a_ref, b_ref, *rest):
        if nk == 1:
            o_ref = rest[-1]
            o_ref[...] = lax.dot_general(a_ref[...], b_ref[...], dims, preferred_element_type=F32).astype(o_ref.dtype)
            return
        o_ref, acc = rest[-2], rest[-1]
        k = pl.program_id(2)
        part = lax.dot_general(a_ref[...], b_ref[...], dims, preferred_element_type=F32)

        @pl.when(k == 0)
        def _():
            acc[...] = part

        @pl.when(jnp.logical_and(k > 0, k < nk - 1))
        def _():
            acc[...] += part

        @pl.when(k == nk - 1)
        def _():
            o_ref[...] = (acc[...] + part).astype(o_ref.dtype)

    return pl.pallas_call(
        body, name=name, out_shape=out_shape, grid=(mo // tmo, no // tno, nk), in_specs=in_specs, out_specs=out_spec,
        scratch_shapes=[pltpu.VMEM((tmo, tno), F32)] if nk > 1 else [], input_output_aliases=aliases,
        compiler_params=_params("parallel", "parallel", "arbitrary"))(*args)


def _ew(fn, ins, out_dtypes, name):
    rows, cols = ins[0].shape
    tr = rows
    for cand in (512, 256, 128, 64, 32, 16):
        if rows % cand == 0 and cand * cols * 4 <= (1 << 20):
            tr = cand
            break
    spec = pl.BlockSpec((tr, cols), lambda i: (i, 0))
    n_in = len(ins)

    def body(*refs):
        outs = fn(*[r[...] for r in refs[:n_in]])
        for o_ref, o in zip(refs[n_in:], outs):
            o_ref[...] = o.astype(o_ref.dtype)

    return pl.pallas_call(
        body, name=name, out_shape=[jax.ShapeDtypeStruct((rows, cols), d) for d in out_dtypes], grid=(rows // tr,),
        in_specs=[spec] * n_in, out_specs=[spec] * len(out_dtypes), compiler_params=_params("parallel"))(*ins)


def _adamw_fn(w, g, m, v):
    m = ADAM_B1 * m + (1.0 - ADAM_B1) * g
    v = ADAM_B2 * v + (1.0 - ADAM_B2) * (g * g)
    m_hat = m / (1.0 - ADAM_B1 ** ADAM_STEP)
    v_hat = v / (1.0 - ADAM_B2 ** ADAM_STEP)
    delta = -ADAM_LR * (m_hat / (jnp.sqrt(v_hat) + ADAM_EPS) + ADAM_WD * w)
    return delta, m, v


def _adamw(w, g, m, v, name):
    shape = w.shape
    cols = shape[-1]
    flat = [t.reshape(-1, cols) for t in (w, g, m, v)]
    outs = _ew(_adamw_fn, flat, [F32] * 3, name)
    return [o.reshape(shape) for o in outs]


def _row_tile(s):
    return _tile(s, (256, 128, 64, 32, 16, 8))


def _rms_fwd(x, g, name, dep=None):
    s, d = x.shape
    r = _row_tile(s)
    deps = [] if dep is None else [dep]

    def body(x_ref, g_ref, *rest):
        o_ref = rest[-1]
        xv = x_ref[...]
        o_ref[...] = (xv * lax.rsqrt(jnp.mean(xv * xv, axis=-1, keepdims=True) + EPS) * g_ref[...]).astype(BF16)

    return pl.pallas_call(
        body, name=name, out_shape=jax.ShapeDtypeStruct((s, d), BF16), grid=(s // r,),
        in_specs=[pl.BlockSpec((r, d), lambda i: (i, 0)), pl.BlockSpec((1, d), lambda i: (0, 0))] + [ANY] * len(deps),
        out_specs=pl.BlockSpec((r, d), lambda i: (i, 0)), compiler_params=_params("parallel"))(x, g, *deps)


def _res_rms_fwd(x, y, g, name):
    s, d = x.shape
    r = _row_tile(s)

    def body(x_ref, y_ref, g_ref, o_ref):
        yv = y_ref[...]
        o_ref[...] = x_ref[...] + yv * lax.rsqrt(jnp.mean(yv * yv, axis=-1, keepdims=True) + EPS) * g_ref[...]

    row = pl.BlockSpec((r, d), lambda i: (i, 0))
    return pl.pallas_call(
        body, name=name, out_shape=jax.ShapeDtypeStruct((s, d), F32), grid=(s // r,),
        in_specs=[row, row, pl.BlockSpec((1, d), lambda i: (0, 0))], out_specs=row,
        compiler_params=_params("parallel"))(x, y, g)


def _rms_bwd(x, g, dy, resid, out_dtype, name, dep=None):
    s, d = x.shape
    r = _row_tile(s)
    has_res = resid is not None
    deps = [] if dep is None else [dep]

    def body(*refs):
        x_ref, g_ref, dy_ref = refs[:3]
        dx_ref, dg_ref = refs[-2:]
        i = pl.program_id(0)
        xv = x_ref[...]
        dyv = dy_ref[...].astype(F32)
        rstd = lax.rsqrt(jnp.mean(xv * xv, axis=-1, keepdims=True) + EPS)
        n = xv * rstd
        dn = dyv * g_ref[...]
        dx = rstd * (dn - n * jnp.mean(dn * n, axis=-1, keepdims=True))
        if has_res:
            dx = dx + refs[3][...]
        dx_ref[...] = dx.astype(dx_ref.dtype)
        part = jnp.sum(dyv * n, axis=0, keepdims=True)

        @pl.when(i == 0)
        def _():
            dg_ref[...] = part

        @pl.when(i > 0)
        def _():
            dg_ref[...] += part

    row = pl.BlockSpec((r, d), lambda i: (i, 0))
    vec = pl.BlockSpec((1, d), lambda i: (0, 0))
    ins = [x, g, dy] + ([resid] if has_res else []) + deps
    return pl.pallas_call(
        body, name=name, out_shape=[jax.ShapeDtypeStruct((s, d), out_dtype), jax.ShapeDtypeStruct((1, d), F32)],
        grid=(s // r,), in_specs=[row, vec, row] + ([row] if has_res else []) + [ANY] * len(deps), out_specs=[row, vec],
        compiler_params=_params("arbitrary"))(*ins)


def _loss(y, t, name):
    s, d = y.shape
    r = _row_tile(s)

    def body(y_ref, t_ref, l_ref, dy_ref):
        i = pl.program_id(0)
        e = y_ref[...] - t_ref[...]
        dy_ref[...] = e * (1.0 / d)
        part = jnp.full((8, 128), 0.5 * jnp.sum(jnp.mean(e * e, axis=-1, keepdims=True)), F32)

        @pl.when(i == 0)
        def _():
            l_ref[...] = part

        @pl.when(i > 0)
        def _():
            l_ref[...] += part

    row = pl.BlockSpec((r, d), lambda i: (i, 0))
    return pl.pallas_call(
        body, name=name, out_shape=[jax.ShapeDtypeStruct((8, 128), F32), jax.ShapeDtypeStruct((s, d), F32)],
        grid=(s // r,), in_specs=[row, row], out_specs=[pl.BlockSpec((8, 128), lambda i: (0, 0)), row],
        compiler_params=_params("arbitrary"))(y, t)


def _rows(xv, a, m, cache):
    r = a % 8
    q = a - r
    if r == 0:
        return xv[q:q + m]
    if r not in cache:
        cache[r] = pltpu.roll(xv, xv.shape[0] - r, 0)
    return cache[r][q:q + m]


def _conv_taps(xv, w, k_w, halo, m, flip):
    cache = {}
    acc = None
    for k in range(k_w):
        a = (k_w - 1 - k) if flip else (halo + k - (k_w - 1))
        term = w[k:k + 1, :] * _rows(xv, a, m, cache)
        acc = term if acc is None else acc + term
    return acc


def _conv_wgrad(dw_ref, dyv, xv, k_w, halo, m):
    cache = {}
    for k in range(k_w):
        xs = _rows(xv, halo + k - (k_w - 1), m, cache)
        dw_ref[pl.ds(k, 1), :] += jnp.sum(dyv * xs, axis=0, keepdims=True)


def _conv_tiles(s, dp, halo):
    r = _tile(s, (256, 128))
    cw = _tile(dp, (256, 128))
    return r, cw, r // halo


A_HALO = 8


def _a_mid_fwd(bcz3, w, name):
    _, s, d = bcz3.shape
    r, cw, rh = _conv_tiles(s, d, A_HALO)
    k_w = w.shape[0]

    def body(m_ref, h_ref, w_ref, o_ref):
        i = pl.program_id(0)
        cz = m_ref[1].astype(F32) * m_ref[2].astype(F32)
        hcz = h_ref[1].astype(F32) * h_ref[2].astype(F32)
        hcz = jnp.where(i == 0, 0.0, hcz)
        xv = jnp.concatenate([hcz, cz], axis=0)
        y = _conv_taps(xv, w_ref[...], k_w, A_HALO, r, False)
        o_ref[...] = (m_ref[0].astype(F32) * y).astype(BF16)

    return pl.pallas_call(
        body, name=name, out_shape=jax.ShapeDtypeStruct((s, d), BF16), grid=(s // r, d // cw),
        in_specs=[pl.BlockSpec((3, r, cw), lambda i, j: (0, i, j)),
                  pl.BlockSpec((3, A_HALO, cw), lambda i, j: (0, jnp.maximum(i * rh - 1, 0), j)),
                  pl.BlockSpec((k_w, cw), lambda i, j: (0, j))],
        out_specs=pl.BlockSpec((r, cw), lambda i, j: (i, j)), compiler_params=_params("parallel", "parallel"))(bcz3, bcz3, w)


def _a_mid_bwd(bcz3, dgated, w, name):
    _, s, d = bcz3.shape
    r, cw, rh = _conv_tiles(s, d, A_HALO)
    k_w = w.shape[0]
    ni = s // r
    last_h = s // A_HALO - 1

    def body(m_ref, hp_ref, hn_ref, dg_ref, dgn_ref, w_ref, o_ref, dw_ref):
        i = pl.program_id(1)
        wv = w_ref[...]
        b = m_ref[0].astype(F32)
        c = m_ref[1].astype(F32)
        z = m_ref[2].astype(F32)
        hcz = jnp.where(i == 0, 0.0, hp_ref[1].astype(F32) * hp_ref[2].astype(F32))
        xv = jnp.concatenate([hcz, c * z], axis=0)
        y = _conv_taps(xv, wv, k_w, A_HALO, r, False)
        dg = dg_ref[...].astype(F32)
        dy = dg * b
        dyn = jnp.where(i == ni - 1, 0.0, dgn_ref[...].astype(F32) * hn_ref[0].astype(F32))
        dcz = _conv_taps(jnp.concatenate([dy, dyn], axis=0), wv, k_w, A_HALO, r, True)
        o_ref[0] = (dg * y).astype(BF16)
        o_ref[1] = (dcz * z).astype(BF16)
        o_ref[2] = (dcz * c).astype(BF16)

        @pl.when(i == 0)
        def _():
            dw_ref[...] = jnp.zeros_like(dw_ref)

        _conv_wgrad(dw_ref, dy, xv, k_w, A_HALO, r)

    return pl.pallas_call(
        body, name=name, out_shape=[jax.ShapeDtypeStruct((3, s, d), BF16), jax.ShapeDtypeStruct((k_w, d), F32)],
        grid=(d // cw, ni),
        in_specs=[pl.BlockSpec((3, r, cw), lambda j, i: (0, i, j)),
                  pl.BlockSpec((3, A_HALO, cw), lambda j, i: (0, jnp.maximum(i * rh - 1, 0), j)),
                  pl.BlockSpec((3, A_HALO, cw), lambda j, i: (0, jnp.minimum((i + 1) * rh, last_h), j)),
                  pl.BlockSpec((r, cw), lambda j, i: (i, j)),
                  pl.BlockSpec((A_HALO, cw), lambda j, i: (jnp.minimum((i + 1) * rh, last_h), j)),
                  pl.BlockSpec((k_w, cw), lambda j, i: (0, j))],
        out_specs=[pl.BlockSpec((3, r, cw), lambda j, i: (0, i, j)), pl.BlockSpec((k_w, cw), lambda j, i: (0, j))],
        compiler_params=_params("parallel", "arbitrary"))(bcz3, bcz3, bcz3, dgated, dgated, w)


C_HALO = 32


def _c_conv_fwd(ag3, w, bias, name):
    _, s, d = ag3.shape
    r, cw, rh = _conv_tiles(s, d, C_HALO)
    k_w = w.shape[0]

    def body(m_ref, h_ref, w_ref, b_ref, o_ref):
        i = pl.program_id(0)
        y1 = m_ref[0].astype(F32) * jax.nn.sigmoid(m_ref[1].astype(F32))
        h1 = jnp.where(i == 0, 0.0, h_ref[0].astype(F32) * jax.nn.sigmoid(h_ref[1].astype(F32)))
        xv = jnp.concatenate([h1, y1], axis=0)
        o_ref[...] = _conv_taps(xv, w_ref[...], k_w, C_HALO, r, False) + b_ref[...]

    return pl.pallas_call(
        body, name=name, out_shape=jax.ShapeDtypeStruct((s, d), F32), grid=(s // r, d // cw),
        in_specs=[pl.BlockSpec((2, r, cw), lambda i, j: (0, i, j)),
                  pl.BlockSpec((2, C_HALO, cw), lambda i, j: (0, jnp.maximum(i * rh - 1, 0), j)),
                  pl.BlockSpec((k_w, cw), lambda i, j: (0, j)), pl.BlockSpec((1, cw), lambda i, j: (0, j))],
        out_specs=pl.BlockSpec((r, cw), lambda i, j: (i, j)),
        compiler_params=_params("parallel", "parallel"))(ag3, ag3, w, bias)


def _c_conv_bwd(ag3, dy2, w, name):
    _, s, d = ag3.shape
    r, cw, rh = _conv_tiles(s, d, C_HALO)
    k_w = w.shape[0]
    ni = s // r
    last_h = s // C_HALO - 1

    def body(m_ref, hp_ref, dy_ref, dyn_ref, w_ref, o_ref, dw_ref, db_ref):
        i = pl.program_id(1)
        wv = w_ref[...]
        a = m_ref[0].astype(F32)
        sg = jax.nn.sigmoid(m_ref[1].astype(F32))
        h1 = jnp.where(i == 0, 0.0, hp_ref[0].astype(F32) * jax.nn.sigmoid(hp_ref[1].astype(F32)))
        xv = jnp.concatenate([h1, a * sg], axis=0)
        dy = dy_ref[...]
        dyn = jnp.where(i == ni - 1, 0.0, dyn_ref[...])
        dy1 = _conv_taps(jnp.concatenate([dy, dyn], axis=0), wv, k_w, C_HALO, r, True)
        o_ref[0] = (dy1 * sg).astype(BF16)
        o_ref[1] = (dy1 * a * sg * (1.0 - sg)).astype(BF16)

        @pl.when(i == 0)
        def _():
            dw_ref[...] = jnp.zeros_like(dw_ref)
            db_ref[...] = jnp.zeros_like(db_ref)

        db_ref[...] += jnp.sum(dy, axis=0, keepdims=True)
        _conv_wgrad(dw_ref, dy, xv, k_w, C_HALO, r)

    return pl.pallas_call(
        body, name=name,
        out_shape=[jax.ShapeDtypeStruct((2, s, d), BF16), jax.ShapeDtypeStruct((k_w, d), F32),
                   jax.ShapeDtypeStruct((1, d), F32)],
        grid=(d // cw, ni),
        in_specs=[pl.BlockSpec((2, r, cw), lambda j, i: (0, i, j)),
                  pl.BlockSpec((2, C_HALO, cw), lambda j, i: (0, jnp.maximum(i * rh - 1, 0), j)),
                  pl.BlockSpec((r, cw), lambda j, i: (i, j)),
                  pl.BlockSpec((C_HALO, cw), lambda j, i: (jnp.minimum((i + 1) * rh, last_h), j)),
                  pl.BlockSpec((k_w, cw), lambda j, i: (0, j))],
        out_specs=[pl.BlockSpec((2, r, cw), lambda j, i: (0, i, j)), pl.BlockSpec((k_w, cw), lambda j, i: (0, j)),
                   pl.BlockSpec((1, cw), lambda j, i: (0, j))],
        compiler_params=_params("parallel", "arbitrary"))(ag3, ag3, dy2, dy2, w)


def _ln_stats(v):
    mu = jnp.mean(v, axis=-1, keepdims=True)
    vc = v - mu
    rstd = lax.rsqrt(jnp.mean(vc * vc, axis=-1, keepdims=True) + EPS)
    return vc * rstd, rstd


def _ln_bwd(dn, g, xh, rstd):
    dxh = dn * g
    return rstd * (dxh - jnp.mean(dxh, axis=-1, keepdims=True) - xh * jnp.mean(dxh * xh, axis=-1, keepdims=True))


def _c_ln_fwd(y2, g, b, name):
    s, d = y2.shape
    r = _row_tile(s)

    def body(y_ref, g_ref, b_ref, o_ref):
        xh, _ = _ln_stats(y_ref[...])
        y3 = xh * g_ref[...] + b_ref[...]
        o_ref[...] = (y3 * jax.nn.sigmoid(y3)).astype(BF16)

    row = pl.BlockSpec((r, d), lambda i: (i, 0))
    vec = pl.BlockSpec((1, d), lambda i: (0, 0))
    return pl.pallas_call(
        body, name=name, out_shape=jax.ShapeDtypeStruct((s, d), BF16), grid=(s // r,), in_specs=[row, vec, vec],
        out_specs=row, compiler_params=_params("parallel"))(y2, g, b)


def _c_ln_bwd(y2, dout, g, b, name):
    s, d = y2.shape
    r = _row_tile(s)

    def body(y_ref, do_ref, g_ref, b_ref, dy_ref, dg_ref, db_ref):
        i = pl.program_id(0)
        xh, rstd = _ln_stats(y_ref[...])
        gv = g_ref[...]
        y3 = xh * gv + b_ref[...]
        sg = jax.nn.sigmoid(y3)
        dy3 = do_ref[...].astype(F32) * (sg + y3 * sg * (1.0 - sg))
        dy_ref[...] = _ln_bwd(dy3, gv, xh, rstd)

        @pl.when(i == 0)
        def _():
            dg_ref[...] = jnp.zeros_like(dg_ref)
            db_ref[...] = jnp.zeros_like(db_ref)

        dg_ref[...] += jnp.sum(dy3 * xh, axis=0, keepdims=True)
        db_ref[...] += jnp.sum(dy3, axis=0, keepdims=True)

    row = pl.BlockSpec((r, d), lambda i: (i, 0))
    vec = pl.BlockSpec((1, d), lambda i: (0, 0))
    return pl.pallas_call(
        body, name=name,
        out_shape=[jax.ShapeDtypeStruct((s, d), F32), jax.ShapeDtypeStruct((1, d), F32), jax.ShapeDtypeStruct((1, d), F32)],
        grid=(s // r,), in_specs=[row, row, vec, vec], out_specs=[row, vec, vec],
        compiler_params=_params("arbitrary"))(y2, dout, g, b)


_GELU_C = 0.7978845608028654
_GELU_A = 0.044715


def _gelu(x):
    return 0.5 * x * (1.0 + jnp.tanh(_GELU_C * (x + _GELU_A * x * x * x)))


def _gelu_grad(x):
    t = jnp.tanh(_GELU_C * (x + _GELU_A * x * x * x))
    return 0.5 * (1.0 + t) + 0.5 * x * (1.0 - t * t) * _GELU_C * (1.0 + 3.0 * _GELU_A * x * x)


def _b_mid_fwd(uv3, vg, vb, ws_m, sbt, name):
    _, s, h = uv3.shape
    g_n, t, _ = ws_m.shape
    gd = h // g_n

    def body(uv_ref, vg_ref, vb_ref, ws_ref, sb_ref, o_ref):
        u = _gelu(uv_ref[0].astype(F32))
        xh, _ = _ln_stats(_gelu(uv_ref[1].astype(F32)))
        vn = (xh * vg_ref[...] + vb_ref[...]).astype(BF16)
        for g in range(g_n):
            sl = slice(g * gd, (g + 1) * gd)
            sv = jnp.dot(ws_ref[g], vn[:, sl], preferred_element_type=F32) + sb_ref[:, g:g + 1]
            o_ref[:, sl] = (u[:, sl] * sv).astype(BF16)

    vec = pl.BlockSpec((1, h), lambda i: (0, 0))
    return pl.pallas_call(
        body, name=name, out_shape=jax.ShapeDtypeStruct((s, h), BF16), grid=(s // t,),
        in_specs=[pl.BlockSpec((2, t, h), lambda i: (0, i, 0)), vec, vec,
                  pl.BlockSpec((g_n, t, t), lambda i: (0, 0, 0)), pl.BlockSpec((t, 128), lambda i: (0, 0))],
        out_specs=pl.BlockSpec((t, h), lambda i: (i, 0)), compiler_params=_params("parallel"))(uv3, vg, vb, ws_m, sbt)


def _b_mid_bwd(uv3, dgated, vg, vb, ws_m, sbt, name):
    _, s, h = uv3.shape
    g_n, t, _ = ws_m.shape
    gd = h // g_n

    def body(uv_ref, dg_ref, vg_ref, vb_ref, ws_ref, sb_ref, o_ref, dvg_ref, dvb_ref, dws_ref, dsb_ref, dvn_ref):
        i = pl.program_id(0)

        @pl.when(i == 0)
        def _():
            dvg_ref[...] = jnp.zeros_like(dvg_ref)
            dvb_ref[...] = jnp.zeros_like(dvb_ref)
            dws_ref[...] = jnp.zeros_like(dws_ref)
            dsb_ref[...] = jnp.zeros_like(dsb_ref)

        upre = uv_ref[0].astype(F32)
        vpre = uv_ref[1].astype(F32)
        u = _gelu(upre)
        xh, rstd = _ln_stats(_gelu(vpre))
        gv = vg_ref[...]
        vn = (xh * gv + vb_ref[...]).astype(BF16)
        causal = lax.broadcasted_iota(jnp.int32, (t, t), 0) >= lax.broadcasted_iota(jnp.int32, (t, t), 1)
        lane = lax.broadcasted_iota(jnp.int32, (t, 128), 1)
        for g in range(g_n):
            sl = slice(g * gd, (g + 1) * gd)
            wsg = ws_ref[g]
            sv = jnp.dot(wsg, vn[:, sl], preferred_element_type=F32) + sb_ref[:, g:g + 1]
            dg = dg_ref[:, sl].astype(F32)
            o_ref[0, :, sl] = (dg * sv * _gelu_grad(upre[:, sl])).astype(BF16)
            dsv = dg * u[:, sl]
            dsvb = dsv.astype(BF16)
            dsb_ref[...] += jnp.where(lane == g, jnp.sum(dsv, axis=1, keepdims=True), 0.0)
            dws = lax.dot_general(dsvb, vn[:, sl], _DIMS["nt"], preferred_element_type=F32)
            dws_ref[g] += jnp.where(causal, dws, 0.0)
            dvn_ref[:, sl] = lax.dot_general(wsg, dsvb, _DIMS["tn"], preferred_element_type=F32)
        dvn = dvn_ref[...]
        dvg_ref[...] += jnp.sum(dvn * xh, axis=0, keepdims=True)
        dvb_ref[...] += jnp.sum(dvn, axis=0, keepdims=True)
        o_ref[1] = (_ln_bwd(dvn, gv, xh, rstd) * _gelu_grad(vpre)).astype(BF16)

    vec = pl.BlockSpec((1, h), lambda i: (0, 0))
    return pl.pallas_call(
        body, name=name,
        out_shape=[jax.ShapeDtypeStruct((2, s, h), BF16), jax.ShapeDtypeStruct((1, h), F32), jax.ShapeDtypeStruct((1, h), F32),
                   jax.ShapeDtypeStruct((g_n, t, t), F32), jax.ShapeDtypeStruct((t, 128), F32)],
        grid=(s // t,),
        in_specs=[pl.BlockSpec((2, t, h), lambda i: (0, i, 0)), pl.BlockSpec((t, h), lambda i: (i, 0)), vec, vec,
                  pl.BlockSpec((g_n, t, t), lambda i: (0, 0, 0)), pl.BlockSpec((t, 128), lambda i: (0, 0))],
        out_specs=[pl.BlockSpec((2, t, h), lambda i: (0, i, 0)), vec, vec,
                   pl.BlockSpec((g_n, t, t), lambda i: (0, 0, 0)), pl.BlockSpec((t, 128), lambda i: (0, 0))],
        scratch_shapes=[pltpu.VMEM((t, h), F32)],
        compiler_params=_params("arbitrary"))(uv3, dgated, vg, vb, ws_m, sbt)


def _softmax_rows(sc):
    e = jnp.exp(sc - jnp.max(sc, axis=-1, keepdims=True))
    return e / jnp.sum(e, axis=-1, keepdims=True)


def _attn_fwd(q, kv3, name):
    s, d = q.shape
    m = kv3.shape[1]
    dh = d // XA_HEADS
    scale = dh ** -0.5
    r = _row_tile(s)

    def body(q_ref, kv_ref, o_ref):
        for hd in range(XA_HEADS):
            sl = slice(hd * dh, (hd + 1) * dh)
            sc = lax.dot_general(q_ref[:, sl], kv_ref[0, :, sl], _DIMS["nt"], preferred_element_type=F32) * scale
            p = _softmax_rows(sc).astype(BF16)
            o_ref[:, sl] = jnp.dot(p, kv_ref[1, :, sl], preferred_element_type=F32).astype(BF16)

    return pl.pallas_call(
        body, name=name, out_shape=jax.ShapeDtypeStruct((s, d), BF16), grid=(s // r,),
        in_specs=[pl.BlockSpec((r, d), lambda i: (i, 0)), pl.BlockSpec((2, m, d), lambda i: (0, 0, 0))],
        out_specs=pl.BlockSpec((r, d), lambda i: (i, 0)), compiler_params=_params("parallel"))(q, kv3)


def _attn_bwd(q, kv3, do, name):
    s, d = q.shape
    m = kv3.shape[1]
    dh = d // XA_HEADS
    scale = dh ** -0.5
    r = _row_tile(s)

    def body(q_ref, kv_ref, do_ref, dq_ref, dkv_ref):
        i = pl.program_id(0)

        @pl.when(i == 0)
        def _():
            dkv_ref[...] = jnp.zeros_like(dkv_ref)

        for hd in range(XA_HEADS):
            sl = slice(hd * dh, (hd + 1) * dh)
            qh = q_ref[:, sl]
            kh = kv_ref[0, :, sl]
            doh = do_ref[:, sl]
            sc = lax.dot_general(qh, kh, _DIMS["nt"], preferred_element_type=F32) * scale
            p = _softmax_rows(sc)
            pb = p.astype(BF16)
            dkv_ref[1, :, sl] += lax.dot_general(pb, doh, _DIMS["tn"], preferred_element_type=F32)
            dp = lax.dot_general(doh, kv_ref[1, :, sl], _DIMS["nt"], preferred_element_type=F32)
            ds = (p * (dp - jnp.sum(dp * p, axis=-1, keepdims=True)) * scale).astype(BF16)
            dq_ref[:, sl] = jnp.dot(ds, kh, preferred_element_type=F32).astype(BF16)
            dkv_ref[0, :, sl] += lax.dot_general(ds, qh, _DIMS["tn"], preferred_element_type=F32)

    row = pl.BlockSpec((r, d), lambda i: (i, 0))
    kvs = pl.BlockSpec((2, m, d), lambda i: (0, 0, 0))
    return pl.pallas_call(
        body, name=name, out_shape=[jax.ShapeDtypeStruct((s, d), BF16), jax.ShapeDtypeStruct((2, m, d), F32)],
        grid=(s // r,), in_specs=[row, kvs, row], out_specs=[row, kvs], compiler_params=_params("arbitrary"))(q, kv3, do)


def _swiglu_fwd(gu3, name):
    _, s, f = gu3.shape
    r = _row_tile(s)
    cw = _tile(f, (512, 256, 128))

    def body(gu_ref, o_ref):
        gate = gu_ref[0].astype(F32)
        o_ref[...] = (gate * jax.nn.sigmoid(gate) * gu_ref[1].astype(F32)).astype(BF16)

    return pl.pallas_call(
        body, name=name, out_shape=jax.ShapeDtypeStruct((s, f), BF16), grid=(s // r, f // cw),
        in_specs=[pl.BlockSpec((2, r, cw), lambda i, j: (0, i, j))], out_specs=pl.BlockSpec((r, cw), lambda i, j: (i, j)),
        compiler_params=_params("parallel", "parallel"))(gu3)


def _swiglu_bwd(gu3, dact, name):
    _, s, f = gu3.shape
    r = _row_tile(s)
    cw = _tile(f, (512, 256, 128))

    def body(gu_ref, da_ref, o_ref):
        gate = gu_ref[0].astype(F32)
        up = gu_ref[1].astype(F32)
        da = da_ref[...].astype(F32)
        sg = jax.nn.sigmoid(gate)
        o_ref[0] = (da * up * (sg + gate * sg * (1.0 - sg))).astype(BF16)
        o_ref[1] = (da * gate * sg).astype(BF16)

    return pl.pallas_call(
        body, name=name, out_shape=jax.ShapeDtypeStruct((2, s, f), BF16), grid=(s // r, f // cw),
        in_specs=[pl.BlockSpec((2, r, cw), lambda i, j: (0, i, j)), pl.BlockSpec((r, cw), lambda i, j: (i, j))],
        out_specs=pl.BlockSpec((2, r, cw), lambda i, j: (0, i, j)),
        compiler_params=_params("parallel", "parallel"))(gu3, dact)


def _ids():
    x, y, c = lax.axis_index("x"), lax.axis_index("y"), lax.axis_index("c")
    return x, y, c, 2 * x + y


def _chip_peers(x, y):
    return [(d - 1, 2 * (x ^ (d >> 1)) + (y ^ (d & 1)), x ^ (d >> 1), y ^ (d & 1)) for d in (1, 2, 3)]


def _remote(src, dst, ssem, rsem, dev):
    return pltpu.make_async_remote_copy(src_ref=src, dst_ref=dst, send_sem=ssem, recv_sem=rsem, device_id=dev,
                                        device_id_type=MESH)


def _gview(ref, kind, j, cc):
    _, k, n = ref.shape
    if kind == "row":
        return ref.at[:, pl.ds(j * (k // N_CHIPS) + cc * (k // (2 * N_CHIPS)), k // (2 * N_CHIPS)), :]
    return ref.at[:, pl.ds(cc * (k // 2), k // 2), pl.ds(j * (n // N_CHIPS), n // N_CHIPS)]


def _sview(ref, cc):
    r = ref.shape[1]
    return ref.at[:, pl.ds(cc * (r // 2), r // 2), :]


def _comm_call(body, name, ins, out_shapes, n_sems, aliases=None):
    return pl.pallas_call(
        body, name=name, out_shape=out_shapes, in_specs=[HBM] * len(ins), out_specs=[HBM] * len(out_shapes),
        scratch_shapes=[pltpu.SemaphoreType.DMA((n,)) for n in n_sems], input_output_aliases=aliases or {},
        compiler_params=pltpu.CompilerParams(has_side_effects=True))(*ins)


def _mesh_scalars():
    x, y, c = lax.axis_index("x"), lax.axis_index("y"), lax.axis_index("c")
    return jnp.stack([2 * x + y, c]).astype(jnp.int32)


def _slab_rows(rows, cols, itemsize=4):
    for cand in (512, 256, 128, 64, 32, 16):
        if rows % cand == 0 and cand * cols * itemsize <= (2 << 20):
            return cand
    return rows


def _ag_place(shard, layer, kind, name):
    _, r, n = shard.shape
    full = (1, r * N_CHIPS, n) if kind == "row" else (1, r, n * N_CHIPS)
    tr = _slab_rows(r, n)
    nt = r // tr
    if kind == "row":
        out_spec = pl.BlockSpec((None, tr, n), lambda t, s: (0, s[0] * nt + t, 0))
    else:
        out_spec = pl.BlockSpec((None, tr, n), lambda t, s: (0, t, s[0]))

    def body(s_ref, i_ref, o_ref):
        o_ref[...] = i_ref[...].astype(BF16)

    return pl.pallas_call(
        body, name=name, out_shape=jax.ShapeDtypeStruct(full, BF16),
        grid_spec=pltpu.PrefetchScalarGridSpec(
            num_scalar_prefetch=1, grid=(nt,), in_specs=[pl.BlockSpec((None, tr, n), lambda t, s: (layer, t, 0))],
            out_specs=out_spec),
        compiler_params=_params("parallel"))(_mesh_scalars(), shard)


SEM = pl.BlockSpec(memory_space=pltpu.SEMAPHORE)
ANY = pl.BlockSpec(memory_space=pl.ANY)
DATAFLOW = pltpu.SideEffectType.DATAFLOW_SIDE_EFFECTING


def _in_hbm(arrs):
    return [pltpu.with_memory_space_constraint(a, pltpu.HBM) for a in arrs]


def _ag_start(bufs, kinds, name):
    n = len(bufs)

    def body(*refs):
        ssem, rsem, token = refs[n], refs[n + 1], refs[-1]
        x, y, c, me = _ids()
        for t in range(n):
            mine = _gview(refs[t], kinds[t], me, c)
            for d, _, px, py in _chip_peers(x, y):
                _remote(mine, mine, ssem.at[3 * t + d], rsem.at[3 * t + d], (px, py, c)).start()
        token[...] = jnp.zeros_like(token)

    outs = pl.pallas_call(
        body, name=name,
        out_shape=(pltpu.SemaphoreType.DMA((3 * n,)), pltpu.SemaphoreType.DMA((3 * n,)),
                   *[pltpu.HBM(b.shape, b.dtype) for b in bufs], jax.ShapeDtypeStruct((8, 128), F32)),
        in_specs=[HBM] * n, out_specs=(SEM, SEM, *[HBM] * n, pl.BlockSpec(memory_space=pltpu.VMEM)),
        input_output_aliases={t: 2 + t for t in range(n)},
        compiler_params=pltpu.CompilerParams(has_side_effects=DATAFLOW))(*_in_hbm(bufs))
    return outs[0], outs[1], list(outs[2:2 + n]), outs[-1]


def _ag_wait(ssem, rsem, bufs, kinds, after, name):
    n = len(bufs)

    def body(*refs):
        ssem_ref, rsem_ref = refs[n], refs[n + 1]
        x, y, c, me = _ids()
        for t in range(n):
            mine = _gview(refs[t], kinds[t], me, c)
            for d, pj, px, py in _chip_peers(x, y):
                theirs = _gview(refs[t], kinds[t], pj, c)
                _remote(mine, mine, ssem_ref.at[3 * t + d], rsem_ref.at[3 * t + d], (px, py, c)).wait_send()
                _remote(theirs, theirs, ssem_ref.at[3 * t + d], rsem_ref.at[3 * t + d], (px, py, c)).wait_recv()

    outs = pl.pallas_call(
        body, name=name, out_shape=[pltpu.HBM(b.shape, b.dtype) for b in bufs],
        in_specs=[HBM] * n + [SEM, SEM, ANY], out_specs=[HBM] * n, input_output_aliases={t: t for t in range(n)},
        compiler_params=pltpu.CompilerParams(has_side_effects=DATAFLOW))(*bufs, ssem, rsem, after)
    return list(outs)


def _ag_forward(bufs, kinds, name):
    n = len(bufs)

    def body(*refs):
        outs = refs[n:2 * n]
        ssem, rsem = refs[2 * n], refs[2 * n + 1]
        x, y, c, _ = _ids()
        sib = (x, y, 1 - c)
        sends = []
        for t in range(n):
            for d, pj, _, _ in _chip_peers(x, y):
                piece = _gview(outs[t], kinds[t], pj, c)
                sends.append(_remote(piece, piece, ssem.at[3 * t + d], rsem.at[3 * t + d], sib))
        for cp in sends:
            cp.start()
        for t in range(n):
            for d, pj, _, _ in _chip_peers(x, y):
                piece = _gview(outs[t], kinds[t], pj, 1 - c)
                _remote(piece, piece, ssem.at[3 * t + d], rsem.at[3 * t + d], sib).wait_recv()
        for cp in sends:
            cp.wait_send()

    return _comm_call(body, name, bufs, [jax.ShapeDtypeStruct(b.shape, b.dtype) for b in bufs], (3 * n, 3 * n),
                      {t: t for t in range(n)})


def _rs1(g_fulls, kinds, name, dep=None):
    n = len(g_fulls)
    outs = []
    for g, kind in zip(g_fulls, kinds):
        l, k, nn = g.shape
        piece = (l, k // (2 * N_CHIPS), nn) if kind == "row" else (l, k // 2, nn // N_CHIPS)
        outs.append(jax.ShapeDtypeStruct((N_CHIPS,) + piece, g.dtype))

    n_in = n + (dep is not None)

    def body(*refs):
        ssem, rsem = refs[n_in + n], refs[n_in + n + 1]
        x, y, c, _ = _ids()
        sends = [_remote(_gview(refs[t], kinds[t], j, 1 - c), refs[n_in + t].at[j], ssem.at[4 * t + j], rsem.at[4 * t + j],
                         (x, y, 1 - c)) for t in range(n) for j in range(N_CHIPS)]
        for cp in sends:
            cp.start()
        for cp in sends:
            cp.wait()

    return _comm_call(body, name, g_fulls + ([] if dep is None else [dep]), outs, (4 * n, 4 * n))


def _rs_add1(g_full, got, kind, name):
    l, k, n = g_full.shape
    _, _, pr, pc = got.shape
    tr = _slab_rows(pr, pc, 2)
    nt = pr // tr
    if kind == "row":
        g_spec = pl.BlockSpec((None, tr, n), lambda j, li, t, s: (li, (2 * j + s[1]) * nt + t, 0))
    else:
        g_spec = pl.BlockSpec((None, tr, pc), lambda j, li, t, s: (li, s[1] * nt + t, j))
    slot = pl.BlockSpec((None, None, tr, pc), lambda j, li, t, s: (j, li, t, 0))

    def body(s_ref, g_ref, got_ref, o_ref):
        o_ref[...] = (g_ref[...].astype(F32) + got_ref[...].astype(F32)).astype(BF16)

    return pl.pallas_call(
        body, name=name, out_shape=jax.ShapeDtypeStruct(got.shape, BF16),
        grid_spec=pltpu.PrefetchScalarGridSpec(num_scalar_prefetch=1, grid=(N_CHIPS, l, nt), in_specs=[g_spec, slot],
                                               out_specs=slot),
        compiler_params=_params("parallel", "parallel", "parallel"))(_mesh_scalars(), g_full, got)


def _rs2_start(ps, name):
    n = len(ps)
    lands = [lax.empty(p.shape, p.dtype) for p in ps]

    def body(*refs):
        ssem, rsem, token = refs[2 * n], refs[2 * n + 1], refs[-1]
        x, y, c, me = _ids()
        for t in range(n):
            for d, pj, px, py in _chip_peers(x, y):
                _remote(refs[t].at[pj], refs[n + t].at[me], ssem.at[3 * t + d], rsem.at[3 * t + d], (px, py, c)).start()
        token[...] = jnp.zeros_like(token)

    outs = pl.pallas_call(
        body, name=name,
        out_shape=(pltpu.SemaphoreType.DMA((3 * n,)), pltpu.SemaphoreType.DMA((3 * n,)),
                   *[pltpu.HBM(p.shape, p.dtype) for p in ps + lands], jax.ShapeDtypeStruct((8, 128), F32)),
        in_specs=[HBM] * (2 * n), out_specs=(SEM, SEM, *[HBM] * (2 * n), pl.BlockSpec(memory_space=pltpu.VMEM)),
        input_output_aliases={t: 2 + t for t in range(2 * n)},
        compiler_params=pltpu.CompilerParams(has_side_effects=DATAFLOW))(*_in_hbm(ps + lands))
    return outs[0], outs[1], list(outs[2:2 + n]), list(outs[2 + n:2 + 2 * n]), outs[-1]


def _rs2_wait(ssem, rsem, ps, lands, after, name):
    n = len(ps)

    def body(*refs):
        ssem_ref, rsem_ref = refs[2 * n], refs[2 * n + 1]
        x, y, c, me = _ids()
        for t in range(n):
            for d, pj, px, py in _chip_peers(x, y):
                _remote(refs[t].at[pj], refs[n + t].at[me], ssem_ref.at[3 * t + d], rsem_ref.at[3 * t + d], (px, py, c)).wait_send()
                _remote(refs[t].at[pj], refs[n + t].at[pj], ssem_ref.at[3 * t + d], rsem_ref.at[3 * t + d], (px, py, c)).wait_recv()

    outs = pl.pallas_call(
        body, name=name, out_shape=[pltpu.HBM(p.shape, p.dtype) for p in ps + lands],
        in_specs=[HBM] * (2 * n) + [SEM, SEM, ANY], out_specs=[HBM] * (2 * n),
        input_output_aliases={t: t for t in range(2 * n)},
        compiler_params=pltpu.CompilerParams(has_side_effects=DATAFLOW))(*ps, *lands, ssem, rsem, after)
    return list(outs[:n]), list(outs[n:])


def _rs_add2(p, got, into, layer, name):
    _, _, pr, pc = p.shape
    tr = _slab_rows(pr, pc)
    nt = pr // tr

    def slot(d):
        return pl.BlockSpec((None, None, tr, pc), lambda t, s: (s[0] ^ d, 0, t, 0))

    def body(s_ref, p_ref, g1_ref, g2_ref, g3_ref, i_ref, o_ref):
        o_ref[...] = (p_ref[...].astype(F32) + g1_ref[...].astype(F32) + g2_ref[...].astype(F32) + g3_ref[...].astype(F32))

    return pl.pallas_call(
        body, name=name, out_shape=jax.ShapeDtypeStruct(into.shape, F32),
        grid_spec=pltpu.PrefetchScalarGridSpec(
            num_scalar_prefetch=1, grid=(nt,), in_specs=[slot(0), slot(1), slot(2), slot(3), HBM],
            out_specs=pl.BlockSpec((None, tr, pc), lambda t, s: (layer, s[1] * nt + t, 0))),
        input_output_aliases={5: 0},
        compiler_params=_params("parallel"))(_mesh_scalars(), p, got, got, got, into)


def _rs3(shards, layers, name):
    n = len(shards)

    def body(*refs):
        outs = refs[n:2 * n]
        token, ssem, rsem = refs[2 * n], refs[2 * n + 1], refs[2 * n + 2]
        x, y, c, _ = _ids()
        sib = (x, y, 1 - c)
        token[...] = jnp.zeros_like(token)

        def half(t, cc):
            return _sview(outs[t].at[pl.ds(layers[t], 1)], cc)

        sends = [_remote(half(t, c), half(t, c), ssem.at[t], rsem.at[t], sib) for t in range(n)]
        for cp in sends:
            cp.start()
        for t in range(n):
            _remote(half(t, 1 - c), half(t, 1 - c), ssem.at[t], rsem.at[t], sib).wait_recv()
        for cp in sends:
            cp.wait_send()

    outs = pl.pallas_call(
        body, name=name, out_shape=[jax.ShapeDtypeStruct(s.shape, s.dtype) for s in shards] + [jax.ShapeDtypeStruct((8, 128), F32)],
        in_specs=[HBM] * n, out_specs=[HBM] * n + [pl.BlockSpec(memory_space=pltpu.VMEM)],
        scratch_shapes=[pltpu.SemaphoreType.DMA((n,)), pltpu.SemaphoreType.DMA((n,))],
        input_output_aliases={t: t for t in range(n)}, compiler_params=pltpu.CompilerParams(has_side_effects=True))(*shards)
    return list(outs[:n]), outs[n]


def _ag_small(sp, name):
    def body(s_ref, o_ref, ssem, rsem, lsem):
        x, y, c, me = _ids()
        local = pltpu.make_async_copy(s_ref, o_ref.at[me], lsem.at[0])
        local.start()
        sends = [_remote(s_ref, o_ref.at[me], ssem.at[d], rsem.at[d], (px, py, c)) for d, _, px, py in _chip_peers(x, y)]
        for cp in sends:
            cp.start()
        for d, pj, px, py in _chip_peers(x, y):
            _remote(s_ref, o_ref.at[pj], ssem.at[d], rsem.at[d], (px, py, c)).wait_recv()
        for cp in sends:
            cp.wait_send()
        local.wait()

    return _comm_call(body, name, [sp], [jax.ShapeDtypeStruct((N_CHIPS,) + sp.shape, sp.dtype)], (3, 3, 1))[0]


def _gather8(g, name):
    def body(g_ref, o_ref, ssem, rsem, lsem):
        x, y, c, _ = _ids()
        me = 4 * x + 2 * y + c
        local = pltpu.make_async_copy(g_ref, o_ref.at[me], lsem.at[0])
        local.start()
        peers = [(d - 1, x ^ (d >> 2), y ^ ((d >> 1) & 1), c ^ (d & 1)) for d in range(1, 8)]
        sends = [_remote(g_ref, o_ref.at[me], ssem.at[d], rsem.at[d], (px, py, pc)) for d, px, py, pc in peers]
        for cp in sends:
            cp.start()
        for d, px, py, pc in peers:
            _remote(g_ref, o_ref.at[4 * px + 2 * py + pc], ssem.at[d], rsem.at[d], (px, py, pc)).wait_recv()
        for cp in sends:
            cp.wait_send()
        local.wait()

    return _comm_call(body, name, [g], [jax.ShapeDtypeStruct((8,) + g.shape, g.dtype)], (7, 7, 1))[0]


def _sum_slots(a, out_dtype, name):
    n = a.shape[0]
    shape = a.shape[1:]
    cols = shape[-1]
    a3 = a.reshape(n, -1, cols)
    rows = a3.shape[1]
    tr = rows
    for cand in (512, 256, 128, 64, 32, 16):
        if rows % cand == 0 and cand * cols * 4 <= (1 << 20):
            tr = cand
            break

    def body(a_ref, o_ref):
        acc = a_ref[0].astype(F32)
        for j in range(1, n):
            acc = acc + a_ref[j].astype(F32)
        o_ref[...] = acc.astype(o_ref.dtype)

    out = pl.pallas_call(
        body, name=name, out_shape=jax.ShapeDtypeStruct((rows, cols), out_dtype), grid=(rows // tr,),
        in_specs=[pl.BlockSpec((n, tr, cols), lambda i: (0, i, 0))], out_specs=pl.BlockSpec((tr, cols), lambda i: (i, 0)),
        compiler_params=_params("parallel"))(a3)
    return out.reshape(shape)


def kernel(x, mem, mix_norm, xa_norm, xa_wq, xa_wkv, xa_wo, ffn_norm, ffn_w_gu, ffn_w_down, a_w_in, a_conv_w, a_w_out, b_w_in, b_v_g, b_v_b, b_w_s, b_s_bias, b_w_out, c_w_in, c_conv_w, c_conv_b, c_ln_g, c_ln_b, c_w_out, loss_target, m_mix_norm, m_xa_norm, m_xa_wq, m_xa_wkv, m_xa_wo, m_ffn_norm, m_ffn_w_gu, m_ffn_w_down, m_a_w_in, m_a_conv_w, m_a_w_out, m_b_w_in, m_b_v_g, m_b_v_b, m_b_w_s, m_b_s_bias, m_b_w_out, m_c_w_in, m_c_conv_w, m_c_conv_b, m_c_ln_g, m_c_ln_b, m_c_w_out, v_mix_norm, v_xa_norm, v_xa_wq, v_xa_wkv, v_xa_wo, v_ffn_norm, v_ffn_w_gu, v_ffn_w_down, v_a_w_in, v_a_conv_w, v_a_w_out, v_b_w_in, v_b_v_g, v_b_v_b, v_b_w_s, v_b_s_bias, v_b_w_out, v_c_w_in, v_c_conv_w, v_c_conv_b, v_c_ln_g, v_c_ln_b, v_c_w_out):
    given = dict(locals())
    w = {n: given[n] for n in WEIGHTS}
    depth = mix_norm.shape[0]
    s, d = x.shape[1], x.shape[2]
    n_mem = mem.shape[1]
    ds = d // N_CHIPS
    xin = x.reshape(s, d)
    memv = mem.reshape(n_mem, d)
    target = loss_target.reshape(s, d)
    me = 2 * lax.axis_index("x") + lax.axis_index("y")

    placed = {(n, l): _ag_place(w[n], l, kind, f"ag_place_{n}_{l}") for n, kind in BIG_KINDS.items()
              for l in range(w[n].shape[0])}
    wg = {n: [None] * w[n].shape[0] for n in BIG_KINDS}

    def mixer_keys(i):
        return [("abc"[i % 3] + "_w_in", i // 3), ("abc"[i % 3] + "_w_out", i // 3)]

    def rest_keys(i):
        return [("xa_wq", i), ("xa_wkv", i), ("xa_wo", i), ("ffn_w_gu", i), ("ffn_w_down", i)]

    def ag_begin(keys, tag):
        kinds = [BIG_KINDS[n] for n, _ in keys]
        ssem, rsem, bufs, token = _ag_start([placed[k] for k in keys], kinds, "ag_start_" + tag)
        return keys, kinds, ssem, rsem, bufs, token, tag

    def ag_end(state, after):
        keys, kinds, ssem, rsem, bufs, _, tag = state
        bufs = _ag_forward(_ag_wait(ssem, rsem, bufs, kinds, after, "ag_wait_" + tag), kinds, "ag_fwd_" + tag)
        for (n, l), buf in zip(keys, bufs):
            wg[n][l] = buf

    def pad8(t):
        return jnp.pad(t, ((0, (-t.shape[0]) % 8), (0, 0)))

    small_rows = [w[n].reshape(-1, ds) for n in SMALL_SHARDED]
    counts = [t.shape[0] for t in small_rows]
    gathered = _ag_small(jnp.concatenate([pad8(t) for t in small_rows], axis=0), "ag_small")
    gathered = jnp.transpose(gathered, (1, 0, 2)).reshape(-1, d)
    full, off = {}, 0
    for n, cnt in zip(SMALL_SHARDED, counts):
        full[n] = gathered[off:off + cnt].reshape(w[n].shape[:-1] + (d,))
        off += cnt + (-cnt) % 8
    t_chunk = b_w_s.shape[-1]
    tril = jnp.tril(jnp.ones((t_chunk, t_chunk), dtype=bool))

    def vec(a):
        return a.reshape(1, -1)

    def b_params(slot):
        ws_m = jnp.where(tril[None], b_w_s[slot], 0.0).astype(BF16)
        sbt = jnp.zeros((t_chunk, 128), F32).at[:, :b_s_bias.shape[1]].set(b_s_bias[slot].T)
        return vec(b_v_g[slot]), vec(b_v_b[slot]), ws_m, sbt

    saved = []
    xc = xin
    state = ag_begin(mixer_keys(0), "0m")
    ag_end(state, state[5])
    for i in range(depth):
        kind, slot = i % 3, i // 3
        t = f"{i}"
        state = ag_begin(rest_keys(0), "0r") if i == 0 else (
            ag_begin(mixer_keys(i + 1) + rest_keys(i + 1), f"{i + 1}") if i + 1 < depth else None)
        sv = {"x0": xc}
        h = _rms_fwd(xc, vec(full["mix_norm"][i, 0]), "rms_mix_" + t, dep=None if state is None else state[5])
        sv["h1"] = h
        if kind == 0:
            pre = _mm("nn", h, wg["a_w_in"], BF16, "a_in_" + t, bl=slot, o_parts=3)
            mid = _a_mid_fwd(pre, full["a_conv_w"][slot], "a_mid_" + t)
            y = _mm("nn", mid, wg["a_w_out"], F32, "a_out_" + t, bl=slot)
        elif kind == 1:
            pre = _mm("nn", h, wg["b_w_in"], BF16, "b_in_" + t, bl=slot, o_parts=2)
            mid = _b_mid_fwd(pre, *b_params(slot), "b_mid_" + t)
            y = _mm("nn", mid, wg["b_w_out"], F32, "b_out_" + t, bl=slot)
        else:
            pre = _mm("nn", h, wg["c_w_in"], BF16, "c_in_" + t, bl=slot, o_parts=2)
            y2 = _c_conv_fwd(pre, full["c_conv_w"][slot], vec(full["c_conv_b"][slot]), "c_conv_" + t)
            sv["cy2"] = y2
            mid = _c_ln_fwd(y2, vec(full["c_ln_g"][slot]), vec(full["c_ln_b"][slot]), "c_ln_" + t)
            y = _mm("nn", mid, wg["c_w_out"], F32, "c_out_" + t, bl=slot)
        sv.update(pre=pre, mid=mid, y1=y)
        xc = _res_rms_fwd(xc, y, vec(full["mix_norm"][i, 1]), "res_mix_" + t)

        sv["x1"] = xc
        if i == 0:
            ag_end(state, xc)
            state = ag_begin(mixer_keys(1) + rest_keys(1), "1")
        h = _rms_fwd(xc, vec(full["xa_norm"][i, 0]), "rms_xa_" + t, dep=state[5] if i == 0 else None)
        mem_n = _rms_fwd(memv, vec(full["xa_norm"][i, 2]), "rms_mem_" + t)
        q = _mm("nn", h, wg["xa_wq"], BF16, "xa_q_" + t, bl=i)
        kv3 = _mm("nn", mem_n, wg["xa_wkv"], BF16, "xa_kv_" + t, bl=i, o_parts=2)
        o = _attn_fwd(q, kv3, "attn_" + t)
        y = _mm("nn", o, wg["xa_wo"], F32, "xa_o_" + t, bl=i)
        sv.update(h2=h, mem_n=mem_n, q=q, kv3=kv3, o=o, y2=y)
        xc = _res_rms_fwd(xc, y, vec(full["xa_norm"][i, 1]), "res_xa_" + t)

        sv["x2"] = xc
        h = _rms_fwd(xc, vec(full["ffn_norm"][i, 0]), "rms_ffn_" + t)
        gu3 = _mm("nn", h, wg["ffn_w_gu"], BF16, "ffn_gu_" + t, bl=i, o_parts=2)
        act = _swiglu_fwd(gu3, "swiglu_" + t)
        y = _mm("nn", act, wg["ffn_w_down"], F32, "ffn_down_" + t, bl=i)
        sv.update(h3=h, gu3=gu3, act=act, y3=y)
        xc = _res_rms_fwd(xc, y, vec(full["ffn_norm"][i, 1]), "res_ffn_" + t)
        saved.append(sv)
        if state is not None:
            ag_end(state, xc)

    loss_blk, dx = _loss(xc, target, "loss")
    loss = lax.psum(loss_blk[0, 0], ("x", "y", "c"))

    gbuf = {}
    gfin = {n: lax.empty(w[n].shape, F32) for n in BIG_KINDS}
    gsmall = {n: [None] * full[n].shape[0] for n in ("mix_norm", "xa_norm", "ffn_norm", "a_conv_w", "c_conv_w", "c_conv_b",
                                                      "c_ln_g", "c_ln_b")}
    grepl = {}

    def wgrad(name, l, a, dy, tag, b_parts=1):
        g2 = _mm("tn", a, dy, BF16, "wg_" + tag, b_parts=b_parts)
        gbuf[name, l] = g2.reshape((1,) + g2.shape)

    def rs_begin(keys, tag, dep):
        kinds = [BIG_KINDS[n] for n, _ in keys]
        gots = _rs1([gbuf[k] for k in keys], kinds, "rs1_" + tag, dep=dep)
        ps = [_rs_add1(gbuf[k], got, kind, f"rs_add1_{k[0]}_{k[1]}") for k, got, kind in zip(keys, gots, kinds)]
        ssem, rsem, ps, lands, token = _rs2_start(ps, "rs2_start_" + tag)
        return keys, ssem, rsem, ps, lands, token, tag

    def rs_end(state, after):
        keys, ssem, rsem, ps, lands, _, tag = state
        ps, lands = _rs2_wait(ssem, rsem, ps, lands, after, "rs2_wait_" + tag)
        for (n, l), p, land in zip(keys, ps, lands):
            gfin[n] = _rs_add2(p, land, gfin[n], l, f"rs_add2_{n}_{l}")
        outs, token = _rs3([gfin[n] for n, _ in keys], [l for _, l in keys], "rs3_" + tag)
        for (n, _), o in zip(keys, outs):
            gfin[n] = o
        return token

    rs_state, rs_token = None, None

    for i in reversed(range(depth)):
        kind, slot = i % 3, i // 3
        t = f"{i}"
        sv = saved[i]
        dy, dg_post = _rms_bwd(sv["y3"], vec(full["ffn_norm"][i, 1]), dx, None, BF16, "rmsb_ffn_post_" + t,
                               dep=None if rs_state is None else rs_state[5])
        wgrad("ffn_w_down", i, sv["act"], dy, "ffn_down_" + t)
        dact = _mm("nt", dy, wg["ffn_w_down"], F32, "dg_ffn_down_" + t, bl=i)
        dgu3 = _swiglu_bwd(sv["gu3"], dact, "swiglu_b_" + t)
        wgrad("ffn_w_gu", i, sv["h3"], dgu3, "ffn_gu_" + t, b_parts=2)
        dh = _mm("nt", dgu3, wg["ffn_w_gu"], F32, "dg_ffn_gu_" + t, bl=i, a_parts=2)
        dx, dg_pre = _rms_bwd(sv["x2"], vec(full["ffn_norm"][i, 0]), dh, dx, F32, "rmsb_ffn_pre_" + t)
        gsmall["ffn_norm"][i] = jnp.concatenate([dg_pre, dg_post], axis=0)
        dy, dg_post = _rms_bwd(sv["y2"], vec(full["xa_norm"][i, 1]), dx, None, BF16, "rmsb_xa_post_" + t)
        wgrad("xa_wo", i, sv["o"], dy, "xa_o_" + t)
        do = _mm("nt", dy, wg["xa_wo"], BF16, "dg_xa_o_" + t, bl=i)
        dq, dkv3 = _attn_bwd(sv["q"], sv["kv3"], do, "attn_b_" + t)
        wgrad("xa_wq", i, sv["h2"], dq, "xa_q_" + t)
        dh = _mm("nt", dq, wg["xa_wq"], F32, "dg_xa_q_" + t, bl=i)
        dkv3 = dkv3.astype(BF16)
        wgrad("xa_wkv", i, sv["mem_n"], dkv3, "xa_kv_" + t, b_parts=2)
        dmem_n = _mm("nt", dkv3, wg["xa_wkv"], F32, "dg_xa_kv_" + t, bl=i, a_parts=2)
        _, dg_mem = _rms_bwd(memv, vec(full["xa_norm"][i, 2]), dmem_n, None, F32, "rmsb_mem_" + t)
        dx, dg_pre = _rms_bwd(sv["x1"], vec(full["xa_norm"][i, 0]), dh, dx, F32, "rmsb_xa_pre_" + t)
        gsmall["xa_norm"][i] = jnp.concatenate([dg_pre, dg_post, dg_mem], axis=0)
        if i == 0:
            rs_token = rs_end(rs_state, dx)
            rs_state = rs_begin(rest_keys(0), "0r", rs_token)
        dy, dg_post = _rms_bwd(sv["y1"], vec(full["mix_norm"][i, 1]), dx, None, BF16, "rmsb_mix_post_" + t,
                               dep=rs_state[5] if i == 0 else None)
        if kind == 0:
            wgrad("a_w_out", slot, sv["mid"], dy, "a_out_" + t)
            dmid = _mm("nt", dy, wg["a_w_out"], F32, "dg_a_out_" + t, bl=slot)
            dpre, dcw = _a_mid_bwd(sv["pre"], dmid, full["a_conv_w"][slot], "a_mid_b_" + t)
            gsmall["a_conv_w"][slot] = dcw
            wgrad("a_w_in", slot, sv["h1"], dpre, "a_in_" + t, b_parts=3)
            dh = _mm("nt", dpre, wg["a_w_in"], F32, "dg_a_in_" + t, bl=slot, a_parts=3)
        elif kind == 1:
            wgrad("b_w_out", slot, sv["mid"], dy, "b_out_" + t)
            dmid = _mm("nt", dy, wg["b_w_out"], F32, "dg_b_out_" + t, bl=slot)
            dpre, dvg, dvb, dws, dsbt = _b_mid_bwd(sv["pre"], dmid, *b_params(slot), "b_mid_b_" + t)
            grepl[slot] = (dvg, dvb, dws, dsbt[:, :b_s_bias.shape[1]].T)
            wgrad("b_w_in", slot, sv["h1"], dpre, "b_in_" + t, b_parts=2)
            dh = _mm("nt", dpre, wg["b_w_in"], F32, "dg_b_in_" + t, bl=slot, a_parts=2)
        else:
            wgrad("c_w_out", slot, sv["mid"], dy, "c_out_" + t)
            dmid = _mm("nt", dy, wg["c_w_out"], F32, "dg_c_out_" + t, bl=slot)
            dy2, dlg, dlb = _c_ln_bwd(sv["cy2"], dmid, vec(full["c_ln_g"][slot]), vec(full["c_ln_b"][slot]), "c_ln_b_" + t)
            dpre, dcw, dcb = _c_conv_bwd(sv["pre"], dy2, full["c_conv_w"][slot], "c_conv_b_" + t)
            gsmall["c_conv_w"][slot], gsmall["c_conv_b"][slot] = dcw, dcb
            gsmall["c_ln_g"][slot], gsmall["c_ln_b"][slot] = dlg, dlb
            wgrad("c_w_in", slot, sv["h1"], dpre, "c_in_" + t, b_parts=2)
            dh = _mm("nt", dpre, wg["c_w_in"], F32, "dg_c_in_" + t, bl=slot, a_parts=2)
        dx, dg_pre = _rms_bwd(sv["x0"], vec(full["mix_norm"][i, 0]), dh, dx, F32, "rmsb_mix_pre_" + t)
        gsmall["mix_norm"][i] = jnp.concatenate([dg_pre, dg_post], axis=0)
        if rs_state is not None:
            rs_token = rs_end(rs_state, dx)
        rs_state = rs_begin(mixer_keys(i) + (rest_keys(i) if i > 0 else []), f"{i}" if i > 0 else "0m", rs_token)
    rs_end(rs_state, rs_state[5])
    grad_x = dx.reshape(x.shape)

    grads = dict(gfin)
    small_g = [jnp.concatenate(gsmall[n], axis=0).reshape(-1, d) for n in SMALL_SHARDED]
    n_b = b_v_g.shape[0]
    repl_g = [jnp.concatenate([grepl[sl][k] for sl in range(n_b)], axis=0) for k in range(4)]
    repl_rows = []
    for g_arr in repl_g:
        flat = g_arr.reshape(-1)
        flat = jnp.concatenate([flat, jnp.zeros(((-flat.shape[0]) % d,), F32)])
        repl_rows.append(flat.reshape(-1, d))
    rows_all = [pad8(t) for t in small_g + repl_rows]
    total = _sum_slots(_gather8(jnp.concatenate(rows_all, axis=0), "gather_small_grads"), F32, "sum_small_grads")
    off = 0
    for n, cnt in zip(SMALL_SHARDED, counts):
        blk = lax.dynamic_slice_in_dim(total[off:off + cnt], me * ds, ds, axis=1)
        grads[n] = blk.reshape(w[n].shape)
        off += cnt + (-cnt) % 8
    for n, g_arr in zip(SMALL_REPL, repl_g):
        cnt = -(-g_arr.size // d)
        grads[n] = total[off:off + cnt].reshape(-1)[:g_arr.size].reshape(w[n].shape)
        off += cnt + (-cnt) % 8

    delta, new_m, new_v = {}, {}, {}
    for n in WEIGHTS:
        delta[n], new_m[n], new_v[n] = _adamw(w[n], grads[n], given["m_" + n], given["v_" + n], "adamw_" + n)
    return (loss, grad_x, *[grads[n] for n in WEIGHTS], *[delta[n] for n in WEIGHTS], *[new_m[n] for n in WEIGHTS],
            *[new_v[n] for n in WEIGHTS])
```

```python
import functools

import jax
import jax.numpy as jnp
from jax import lax
from jax.experimental import pallas as pl
from jax.experimental.pallas import tpu as pltpu

F32 = jnp.float32
BF16 = jnp.bfloat16
EPS = 1e-6
XA_HEADS = 4
CHUNK = 128
GMLP_GROUPS = 8
ADAM_LR, ADAM_B1, ADAM_B2, ADAM_EPS, ADAM_WD, ADAM_STEP = 0.001, 0.9, 0.999, 1e-08, 0.01, 10
VMEM_LIMIT_V7X = 48 * 1024 * 1024
HBM = pl.BlockSpec(memory_space=pltpu.HBM)
MESH = pl.DeviceIdType.MESH
N_CHIPS = 4
BIG_KINDS = {"xa_wq": "row", "xa_wkv": "col", "xa_wo": "row", "ffn_w_gu": "col", "ffn_w_down": "row",
             "a_w_in": "col", "a_w_out": "row", "b_w_in": "col", "b_w_out": "row", "c_w_in": "col", "c_w_out": "row"}
SMALL_SHARDED = ["mix_norm", "xa_norm", "ffn_norm", "a_conv_w", "c_conv_w", "c_conv_b", "c_ln_g", "c_ln_b"]
SMALL_REPL = ["b_v_g", "b_v_b", "b_w_s", "b_s_bias"]
WEIGHTS = ["mix_norm", "xa_norm", "xa_wq", "xa_wkv", "xa_wo", "ffn_norm", "ffn_w_gu", "ffn_w_down", "a_w_in", "a_conv_w",
           "a_w_out", "b_w_in", "b_v_g", "b_v_b", "b_w_s", "b_s_bias", "b_w_out", "c_w_in", "c_conv_w", "c_conv_b",
           "c_ln_g", "c_ln_b", "c_w_out"]


def _params(*sem):
    return pltpu.CompilerParams(dimension_semantics=sem, vmem_limit_bytes=VMEM_LIMIT_V7X)


def _tile(n, cands=(1024, 512, 256, 128)):
    for c in cands:
        if n % c == 0:
            return c
    return n


def _div_tile(n, cap):
    best = None
    for t in range(128, min(n, cap) + 1, 128):
        if n % t == 0:
            best = t
    return best or n


MM_OUT_TILE_CAP = 1408
MM_K_TILE_CAP = 2816

_DIMS = {"nn": (((1,), (0,)), ((), ())), "nt": (((1,), (1,)), ((), ())), "tn": (((0,), (0,)), ((), ()))}


def _mm(mode, a, b, out_dtype, name, *, bl=None, a_parts=1, b_parts=1, o_parts=1, dep=None):
    if isinstance(b, list):
        b, bl = b[bl], 0
    bshape = b.shape[1:] if bl is not None else b.shape
    if mode == "nn":
        mo, c = a.shape
        no = bshape[1]
    elif mode == "nt":
        mo, c = (a.shape[1], a.shape[0] * a.shape[2]) if a_parts > 1 else a.shape
        no = bshape[0]
    else:
        c, mo = a.shape
        no = b.shape[0] * b.shape[2] if b_parts > 1 else bshape[1]
    tmo = _div_tile(mo, MM_OUT_TILE_CAP)
    tno = _div_tile(no // max(o_parts, b_parts), MM_OUT_TILE_CAP)
    tc = _div_tile(c // a_parts, MM_K_TILE_CAP)
    nk = c // tc
    nkp = nk // a_parts
    njp = (no // tno) // max(o_parts, b_parts)
    lead = (None,) if bl is not None else ()
    lidx = (bl,) if bl is not None else ()

    if mode == "nn":
        a_spec = pl.BlockSpec((tmo, tc), lambda i, j, k: (i, k))
        b_spec = pl.BlockSpec(lead + (tc, tno), lambda i, j, k: lidx + (k, j))
    elif mode == "nt":
        if a_parts > 1:
            a_spec = pl.BlockSpec((None, tmo, tc), lambda i, j, k: (k // nkp, i, k % nkp))
        else:
            a_spec = pl.BlockSpec((tmo, tc), lambda i, j, k: (i, k))
        b_spec = pl.BlockSpec(lead + (tno, tc), lambda i, j, k: lidx + (j, k))
    else:
        a_spec = pl.BlockSpec((tc, tmo), lambda i, j, k: (k, i))
        if b_parts > 1:
            b_spec = pl.BlockSpec((None, tc, tno), lambda i, j, k: (j // njp, k, j % njp))
        else:
            b_spec = pl.BlockSpec((tc, tno), lambda i, j, k: (k, j))

    in_specs = [a_spec, b_spec]
    args = [a, b]
    if dep is not None:
        in_specs.append(pl.BlockSpec(memory_space=pl.ANY))
        args.append(dep)
    if o_parts > 1:
        out_shape = jax.ShapeDtypeStruct((o_parts, mo, no // o_parts), out_dtype)
        out_spec = pl.BlockSpec((None, tmo, tno), lambda i, j, k: (j // njp, i, j % njp))
    else:
        out_shape = jax.ShapeDtypeStruct((mo, no), out_dtype)
        out_spec = pl.BlockSpec((tmo, tno), lambda i, j, k: (i, j))
    dims = _DIMS[mode]

    def body(a_ref, b_ref, *rest):
        if nk == 1:
            o_ref = rest[-1]
            o_ref[...] = lax.dot_general(a_ref[...], b_ref[...], dims, preferred_element_type=F32).astype(o_ref.dtype)
            return
        o_ref, acc = rest[-2], rest[-1]
        k = pl.program_id(2)
        part = lax.dot_general(a_ref[...], b_ref[...], dims, preferred_element_type=F32)

        @pl.when(k == 0)
        def _():
            acc[...] = part

        @pl.when(jnp.logical_and(k > 0, k < nk - 1))
        def _():
            acc[...] += part

        @pl.when(k == nk - 1)
        def _():
            o_ref[...] = (acc[...] + part).astype(o_ref.dtype)

    return pl.pallas_call(
        body, name=name, out_shape=out_shape, grid=(mo // tmo, no // tno, nk), in_specs=in_specs, out_specs=out_spec,
        scratch_shapes=[pltpu.VMEM((tmo, tno), F32)] if nk > 1 else [],
        compiler_params=_params("parallel", "parallel", "arbitrary"))(*args)


def _ew(fn, ins, out_dtypes, name):
    rows, cols = ins[0].shape
    tr = rows
    for cand in (512, 256, 128, 64, 32, 16):
        if rows % cand == 0 and cand * cols * 4 <= (1 << 20):
            tr = cand
            break
    spec = pl.BlockSpec((tr, cols), lambda i: (i, 0))
    n_in = len(ins)

    def body(*refs):
        outs = fn(*[r[...] for r in refs[:n_in]])
        for o_ref, o in zip(refs[n_in:], outs):
            o_ref[...] = o.astype(o_ref.dtype)

    return pl.pallas_call(
        body, name=name, out_shape=[jax.ShapeDtypeStruct((rows, cols), d) for d in out_dtypes], grid=(rows // tr,),
        in_specs=[spec] * n_in, out_specs=[spec] * len(out_dtypes), compiler_params=_params("parallel"))(*ins)


def _adamw_fn(w, g, m, v):
    m = ADAM_B1 * m + (1.0 - ADAM_B1) * g
    v = ADAM_B2 * v + (1.0 - ADAM_B2) * (g * g)
    m_hat = m / (1.0 - ADAM_B1 ** ADAM_STEP)
    v_hat = v / (1.0 - ADAM_B2 ** ADAM_STEP)
    delta = -ADAM_LR * (m_hat / (jnp.sqrt(v_hat) + ADAM_EPS) + ADAM_WD * w)
    return delta, m, v


def _adamw(w, g, m, v, name):
    shape = w.shape
    cols = shape[-1]
    flat = [t.reshape(-1, cols) for t in (w, g, m, v)]
    outs = _ew(_adamw_fn, flat, [F32] * 3, name)
    return [o.reshape(shape) for o in outs]


def _row_tile(s):
    return _tile(s, (256, 128, 64, 32, 16, 8))


def _rms_fwd(x, g, name, dep=None):
    s, d = x.shape
    r = _row_tile(s)
    deps = [] if dep is None else [dep]

    def body(x_ref, g_ref, *rest):
        o_ref = rest[-1]
        xv = x_ref[...]
        o_ref[...] = (xv * lax.rsqrt(jnp.mean(xv * xv, axis=-1, keepdims=True) + EPS) * g_ref[...]).astype(BF16)

    return pl.pallas_call(
        body, name=name, out_shape=jax.ShapeDtypeStruct((s, d), BF16), grid=(s // r,),
        in_specs=[pl.BlockSpec((r, d), lambda i: (i, 0)), pl.BlockSpec((1, d), lambda i: (0, 0))] + [ANY] * len(deps),
        out_specs=pl.BlockSpec((r, d), lambda i: (i, 0)), compiler_params=_params("parallel"))(x, g, *deps)


def _res_rms_fwd(x, y, g, g_next, name):
    s, d = x.shape
    r = _row_tile(s)
    has_next = g_next is not None

    def body(x_ref, y_ref, g_ref, *rest):
        yv = y_ref[...]
        xn = x_ref[...] + yv * lax.rsqrt(jnp.mean(yv * yv, axis=-1, keepdims=True) + EPS) * g_ref[...]
        rest[-2 if has_next else -1][...] = xn
        if has_next:
            rest[-1][...] = (xn * lax.rsqrt(jnp.mean(xn * xn, axis=-1, keepdims=True) + EPS) * rest[0][...]).astype(BF16)

    row = pl.BlockSpec((r, d), lambda i: (i, 0))
    vec = pl.BlockSpec((1, d), lambda i: (0, 0))
    outs = pl.pallas_call(
        body, name=name,
        out_shape=[jax.ShapeDtypeStruct((s, d), F32)] + ([jax.ShapeDtypeStruct((s, d), BF16)] if has_next else []),
        grid=(s // r,), in_specs=[row, row, vec] + ([vec] if has_next else []), out_specs=[row] * (2 if has_next else 1),
        compiler_params=_params("parallel"))(*([x, y, g] + ([g_next] if has_next else [])))
    return outs[0], (outs[1] if has_next else None)


def _rms_bwd(x, g, dy, resid, out_dtype, name, dep=None):
    s, d = x.shape
    r = _row_tile(s)
    has_res = resid is not None
    deps = [] if dep is None else [dep]

    def body(*refs):
        x_ref, g_ref, dy_ref = refs[:3]
        dx_ref, dg_ref = refs[-2:]
        i = pl.program_id(0)
        xv = x_ref[...]
        dyv = dy_ref[...].astype(F32)
        rstd = lax.rsqrt(jnp.mean(xv * xv, axis=-1, keepdims=True) + EPS)
        n = xv * rstd
        dn = dyv * g_ref[...]
        dx = rstd * (dn - n * jnp.mean(dn * n, axis=-1, keepdims=True))
        if has_res:
            dx = dx + refs[3][...]
        dx_ref[...] = dx.astype(dx_ref.dtype)
        part = jnp.sum(dyv * n, axis=0, keepdims=True)

        @pl.when(i == 0)
        def _():
            dg_ref[...] = part

        @pl.when(i > 0)
        def _():
            dg_ref[...] += part

    row = pl.BlockSpec((r, d), lambda i: (i, 0))
    vec = pl.BlockSpec((1, d), lambda i: (0, 0))
    ins = [x, g, dy] + ([resid] if has_res else []) + deps
    return pl.pallas_call(
        body, name=name, out_shape=[jax.ShapeDtypeStruct((s, d), out_dtype), jax.ShapeDtypeStruct((1, d), F32)],
        grid=(s // r,), in_specs=[row, vec, row] + ([row] if has_res else []) + [ANY] * len(deps), out_specs=[row, vec],
        compiler_params=_params("arbitrary"))(*ins)


def _loss(y, t, name):
    s, d = y.shape
    r = _row_tile(s)

    def body(y_ref, t_ref, l_ref, dy_ref):
        i = pl.program_id(0)
        e = y_ref[...] - t_ref[...]
        dy_ref[...] = e * (1.0 / d)
        part = jnp.full((8, 128), 0.5 * jnp.sum(jnp.mean(e * e, axis=-1, keepdims=True)), F32)

        @pl.when(i == 0)
        def _():
            l_ref[...] = part

        @pl.when(i > 0)
        def _():
            l_ref[...] += part

    row = pl.BlockSpec((r, d), lambda i: (i, 0))
    return pl.pallas_call(
        body, name=name, out_shape=[jax.ShapeDtypeStruct((8, 128), F32), jax.ShapeDtypeStruct((s, d), F32)],
        grid=(s // r,), in_specs=[row, row], out_specs=[pl.BlockSpec((8, 128), lambda i: (0, 0)), row],
        compiler_params=_params("arbitrary"))(y, t)


def _rows(xv, a, m, cache):
    r = a % 8
    q = a - r
    if r == 0:
        return xv[q:q + m]
    if r not in cache:
        cache[r] = pltpu.roll(xv, xv.shape[0] - r, 0)
    return cache[r][q:q + m]


def _conv_taps(xv, w, k_w, halo, m, flip):
    cache = {}
    acc = None
    for k in range(k_w):
        a = (k_w - 1 - k) if flip else (halo + k - (k_w - 1))
        term = w[k:k + 1, :] * _rows(xv, a, m, cache)
        acc = term if acc is None else acc + term
    return acc


def _conv_wgrad(dw_ref, dyv, xv, k_w, halo, m):
    cache = {}
    for k in range(k_w):
        xs = _rows(xv, halo + k - (k_w - 1), m, cache)
        dw_ref[pl.ds(k, 1), :] += jnp.sum(dyv * xs, axis=0, keepdims=True)


def _conv_tiles(s, dp, halo):
    r = _tile(s, (256, 128))
    cw = _tile(dp, (256, 128))
    return r, cw, r // halo


A_HALO = 8


def _a_mid_fwd(bcz3, w, name):
    _, s, d = bcz3.shape
    r, cw, rh = _conv_tiles(s, d, A_HALO)
    k_w = w.shape[0]

    def body(m_ref, h_ref, w_ref, o_ref):
        i = pl.program_id(0)
        cz = m_ref[1].astype(F32) * m_ref[2].astype(F32)
        hcz = h_ref[1].astype(F32) * h_ref[2].astype(F32)
        hcz = jnp.where(i == 0, 0.0, hcz)
        xv = jnp.concatenate([hcz, cz], axis=0)
        y = _conv_taps(xv, w_ref[...], k_w, A_HALO, r, False)
        o_ref[...] = (m_ref[0].astype(F32) * y).astype(BF16)

    return pl.pallas_call(
        body, name=name, out_shape=jax.ShapeDtypeStruct((s, d), BF16), grid=(s // r, d // cw),
        in_specs=[pl.BlockSpec((3, r, cw), lambda i, j: (0, i, j)),
                  pl.BlockSpec((3, A_HALO, cw), lambda i, j: (0, jnp.maximum(i * rh - 1, 0), j)),
                  pl.BlockSpec((k_w, cw), lambda i, j: (0, j))],
        out_specs=pl.BlockSpec((r, cw), lambda i, j: (i, j)), compiler_params=_params("parallel", "parallel"))(bcz3, bcz3, w)


def _a_mid_bwd(bcz3, dgated, w, name):
    _, s, d = bcz3.shape
    r, cw, rh = _conv_tiles(s, d, A_HALO)
    k_w = w.shape[0]
    ni = s // r
    last_h = s // A_HALO - 1

    def body(m_ref, hp_ref, hn_ref, dg_ref, dgn_ref, w_ref, o_ref, dw_ref):
        i = pl.program_id(1)
        wv = w_ref[...]
        b = m_ref[0].astype(F32)
        c = m_ref[1].astype(F32)
        z = m_ref[2].astype(F32)
        hcz = jnp.where(i == 0, 0.0, hp_ref[1].astype(F32) * hp_ref[2].astype(F32))
        xv = jnp.concatenate([hcz, c * z], axis=0)
        y = _conv_taps(xv, wv, k_w, A_HALO, r, False)
        dg = dg_ref[...].astype(F32)
        dy = dg * b
        dyn = jnp.where(i == ni - 1, 0.0, dgn_ref[...].astype(F32) * hn_ref[0].astype(F32))
        dcz = _conv_taps(jnp.concatenate([dy, dyn], axis=0), wv, k_w, A_HALO, r, True)
        o_ref[0] = (dg * y).astype(BF16)
        o_ref[1] = (dcz * z).astype(BF16)
        o_ref[2] = (dcz * c).astype(BF16)

        @pl.when(i == 0)
        def _():
            dw_ref[...] = jnp.zeros_like(dw_ref)

        _conv_wgrad(dw_ref, dy, xv, k_w, A_HALO, r)

    return pl.pallas_call(
        body, name=name, out_shape=[jax.ShapeDtypeStruct((3, s, d), BF16), jax.ShapeDtypeStruct((k_w, d), F32)],
        grid=(d // cw, ni),
        in_specs=[pl.BlockSpec((3, r, cw), lambda j, i: (0, i, j)),
                  pl.BlockSpec((3, A_HALO, cw), lambda j, i: (0, jnp.maximum(i * rh - 1, 0), j)),
                  pl.BlockSpec((3, A_HALO, cw), lambda j, i: (0, jnp.minimum((i + 1) * rh, last_h), j)),
                  pl.BlockSpec((r, cw), lambda j, i: (i, j)),
                  pl.BlockSpec((A_HALO, cw), lambda j, i: (jnp.minimum((i + 1) * rh, last_h), j)),
                  pl.BlockSpec((k_w, cw), lambda j, i: (0, j))],
        out_specs=[pl.BlockSpec((3, r, cw), lambda j, i: (0, i, j)), pl.BlockSpec((k_w, cw), lambda j, i: (0, j))],
        compiler_params=_params("parallel", "arbitrary"))(bcz3, bcz3, bcz3, dgated, dgated, w)


C_HALO = 32


def _c_conv_fwd(ag3, w, bias, name):
    _, s, d = ag3.shape
    r, cw, rh = _conv_tiles(s, d, C_HALO)
    k_w = w.shape[0]

    def body(m_ref, h_ref, w_ref, b_ref, o_ref):
        i = pl.program_id(0)
        y1 = m_ref[0].astype(F32) * jax.nn.sigmoid(m_ref[1].astype(F32))
        h1 = jnp.where(i == 0, 0.0, h_ref[0].astype(F32) * jax.nn.sigmoid(h_ref[1].astype(F32)))
        xv = jnp.concatenate([h1, y1], axis=0)
        o_ref[...] = _conv_taps(xv, w_ref[...], k_w, C_HALO, r, False) + b_ref[...]

    return pl.pallas_call(
        body, name=name, out_shape=jax.ShapeDtypeStruct((s, d), F32), grid=(s // r, d // cw),
        in_specs=[pl.BlockSpec((2, r, cw), lambda i, j: (0, i, j)),
                  pl.BlockSpec((2, C_HALO, cw), lambda i, j: (0, jnp.maximum(i * rh - 1, 0), j)),
                  pl.BlockSpec((k_w, cw), lambda i, j: (0, j)), pl.BlockSpec((1, cw), lambda i, j: (0, j))],
        out_specs=pl.BlockSpec((r, cw), lambda i, j: (i, j)),
        compiler_params=_params("parallel", "parallel"))(ag3, ag3, w, bias)


def _c_conv_bwd(ag3, dy2, w, name):
    _, s, d = ag3.shape
    r, cw, rh = _conv_tiles(s, d, C_HALO)
    k_w = w.shape[0]
    ni = s // r
    last_h = s // C_HALO - 1

    def body(m_ref, hp_ref, dy_ref, dyn_ref, w_ref, o_ref, dw_ref, db_ref):
        i = pl.program_id(1)
        wv = w_ref[...]
        a = m_ref[0].astype(F32)
        sg = jax.nn.sigmoid(m_ref[1].astype(F32))
        h1 = jnp.where(i == 0, 0.0, hp_ref[0].astype(F32) * jax.nn.sigmoid(hp_ref[1].astype(F32)))
        xv = jnp.concatenate([h1, a * sg], axis=0)
        dy = dy_ref[...]
        dyn = jnp.where(i == ni - 1, 0.0, dyn_ref[...])
        dy1 = _conv_taps(jnp.concatenate([dy, dyn], axis=0), wv, k_w, C_HALO, r, True)
        o_ref[0] = (dy1 * sg).astype(BF16)
        o_ref[1] = (dy1 * a * sg * (1.0 - sg)).astype(BF16)

        @pl.when(i == 0)
        def _():
            dw_ref[...] = jnp.zeros_like(dw_ref)
            db_ref[...] = jnp.zeros_like(db_ref)

        db_ref[...] += jnp.sum(dy, axis=0, keepdims=True)
        _conv_wgrad(dw_ref, dy, xv, k_w, C_HALO, r)

    return pl.pallas_call(
        body, name=name,
        out_shape=[jax.ShapeDtypeStruct((2, s, d), BF16), jax.ShapeDtypeStruct((k_w, d), F32),
                   jax.ShapeDtypeStruct((1, d), F32)],
        grid=(d // cw, ni),
        in_specs=[pl.BlockSpec((2, r, cw), lambda j, i: (0, i, j)),
                  pl.BlockSpec((2, C_HALO, cw), lambda j, i: (0, jnp.maximum(i * rh - 1, 0), j)),
                  pl.BlockSpec((r, cw), lambda j, i: (i, j)),
                  pl.BlockSpec((C_HALO, cw), lambda j, i: (jnp.minimum((i + 1) * rh, last_h), j)),
                  pl.BlockSpec((k_w, cw), lambda j, i: (0, j))],
        out_specs=[pl.BlockSpec((2, r, cw), lambda j, i: (0, i, j)), pl.BlockSpec((k_w, cw), lambda j, i: (0, j)),
                   pl.BlockSpec((1, cw), lambda j, i: (0, j))],
        compiler_params=_params("parallel", "arbitrary"))(ag3, ag3, dy2, dy2, w)


def _ln_stats(v):
    mu = jnp.mean(v, axis=-1, keepdims=True)
    vc = v - mu
    rstd = lax.rsqrt(jnp.mean(vc * vc, axis=-1, keepdims=True) + EPS)
    return vc * rstd, rstd


def _ln_bwd(dn, g, xh, rstd):
    dxh = dn * g
    return rstd * (dxh - jnp.mean(dxh, axis=-1, keepdims=True) - xh * jnp.mean(dxh * xh, axis=-1, keepdims=True))


def _c_ln_fwd(y2, g, b, name):
    s, d = y2.shape
    r = _row_tile(s)

    def body(y_ref, g_ref, b_ref, o_ref):
        xh, _ = _ln_stats(y_ref[...])
        y3 = xh * g_ref[...] + b_ref[...]
        o_ref[...] = (y3 * jax.nn.sigmoid(y3)).astype(BF16)

    row = pl.BlockSpec((r, d), lambda i: (i, 0))
    vec = pl.BlockSpec((1, d), lambda i: (0, 0))
    return pl.pallas_call(
        body, name=name, out_shape=jax.ShapeDtypeStruct((s, d), BF16), grid=(s // r,), in_specs=[row, vec, vec],
        out_specs=row, compiler_params=_params("parallel"))(y2, g, b)


def _c_ln_bwd(y2, dout, g, b, name):
    s, d = y2.shape
    r = _row_tile(s)

    def body(y_ref, do_ref, g_ref, b_ref, dy_ref, dg_ref, db_ref):
        i = pl.program_id(0)
        xh, rstd = _ln_stats(y_ref[...])
        gv = g_ref[...]
        y3 = xh * gv + b_ref[...]
        sg = jax.nn.sigmoid(y3)
        dy3 = do_ref[...].astype(F32) * (sg + y3 * sg * (1.0 - sg))
        dy_ref[...] = _ln_bwd(dy3, gv, xh, rstd)

        @pl.when(i == 0)
        def _():
            dg_ref[...] = jnp.zeros_like(dg_ref)
            db_ref[...] = jnp.zeros_like(db_ref)

        dg_ref[...] += jnp.sum(dy3 * xh, axis=0, keepdims=True)
        db_ref[...] += jnp.sum(dy3, axis=0, keepdims=True)

    row = pl.BlockSpec((r, d), lambda i: (i, 0))
    vec = pl.BlockSpec((1, d), lambda i: (0, 0))
    return pl.pallas_call(
        body, name=name,
        out_shape=[jax.ShapeDtypeStruct((s, d), F32), jax.ShapeDtypeStruct((1, d), F32), jax.ShapeDtypeStruct((1, d), F32)],
        grid=(s // r,), in_specs=[row, row, vec, vec], out_specs=[row, vec, vec],
        compiler_params=_params("arbitrary"))(y2, dout, g, b)


_GELU_C = 0.7978845608028654
_GELU_A = 0.044715


def _gelu(x):
    return 0.5 * x * (1.0 + jnp.tanh(_GELU_C * (x + _GELU_A * x * x * x)))


def _gelu_grad(x):
    t = jnp.tanh(_GELU_C * (x + _GELU_A * x * x * x))
    return 0.5 * (1.0 + t) + 0.5 * x * (1.0 - t * t) * _GELU_C * (1.0 + 3.0 * _GELU_A * x * x)


def _b_mid_fwd(uv3, vg, vb, ws_m, sbt, name):
    _, s, h = uv3.shape
    g_n, t, _ = ws_m.shape
    gd = h // g_n

    def body(uv_ref, vg_ref, vb_ref, ws_ref, sb_ref, o_ref):
        u = _gelu(uv_ref[0].astype(F32))
        xh, _ = _ln_stats(_gelu(uv_ref[1].astype(F32)))
        vn = (xh * vg_ref[...] + vb_ref[...]).astype(BF16)
        for g in range(g_n):
            sl = slice(g * gd, (g + 1) * gd)
            sv = jnp.dot(ws_ref[g], vn[:, sl], preferred_element_type=F32) + sb_ref[:, g:g + 1]
            o_ref[:, sl] = (u[:, sl] * sv).astype(BF16)

    vec = pl.BlockSpec((1, h), lambda i: (0, 0))
    return pl.pallas_call(
        body, name=name, out_shape=jax.ShapeDtypeStruct((s, h), BF16), grid=(s // t,),
        in_specs=[pl.BlockSpec((2, t, h), lambda i: (0, i, 0)), vec, vec,
                  pl.BlockSpec((g_n, t, t), lambda i: (0, 0, 0)), pl.BlockSpec((t, 128), lambda i: (0, 0))],
        out_specs=pl.BlockSpec((t, h), lambda i: (i, 0)), compiler_params=_params("parallel"))(uv3, vg, vb, ws_m, sbt)


def _b_mid_bwd(uv3, dgated, vg, vb, ws_m, sbt, name):
    _, s, h = uv3.shape
    g_n, t, _ = ws_m.shape
    gd = h // g_n

    def body(uv_ref, dg_ref, vg_ref, vb_ref, ws_ref, sb_ref, o_ref, dvg_ref, dvb_ref, dws_ref, dsb_ref, dvn_ref):
        i = pl.program_id(0)

        @pl.when(i == 0)
        def _():
            dvg_ref[...] = jnp.zeros_like(dvg_ref)
            dvb_ref[...] = jnp.zeros_like(dvb_ref)
            dws_ref[...] = jnp.zeros_like(dws_ref)
            dsb_ref[...] = jnp.zeros_like(dsb_ref)

        upre = uv_ref[0].astype(F32)
        vpre = uv_ref[1].astype(F32)
        u = _gelu(upre)
        xh, rstd = _ln_stats(_gelu(vpre))
        gv = vg_ref[...]
        vn = (xh * gv + vb_ref[...]).astype(BF16)
        causal = lax.broadcasted_iota(jnp.int32, (t, t), 0) >= lax.broadcasted_iota(jnp.int32, (t, t), 1)
        lane = lax.broadcasted_iota(jnp.int32, (t, 128), 1)
        for g in range(g_n):
            sl = slice(g * gd, (g + 1) * gd)
            wsg = ws_ref[g]
            sv = jnp.dot(wsg, vn[:, sl], preferred_element_type=F32) + sb_ref[:, g:g + 1]
            dg = dg_ref[:, sl].astype(F32)
            o_ref[0, :, sl] = (dg * sv * _gelu_grad(upre[:, sl])).astype(BF16)
            dsv = dg * u[:, sl]
            dsvb = dsv.astype(BF16)
            dsb_ref[...] += jnp.where(lane == g, jnp.sum(dsv, axis=1, keepdims=True), 0.0)
            dws = lax.dot_general(dsvb, vn[:, sl], _DIMS["nt"], preferred_element_type=F32)
            dws_ref[g] += jnp.where(causal, dws, 0.0)
            dvn_ref[:, sl] = lax.dot_general(wsg, dsvb, _DIMS["tn"], preferred_element_type=F32)
        dvn = dvn_ref[...]
        dvg_ref[...] += jnp.sum(dvn * xh, axis=0, keepdims=True)
        dvb_ref[...] += jnp.sum(dvn, axis=0, keepdims=True)
        o_ref[1] = (_ln_bwd(dvn, gv, xh, rstd) * _gelu_grad(vpre)).astype(BF16)

    vec = pl.BlockSpec((1, h), lambda i: (0, 0))
    return pl.pallas_call(
        body, name=name,
        out_shape=[jax.ShapeDtypeStruct((2, s, h), BF16), jax.ShapeDtypeStruct((1, h), F32), jax.ShapeDtypeStruct((1, h), F32),
                   jax.ShapeDtypeStruct((g_n, t, t), F32), jax.ShapeDtypeStruct((t, 128), F32)],
        grid=(s // t,),
        in_specs=[pl.BlockSpec((2, t, h), lambda i: (0, i, 0)), pl.BlockSpec((t, h), lambda i: (i, 0)), vec, vec,
                  pl.BlockSpec((g_n, t, t), lambda i: (0, 0, 0)), pl.BlockSpec((t, 128), lambda i: (0, 0))],
        out_specs=[pl.BlockSpec((2, t, h), lambda i: (0, i, 0)), vec, vec,
                   pl.BlockSpec((g_n, t, t), lambda i: (0, 0, 0)), pl.BlockSpec((t, 128), lambda i: (0, 0))],
        scratch_shapes=[pltpu.VMEM((t, h), F32)],
        compiler_params=_params("arbitrary"))(uv3, dgated, vg, vb, ws_m, sbt)


def _softmax_rows(sc):
    e = jnp.exp(sc - jnp.max(sc, axis=-1, keepdims=True))
    return e / jnp.sum(e, axis=-1, keepdims=True)


def _attn_fwd(q, kv3, name):
    s, d = q.shape
    m = kv3.shape[1]
    dh = d // XA_HEADS
    scale = dh ** -0.5
    r = _row_tile(s)

    def body(q_ref, kv_ref, o_ref):
        for hd in range(XA_HEADS):
            sl = slice(hd * dh, (hd + 1) * dh)
            sc = lax.dot_general(q_ref[:, sl], kv_ref[0, :, sl], _DIMS["nt"], preferred_element_type=F32) * scale
            p = _softmax_rows(sc).astype(BF16)
            o_ref[:, sl] = jnp.dot(p, kv_ref[1, :, sl], preferred_element_type=F32).astype(BF16)

    return pl.pallas_call(
        body, name=name, out_shape=jax.ShapeDtypeStruct((s, d), BF16), grid=(s // r,),
        in_specs=[pl.BlockSpec((r, d), lambda i: (i, 0)), pl.BlockSpec((2, m, d), lambda i: (0, 0, 0))],
        out_specs=pl.BlockSpec((r, d), lambda i: (i, 0)), compiler_params=_params("parallel"))(q, kv3)


def _attn_bwd(q, kv3, do, name):
    s, d = q.shape
    m = kv3.shape[1]
    dh = d // XA_HEADS
    scale = dh ** -0.5
    r = _row_tile(s)

    def body(q_ref, kv_ref, do_ref, dq_ref, dkv_ref):
        i = pl.program_id(0)

        @pl.when(i == 0)
        def _():
            dkv_ref[...] = jnp.zeros_like(dkv_ref)

        for hd in range(XA_HEADS):
            sl = slice(hd * dh, (hd + 1) * dh)
            qh = q_ref[:, sl]
            kh = kv_ref[0, :, sl]
            doh = do_ref[:, sl]
            sc = lax.dot_general(qh, kh, _DIMS["nt"], preferred_element_type=F32) * scale
            p = _softmax_rows(sc)
            pb = p.astype(BF16)
            dkv_ref[1, :, sl] += lax.dot_general(pb, doh, _DIMS["tn"], preferred_element_type=F32)
            dp = lax.dot_general(doh, kv_ref[1, :, sl], _DIMS["nt"], preferred_element_type=F32)
            ds = (p * (dp - jnp.sum(dp * p, axis=-1, keepdims=True)) * scale).astype(BF16)
            dq_ref[:, sl] = jnp.dot(ds, kh, preferred_element_type=F32).astype(BF16)
            dkv_ref[0, :, sl] += lax.dot_general(ds, qh, _DIMS["tn"], preferred_element_type=F32)

    row = pl.BlockSpec((r, d), lambda i: (i, 0))
    kvs = pl.BlockSpec((2, m, d), lambda i: (0, 0, 0))
    return pl.pallas_call(
        body, name=name, out_shape=[jax.ShapeDtypeStruct((s, d), BF16), jax.ShapeDtypeStruct((2, m, d), F32)],
        grid=(s // r,), in_specs=[row, kvs, row], out_specs=[row, kvs], compiler_params=_params("arbitrary"))(q, kv3, do)


FFN_COL_TILE = 512


def _ffn_gu_fwd(h, w_gu, name):
    s, d = h.shape
    f = w_gu.shape[2] // 2
    tm = _div_tile(s, MM_OUT_TILE_CAP)
    tn = _div_tile(f, FFN_COL_TILE)
    nj = f // tn

    def body(a_ref, bg_ref, bu_ref, gu_ref, act_ref):
        a = a_ref[...]
        gate = jnp.dot(a, bg_ref[...], preferred_element_type=F32)
        up = jnp.dot(a, bu_ref[...], preferred_element_type=F32)
        gu_ref[0] = gate.astype(BF16)
        gu_ref[1] = up.astype(BF16)
        act_ref[...] = (gate * jax.nn.sigmoid(gate) * up).astype(BF16)

    return pl.pallas_call(
        body, name=name, out_shape=[jax.ShapeDtypeStruct((2, s, f), BF16), jax.ShapeDtypeStruct((s, f), BF16)],
        grid=(s // tm, nj),
        in_specs=[pl.BlockSpec((tm, d), lambda i, j: (i, 0)), pl.BlockSpec((None, d, tn), lambda i, j: (0, 0, j)),
                  pl.BlockSpec((None, d, tn), lambda i, j: (0, 0, j + nj))],
        out_specs=[pl.BlockSpec((2, tm, tn), lambda i, j: (0, i, j)), pl.BlockSpec((tm, tn), lambda i, j: (i, j))],
        compiler_params=_params("parallel", "parallel"))(h, w_gu, w_gu)


def _ffn_down_bwd(dy, w_down, gu3, name):
    s, d = dy.shape
    f = w_down.shape[1]
    tm = _div_tile(s, MM_OUT_TILE_CAP)
    tn = _div_tile(f, FFN_COL_TILE)

    def body(dy_ref, w_ref, gu_ref, o_ref):
        da = lax.dot_general(dy_ref[...], w_ref[...], _DIMS["nt"], preferred_element_type=F32)
        gate = gu_ref[0].astype(F32)
        up = gu_ref[1].astype(F32)
        sg = jax.nn.sigmoid(gate)
        o_ref[0] = (da * up * (sg + gate * sg * (1.0 - sg))).astype(BF16)
        o_ref[1] = (da * gate * sg).astype(BF16)

    return pl.pallas_call(
        body, name=name, out_shape=jax.ShapeDtypeStruct((2, s, f), BF16), grid=(s // tm, f // tn),
        in_specs=[pl.BlockSpec((tm, d), lambda i, j: (i, 0)), pl.BlockSpec((None, tn, d), lambda i, j: (0, j, 0)),
                  pl.BlockSpec((2, tm, tn), lambda i, j: (0, i, j))],
        out_specs=pl.BlockSpec((2, tm, tn), lambda i, j: (0, i, j)),
        compiler_params=_params("parallel", "parallel"))(dy, w_down, gu3)


def _ids():
    x, y, c = lax.axis_index("x"), lax.axis_index("y"), lax.axis_index("c")
    return x, y, c, 2 * x + y


def _chip_peers(x, y):
    return [(d - 1, 2 * (x ^ (d >> 1)) + (y ^ (d & 1)), x ^ (d >> 1), y ^ (d & 1)) for d in (1, 2, 3)]


def _remote(src, dst, ssem, rsem, dev):
    return pltpu.make_async_remote_copy(src_ref=src, dst_ref=dst, send_sem=ssem, recv_sem=rsem, device_id=dev,
                                        device_id_type=MESH)


def _gview(ref, kind, j, cc):
    _, k, n = ref.shape
    if kind == "row":
        return ref.at[:, pl.ds(j * (k // N_CHIPS) + cc * (k // (2 * N_CHIPS)), k // (2 * N_CHIPS)), :]
    return ref.at[:, pl.ds(cc * (k // 2), k // 2), pl.ds(j * (n // N_CHIPS), n // N_CHIPS)]


def _sview(ref, cc):
    r = ref.shape[1]
    return ref.at[:, pl.ds(cc * (r // 2), r // 2), :]


def _comm_call(body, name, ins, out_shapes, n_sems, aliases=None):
    return pl.pallas_call(
        body, name=name, out_shape=out_shapes, in_specs=[HBM] * len(ins), out_specs=[HBM] * len(out_shapes),
        scratch_shapes=[pltpu.SemaphoreType.DMA((n,)) for n in n_sems], input_output_aliases=aliases or {},
        compiler_params=pltpu.CompilerParams(has_side_effects=True))(*ins)


def _mesh_scalars():
    x, y, c = lax.axis_index("x"), lax.axis_index("y"), lax.axis_index("c")
    return jnp.stack([2 * x + y, c]).astype(jnp.int32)


def _slab_rows(rows, cols, itemsize=4):
    for cand in (512, 256, 128, 64, 32, 16):
        if rows % cand == 0 and cand * cols * itemsize <= (2 << 20):
            return cand
    return rows


def _ag_place(shard, layer, kind, name):
    _, r, n = shard.shape
    full = (1, r * N_CHIPS, n) if kind == "row" else (1, r, n * N_CHIPS)
    tr = _slab_rows(r, n)
    nt = r // tr
    if kind == "row":
        out_spec = pl.BlockSpec((None, tr, n), lambda t, s: (0, s[0] * nt + t, 0))
    else:
        out_spec = pl.BlockSpec((None, tr, n), lambda t, s: (0, t, s[0]))

    def body(s_ref, i_ref, o_ref):
        o_ref[...] = i_ref[...].astype(BF16)

    return pl.pallas_call(
        body, name=name, out_shape=jax.ShapeDtypeStruct(full, BF16),
        grid_spec=pltpu.PrefetchScalarGridSpec(
            num_scalar_prefetch=1, grid=(nt,), in_specs=[pl.BlockSpec((None, tr, n), lambda t, s: (layer, t, 0))],
            out_specs=out_spec),
        compiler_params=_params("parallel"))(_mesh_scalars(), shard)


SEM = pl.BlockSpec(memory_space=pltpu.SEMAPHORE)
ANY = pl.BlockSpec(memory_space=pl.ANY)
DATAFLOW = pltpu.SideEffectType.DATAFLOW_SIDE_EFFECTING


def _in_hbm(arrs):
    return [pltpu.with_memory_space_constraint(a, pltpu.HBM) for a in arrs]


def _ag_start(bufs, kinds, name):
    n = len(bufs)

    def body(*refs):
        ssem, rsem, token = refs[n], refs[n + 1], refs[-1]
        x, y, c, me = _ids()
        for t in range(n):
            mine = _gview(refs[t], kinds[t], me, c)
            for d, _, px, py in _chip_peers(x, y):
                _remote(mine, mine, ssem.at[3 * t + d], rsem.at[3 * t + d], (px, py, c)).start()
        token[...] = jnp.zeros_like(token)

    outs = pl.pallas_call(
        body, name=name,
        out_shape=(pltpu.SemaphoreType.DMA((3 * n,)), pltpu.SemaphoreType.DMA((3 * n,)),
                   *[pltpu.HBM(b.shape, b.dtype) for b in bufs], jax.ShapeDtypeStruct((8, 128), F32)),
        in_specs=[HBM] * n, out_specs=(SEM, SEM, *[HBM] * n, pl.BlockSpec(memory_space=pltpu.VMEM)),
        input_output_aliases={t: 2 + t for t in range(n)},
        compiler_params=pltpu.CompilerParams(has_side_effects=DATAFLOW))(*_in_hbm(bufs))
    return outs[0], outs[1], list(outs[2:2 + n]), outs[-1]


def _ag_wait(ssem, rsem, bufs, kinds, after, name):
    n = len(bufs)

    def body(*refs):
        ssem_ref, rsem_ref = refs[n], refs[n + 1]
        x, y, c, me = _ids()
        for t in range(n):
            mine = _gview(refs[t], kinds[t], me, c)
            for d, pj, px, py in _chip_peers(x, y):
                theirs = _gview(refs[t], kinds[t], pj, c)
                _remote(mine, mine, ssem_ref.at[3 * t + d], rsem_ref.at[3 * t + d], (px, py, c)).wait_send()
                _remote(theirs, theirs, ssem_ref.at[3 * t + d], rsem_ref.at[3 * t + d], (px, py, c)).wait_recv()

    outs = pl.pallas_call(
        body, name=name, out_shape=[pltpu.HBM(b.shape, b.dtype) for b in bufs],
        in_specs=[HBM] * n + [SEM, SEM, ANY], out_specs=[HBM] * n, input_output_aliases={t: t for t in range(n)},
        compiler_params=pltpu.CompilerParams(has_side_effects=DATAFLOW))(*bufs, ssem, rsem, after)
    return list(outs)


def _ag_forward(bufs, kinds, name):
    n = len(bufs)

    def body(*refs):
        outs = refs[n:2 * n]
        ssem, rsem = refs[2 * n], refs[2 * n + 1]
        x, y, c, _ = _ids()
        sib = (x, y, 1 - c)
        sends = []
        for t in range(n):
            for d, pj, _, _ in _chip_peers(x, y):
                piece = _gview(outs[t], kinds[t], pj, c)
                sends.append(_remote(piece, piece, ssem.at[3 * t + d], rsem.at[3 * t + d], sib))
        for cp in sends:
            cp.start()
        for t in range(n):
            for d, pj, _, _ in _chip_peers(x, y):
                piece = _gview(outs[t], kinds[t], pj, 1 - c)
                _remote(piece, piece, ssem.at[3 * t + d], rsem.at[3 * t + d], sib).wait_recv()
        for cp in sends:
            cp.wait_send()

    return _comm_call(body, name, bufs, [jax.ShapeDtypeStruct(b.shape, b.dtype) for b in bufs], (3 * n, 3 * n),
                      {t: t for t in range(n)})


def _rs1(g_fulls, kinds, name, dep=None):
    n = len(g_fulls)
    outs = []
    for g, kind in zip(g_fulls, kinds):
        l, k, nn = g.shape
        piece = (l, k // (2 * N_CHIPS), nn) if kind == "row" else (l, k // 2, nn // N_CHIPS)
        outs.append(jax.ShapeDtypeStruct((N_CHIPS,) + piece, g.dtype))

    n_in = n + (dep is not None)

    def body(*refs):
        ssem, rsem = refs[n_in + n], refs[n_in + n + 1]
        x, y, c, _ = _ids()
        sends = [_remote(_gview(refs[t], kinds[t], j, 1 - c), refs[n_in + t].at[j], ssem.at[4 * t + j], rsem.at[4 * t + j],
                         (x, y, 1 - c)) for t in range(n) for j in range(N_CHIPS)]
        for cp in sends:
            cp.start()
        for cp in sends:
            cp.wait()

    return _comm_call(body, name, g_fulls + ([] if dep is None else [dep]), outs, (4 * n, 4 * n))


def _rs_add1(g_full, got, kind, name):
    l, k, n = g_full.shape
    _, _, pr, pc = got.shape
    tr = _slab_rows(pr, pc, 2)
    nt = pr // tr
    if kind == "row":
        g_spec = pl.BlockSpec((None, tr, n), lambda j, li, t, s: (li, (2 * j + s[1]) * nt + t, 0))
    else:
        g_spec = pl.BlockSpec((None, tr, pc), lambda j, li, t, s: (li, s[1] * nt + t, j))
    slot = pl.BlockSpec((None, None, tr, pc), lambda j, li, t, s: (j, li, t, 0))

    def body(s_ref, g_ref, got_ref, o_ref):
        o_ref[...] = (g_ref[...].astype(F32) + got_ref[...].astype(F32)).astype(BF16)

    return pl.pallas_call(
        body, name=name, out_shape=jax.ShapeDtypeStruct(got.shape, BF16),
        grid_spec=pltpu.PrefetchScalarGridSpec(num_scalar_prefetch=1, grid=(N_CHIPS, l, nt), in_specs=[g_spec, slot],
                                               out_specs=slot),
        compiler_params=_params("parallel", "parallel", "parallel"))(_mesh_scalars(), g_full, got)


def _rs2_start(ps, name):
    n = len(ps)
    lands = [lax.empty(p.shape, p.dtype) for p in ps]

    def body(*refs):
        ssem, rsem, token = refs[2 * n], refs[2 * n + 1], refs[-1]
        x, y, c, me = _ids()
        for t in range(n):
            for d, pj, px, py in _chip_peers(x, y):
                _remote(refs[t].at[pj], refs[n + t].at[me], ssem.at[3 * t + d], rsem.at[3 * t + d], (px, py, c)).start()
        token[...] = jnp.zeros_like(token)

    outs = pl.pallas_call(
        body, name=name,
        out_shape=(pltpu.SemaphoreType.DMA((3 * n,)), pltpu.SemaphoreType.DMA((3 * n,)),
                   *[pltpu.HBM(p.shape, p.dtype) for p in ps + lands], jax.ShapeDtypeStruct((8, 128), F32)),
        in_specs=[HBM] * (2 * n), out_specs=(SEM, SEM, *[HBM] * (2 * n), pl.BlockSpec(memory_space=pltpu.VMEM)),
        input_output_aliases={t: 2 + t for t in range(2 * n)},
        compiler_params=pltpu.CompilerParams(has_side_effects=DATAFLOW))(*_in_hbm(ps + lands))
    return outs[0], outs[1], list(outs[2:2 + n]), list(outs[2 + n:2 + 2 * n]), outs[-1]


def _rs2_wait(ssem, rsem, ps, lands, after, name):
    n = len(ps)

    def body(*refs):
        ssem_ref, rsem_ref = refs[2 * n], refs[2 * n + 1]
        x, y, c, me = _ids()
        for t in range(n):
            for d, pj, px, py in _chip_peers(x, y):
                _remote(refs[t].at[pj], refs[n + t].at[me], ssem_ref.at[3 * t + d], rsem_ref.at[3 * t + d], (px, py, c)).wait_send()
                _remote(refs[t].at[pj], refs[n + t].at[pj], ssem_ref.at[3 * t + d], rsem_ref.at[3 * t + d], (px, py, c)).wait_recv()

    outs = pl.pallas_call(
        body, name=name, out_shape=[pltpu.HBM(p.shape, p.dtype) for p in ps + lands],
        in_specs=[HBM] * (2 * n) + [SEM, SEM, ANY], out_specs=[HBM] * (2 * n),
        input_output_aliases={t: t for t in range(2 * n)},
        compiler_params=pltpu.CompilerParams(has_side_effects=DATAFLOW))(*ps, *lands, ssem, rsem, after)
    return list(outs[:n]), list(outs[n:])


def _rs_add2(p, got, into, layer, name):
    _, _, pr, pc = p.shape
    tr = _slab_rows(pr, pc)
    nt = pr // tr

    def slot(d):
        return pl.BlockSpec((None, None, tr, pc), lambda t, s: (s[0] ^ d, 0, t, 0))

    def body(s_ref, p_ref, g1_ref, g2_ref, g3_ref, i_ref, o_ref):
        o_ref[...] = (p_ref[...].astype(F32) + g1_ref[...].astype(F32) + g2_ref[...].astype(F32) + g3_ref[...].astype(F32))

    return pl.pallas_call(
        body, name=name, out_shape=jax.ShapeDtypeStruct(into.shape, F32),
        grid_spec=pltpu.PrefetchScalarGridSpec(
            num_scalar_prefetch=1, grid=(nt,), in_specs=[slot(0), slot(1), slot(2), slot(3), HBM],
            out_specs=pl.BlockSpec((None, tr, pc), lambda t, s: (layer, s[1] * nt + t, 0))),
        input_output_aliases={5: 0},
        compiler_params=_params("parallel"))(_mesh_scalars(), p, got, got, got, into)


def _rs3(shards, layers, name):
    n = len(shards)

    def body(*refs):
        outs = refs[n:2 * n]
        token, ssem, rsem = refs[2 * n], refs[2 * n + 1], refs[2 * n + 2]
        x, y, c, _ = _ids()
        sib = (x, y, 1 - c)
        token[...] = jnp.zeros_like(token)

        def half(t, cc):
            return _sview(outs[t].at[pl.ds(layers[t], 1)], cc)

        sends = [_remote(half(t, c), half(t, c), ssem.at[t], rsem.at[t], sib) for t in range(n)]
        for cp in sends:
            cp.start()
        for t in range(n):
            _remote(half(t, 1 - c), half(t, 1 - c), ssem.at[t], rsem.at[t], sib).wait_recv()
        for cp in sends:
            cp.wait_send()

    outs = pl.pallas_call(
        body, name=name, out_shape=[jax.ShapeDtypeStruct(s.shape, s.dtype) for s in shards] + [jax.ShapeDtypeStruct((8, 128), F32)],
        in_specs=[HBM] * n, out_specs=[HBM] * n + [pl.BlockSpec(memory_space=pltpu.VMEM)],
        scratch_shapes=[pltpu.SemaphoreType.DMA((n,)), pltpu.SemaphoreType.DMA((n,))],
        input_output_aliases={t: t for t in range(n)}, compiler_params=pltpu.CompilerParams(has_side_effects=True))(*shards)
    return list(outs[:n]), outs[n]


def _ag_small(sp, name):
    def body(s_ref, o_ref, ssem, rsem, lsem):
        x, y, c, me = _ids()
        local = pltpu.make_async_copy(s_ref, o_ref.at[me], lsem.at[0])
        local.start()
        sends = [_remote(s_ref, o_ref.at[me], ssem.at[d], rsem.at[d], (px, py, c)) for d, _, px, py in _chip_peers(x, y)]
        for cp in sends:
            cp.start()
        for d, pj, px, py in _chip_peers(x, y):
            _remote(s_ref, o_ref.at[pj], ssem.at[d], rsem.at[d], (px, py, c)).wait_recv()
        for cp in sends:
            cp.wait_send()
        local.wait()

    return _comm_call(body, name, [sp], [jax.ShapeDtypeStruct((N_CHIPS,) + sp.shape, sp.dtype)], (3, 3, 1))[0]


def _gather8(g, name):
    def body(g_ref, o_ref, ssem, rsem, lsem):
        x, y, c, _ = _ids()
        me = 4 * x + 2 * y + c
        local = pltpu.make_async_copy(g_ref, o_ref.at[me], lsem.at[0])
        local.start()
        peers = [(d - 1, x ^ (d >> 2), y ^ ((d >> 1) & 1), c ^ (d & 1)) for d in range(1, 8)]
        sends = [_remote(g_ref, o_ref.at[me], ssem.at[d], rsem.at[d], (px, py, pc)) for d, px, py, pc in peers]
        for cp in sends:
            cp.start()
        for d, px, py, pc in peers:
            _remote(g_ref, o_ref.at[4 * px + 2 * py + pc], ssem.at[d], rsem.at[d], (px, py, pc)).wait_recv()
        for cp in sends:
            cp.wait_send()
        local.wait()

    return _comm_call(body, name, [g], [jax.ShapeDtypeStruct((8,) + g.shape, g.dtype)], (7, 7, 1))[0]


def _sum_slots(a, out_dtype, name):
    n = a.shape[0]
    shape = a.shape[1:]
    cols = shape[-1]
    a3 = a.reshape(n, -1, cols)
    rows = a3.shape[1]
    tr = rows
    for cand in (512, 256, 128, 64, 32, 16):
        if rows % cand == 0 and cand * cols * 4 <= (1 << 20):
            tr = cand
            break

    def body(a_ref, o_ref):
        acc = a_ref[0].astype(F32)
        for j in range(1, n):
            acc = acc + a_ref[j].astype(F32)
        o_ref[...] = acc.astype(o_ref.dtype)

    out = pl.pallas_call(
        body, name=name, out_shape=jax.ShapeDtypeStruct((rows, cols), out_dtype), grid=(rows // tr,),
        in_specs=[pl.BlockSpec((n, tr, cols), lambda i: (0, i, 0))], out_specs=pl.BlockSpec((tr, cols), lambda i: (i, 0)),
        compiler_params=_params("parallel"))(a3)
    return out.reshape(shape)


def kernel(x, mem, mix_norm, xa_norm, xa_wq, xa_wkv, xa_wo, ffn_norm, ffn_w_gu, ffn_w_down, a_w_in, a_conv_w, a_w_out, b_w_in, b_v_g, b_v_b, b_w_s, b_s_bias, b_w_out, c_w_in, c_conv_w, c_conv_b, c_ln_g, c_ln_b, c_w_out, loss_target, m_mix_norm, m_xa_norm, m_xa_wq, m_xa_wkv, m_xa_wo, m_ffn_norm, m_ffn_w_gu, m_ffn_w_down, m_a_w_in, m_a_conv_w, m_a_w_out, m_b_w_in, m_b_v_g, m_b_v_b, m_b_w_s, m_b_s_bias, m_b_w_out, m_c_w_in, m_c_conv_w, m_c_conv_b, m_c_ln_g, m_c_ln_b, m_c_w_out, v_mix_norm, v_xa_norm, v_xa_wq, v_xa_wkv, v_xa_wo, v_ffn_norm, v_ffn_w_gu, v_ffn_w_down, v_a_w_in, v_a_conv_w, v_a_w_out, v_b_w_in, v_b_v_g, v_b_v_b, v_b_w_s, v_b_s_bias, v_b_w_out, v_c_w_in, v_c_conv_w, v_c_conv_b, v_c_ln_g, v_c_ln_b, v_c_w_out):
    given = dict(locals())
    w = {n: given[n] for n in WEIGHTS}
    depth = mix_norm.shape[0]
    s, d = x.shape[1], x.shape[2]
    n_mem = mem.shape[1]
    ds = d // N_CHIPS
    xin = x.reshape(s, d)
    memv = mem.reshape(n_mem, d)
    target = loss_target.reshape(s, d)
    me = 2 * lax.axis_index("x") + lax.axis_index("y")

    placed = {(n, l): _ag_place(w[n], l, kind, f"ag_place_{n}_{l}") for n, kind in BIG_KINDS.items()
              for l in range(w[n].shape[0])}
    wg = {n: [None] * w[n].shape[0] for n in BIG_KINDS}

    def mixer_keys(i):
        return [("abc"[i % 3] + "_w_in", i // 3), ("abc"[i % 3] + "_w_out", i // 3)]

    def rest_keys(i):
        return [("xa_wq", i), ("xa_wkv", i), ("xa_wo", i), ("ffn_w_gu", i), ("ffn_w_down", i)]

    def ag_begin(keys, tag):
        kinds = [BIG_KINDS[n] for n, _ in keys]
        ssem, rsem, bufs, token = _ag_start([placed[k] for k in keys], kinds, "ag_start_" + tag)
        return keys, kinds, ssem, rsem, bufs, token, tag

    def ag_end(state, after):
        keys, kinds, ssem, rsem, bufs, _, tag = state
        bufs = _ag_forward(_ag_wait(ssem, rsem, bufs, kinds, after, "ag_wait_" + tag), kinds, "ag_fwd_" + tag)
        for (n, l), buf in zip(keys, bufs):
            wg[n][l] = buf

    def pad8(t):
        return jnp.pad(t, ((0, (-t.shape[0]) % 8), (0, 0)))

    small_rows = [w[n].reshape(-1, ds) for n in SMALL_SHARDED]
    counts = [t.shape[0] for t in small_rows]
    gathered = _ag_small(jnp.concatenate([pad8(t) for t in small_rows], axis=0), "ag_small")
    gathered = jnp.transpose(gathered, (1, 0, 2)).reshape(-1, d)
    full, off = {}, 0
    for n, cnt in zip(SMALL_SHARDED, counts):
        full[n] = gathered[off:off + cnt].reshape(w[n].shape[:-1] + (d,))
        off += cnt + (-cnt) % 8
    t_chunk = b_w_s.shape[-1]
    tril = jnp.tril(jnp.ones((t_chunk, t_chunk), dtype=bool))

    def vec(a):
        return a.reshape(1, -1)

    def b_params(slot):
        ws_m = jnp.where(tril[None], b_w_s[slot], 0.0).astype(BF16)
        sbt = jnp.zeros((t_chunk, 128), F32).at[:, :b_s_bias.shape[1]].set(b_s_bias[slot].T)
        return vec(b_v_g[slot]), vec(b_v_b[slot]), ws_m, sbt

    saved = []
    xc = xin
    state = ag_begin(mixer_keys(0), "0m")
    ag_end(state, state[5])
    for i in range(depth):
        kind, slot = i % 3, i // 3
        t = f"{i}"
        state = ag_begin(rest_keys(0), "0r") if i == 0 else (
            ag_begin(mixer_keys(i + 1) + rest_keys(i + 1), f"{i + 1}") if i + 1 < depth else None)
        sv = {"x0": xc}
        dep = None if state is None else state[5]
        if i == 0:
            h = _rms_fwd(xc, vec(full["mix_norm"][i, 0]), "rms_mix_" + t)
        sv["h1"] = h
        if kind == 0:
            pre = _mm("nn", h, wg["a_w_in"], BF16, "a_in_" + t, bl=slot, o_parts=3, dep=dep)
            mid = _a_mid_fwd(pre, full["a_conv_w"][slot], "a_mid_" + t)
            y = _mm("nn", mid, wg["a_w_out"], F32, "a_out_" + t, bl=slot)
        elif kind == 1:
            pre = _mm("nn", h, wg["b_w_in"], BF16, "b_in_" + t, bl=slot, o_parts=2, dep=dep)
            mid = _b_mid_fwd(pre, *b_params(slot), "b_mid_" + t)
            y = _mm("nn", mid, wg["b_w_out"], F32, "b_out_" + t, bl=slot)
        else:
            pre = _mm("nn", h, wg["c_w_in"], BF16, "c_in_" + t, bl=slot, o_parts=2, dep=dep)
            y2 = _c_conv_fwd(pre, full["c_conv_w"][slot], vec(full["c_conv_b"][slot]), "c_conv_" + t)
            sv["cy2"] = y2
            mid = _c_ln_fwd(y2, vec(full["c_ln_g"][slot]), vec(full["c_ln_b"][slot]), "c_ln_" + t)
            y = _mm("nn", mid, wg["c_w_out"], F32, "c_out_" + t, bl=slot)
        sv.update(pre=pre, mid=mid, y1=y)
        xc, h = _res_rms_fwd(xc, y, vec(full["mix_norm"][i, 1]), vec(full["xa_norm"][i, 0]), "res_mix_" + t)

        sv["x1"] = xc
        if i == 0:
            ag_end(state, xc)
            state = ag_begin(mixer_keys(1) + rest_keys(1), "1")
        mem_n = _rms_fwd(memv, vec(full["xa_norm"][i, 2]), "rms_mem_" + t)
        q = _mm("nn", h, wg["xa_wq"], BF16, "xa_q_" + t, bl=i, dep=state[5] if i == 0 else None)
        kv3 = _mm("nn", mem_n, wg["xa_wkv"], BF16, "xa_kv_" + t, bl=i, o_parts=2)
        o = _attn_fwd(q, kv3, "attn_" + t)
        y = _mm("nn", o, wg["xa_wo"], F32, "xa_o_" + t, bl=i)
        sv.update(h2=h, mem_n=mem_n, q=q, kv3=kv3, o=o, y2=y)
        xc, h = _res_rms_fwd(xc, y, vec(full["xa_norm"][i, 1]), vec(full["ffn_norm"][i, 0]), "res_xa_" + t)

        sv["x2"] = xc
        gu3, act = _ffn_gu_fwd(h, wg["ffn_w_gu"][i], "ffn_gu_" + t)
        y = _mm("nn", act, wg["ffn_w_down"], F32, "ffn_down_" + t, bl=i)
        sv.update(h3=h, gu3=gu3, act=act, y3=y)
        xc, h = _res_rms_fwd(xc, y, vec(full["ffn_norm"][i, 1]),
                             vec(full["mix_norm"][i + 1, 0]) if i + 1 < depth else None, "res_ffn_" + t)
        saved.append(sv)
        if state is not None:
            ag_end(state, xc)

    loss_blk, dx = _loss(xc, target, "loss")
    loss = lax.psum(loss_blk[0, 0], ("x", "y", "c"))

    gbuf = {}
    gfin = {n: lax.empty(w[n].shape, F32) for n in BIG_KINDS}
    gsmall = {n: [None] * full[n].shape[0] for n in ("mix_norm", "xa_norm", "ffn_norm", "a_conv_w", "c_conv_w", "c_conv_b",
                                                      "c_ln_g", "c_ln_b")}
    grepl = {}

    def wgrad(name, l, a, dy, tag, b_parts=1):
        g2 = _mm("tn", a, dy, BF16, "wg_" + tag, b_parts=b_parts)
        gbuf[name, l] = g2.reshape((1,) + g2.shape)

    def rs_begin(keys, tag, dep):
        kinds = [BIG_KINDS[n] for n, _ in keys]
        gots = _rs1([gbuf[k] for k in keys], kinds, "rs1_" + tag, dep=dep)
        ps = [_rs_add1(gbuf[k], got, kind, f"rs_add1_{k[0]}_{k[1]}") for k, got, kind in zip(keys, gots, kinds)]
        ssem, rsem, ps, lands, token = _rs2_start(ps, "rs2_start_" + tag)
        return keys, ssem, rsem, ps, lands, token, tag

    def rs_end(state, after):
        keys, ssem, rsem, ps, lands, _, tag = state
        ps, lands = _rs2_wait(ssem, rsem, ps, lands, after, "rs2_wait_" + tag)
        for (n, l), p, land in zip(keys, ps, lands):
            gfin[n] = _rs_add2(p, land, gfin[n], l, f"rs_add2_{n}_{l}")
        outs, token = _rs3([gfin[n] for n, _ in keys], [l for _, l in keys], "rs3_" + tag)
        for (n, _), o in zip(keys, outs):
            gfin[n] = o
        return token

    rs_state, rs_token = None, None

    for i in reversed(range(depth)):
        kind, slot = i % 3, i // 3
        t = f"{i}"
        sv = saved[i]
        dy, dg_post = _rms_bwd(sv["y3"], vec(full["ffn_norm"][i, 1]), dx, None, BF16, "rmsb_ffn_post_" + t,
                               dep=None if rs_state is None else rs_state[5])
        wgrad("ffn_w_down", i, sv["act"], dy, "ffn_down_" + t)
        dgu3 = _ffn_down_bwd(dy, wg["ffn_w_down"][i], sv["gu3"], "dg_ffn_down_" + t)
        wgrad("ffn_w_gu", i, sv["h3"], dgu3, "ffn_gu_" + t, b_parts=2)
        dh = _mm("nt", dgu3, wg["ffn_w_gu"], F32, "dg_ffn_gu_" + t, bl=i, a_parts=2)
        dx, dg_pre = _rms_bwd(sv["x2"], vec(full["ffn_norm"][i, 0]), dh, dx, F32, "rmsb_ffn_pre_" + t)
        gsmall["ffn_norm"][i] = jnp.concatenate([dg_pre, dg_post], axis=0)
        dy, dg_post = _rms_bwd(sv["y2"], vec(full["xa_norm"][i, 1]), dx, None, BF16, "rmsb_xa_post_" + t)
        wgrad("xa_wo", i, sv["o"], dy, "xa_o_" + t)
        do = _mm("nt", dy, wg["xa_wo"], BF16, "dg_xa_o_" + t, bl=i)
        dq, dkv3 = _attn_bwd(sv["q"], sv["kv3"], do, "attn_b_" + t)
        wgrad("xa_wq", i, sv["h2"], dq, "xa_q_" + t)
        dh = _mm("nt", dq, wg["xa_wq"], F32, "dg_xa_q_" + t, bl=i)
        dkv3 = dkv3.astype(BF16)
        wgrad("xa_wkv", i, sv["mem_n"], dkv3, "xa_kv_" + t, b_parts=2)
        dmem_n = _mm("nt", dkv3, wg["xa_wkv"], F32, "dg_xa_kv_" + t, bl=i, a_parts=2)
        _, dg_mem = _rms_bwd(memv, vec(full["xa_norm"][i, 2]), dmem_n, None, F32, "rmsb_mem_" + t)
        dx, dg_pre = _rms_bwd(sv["x1"], vec(full["xa_norm"][i, 0]), dh, dx, F32, "rmsb_xa_pre_" + t)
        gsmall["xa_norm"][i] = jnp.concatenate([dg_pre, dg_post, dg_mem], axis=0)
        if i == 0:
            rs_token = rs_end(rs_state, dx)
            rs_state = rs_begin(rest_keys(0), "0r", rs_token)
        dy, dg_post = _rms_bwd(sv["y1"], vec(full["mix_norm"][i, 1]), dx, None, BF16, "rmsb_mix_post_" + t,
                               dep=rs_state[5] if i == 0 else None)
        if kind == 0:
            wgrad("a_w_out", slot, sv["mid"], dy, "a_out_" + t)
            dmid = _mm("nt", dy, wg["a_w_out"], F32, "dg_a_out_" + t, bl=slot)
            dpre, dcw = _a_mid_bwd(sv["pre"], dmid, full["a_conv_w"][slot], "a_mid_b_" + t)
            gsmall["a_conv_w"][slot] = dcw
            wgrad("a_w_in", slot, sv["h1"], dpre, "a_in_" + t, b_parts=3)
            dh = _mm("nt", dpre, wg["a_w_in"], F32, "dg_a_in_" + t, bl=slot, a_parts=3)
        elif kind == 1:
            wgrad("b_w_out", slot, sv["mid"], dy, "b_out_" + t)
            dmid = _mm("nt", dy, wg["b_w_out"], F32, "dg_b_out_" + t, bl=slot)
            dpre, dvg, dvb, dws, dsbt = _b_mid_bwd(sv["pre"], dmid, *b_params(slot), "b_mid_b_" + t)
            grepl[slot] = (dvg, dvb, dws, dsbt[:, :b_s_bias.shape[1]].T)
            wgrad("b_w_in", slot, sv["h1"], dpre, "b_in_" + t, b_parts=2)
            dh = _mm("nt", dpre, wg["b_w_in"], F32, "dg_b_in_" + t, bl=slot, a_parts=2)
        else:
            wgrad("c_w_out", slot, sv["mid"], dy, "c_out_" + t)
            dmid = _mm("nt", dy, wg["c_w_out"], F32, "dg_c_out_" + t, bl=slot)
            dy2, dlg, dlb = _c_ln_bwd(sv["cy2"], dmid, vec(full["c_ln_g"][slot]), vec(full["c_ln_b"][slot]), "c_ln_b_" + t)
            dpre, dcw, dcb = _c_conv_bwd(sv["pre"], dy2, full["c_conv_w"][slot], "c_conv_b_" + t)
            gsmall["c_conv_w"][slot], gsmall["c_conv_b"][slot] = dcw, dcb
            gsmall["c_ln_g"][slot], gsmall["c_ln_b"][slot] = dlg, dlb
            wgrad("c_w_in", slot, sv["h1"], dpre, "c_in_" + t, b_parts=2)
            dh = _mm("nt", dpre, wg["c_w_in"], F32, "dg_c_in_" + t, bl=slot, a_parts=2)
        dx, dg_pre = _rms_bwd(sv["x0"], vec(full["mix_norm"][i, 0]), dh, dx, F32, "rmsb_mix_pre_" + t)
        gsmall["mix_norm"][i] = jnp.concatenate([dg_pre, dg_post], axis=0)
        if rs_state is not None:
            rs_token = rs_end(rs_state, dx)
        rs_state = rs_begin(mixer_keys(i) + (rest_keys(i) if i > 0 else []), f"{i}" if i > 0 else "0m", rs_token)
    rs_end(rs_state, rs_state[5])
    grad_x = dx.reshape(x.shape)

    grads = dict(gfin)
    small_g = [jnp.concatenate(gsmall[n], axis=0).reshape(-1, d) for n in SMALL_SHARDED]
    n_b = b_v_g.shape[0]
    repl_g = [jnp.concatenate([grepl[sl][k] for sl in range(n_b)], axis=0) for k in range(4)]
    repl_rows = []
    for g_arr in repl_g:
        flat = g_arr.reshape(-1)
        flat = jnp.concatenate([flat, jnp.zeros(((-flat.shape[0]) % d,), F32)])
        repl_rows.append(flat.reshape(-1, d))
    rows_all = [pad8(t) for t in small_g + repl_rows]
    total = _sum_slots(_gather8(jnp.concatenate(rows_all, axis=0), "gather_small_grads"), F32, "sum_small_grads")
    off = 0
    for n, cnt in zip(SMALL_SHARDED, counts):
        blk = lax.dynamic_slice_in_dim(total[off:off + cnt], me * ds, ds, axis=1)
        grads[n] = blk.reshape(w[n].shape)
        off += cnt + (-cnt) % 8
    for n, g_arr in zip(SMALL_REPL, repl_g):
        cnt = -(-g_arr.size // d)
        grads[n] = total[off:off + cnt].reshape(-1)[:g_arr.size].reshape(w[n].shape)
        off += cnt + (-cnt) % 8

    delta, new_m, new_v = {}, {}, {}
    for n in WEIGHTS:
        delta[n], new_m[n], new_v[n] = _adamw(w[n], grads[n], given["m_" + n], given["v_" + n], "adamw_" + n)
    return (loss, grad_x, *[grads[n] for n in WEIGHTS], *[delta[n] for n in WEIGHTS], *[new_m[n] for n in WEIGHTS],
            *[new_v[n] for n in WEIGHTS])
```

```python
import functools

import jax
import jax.numpy as jnp
from jax import lax
from jax.experimental import pallas as pl
from jax.experimental.pallas import tpu as pltpu

F32 = jnp.float32
BF16 = jnp.bfloat16
EPS = 1e-6
XA_HEADS = 4
CHUNK = 128
GMLP_GROUPS = 8
ADAM_LR, ADAM_B1, ADAM_B2, ADAM_EPS, ADAM_WD, ADAM_STEP = 0.001, 0.9, 0.999, 1e-08, 0.01, 10
VMEM_LIMIT_V7X = 48 * 1024 * 1024
HBM = pl.BlockSpec(memory_space=pltpu.HBM)
MESH = pl.DeviceIdType.MESH
N_CHIPS = 4
BIG_KINDS = {"xa_wq": "row", "xa_wkv": "col", "xa_wo": "row", "ffn_w_gu": "col", "ffn_w_down": "row",
             "a_w_in": "col", "a_w_out": "row", "b_w_in": "col", "b_w_out": "row", "c_w_in": "col", "c_w_out": "row"}
SMALL_SHARDED = ["mix_norm", "xa_norm", "ffn_norm", "a_conv_w", "c_conv_w", "c_conv_b", "c_ln_g", "c_ln_b"]
SMALL_REPL = ["b_v_g", "b_v_b", "b_w_s", "b_s_bias"]
WEIGHTS = ["mix_norm", "xa_norm", "xa_wq", "xa_wkv", "xa_wo", "ffn_norm", "ffn_w_gu", "ffn_w_down", "a_w_in", "a_conv_w",
           "a_w_out", "b_w_in", "b_v_g", "b_v_b", "b_w_s", "b_s_bias", "b_w_out", "c_w_in", "c_conv_w", "c_conv_b",
           "c_ln_g", "c_ln_b", "c_w_out"]


def _params(*sem):
    return pltpu.CompilerParams(dimension_semantics=sem, vmem_limit_bytes=VMEM_LIMIT_V7X)


def _tile(n, cands=(1024, 512, 256, 128)):
    for c in cands:
        if n % c == 0:
            return c
    return n


def _div_tile(n, cap):
    best = None
    for t in range(128, min(n, cap) + 1, 128):
        if n % t == 0:
            best = t
    return best or n


MM_OUT_TILE_CAP = 1408
MM_K_TILE_CAP = 2816

_DIMS = {"nn": (((1,), (0,)), ((), ())), "nt": (((1,), (1,)), ((), ())), "tn": (((0,), (0,)), ((), ()))}


def _mm(mode, a, b, out_dtype, name, *, bl=None, a_parts=1, b_parts=1, o_parts=1, dep=None):
    if isinstance(b, list):
        b, bl = b[bl], 0
    bshape = b.shape[1:] if bl is not None else b.shape
    if mode == "nn":
        mo, c = a.shape
        no = bshape[1]
    elif mode == "nt":
        mo, c = (a.shape[1], a.shape[0] * a.shape[2]) if a_parts > 1 else a.shape
        no = bshape[0]
    else:
        c, mo = a.shape
        no = b.shape[0] * b.shape[2] if b_parts > 1 else bshape[1]
    tmo = _div_tile(mo, MM_OUT_TILE_CAP)
    tno = _div_tile(no // max(o_parts, b_parts), MM_OUT_TILE_CAP)
    tc = _div_tile(c // a_parts, MM_K_TILE_CAP)
    nk = c // tc
    nkp = nk // a_parts
    njp = (no // tno) // max(o_parts, b_parts)
    lead = (None,) if bl is not None else ()
    lidx = (bl,) if bl is not None else ()

    if mode == "nn":
        a_spec = pl.BlockSpec((tmo, tc), lambda i, j, k: (i, k))
        b_spec = pl.BlockSpec(lead + (tc, tno), lambda i, j, k: lidx + (k, j))
    elif mode == "nt":
        if a_parts > 1:
            a_spec = pl.BlockSpec((None, tmo, tc), lambda i, j, k: (k // nkp, i, k % nkp))
        else:
            a_spec = pl.BlockSpec((tmo, tc), lambda i, j, k: (i, k))
        b_spec = pl.BlockSpec(lead + (tno, tc), lambda i, j, k: lidx + (j, k))
    else:
        a_spec = pl.BlockSpec((tc, tmo), lambda i, j, k: (k, i))
        if b_parts > 1:
            b_spec = pl.BlockSpec((None, tc, tno), lambda i, j, k: (j // njp, k, j % njp))
        else:
            b_spec = pl.BlockSpec((tc, tno), lambda i, j, k: (k, j))

    in_specs = [a_spec, b_spec]
    args = [a, b]
    if dep is not None:
        in_specs.append(pl.BlockSpec(memory_space=pl.ANY))
        args.append(dep)
    if o_parts > 1:
        out_shape = jax.ShapeDtypeStruct((o_parts, mo, no // o_parts), out_dtype)
        out_spec = pl.BlockSpec((None, tmo, tno), lambda i, j, k: (j // njp, i, j % njp))
    else:
        out_shape = jax.ShapeDtypeStruct((mo, no), out_dtype)
        out_spec = pl.BlockSpec((tmo, tno), lambda i, j, k: (i, j))
    dims = _DIMS[mode]

    def body(a_ref, b_ref, *rest):
        if nk == 1:
            o_ref = rest[-1]
            o_ref[...] = lax.dot_general(a_ref[...], b_ref[...], dims, preferred_element_type=F32).astype(o_ref.dtype)
            return
        o_ref, acc = rest[-2], rest[-1]
        k = pl.program_id(2)
        part = lax.dot_general(a_ref[...], b_ref[...], dims, preferred_element_type=F32)

        @pl.when(k == 0)
        def _():
            acc[...] = part

        @pl.when(jnp.logical_and(k > 0, k < nk - 1))
        def _():
            acc[...] += part

        @pl.when(k == nk - 1)
        def _():
            o_ref[...] = (acc[...] + part).astype(o_ref.dtype)

    return pl.pallas_call(
        body, name=name, out_shape=out_shape, grid=(mo // tmo, no // tno, nk), in_specs=in_specs, out_specs=out_spec,
        scratch_shapes=[pltpu.VMEM((tmo, tno), F32)] if nk > 1 else [],
        compiler_params=_params("parallel", "parallel", "arbitrary"))(*args)


def _ew(fn, ins, out_dtypes, name, dep=None):
    rows, cols = ins[0].shape
    deps = [] if dep is None else [dep]
    tr = rows
    for cand in (512, 256, 128, 64, 32, 16):
        if rows % cand == 0 and cand * cols * 4 <= (1 << 20):
            tr = cand
            break
    spec = pl.BlockSpec((tr, cols), lambda i: (i, 0))
    n_in = len(ins)

    def body(*refs):
        outs = fn(*[r[...] for r in refs[:n_in]])
        for o_ref, o in zip(refs[n_in + len(deps):], outs):
            o_ref[...] = o.astype(o_ref.dtype)

    return pl.pallas_call(
        body, name=name, out_shape=[jax.ShapeDtypeStruct((rows, cols), d) for d in out_dtypes], grid=(rows // tr,),
        in_specs=[spec] * n_in + [pl.BlockSpec(memory_space=pl.ANY)] * len(deps), out_specs=[spec] * len(out_dtypes),
        compiler_params=_params("parallel"))(*ins, *deps)


def _adamw_fn(w, g, m, v):
    m = ADAM_B1 * m + (1.0 - ADAM_B1) * g
    v = ADAM_B2 * v + (1.0 - ADAM_B2) * (g * g)
    m_hat = m / (1.0 - ADAM_B1 ** ADAM_STEP)
    v_hat = v / (1.0 - ADAM_B2 ** ADAM_STEP)
    delta = -ADAM_LR * (m_hat / (jnp.sqrt(v_hat) + ADAM_EPS) + ADAM_WD * w)
    return delta, m, v


def _adamw(w, g, m, v, name, dep=None, with_grad=False):
    shape = w.shape
    cols = shape[-1]
    flat = [t.reshape(-1, cols) for t in (w, g, m, v)]
    fn = (lambda wv, gv, mv, vv: _adamw_fn(wv, gv, mv, vv) + (gv,)) if with_grad else _adamw_fn
    outs = _ew(fn, flat, [F32] * (4 if with_grad else 3), name, dep=dep)
    return [o.reshape(shape) for o in outs]


def _row_tile(s):
    return _tile(s, (256, 128, 64, 32, 16, 8))


def _rms_fwd(x, g, name, dep=None):
    s, d = x.shape
    r = _row_tile(s)
    deps = [] if dep is None else [dep]

    def body(x_ref, g_ref, *rest):
        o_ref = rest[-1]
        xv = x_ref[...]
        o_ref[...] = (xv * lax.rsqrt(jnp.mean(xv * xv, axis=-1, keepdims=True) + EPS) * g_ref[...]).astype(BF16)

    return pl.pallas_call(
        body, name=name, out_shape=jax.ShapeDtypeStruct((s, d), BF16), grid=(s // r,),
        in_specs=[pl.BlockSpec((r, d), lambda i: (i, 0)), pl.BlockSpec((1, d), lambda i: (0, 0))] + [ANY] * len(deps),
        out_specs=pl.BlockSpec((r, d), lambda i: (i, 0)), compiler_params=_params("parallel"))(x, g, *deps)


def _res_rms_fwd(x, y, g, g_next, name):
    s, d = x.shape
    r = _row_tile(s)
    has_next = g_next is not None

    def body(x_ref, y_ref, g_ref, *rest):
        yv = y_ref[...]
        xn = x_ref[...] + yv * lax.rsqrt(jnp.mean(yv * yv, axis=-1, keepdims=True) + EPS) * g_ref[...]
        rest[-2 if has_next else -1][...] = xn
        if has_next:
            rest[-1][...] = (xn * lax.rsqrt(jnp.mean(xn * xn, axis=-1, keepdims=True) + EPS) * rest[0][...]).astype(BF16)

    row = pl.BlockSpec((r, d), lambda i: (i, 0))
    vec = pl.BlockSpec((1, d), lambda i: (0, 0))
    outs = pl.pallas_call(
        body, name=name,
        out_shape=[jax.ShapeDtypeStruct((s, d), F32)] + ([jax.ShapeDtypeStruct((s, d), BF16)] if has_next else []),
        grid=(s // r,), in_specs=[row, row, vec] + ([vec] if has_next else []), out_specs=[row] * (2 if has_next else 1),
        compiler_params=_params("parallel"))(*([x, y, g] + ([g_next] if has_next else [])))
    return outs[0], (outs[1] if has_next else None)


def _rms_bwd(x, g, dy, resid, out_dtype, name, dep=None):
    s, d = x.shape
    r = _row_tile(s)
    has_res = resid is not None
    deps = [] if dep is None else [dep]

    def body(*refs):
        x_ref, g_ref, dy_ref = refs[:3]
        dx_ref, dg_ref = refs[-2:]
        i = pl.program_id(0)
        xv = x_ref[...]
        dyv = dy_ref[...].astype(F32)
        rstd = lax.rsqrt(jnp.mean(xv * xv, axis=-1, keepdims=True) + EPS)
        n = xv * rstd
        dn = dyv * g_ref[...]
        dx = rstd * (dn - n * jnp.mean(dn * n, axis=-1, keepdims=True))
        if has_res:
            dx = dx + refs[3][...]
        dx_ref[...] = dx.astype(dx_ref.dtype)
        part = jnp.sum(dyv * n, axis=0, keepdims=True)

        @pl.when(i == 0)
        def _():
            dg_ref[...] = part

        @pl.when(i > 0)
        def _():
            dg_ref[...] += part

    row = pl.BlockSpec((r, d), lambda i: (i, 0))
    vec = pl.BlockSpec((1, d), lambda i: (0, 0))
    ins = [x, g, dy] + ([resid] if has_res else []) + deps
    return pl.pallas_call(
        body, name=name, out_shape=[jax.ShapeDtypeStruct((s, d), out_dtype), jax.ShapeDtypeStruct((1, d), F32)],
        grid=(s // r,), in_specs=[row, vec, row] + ([row] if has_res else []) + [ANY] * len(deps), out_specs=[row, vec],
        compiler_params=_params("arbitrary"))(*ins)


def _loss(y, t, name):
    s, d = y.shape
    r = _row_tile(s)

    def body(y_ref, t_ref, l_ref, dy_ref):
        i = pl.program_id(0)
        e = y_ref[...] - t_ref[...]
        dy_ref[...] = e * (1.0 / d)
        part = jnp.full((8, 128), 0.5 * jnp.sum(jnp.mean(e * e, axis=-1, keepdims=True)), F32)

        @pl.when(i == 0)
        def _():
            l_ref[...] = part

        @pl.when(i > 0)
        def _():
            l_ref[...] += part

    row = pl.BlockSpec((r, d), lambda i: (i, 0))
    return pl.pallas_call(
        body, name=name, out_shape=[jax.ShapeDtypeStruct((8, 128), F32), jax.ShapeDtypeStruct((s, d), F32)],
        grid=(s // r,), in_specs=[row, row], out_specs=[pl.BlockSpec((8, 128), lambda i: (0, 0)), row],
        compiler_params=_params("arbitrary"))(y, t)


def _rows(xv, a, m, cache):
    r = a % 8
    q = a - r
    if r == 0:
        return xv[q:q + m]
    if r not in cache:
        cache[r] = pltpu.roll(xv, xv.shape[0] - r, 0)
    return cache[r][q:q + m]


def _conv_taps(xv, w, k_w, halo, m, flip):
    cache = {}
    acc = None
    for k in range(k_w):
        a = (k_w - 1 - k) if flip else (halo + k - (k_w - 1))
        term = w[k:k + 1, :] * _rows(xv, a, m, cache)
        acc = term if acc is None else acc + term
    return acc


def _conv_wgrad(dw_ref, dyv, xv, k_w, halo, m):
    cache = {}
    for k in range(k_w):
        xs = _rows(xv, halo + k - (k_w - 1), m, cache)
        dw_ref[pl.ds(k, 1), :] += jnp.sum(dyv * xs, axis=0, keepdims=True)


def _conv_tiles(s, dp, halo):
    r = _tile(s, (256, 128))
    cw = _tile(dp, (256, 128))
    return r, cw, r // halo


A_HALO = 8


def _a_mid_fwd(bcz3, w, name):
    _, s, d = bcz3.shape
    r, cw, rh = _conv_tiles(s, d, A_HALO)
    k_w = w.shape[0]

    def body(m_ref, h_ref, w_ref, o_ref):
        i = pl.program_id(0)
        cz = m_ref[1].astype(F32) * m_ref[2].astype(F32)
        hcz = h_ref[1].astype(F32) * h_ref[2].astype(F32)
        hcz = jnp.where(i == 0, 0.0, hcz)
        xv = jnp.concatenate([hcz, cz], axis=0)
        y = _conv_taps(xv, w_ref[...], k_w, A_HALO, r, False)
        o_ref[...] = (m_ref[0].astype(F32) * y).astype(BF16)

    return pl.pallas_call(
        body, name=name, out_shape=jax.ShapeDtypeStruct((s, d), BF16), grid=(s // r, d // cw),
        in_specs=[pl.BlockSpec((3, r, cw), lambda i, j: (0, i, j)),
                  pl.BlockSpec((3, A_HALO, cw), lambda i, j: (0, jnp.maximum(i * rh - 1, 0), j)),
                  pl.BlockSpec((k_w, cw), lambda i, j: (0, j))],
        out_specs=pl.BlockSpec((r, cw), lambda i, j: (i, j)), compiler_params=_params("parallel", "parallel"))(bcz3, bcz3, w)


def _a_mid_bwd(bcz3, dgated, w, name):
    _, s, d = bcz3.shape
    r, cw, rh = _conv_tiles(s, d, A_HALO)
    k_w = w.shape[0]
    ni = s // r
    last_h = s // A_HALO - 1

    def body(m_ref, hp_ref, hn_ref, dg_ref, dgn_ref, w_ref, o_ref, dw_ref):
        i = pl.program_id(1)
        wv = w_ref[...]
        b = m_ref[0].astype(F32)
        c = m_ref[1].astype(F32)
        z = m_ref[2].astype(F32)
        hcz = jnp.where(i == 0, 0.0, hp_ref[1].astype(F32) * hp_ref[2].astype(F32))
        xv = jnp.concatenate([hcz, c * z], axis=0)
        y = _conv_taps(xv, wv, k_w, A_HALO, r, False)
        dg = dg_ref[...].astype(F32)
        dy = dg * b
        dyn = jnp.where(i == ni - 1, 0.0, dgn_ref[...].astype(F32) * hn_ref[0].astype(F32))
        dcz = _conv_taps(jnp.concatenate([dy, dyn], axis=0), wv, k_w, A_HALO, r, True)
        o_ref[0] = (dg * y).astype(BF16)
        o_ref[1] = (dcz * z).astype(BF16)
        o_ref[2] = (dcz * c).astype(BF16)

        @pl.when(i == 0)
        def _():
            dw_ref[...] = jnp.zeros_like(dw_ref)

        _conv_wgrad(dw_ref, dy, xv, k_w, A_HALO, r)

    return pl.pallas_call(
        body, name=name, out_shape=[jax.ShapeDtypeStruct((3, s, d), BF16), jax.ShapeDtypeStruct((k_w, d), F32)],
        grid=(d // cw, ni),
        in_specs=[pl.BlockSpec((3, r, cw), lambda j, i: (0, i, j)),
                  pl.BlockSpec((3, A_HALO, cw), lambda j, i: (0, jnp.maximum(i * rh - 1, 0), j)),
                  pl.BlockSpec((3, A_HALO, cw), lambda j, i: (0, jnp.minimum((i + 1) * rh, last_h), j)),
                  pl.BlockSpec((r, cw), lambda j, i: (i, j)),
                  pl.BlockSpec((A_HALO, cw), lambda j, i: (jnp.minimum((i + 1) * rh, last_h), j)),
                  pl.BlockSpec((k_w, cw), lambda j, i: (0, j))],
        out_specs=[pl.BlockSpec((3, r, cw), lambda j, i: (0, i, j)), pl.BlockSpec((k_w, cw), lambda j, i: (0, j))],
        compiler_params=_params("parallel", "arbitrary"))(bcz3, bcz3, bcz3, dgated, dgated, w)


C_HALO = 32


def _c_conv_fwd(ag3, w, bias, name):
    _, s, d = ag3.shape
    r, cw, rh = _conv_tiles(s, d, C_HALO)
    k_w = w.shape[0]

    def body(m_ref, h_ref, w_ref, b_ref, o_ref):
        i = pl.program_id(0)
        y1 = m_ref[0].astype(F32) * jax.nn.sigmoid(m_ref[1].astype(F32))
        h1 = jnp.where(i == 0, 0.0, h_ref[0].astype(F32) * jax.nn.sigmoid(h_ref[1].astype(F32)))
        xv = jnp.concatenate([h1, y1], axis=0)
        o_ref[...] = _conv_taps(xv, w_ref[...], k_w, C_HALO, r, False) + b_ref[...]

    return pl.pallas_call(
        body, name=name, out_shape=jax.ShapeDtypeStruct((s, d), F32), grid=(s // r, d // cw),
        in_specs=[pl.BlockSpec((2, r, cw), lambda i, j: (0, i, j)),
                  pl.BlockSpec((2, C_HALO, cw), lambda i, j: (0, jnp.maximum(i * rh - 1, 0), j)),
                  pl.BlockSpec((k_w, cw), lambda i, j: (0, j)), pl.BlockSpec((1, cw), lambda i, j: (0, j))],
        out_specs=pl.BlockSpec((r, cw), lambda i, j: (i, j)),
        compiler_params=_params("parallel", "parallel"))(ag3, ag3, w, bias)


def _c_conv_bwd(ag3, dy2, w, name):
    _, s, d = ag3.shape
    r, cw, rh = _conv_tiles(s, d, C_HALO)
    k_w = w.shape[0]
    ni = s // r
    last_h = s // C_HALO - 1

    def body(m_ref, hp_ref, dy_ref, dyn_ref, w_ref, o_ref, dw_ref, db_ref):
        i = pl.program_id(1)
        wv = w_ref[...]
        a = m_ref[0].astype(F32)
        sg = jax.nn.sigmoid(m_ref[1].astype(F32))
        h1 = jnp.where(i == 0, 0.0, hp_ref[0].astype(F32) * jax.nn.sigmoid(hp_ref[1].astype(F32)))
        xv = jnp.concatenate([h1, a * sg], axis=0)
        dy = dy_ref[...]
        dyn = jnp.where(i == ni - 1, 0.0, dyn_ref[...])
        dy1 = _conv_taps(jnp.concatenate([dy, dyn], axis=0), wv, k_w, C_HALO, r, True)
        o_ref[0] = (dy1 * sg).astype(BF16)
        o_ref[1] = (dy1 * a * sg * (1.0 - sg)).astype(BF16)

        @pl.when(i == 0)
        def _():
            dw_ref[...] = jnp.zeros_like(dw_ref)
            db_ref[...] = jnp.zeros_like(db_ref)

        db_ref[...] += jnp.sum(dy, axis=0, keepdims=True)
        _conv_wgrad(dw_ref, dy, xv, k_w, C_HALO, r)

    return pl.pallas_call(
        body, name=name,
        out_shape=[jax.ShapeDtypeStruct((2, s, d), BF16), jax.ShapeDtypeStruct((k_w, d), F32),
                   jax.ShapeDtypeStruct((1, d), F32)],
        grid=(d // cw, ni),
        in_specs=[pl.BlockSpec((2, r, cw), lambda j, i: (0, i, j)),
                  pl.BlockSpec((2, C_HALO, cw), lambda j, i: (0, jnp.maximum(i * rh - 1, 0), j)),
                  pl.BlockSpec((r, cw), lambda j, i: (i, j)),
                  pl.BlockSpec((C_HALO, cw), lambda j, i: (jnp.minimum((i + 1) * rh, last_h), j)),
                  pl.BlockSpec((k_w, cw), lambda j, i: (0, j))],
        out_specs=[pl.BlockSpec((2, r, cw), lambda j, i: (0, i, j)), pl.BlockSpec((k_w, cw), lambda j, i: (0, j)),
                   pl.BlockSpec((1, cw), lambda j, i: (0, j))],
        compiler_params=_params("parallel", "arbitrary"))(ag3, ag3, dy2, dy2, w)


def _ln_stats(v):
    mu = jnp.mean(v, axis=-1, keepdims=True)
    vc = v - mu
    rstd = lax.rsqrt(jnp.mean(vc * vc, axis=-1, keepdims=True) + EPS)
    return vc * rstd, rstd


def _ln_bwd(dn, g, xh, rstd):
    dxh = dn * g
    return rstd * (dxh - jnp.mean(dxh, axis=-1, keepdims=True) - xh * jnp.mean(dxh * xh, axis=-1, keepdims=True))


def _c_ln_fwd(y2, g, b, name):
    s, d = y2.shape
    r = _row_tile(s)

    def body(y_ref, g_ref, b_ref, o_ref):
        xh, _ = _ln_stats(y_ref[...])
        y3 = xh * g_ref[...] + b_ref[...]
        o_ref[...] = (y3 * jax.nn.sigmoid(y3)).astype(BF16)

    row = pl.BlockSpec((r, d), lambda i: (i, 0))
    vec = pl.BlockSpec((1, d), lambda i: (0, 0))
    return pl.pallas_call(
        body, name=name, out_shape=jax.ShapeDtypeStruct((s, d), BF16), grid=(s // r,), in_specs=[row, vec, vec],
        out_specs=row, compiler_params=_params("parallel"))(y2, g, b)


def _c_ln_bwd(y2, dout, g, b, name):
    s, d = y2.shape
    r = _row_tile(s)

    def body(y_ref, do_ref, g_ref, b_ref, dy_ref, dg_ref, db_ref):
        i = pl.program_id(0)
        xh, rstd = _ln_stats(y_ref[...])
        gv = g_ref[...]
        y3 = xh * gv + b_ref[...]
        sg = jax.nn.sigmoid(y3)
        dy3 = do_ref[...].astype(F32) * (sg + y3 * sg * (1.0 - sg))
        dy_ref[...] = _ln_bwd(dy3, gv, xh, rstd)

        @pl.when(i == 0)
        def _():
            dg_ref[...] = jnp.zeros_like(dg_ref)
            db_ref[...] = jnp.zeros_like(db_ref)

        dg_ref[...] += jnp.sum(dy3 * xh, axis=0, keepdims=True)
        db_ref[...] += jnp.sum(dy3, axis=0, keepdims=True)

    row = pl.BlockSpec((r, d), lambda i: (i, 0))
    vec = pl.BlockSpec((1, d), lambda i: (0, 0))
    return pl.pallas_call(
        body, name=name,
        out_shape=[jax.ShapeDtypeStruct((s, d), F32), jax.ShapeDtypeStruct((1, d), F32), jax.ShapeDtypeStruct((1, d), F32)],
        grid=(s // r,), in_specs=[row, row, vec, vec], out_specs=[row, vec, vec],
        compiler_params=_params("arbitrary"))(y2, dout, g, b)


_GELU_C = 0.7978845608028654
_GELU_A = 0.044715


def _gelu(x):
    return 0.5 * x * (1.0 + jnp.tanh(_GELU_C * (x + _GELU_A * x * x * x)))


def _gelu_grad(x):
    t = jnp.tanh(_GELU_C * (x + _GELU_A * x * x * x))
    return 0.5 * (1.0 + t) + 0.5 * x * (1.0 - t * t) * _GELU_C * (1.0 + 3.0 * _GELU_A * x * x)


def _b_mid_fwd(uv3, vg, vb, ws_m, sbt, name):
    _, s, h = uv3.shape
    g_n, t, _ = ws_m.shape
    gd = h // g_n

    def body(uv_ref, vg_ref, vb_ref, ws_ref, sb_ref, o_ref):
        u = _gelu(uv_ref[0].astype(F32))
        xh, _ = _ln_stats(_gelu(uv_ref[1].astype(F32)))
        vn = (xh * vg_ref[...] + vb_ref[...]).astype(BF16)
        for g in range(g_n):
            sl = slice(g * gd, (g + 1) * gd)
            sv = jnp.dot(ws_ref[g], vn[:, sl], preferred_element_type=F32) + sb_ref[:, g:g + 1]
            o_ref[:, sl] = (u[:, sl] * sv).astype(BF16)

    vec = pl.BlockSpec((1, h), lambda i: (0, 0))
    return pl.pallas_call(
        body, name=name, out_shape=jax.ShapeDtypeStruct((s, h), BF16), grid=(s // t,),
        in_specs=[pl.BlockSpec((2, t, h), lambda i: (0, i, 0)), vec, vec,
                  pl.BlockSpec((g_n, t, t), lambda i: (0, 0, 0)), pl.BlockSpec((t, 128), lambda i: (0, 0))],
        out_specs=pl.BlockSpec((t, h), lambda i: (i, 0)), compiler_params=_params("parallel"))(uv3, vg, vb, ws_m, sbt)


def _b_mid_bwd(uv3, dgated, vg, vb, ws_m, sbt, name):
    _, s, h = uv3.shape
    g_n, t, _ = ws_m.shape
    gd = h // g_n

    def body(uv_ref, dg_ref, vg_ref, vb_ref, ws_ref, sb_ref, o_ref, dvg_ref, dvb_ref, dws_ref, dsb_ref, dvn_ref):
        i = pl.program_id(0)

        @pl.when(i == 0)
        def _():
            dvg_ref[...] = jnp.zeros_like(dvg_ref)
            dvb_ref[...] = jnp.zeros_like(dvb_ref)
            dws_ref[...] = jnp.zeros_like(dws_ref)
            dsb_ref[...] = jnp.zeros_like(dsb_ref)

        upre = uv_ref[0].astype(F32)
        vpre = uv_ref[1].astype(F32)
        u = _gelu(upre)
        xh, rstd = _ln_stats(_gelu(vpre))
        gv = vg_ref[...]
        vn = (xh * gv + vb_ref[...]).astype(BF16)
        causal = lax.broadcasted_iota(jnp.int32, (t, t), 0) >= lax.broadcasted_iota(jnp.int32, (t, t), 1)
        lane = lax.broadcasted_iota(jnp.int32, (t, 128), 1)
        for g in range(g_n):
            sl = slice(g * gd, (g + 1) * gd)
            wsg = ws_ref[g]
            sv = jnp.dot(wsg, vn[:, sl], preferred_element_type=F32) + sb_ref[:, g:g + 1]
            dg = dg_ref[:, sl].astype(F32)
            o_ref[0, :, sl] = (dg * sv * _gelu_grad(upre[:, sl])).astype(BF16)
            dsv = dg * u[:, sl]
            dsvb = dsv.astype(BF16)
            dsb_ref[...] += jnp.where(lane == g, jnp.sum(dsv, axis=1, keepdims=True), 0.0)
            dws = lax.dot_general(dsvb, vn[:, sl], _DIMS["nt"], preferred_element_type=F32)
            dws_ref[g] += jnp.where(causal, dws, 0.0)
            dvn_ref[:, sl] = lax.dot_general(wsg, dsvb, _DIMS["tn"], preferred_element_type=F32)
        dvn = dvn_ref[...]
        dvg_ref[...] += jnp.sum(dvn * xh, axis=0, keepdims=True)
        dvb_ref[...] += jnp.sum(dvn, axis=0, keepdims=True)
        o_ref[1] = (_ln_bwd(dvn, gv, xh, rstd) * _gelu_grad(vpre)).astype(BF16)

    vec = pl.BlockSpec((1, h), lambda i: (0, 0))
    return pl.pallas_call(
        body, name=name,
        out_shape=[jax.ShapeDtypeStruct((2, s, h), BF16), jax.ShapeDtypeStruct((1, h), F32), jax.ShapeDtypeStruct((1, h), F32),
                   jax.ShapeDtypeStruct((g_n, t, t), F32), jax.ShapeDtypeStruct((t, 128), F32)],
        grid=(s // t,),
        in_specs=[pl.BlockSpec((2, t, h), lambda i: (0, i, 0)), pl.BlockSpec((t, h), lambda i: (i, 0)), vec, vec,
                  pl.BlockSpec((g_n, t, t), lambda i: (0, 0, 0)), pl.BlockSpec((t, 128), lambda i: (0, 0))],
        out_specs=[pl.BlockSpec((2, t, h), lambda i: (0, i, 0)), vec, vec,
                   pl.BlockSpec((g_n, t, t), lambda i: (0, 0, 0)), pl.BlockSpec((t, 128), lambda i: (0, 0))],
        scratch_shapes=[pltpu.VMEM((t, h), F32)],
        compiler_params=_params("arbitrary"))(uv3, dgated, vg, vb, ws_m, sbt)


def _softmax_rows(sc):
    e = jnp.exp(sc - jnp.max(sc, axis=-1, keepdims=True))
    return e / jnp.sum(e, axis=-1, keepdims=True)


def _attn_fwd(q, kv3, name):
    s, d = q.shape
    m = kv3.shape[1]
    dh = d // XA_HEADS
    scale = dh ** -0.5
    r = _row_tile(s)

    def body(q_ref, kv_ref, o_ref):
        for hd in range(XA_HEADS):
            sl = slice(hd * dh, (hd + 1) * dh)
            sc = lax.dot_general(q_ref[:, sl], kv_ref[0, :, sl], _DIMS["nt"], preferred_element_type=F32) * scale
            p = _softmax_rows(sc).astype(BF16)
            o_ref[:, sl] = jnp.dot(p, kv_ref[1, :, sl], preferred_element_type=F32).astype(BF16)

    return pl.pallas_call(
        body, name=name, out_shape=jax.ShapeDtypeStruct((s, d), BF16), grid=(s // r,),
        in_specs=[pl.BlockSpec((r, d), lambda i: (i, 0)), pl.BlockSpec((2, m, d), lambda i: (0, 0, 0))],
        out_specs=pl.BlockSpec((r, d), lambda i: (i, 0)), compiler_params=_params("parallel"))(q, kv3)


def _attn_bwd(q, kv3, do, name):
    s, d = q.shape
    m = kv3.shape[1]
    dh = d // XA_HEADS
    scale = dh ** -0.5
    r = _row_tile(s)

    def body(q_ref, kv_ref, do_ref, dq_ref, dkv_ref):
        i = pl.program_id(0)

        @pl.when(i == 0)
        def _():
            dkv_ref[...] = jnp.zeros_like(dkv_ref)

        for hd in range(XA_HEADS):
            sl = slice(hd * dh, (hd + 1) * dh)
            qh = q_ref[:, sl]
            kh = kv_ref[0, :, sl]
            doh = do_ref[:, sl]
            sc = lax.dot_general(qh, kh, _DIMS["nt"], preferred_element_type=F32) * scale
            p = _softmax_rows(sc)
            pb = p.astype(BF16)
            dkv_ref[1, :, sl] += lax.dot_general(pb, doh, _DIMS["tn"], preferred_element_type=F32)
            dp = lax.dot_general(doh, kv_ref[1, :, sl], _DIMS["nt"], preferred_element_type=F32)
            ds = (p * (dp - jnp.sum(dp * p, axis=-1, keepdims=True)) * scale).astype(BF16)
            dq_ref[:, sl] = jnp.dot(ds, kh, preferred_element_type=F32).astype(BF16)
            dkv_ref[0, :, sl] += lax.dot_general(ds, qh, _DIMS["tn"], preferred_element_type=F32)

    row = pl.BlockSpec((r, d), lambda i: (i, 0))
    kvs = pl.BlockSpec((2, m, d), lambda i: (0, 0, 0))
    return pl.pallas_call(
        body, name=name, out_shape=[jax.ShapeDtypeStruct((s, d), BF16), jax.ShapeDtypeStruct((2, m, d), F32)],
        grid=(s // r,), in_specs=[row, kvs, row], out_specs=[row, kvs], compiler_params=_params("arbitrary"))(q, kv3, do)


FFN_COL_TILE = 512


def _ffn_gu_fwd(h, w_gu, name):
    s, d = h.shape
    f = w_gu.shape[2] // 2
    tm = _div_tile(s, MM_OUT_TILE_CAP)
    tn = _div_tile(f, FFN_COL_TILE)
    nj = f // tn

    def body(a_ref, bg_ref, bu_ref, gu_ref, act_ref):
        a = a_ref[...]
        gate = jnp.dot(a, bg_ref[...], preferred_element_type=F32)
        up = jnp.dot(a, bu_ref[...], preferred_element_type=F32)
        gu_ref[0] = gate.astype(BF16)
        gu_ref[1] = up.astype(BF16)
        act_ref[...] = (gate * jax.nn.sigmoid(gate) * up).astype(BF16)

    return pl.pallas_call(
        body, name=name, out_shape=[jax.ShapeDtypeStruct((2, s, f), BF16), jax.ShapeDtypeStruct((s, f), BF16)],
        grid=(s // tm, nj),
        in_specs=[pl.BlockSpec((tm, d), lambda i, j: (i, 0)), pl.BlockSpec((None, d, tn), lambda i, j: (0, 0, j)),
                  pl.BlockSpec((None, d, tn), lambda i, j: (0, 0, j + nj))],
        out_specs=[pl.BlockSpec((2, tm, tn), lambda i, j: (0, i, j)), pl.BlockSpec((tm, tn), lambda i, j: (i, j))],
        compiler_params=_params("parallel", "parallel"))(h, w_gu, w_gu)


def _ffn_down_bwd(dy, w_down, gu3, name):
    s, d = dy.shape
    f = w_down.shape[1]
    tm = _div_tile(s, MM_OUT_TILE_CAP)
    tn = _div_tile(f, FFN_COL_TILE)

    def body(dy_ref, w_ref, gu_ref, o_ref):
        da = lax.dot_general(dy_ref[...], w_ref[...], _DIMS["nt"], preferred_element_type=F32)
        gate = gu_ref[0].astype(F32)
        up = gu_ref[1].astype(F32)
        sg = jax.nn.sigmoid(gate)
        o_ref[0] = (da * up * (sg + gate * sg * (1.0 - sg))).astype(BF16)
        o_ref[1] = (da * gate * sg).astype(BF16)

    return pl.pallas_call(
        body, name=name, out_shape=jax.ShapeDtypeStruct((2, s, f), BF16), grid=(s // tm, f // tn),
        in_specs=[pl.BlockSpec((tm, d), lambda i, j: (i, 0)), pl.BlockSpec((None, tn, d), lambda i, j: (0, j, 0)),
                  pl.BlockSpec((2, tm, tn), lambda i, j: (0, i, j))],
        out_specs=pl.BlockSpec((2, tm, tn), lambda i, j: (0, i, j)),
        compiler_params=_params("parallel", "parallel"))(dy, w_down, gu3)


def _ids():
    x, y, c = lax.axis_index("x"), lax.axis_index("y"), lax.axis_index("c")
    return x, y, c, 2 * x + y


def _chip_peers(x, y):
    return [(d - 1, 2 * (x ^ (d >> 1)) + (y ^ (d & 1)), x ^ (d >> 1), y ^ (d & 1)) for d in (1, 2, 3)]


def _remote(src, dst, ssem, rsem, dev):
    return pltpu.make_async_remote_copy(src_ref=src, dst_ref=dst, send_sem=ssem, recv_sem=rsem, device_id=dev,
                                        device_id_type=MESH)


def _gview(ref, kind, j, cc):
    _, k, n = ref.shape
    if kind == "row":
        return ref.at[:, pl.ds(j * (k // N_CHIPS) + cc * (k // (2 * N_CHIPS)), k // (2 * N_CHIPS)), :]
    return ref.at[:, pl.ds(cc * (k // 2), k // 2), pl.ds(j * (n // N_CHIPS), n // N_CHIPS)]


def _sview(ref, cc):
    r = ref.shape[1]
    return ref.at[:, pl.ds(cc * (r // 2), r // 2), :]


def _comm_call(body, name, ins, out_shapes, n_sems, aliases=None):
    return pl.pallas_call(
        body, name=name, out_shape=out_shapes, in_specs=[HBM] * len(ins), out_specs=[HBM] * len(out_shapes),
        scratch_shapes=[pltpu.SemaphoreType.DMA((n,)) for n in n_sems], input_output_aliases=aliases or {},
        compiler_params=pltpu.CompilerParams(has_side_effects=True))(*ins)


def _mesh_scalars():
    x, y, c = lax.axis_index("x"), lax.axis_index("y"), lax.axis_index("c")
    return jnp.stack([2 * x + y, c]).astype(jnp.int32)


def _slab_rows(rows, cols, itemsize=4):
    best = None
    for cand in range(16, rows + 1, 16):
        if rows % cand == 0 and cand * cols * itemsize <= (2 << 20):
            best = cand
    return best or rows


def _ag_place(shard, layer, kind, name, dep=None):
    deps = [] if dep is None else [dep]
    _, r, n = shard.shape
    full = (1, r * N_CHIPS, n) if kind == "row" else (1, r, n * N_CHIPS)
    tr = _slab_rows(r, n)
    nt = r // tr
    if kind == "row":
        out_spec = pl.BlockSpec((None, tr, n), lambda t, s: (0, s[0] * nt + t, 0))
    else:
        out_spec = pl.BlockSpec((None, tr, n), lambda t, s: (0, t, s[0]))

    def body(s_ref, i_ref, *rest):
        rest[-1][...] = i_ref[...].astype(BF16)

    return pl.pallas_call(
        body, name=name, out_shape=jax.ShapeDtypeStruct(full, BF16),
        grid_spec=pltpu.PrefetchScalarGridSpec(
            num_scalar_prefetch=1, grid=(nt,),
            in_specs=[pl.BlockSpec((None, tr, n), lambda t, s: (layer, t, 0))] + [pl.BlockSpec(memory_space=pl.ANY)] * len(deps),
            out_specs=out_spec),
        compiler_params=_params("parallel"))(_mesh_scalars(), shard, *deps)


SEM = pl.BlockSpec(memory_space=pltpu.SEMAPHORE)
ANY = pl.BlockSpec(memory_space=pl.ANY)
DATAFLOW = pltpu.SideEffectType.DATAFLOW_SIDE_EFFECTING


def _in_hbm(arrs):
    return [pltpu.with_memory_space_constraint(a, pltpu.HBM) for a in arrs]


def _ag_start(bufs, kinds, name):
    n = len(bufs)

    def body(*refs):
        ssem, rsem, token = refs[n], refs[n + 1], refs[-1]
        x, y, c, me = _ids()
        for t in range(n):
            mine = _gview(refs[t], kinds[t], me, c)
            for d, _, px, py in _chip_peers(x, y):
                _remote(mine, mine, ssem.at[3 * t + d], rsem.at[3 * t + d], (px, py, c)).start()
        token[...] = jnp.zeros_like(token)

    outs = pl.pallas_call(
        body, name=name,
        out_shape=(pltpu.SemaphoreType.DMA((3 * n,)), pltpu.SemaphoreType.DMA((3 * n,)),
                   *[pltpu.HBM(b.shape, b.dtype) for b in bufs], jax.ShapeDtypeStruct((8, 128), F32)),
        in_specs=[HBM] * n, out_specs=(SEM, SEM, *[HBM] * n, pl.BlockSpec(memory_space=pltpu.VMEM)),
        input_output_aliases={t: 2 + t for t in range(n)},
        compiler_params=pltpu.CompilerParams(has_side_effects=DATAFLOW))(*_in_hbm(bufs))
    return outs[0], outs[1], list(outs[2:2 + n]), outs[-1]


def _ag_wait(ssem, rsem, bufs, kinds, after, name):
    n = len(bufs)

    def body(*refs):
        ssem_ref, rsem_ref = refs[n], refs[n + 1]
        x, y, c, me = _ids()
        for t in range(n):
            mine = _gview(refs[t], kinds[t], me, c)
            for d, pj, px, py in _chip_peers(x, y):
                theirs = _gview(refs[t], kinds[t], pj, c)
                _remote(mine, mine, ssem_ref.at[3 * t + d], rsem_ref.at[3 * t + d], (px, py, c)).wait_send()
                _remote(theirs, theirs, ssem_ref.at[3 * t + d], rsem_ref.at[3 * t + d], (px, py, c)).wait_recv()

    outs = pl.pallas_call(
        body, name=name, out_shape=[pltpu.HBM(b.shape, b.dtype) for b in bufs],
        in_specs=[HBM] * n + [SEM, SEM, ANY], out_specs=[HBM] * n, input_output_aliases={t: t for t in range(n)},
        compiler_params=pltpu.CompilerParams(has_side_effects=DATAFLOW))(*bufs, ssem, rsem, after)
    return list(outs)


def _ag_forward(bufs, kinds, name):
    n = len(bufs)

    def body(*refs):
        outs = refs[n:2 * n]
        ssem, rsem = refs[2 * n], refs[2 * n + 1]
        x, y, c, _ = _ids()
        sib = (x, y, 1 - c)
        sends = []
        for t in range(n):
            for d, pj, _, _ in _chip_peers(x, y):
                piece = _gview(outs[t], kinds[t], pj, c)
                sends.append(_remote(piece, piece, ssem.at[3 * t + d], rsem.at[3 * t + d], sib))
        for cp in sends:
            cp.start()
        for t in range(n):
            for d, pj, _, _ in _chip_peers(x, y):
                piece = _gview(outs[t], kinds[t], pj, 1 - c)
                _remote(piece, piece, ssem.at[3 * t + d], rsem.at[3 * t + d], sib).wait_recv()
        for cp in sends:
            cp.wait_send()

    return _comm_call(body, name, bufs, [jax.ShapeDtypeStruct(b.shape, b.dtype) for b in bufs], (3 * n, 3 * n),
                      {t: t for t in range(n)})


def _rs1(g_fulls, kinds, name, dep=None):
    n = len(g_fulls)
    outs = []
    for g, kind in zip(g_fulls, kinds):
        l, k, nn = g.shape
        piece = (l, k // (2 * N_CHIPS), nn) if kind == "row" else (l, k // 2, nn // N_CHIPS)
        outs.append(jax.ShapeDtypeStruct((N_CHIPS,) + piece, g.dtype))

    n_in = n + (dep is not None)

    def body(*refs):
        ssem, rsem = refs[n_in + n], refs[n_in + n + 1]
        x, y, c, _ = _ids()
        sends = [_remote(_gview(refs[t], kinds[t], j, 1 - c), refs[n_in + t].at[j], ssem.at[4 * t + j], rsem.at[4 * t + j],
                         (x, y, 1 - c)) for t in range(n) for j in range(N_CHIPS)]
        for cp in sends:
            cp.start()
        for cp in sends:
            cp.wait()

    return _comm_call(body, name, g_fulls + ([] if dep is None else [dep]), outs, (4 * n, 4 * n))


def _rs_add1(g_full, got, kind, name):
    l, k, n = g_full.shape
    _, _, pr, pc = got.shape
    tr = _slab_rows(pr, pc, 2)
    nt = pr // tr
    if kind == "row":
        g_spec = pl.BlockSpec((None, tr, n), lambda j, li, t, s: (li, (2 * j + s[1]) * nt + t, 0))
    else:
        g_spec = pl.BlockSpec((None, tr, pc), lambda j, li, t, s: (li, s[1] * nt + t, j))
    slot = pl.BlockSpec((None, None, tr, pc), lambda j, li, t, s: (j, li, t, 0))

    def body(s_ref, g_ref, got_ref, o_ref):
        o_ref[...] = g_ref[...] + got_ref[...]

    return pl.pallas_call(
        body, name=name, out_shape=jax.ShapeDtypeStruct(got.shape, BF16),
        grid_spec=pltpu.PrefetchScalarGridSpec(num_scalar_prefetch=1, grid=(N_CHIPS, l, nt), in_specs=[g_spec, slot],
                                               out_specs=slot),
        compiler_params=_params("parallel", "parallel", "parallel"))(_mesh_scalars(), g_full, got)


def _rs2_start(ps, name):
    n = len(ps)
    lands = [lax.empty(p.shape, p.dtype) for p in ps]

    def body(*refs):
        ssem, rsem, token = refs[2 * n], refs[2 * n + 1], refs[-1]
        x, y, c, me = _ids()
        for t in range(n):
            for d, pj, px, py in _chip_peers(x, y):
                _remote(refs[t].at[pj], refs[n + t].at[me], ssem.at[3 * t + d], rsem.at[3 * t + d], (px, py, c)).start()
        token[...] = jnp.zeros_like(token)

    outs = pl.pallas_call(
        body, name=name,
        out_shape=(pltpu.SemaphoreType.DMA((3 * n,)), pltpu.SemaphoreType.DMA((3 * n,)),
                   *[pltpu.HBM(p.shape, p.dtype) for p in ps + lands], jax.ShapeDtypeStruct((8, 128), F32)),
        in_specs=[HBM] * (2 * n), out_specs=(SEM, SEM, *[HBM] * (2 * n), pl.BlockSpec(memory_space=pltpu.VMEM)),
        input_output_aliases={t: 2 + t for t in range(2 * n)},
        compiler_params=pltpu.CompilerParams(has_side_effects=DATAFLOW))(*_in_hbm(ps + lands))
    return outs[0], outs[1], list(outs[2:2 + n]), list(outs[2 + n:2 + 2 * n]), outs[-1]


def _rs2_wait(ssem, rsem, ps, lands, after, name):
    n = len(ps)

    def body(*refs):
        ssem_ref, rsem_ref = refs[2 * n], refs[2 * n + 1]
        x, y, c, me = _ids()
        for t in range(n):
            for d, pj, px, py in _chip_peers(x, y):
                _remote(refs[t].at[pj], refs[n + t].at[me], ssem_ref.at[3 * t + d], rsem_ref.at[3 * t + d], (px, py, c)).wait_send()
                _remote(refs[t].at[pj], refs[n + t].at[pj], ssem_ref.at[3 * t + d], rsem_ref.at[3 * t + d], (px, py, c)).wait_recv()

    outs = pl.pallas_call(
        body, name=name, out_shape=[pltpu.HBM(p.shape, p.dtype) for p in ps + lands],
        in_specs=[HBM] * (2 * n) + [SEM, SEM, ANY], out_specs=[HBM] * (2 * n),
        input_output_aliases={t: t for t in range(2 * n)},
        compiler_params=pltpu.CompilerParams(has_side_effects=DATAFLOW))(*ps, *lands, ssem, rsem, after)
    return list(outs[:n]), list(outs[n:])


def _rs_add2(p, got, into, layer, name):
    _, _, pr, pc = p.shape
    tr = _slab_rows(pr, pc)
    nt = pr // tr

    def slot(d):
        return pl.BlockSpec((None, None, tr, pc), lambda t, s: (s[0] ^ d, 0, t, 0))

    def body(s_ref, p_ref, g1_ref, g2_ref, g3_ref, i_ref, o_ref):
        o_ref[...] = (p_ref[...].astype(F32) + g1_ref[...].astype(F32) + g2_ref[...].astype(F32) + g3_ref[...].astype(F32))

    return pl.pallas_call(
        body, name=name, out_shape=jax.ShapeDtypeStruct(into.shape, F32),
        grid_spec=pltpu.PrefetchScalarGridSpec(
            num_scalar_prefetch=1, grid=(nt,), in_specs=[slot(0), slot(1), slot(2), slot(3), HBM],
            out_specs=pl.BlockSpec((None, tr, pc), lambda t, s: (layer, s[1] * nt + t, 0))),
        input_output_aliases={5: 0},
        compiler_params=_params("parallel"))(_mesh_scalars(), p, got, got, got, into)


def _rs3(shards, layers, name):
    n = len(shards)

    def body(*refs):
        outs = refs[n:2 * n]
        token, ssem, rsem = refs[2 * n], refs[2 * n + 1], refs[2 * n + 2]
        x, y, c, _ = _ids()
        sib = (x, y, 1 - c)
        token[...] = jnp.zeros_like(token)

        def half(t, cc):
            return _sview(outs[t].at[pl.ds(layers[t], 1)], cc)

        sends = [_remote(half(t, c), half(t, c), ssem.at[t], rsem.at[t], sib) for t in range(n)]
        for cp in sends:
            cp.start()
        for t in range(n):
            _remote(half(t, 1 - c), half(t, 1 - c), ssem.at[t], rsem.at[t], sib).wait_recv()
        for cp in sends:
            cp.wait_send()

    outs = pl.pallas_call(
        body, name=name, out_shape=[jax.ShapeDtypeStruct(s.shape, s.dtype) for s in shards] + [jax.ShapeDtypeStruct((8, 128), F32)],
        in_specs=[HBM] * n, out_specs=[HBM] * n + [pl.BlockSpec(memory_space=pltpu.VMEM)],
        scratch_shapes=[pltpu.SemaphoreType.DMA((n,)), pltpu.SemaphoreType.DMA((n,))],
        input_output_aliases={t: t for t in range(n)}, compiler_params=pltpu.CompilerParams(has_side_effects=True))(*shards)
    return list(outs[:n]), outs[n]


def _ag_small(sp, name):
    def body(s_ref, o_ref, ssem, rsem, lsem):
        x, y, c, me = _ids()
        local = pltpu.make_async_copy(s_ref, o_ref.at[me], lsem.at[0])
        local.start()
        sends = [_remote(s_ref, o_ref.at[me], ssem.at[d], rsem.at[d], (px, py, c)) for d, _, px, py in _chip_peers(x, y)]
        for cp in sends:
            cp.start()
        for d, pj, px, py in _chip_peers(x, y):
            _remote(s_ref, o_ref.at[pj], ssem.at[d], rsem.at[d], (px, py, c)).wait_recv()
        for cp in sends:
            cp.wait_send()
        local.wait()

    return _comm_call(body, name, [sp], [jax.ShapeDtypeStruct((N_CHIPS,) + sp.shape, sp.dtype)], (3, 3, 1))[0]


def _gather8(g, name):
    def body(g_ref, o_ref, ssem, rsem, lsem):
        x, y, c, _ = _ids()
        me = 4 * x + 2 * y + c
        local = pltpu.make_async_copy(g_ref, o_ref.at[me], lsem.at[0])
        local.start()
        peers = [(d - 1, x ^ (d >> 2), y ^ ((d >> 1) & 1), c ^ (d & 1)) for d in range(1, 8)]
        sends = [_remote(g_ref, o_ref.at[me], ssem.at[d], rsem.at[d], (px, py, pc)) for d, px, py, pc in peers]
        for cp in sends:
            cp.start()
        for d, px, py, pc in peers:
            _remote(g_ref, o_ref.at[4 * px + 2 * py + pc], ssem.at[d], rsem.at[d], (px, py, pc)).wait_recv()
        for cp in sends:
            cp.wait_send()
        local.wait()

    return _comm_call(body, name, [g], [jax.ShapeDtypeStruct((8,) + g.shape, g.dtype)], (7, 7, 1))[0]


def _sum_slots(a, out_dtype, name):
    n = a.shape[0]
    shape = a.shape[1:]
    cols = shape[-1]
    a3 = a.reshape(n, -1, cols)
    rows = a3.shape[1]
    tr = rows
    for cand in (512, 256, 128, 64, 32, 16):
        if rows % cand == 0 and cand * cols * 4 <= (1 << 20):
            tr = cand
            break

    def body(a_ref, o_ref):
        acc = a_ref[0].astype(F32)
        for j in range(1, n):
            acc = acc + a_ref[j].astype(F32)
        o_ref[...] = acc.astype(o_ref.dtype)

    out = pl.pallas_call(
        body, name=name, out_shape=jax.ShapeDtypeStruct((rows, cols), out_dtype), grid=(rows // tr,),
        in_specs=[pl.BlockSpec((n, tr, cols), lambda i: (0, i, 0))], out_specs=pl.BlockSpec((tr, cols), lambda i: (i, 0)),
        compiler_params=_params("parallel"))(a3)
    return out.reshape(shape)


def kernel(x, mem, mix_norm, xa_norm, xa_wq, xa_wkv, xa_wo, ffn_norm, ffn_w_gu, ffn_w_down, a_w_in, a_conv_w, a_w_out, b_w_in, b_v_g, b_v_b, b_w_s, b_s_bias, b_w_out, c_w_in, c_conv_w, c_conv_b, c_ln_g, c_ln_b, c_w_out, loss_target, m_mix_norm, m_xa_norm, m_xa_wq, m_xa_wkv, m_xa_wo, m_ffn_norm, m_ffn_w_gu, m_ffn_w_down, m_a_w_in, m_a_conv_w, m_a_w_out, m_b_w_in, m_b_v_g, m_b_v_b, m_b_w_s, m_b_s_bias, m_b_w_out, m_c_w_in, m_c_conv_w, m_c_conv_b, m_c_ln_g, m_c_ln_b, m_c_w_out, v_mix_norm, v_xa_norm, v_xa_wq, v_xa_wkv, v_xa_wo, v_ffn_norm, v_ffn_w_gu, v_ffn_w_down, v_a_w_in, v_a_conv_w, v_a_w_out, v_b_w_in, v_b_v_g, v_b_v_b, v_b_w_s, v_b_s_bias, v_b_w_out, v_c_w_in, v_c_conv_w, v_c_conv_b, v_c_ln_g, v_c_ln_b, v_c_w_out):
    given = dict(locals())
    w = {n: given[n] for n in WEIGHTS}
    depth = mix_norm.shape[0]
    s, d = x.shape[1], x.shape[2]
    n_mem = mem.shape[1]
    ds = d // N_CHIPS
    xin = x.reshape(s, d)
    memv = mem.reshape(n_mem, d)
    target = loss_target.reshape(s, d)
    me = 2 * lax.axis_index("x") + lax.axis_index("y")

    wg = {n: [None] * w[n].shape[0] for n in BIG_KINDS}

    def mixer_keys(i):
        return [("abc"[i % 3] + "_w_in", i // 3), ("abc"[i % 3] + "_w_out", i // 3)]

    def rest_keys(i):
        return [("xa_wq", i), ("xa_wkv", i), ("xa_wo", i), ("ffn_w_gu", i), ("ffn_w_down", i)]

    def ag_begin(keys, tag):
        kinds = [BIG_KINDS[n] for n, _ in keys]
        ssem, rsem, bufs, token = _ag_start([placed[k] for k in keys], kinds, "ag_start_" + tag)
        return keys, kinds, ssem, rsem, bufs, token, tag

    def ag_end(state, after):
        keys, kinds, ssem, rsem, bufs, _, tag = state
        bufs = _ag_forward(_ag_wait(ssem, rsem, bufs, kinds, after, "ag_wait_" + tag), kinds, "ag_fwd_" + tag)
        for (n, l), buf in zip(keys, bufs):
            wg[n][l] = buf

    def pad8(t):
        return jnp.pad(t, ((0, (-t.shape[0]) % 8), (0, 0)))

    small_rows = [w[n].reshape(-1, ds) for n in SMALL_SHARDED]
    counts = [t.shape[0] for t in small_rows]
    gathered = _ag_small(jnp.concatenate([pad8(t) for t in small_rows], axis=0), "ag_small")
    placed = {(n, l): _ag_place(w[n], l, kind, f"ag_place_{n}_{l}", dep=gathered)
              for n, kind in BIG_KINDS.items() for l in range(w[n].shape[0])}
    gathered = jnp.transpose(gathered, (1, 0, 2)).reshape(-1, d)
    full, off = {}, 0
    for n, cnt in zip(SMALL_SHARDED, counts):
        full[n] = gathered[off:off + cnt].reshape(w[n].shape[:-1] + (d,))
        off += cnt + (-cnt) % 8
    t_chunk = b_w_s.shape[-1]
    tril = jnp.tril(jnp.ones((t_chunk, t_chunk), dtype=bool))

    def vec(a):
        return a.reshape(1, -1)

    def b_params(slot):
        ws_m = jnp.where(tril[None], b_w_s[slot], 0.0).astype(BF16)
        sbt = jnp.zeros((t_chunk, 128), F32).at[:, :b_s_bias.shape[1]].set(b_s_bias[slot].T)
        return vec(b_v_g[slot]), vec(b_v_b[slot]), ws_m, sbt

    saved = []
    xc = xin
    ag_groups = [(mixer_keys(0), "0m"), (rest_keys(0), "0r")] + [(mixer_keys(j) + rest_keys(j), f"{j}") for j in range(1, depth)]
    ag_state = {}

    def ag_begin_group(k):
        if k >= len(ag_groups):
            return None
        ag_state[k] = ag_begin(*ag_groups[k])
        return ag_state[k][5]

    first_token = ag_begin_group(0)
    ag_end(ag_state[0], first_token)
    for i in range(depth):
        kind, slot = i % 3, i // 3
        t = f"{i}"
        dep = ag_begin_group(1 if i == 0 else i + 2)
        sv = {"x0": xc}
        if i == 0:
            h = _rms_fwd(xc, vec(full["mix_norm"][i, 0]), "rms_mix_" + t)
        sv["h1"] = h
        if kind == 0:
            pre = _mm("nn", h, wg["a_w_in"], BF16, "a_in_" + t, bl=slot, o_parts=3, dep=dep)
            mid = _a_mid_fwd(pre, full["a_conv_w"][slot], "a_mid_" + t)
            y = _mm("nn", mid, wg["a_w_out"], F32, "a_out_" + t, bl=slot)
        elif kind == 1:
            pre = _mm("nn", h, wg["b_w_in"], BF16, "b_in_" + t, bl=slot, o_parts=2, dep=dep)
            mid = _b_mid_fwd(pre, *b_params(slot), "b_mid_" + t)
            y = _mm("nn", mid, wg["b_w_out"], F32, "b_out_" + t, bl=slot)
        else:
            pre = _mm("nn", h, wg["c_w_in"], BF16, "c_in_" + t, bl=slot, o_parts=2, dep=dep)
            y2 = _c_conv_fwd(pre, full["c_conv_w"][slot], vec(full["c_conv_b"][slot]), "c_conv_" + t)
            sv["cy2"] = y2
            mid = _c_ln_fwd(y2, vec(full["c_ln_g"][slot]), vec(full["c_ln_b"][slot]), "c_ln_" + t)
            y = _mm("nn", mid, wg["c_w_out"], F32, "c_out_" + t, bl=slot)
        sv.update(pre=pre, mid=mid, y1=y)
        xc, h = _res_rms_fwd(xc, y, vec(full["mix_norm"][i, 1]), vec(full["xa_norm"][i, 0]), "res_mix_" + t)

        sv["x1"] = xc
        dep = None
        if i == 0:
            ag_end(ag_state[1], xc)
            dep = ag_begin_group(2)
        mem_n = _rms_fwd(memv, vec(full["xa_norm"][i, 2]), "rms_mem_" + t)
        q = _mm("nn", h, wg["xa_wq"], BF16, "xa_q_" + t, bl=i, dep=dep)
        kv3 = _mm("nn", mem_n, wg["xa_wkv"], BF16, "xa_kv_" + t, bl=i, o_parts=2)
        o = _attn_fwd(q, kv3, "attn_" + t)
        y = _mm("nn", o, wg["xa_wo"], F32, "xa_o_" + t, bl=i)
        sv.update(h2=h, mem_n=mem_n, q=q, kv3=kv3, o=o, y2=y)
        xc, h = _res_rms_fwd(xc, y, vec(full["xa_norm"][i, 1]), vec(full["ffn_norm"][i, 0]), "res_xa_" + t)

        sv["x2"] = xc
        gu3, act = _ffn_gu_fwd(h, wg["ffn_w_gu"][i], "ffn_gu_" + t)
        y = _mm("nn", act, wg["ffn_w_down"], F32, "ffn_down_" + t, bl=i)
        sv.update(h3=h, gu3=gu3, act=act, y3=y)
        xc, h = _res_rms_fwd(xc, y, vec(full["ffn_norm"][i, 1]),
                             vec(full["mix_norm"][i + 1, 0]) if i + 1 < depth else None, "res_ffn_" + t)
        saved.append(sv)
        if i + 2 in ag_state:
            ag_end(ag_state[i + 2], xc)

    loss_blk, dx = _loss(xc, target, "loss")
    loss = lax.psum(loss_blk[0, 0], ("x", "y", "c"))

    gbuf = {}
    gfin = {n: lax.empty(w[n].shape, F32) for n in BIG_KINDS}
    gsmall = {n: [None] * full[n].shape[0] for n in ("mix_norm", "xa_norm", "ffn_norm", "a_conv_w", "c_conv_w", "c_conv_b",
                                                      "c_ln_g", "c_ln_b")}
    grepl = {}

    def wgrad(name, l, a, dy, tag, b_parts=1):
        g2 = _mm("tn", a, dy, BF16, "wg_" + tag, b_parts=b_parts)
        gbuf[name, l] = g2.reshape((1,) + g2.shape)

    def rs_begin(keys, tag, dep):
        kinds = [BIG_KINDS[n] for n, _ in keys]
        gots = _rs1([gbuf[k] for k in keys], kinds, "rs1_" + tag, dep=dep)
        ps = [_rs_add1(gbuf[k], got, kind, f"rs_add1_{k[0]}_{k[1]}") for k, got, kind in zip(keys, gots, kinds)]
        ssem, rsem, ps, lands, token = _rs2_start(ps, "rs2_start_" + tag)
        return keys, ssem, rsem, ps, lands, token, tag

    def rs_end(state, after):
        keys, ssem, rsem, ps, lands, _, tag = state
        ps, lands = _rs2_wait(ssem, rsem, ps, lands, after, "rs2_wait_" + tag)
        for (n, l), p, land in zip(keys, ps, lands):
            gfin[n] = _rs_add2(p, land, gfin[n], l, f"rs_add2_{n}_{l}")
        outs, token = _rs3([gfin[n] for n, _ in keys], [l for _, l in keys], "rs3_" + tag)
        for (n, _), o in zip(keys, outs):
            gfin[n] = o
        return token

    rs_state, rs_token = None, None

    for i in reversed(range(depth)):
        kind, slot = i % 3, i // 3
        t = f"{i}"
        sv = saved[i]
        dy, dg_post = _rms_bwd(sv["y3"], vec(full["ffn_norm"][i, 1]), dx, None, BF16, "rmsb_ffn_post_" + t,
                               dep=None if rs_state is None else rs_state[5])
        wgrad("ffn_w_down", i, sv["act"], dy, "ffn_down_" + t)
        dgu3 = _ffn_down_bwd(dy, wg["ffn_w_down"][i], sv["gu3"], "dg_ffn_down_" + t)
        wgrad("ffn_w_gu", i, sv["h3"], dgu3, "ffn_gu_" + t, b_parts=2)
        dh = _mm("nt", dgu3, wg["ffn_w_gu"], F32, "dg_ffn_gu_" + t, bl=i, a_parts=2)
        dx, dg_pre = _rms_bwd(sv["x2"], vec(full["ffn_norm"][i, 0]), dh, dx, F32, "rmsb_ffn_pre_" + t)
        gsmall["ffn_norm"][i] = jnp.concatenate([dg_pre, dg_post], axis=0)
        if i == 0:
            rs_state_f = rs_begin(rest_keys(0)[3:], "0f", None)
        dy, dg_post = _rms_bwd(sv["y2"], vec(full["xa_norm"][i, 1]), dx, None, BF16, "rmsb_xa_post_" + t,
                               dep=rs_state_f[5] if i == 0 else None)
        wgrad("xa_wo", i, sv["o"], dy, "xa_o_" + t)
        do = _mm("nt", dy, wg["xa_wo"], BF16, "dg_xa_o_" + t, bl=i)
        dq, dkv3 = _attn_bwd(sv["q"], sv["kv3"], do, "attn_b_" + t)
        wgrad("xa_wq", i, sv["h2"], dq, "xa_q_" + t)
        dh = _mm("nt", dq, wg["xa_wq"], F32, "dg_xa_q_" + t, bl=i)
        dkv3 = dkv3.astype(BF16)
        wgrad("xa_wkv", i, sv["mem_n"], dkv3, "xa_kv_" + t, b_parts=2)
        dmem_n = _mm("nt", dkv3, wg["xa_wkv"], F32, "dg_xa_kv_" + t, bl=i, a_parts=2)
        _, dg_mem = _rms_bwd(memv, vec(full["xa_norm"][i, 2]), dmem_n, None, F32, "rmsb_mem_" + t)
        dx, dg_pre = _rms_bwd(sv["x1"], vec(full["xa_norm"][i, 0]), dh, dx, F32, "rmsb_xa_pre_" + t)
        gsmall["xa_norm"][i] = jnp.concatenate([dg_pre, dg_post, dg_mem], axis=0)
        if i == 0:
            rs_token = rs_end(rs_state, dx)
            rs_state = rs_begin(rest_keys(0)[:3], "0x", rs_token)
        dy, dg_post = _rms_bwd(sv["y1"], vec(full["mix_norm"][i, 1]), dx, None, BF16, "rmsb_mix_post_" + t,
                               dep=rs_state[5] if i == 0 else None)
        if kind == 0:
            wgrad("a_w_out", slot, sv["mid"], dy, "a_out_" + t)
            dmid = _mm("nt", dy, wg["a_w_out"], F32, "dg_a_out_" + t, bl=slot)
            dpre, dcw = _a_mid_bwd(sv["pre"], dmid, full["a_conv_w"][slot], "a_mid_b_" + t)
            gsmall["a_conv_w"][slot] = dcw
            wgrad("a_w_in", slot, sv["h1"], dpre, "a_in_" + t, b_parts=3)
            dh = _mm("nt", dpre, wg["a_w_in"], F32, "dg_a_in_" + t, bl=slot, a_parts=3)
        elif kind == 1:
            wgrad("b_w_out", slot, sv["mid"], dy, "b_out_" + t)
            dmid = _mm("nt", dy, wg["b_w_out"], F32, "dg_b_out_" + t, bl=slot)
            dpre, dvg, dvb, dws, dsbt = _b_mid_bwd(sv["pre"], dmid, *b_params(slot), "b_mid_b_" + t)
            grepl[slot] = (dvg, dvb, dws, dsbt[:, :b_s_bias.shape[1]].T)
            wgrad("b_w_in", slot, sv["h1"], dpre, "b_in_" + t, b_parts=2)
            dh = _mm("nt", dpre, wg["b_w_in"], F32, "dg_b_in_" + t, bl=slot, a_parts=2)
        else:
            wgrad("c_w_out", slot, sv["mid"], dy, "c_out_" + t)
            dmid = _mm("nt", dy, wg["c_w_out"], F32, "dg_c_out_" + t, bl=slot)
            dy2, dlg, dlb = _c_ln_bwd(sv["cy2"], dmid, vec(full["c_ln_g"][slot]), vec(full["c_ln_b"][slot]), "c_ln_b_" + t)
            dpre, dcw, dcb = _c_conv_bwd(sv["pre"], dy2, full["c_conv_w"][slot], "c_conv_b_" + t)
            gsmall["c_conv_w"][slot], gsmall["c_conv_b"][slot] = dcw, dcb
            gsmall["c_ln_g"][slot], gsmall["c_ln_b"][slot] = dlg, dlb
            wgrad("c_w_in", slot, sv["h1"], dpre, "c_in_" + t, b_parts=2)
            dh = _mm("nt", dpre, wg["c_w_in"], F32, "dg_c_in_" + t, bl=slot, a_parts=2)
        dx, dg_pre = _rms_bwd(sv["x0"], vec(full["mix_norm"][i, 0]), dh, dx, F32, "rmsb_mix_pre_" + t)
        gsmall["mix_norm"][i] = jnp.concatenate([dg_pre, dg_post], axis=0)
        if i == 0:
            rs_end(rs_state_f, dx)
        if rs_state is not None:
            rs_token = rs_end(rs_state, dx)
        rs_state = rs_begin(mixer_keys(i) + (rest_keys(i) if i > 0 else []), f"{i}" if i > 0 else "0m", rs_token)
    grad_x = dx.reshape(x.shape)

    delta, new_m, new_v, grads = {}, {}, {}, {}
    last_keys = {n for n, _ in mixer_keys(0)}
    dep = rs_state[5]
    for n in BIG_KINDS:
        if n not in last_keys:
            delta[n], new_m[n], new_v[n], grads[n] = _adamw(w[n], gfin[n], given["m_" + n], given["v_" + n], "adamw_" + n,
                                                            dep=dep, with_grad=True)
            dep = delta[n]
    rs_end(rs_state, dep)
    for n in sorted(last_keys):
        delta[n], new_m[n], new_v[n], grads[n] = _adamw(w[n], gfin[n], given["m_" + n], given["v_" + n], "adamw_" + n,
                                                        with_grad=True)

    small_g = [jnp.concatenate(gsmall[n], axis=0).reshape(-1, d) for n in SMALL_SHARDED]
    n_b = b_v_g.shape[0]
    repl_g = [jnp.concatenate([grepl[sl][k] for sl in range(n_b)], axis=0) for k in range(4)]
    repl_rows = []
    for g_arr in repl_g:
        flat = g_arr.reshape(-1)
        flat = jnp.concatenate([flat, jnp.zeros(((-flat.shape[0]) % d,), F32)])
        repl_rows.append(flat.reshape(-1, d))
    rows_all = [pad8(t) for t in small_g + repl_rows]
    total = _sum_slots(_gather8(jnp.concatenate(rows_all, axis=0), "gather_small_grads"), F32, "sum_small_grads")
    off = 0
    for n, cnt in zip(SMALL_SHARDED, counts):
        blk = lax.dynamic_slice_in_dim(total[off:off + cnt], me * ds, ds, axis=1)
        grads[n] = blk.reshape(w[n].shape)
        off += cnt + (-cnt) % 8
    for n, g_arr in zip(SMALL_REPL, repl_g):
        cnt = -(-g_arr.size // d)
        grads[n] = total[off:off + cnt].reshape(-1)[:g_arr.size].reshape(w[n].shape)
        off += cnt + (-cnt) % 8

    for n in WEIGHTS:
        if n not in delta:
            delta[n], new_m[n], new_v[n] = _adamw(w[n], grads[n], given["m_" + n], given["v_" + n], "adamw_" + n)
    return (loss, grad_x, *[grads[n] for n in WEIGHTS], *[delta[n] for n in WEIGHTS], *[new_m[n] for n in WEIGHTS],
            *[new_v[n] for n in WEIGHTS])
```

```python
import functools

import jax
import jax.numpy as jnp
from jax import lax
from jax.experimental import pallas as pl
from jax.experimental.pallas import tpu as pltpu

F32 = jnp.float32
BF16 = jnp.bfloat16
EPS = 1e-6
XA_HEADS = 4
CHUNK = 128
GMLP_GROUPS = 8
ADAM_LR, ADAM_B1, ADAM_B2, ADAM_EPS, ADAM_WD, ADAM_STEP = 0.001, 0.9, 0.999, 1e-08, 0.01, 10
VMEM_LIMIT_V7X = 48 * 1024 * 1024
HBM = pl.BlockSpec(memory_space=pltpu.HBM)
MESH = pl.DeviceIdType.MESH
N_CHIPS = 4
BIG_KINDS = {"xa_wq": "row", "xa_wkv": "col", "xa_wo": "row", "ffn_w_gu": "col", "ffn_w_down": "row",
             "a_w_in": "col", "a_w_out": "row", "b_w_in": "col", "b_w_out": "row", "c_w_in": "col", "c_w_out": "row"}
SMALL_SHARDED = ["mix_norm", "xa_norm", "ffn_norm", "a_conv_w", "c_conv_w", "c_conv_b", "c_ln_g", "c_ln_b"]
SMALL_REPL = ["b_v_g", "b_v_b", "b_w_s", "b_s_bias"]
WEIGHTS = ["mix_norm", "xa_norm", "xa_wq", "xa_wkv", "xa_wo", "ffn_norm", "ffn_w_gu", "ffn_w_down", "a_w_in", "a_conv_w",
           "a_w_out", "b_w_in", "b_v_g", "b_v_b", "b_w_s", "b_s_bias", "b_w_out", "c_w_in", "c_conv_w", "c_conv_b",
           "c_ln_g", "c_ln_b", "c_w_out"]


def _params(*sem):
    return pltpu.CompilerParams(dimension_semantics=sem, vmem_limit_bytes=VMEM_LIMIT_V7X)


def _tile(n, cands=(1024, 512, 256, 128)):
    for c in cands:
        if n % c == 0:
            return c
    return n


def _div_tile(n, cap):
    best = None
    for t in range(128, min(n, cap) + 1, 128):
        if n % t == 0:
            best = t
    return best or n


MM_OUT_TILE_CAP = 1408
MM_K_TILE_CAP = 2816

_DIMS = {"nn": (((1,), (0,)), ((), ())), "nt": (((1,), (1,)), ((), ())), "tn": (((0,), (0,)), ((), ()))}


def _mm(mode, a, b, out_dtype, name, *, bl=None, a_parts=1, b_parts=1, o_parts=1, dep=None):
    if isinstance(b, list):
        b, bl = b[bl], 0
    bshape = b.shape[1:] if bl is not None else b.shape
    if mode == "nn":
        mo, c = a.shape
        no = bshape[1]
    elif mode == "nt":
        mo, c = (a.shape[1], a.shape[0] * a.shape[2]) if a_parts > 1 else a.shape
        no = bshape[0]
    else:
        c, mo = a.shape
        no = b.shape[0] * b.shape[2] if b_parts > 1 else bshape[1]
    tmo = _div_tile(mo, MM_OUT_TILE_CAP)
    tno = _div_tile(no // max(o_parts, b_parts), MM_OUT_TILE_CAP)
    tc = _div_tile(c // a_parts, MM_K_TILE_CAP)
    nk = c // tc
    nkp = nk // a_parts
    njp = (no // tno) // max(o_parts, b_parts)
    lead = (None,) if bl is not None else ()
    lidx = (bl,) if bl is not None else ()

    if mode == "nn":
        a_spec = pl.BlockSpec((tmo, tc), lambda i, j, k: (i, k))
        b_spec = pl.BlockSpec(lead + (tc, tno), lambda i, j, k: lidx + (k, j))
    elif mode == "nt":
        if a_parts > 1:
            a_spec = pl.BlockSpec((None, tmo, tc), lambda i, j, k: (k // nkp, i, k % nkp))
        else:
            a_spec = pl.BlockSpec((tmo, tc), lambda i, j, k: (i, k))
        b_spec = pl.BlockSpec(lead + (tno, tc), lambda i, j, k: lidx + (j, k))
    else:
        a_spec = pl.BlockSpec((tc, tmo), lambda i, j, k: (k, i))
        if b_parts > 1:
            b_spec = pl.BlockSpec((None, tc, tno), lambda i, j, k: (j // njp, k, j % njp))
        else:
            b_spec = pl.BlockSpec((tc, tno), lambda i, j, k: (k, j))

    in_specs = [a_spec, b_spec]
    args = [a, b]
    if dep is not None:
        in_specs.append(pl.BlockSpec(memory_space=pl.ANY))
        args.append(dep)
    if o_parts > 1:
        out_shape = jax.ShapeDtypeStruct((o_parts, mo, no // o_parts), out_dtype)
        out_spec = pl.BlockSpec((None, tmo, tno), lambda i, j, k: (j // njp, i, j % njp))
    else:
        out_shape = jax.ShapeDtypeStruct((mo, no), out_dtype)
        out_spec = pl.BlockSpec((tmo, tno), lambda i, j, k: (i, j))
    dims = _DIMS[mode]

    def body(a_ref, b_ref, *rest):
        if nk == 1:
            o_ref = rest[-1]
            o_ref[...] = lax.dot_general(a_ref[...], b_ref[...], dims, preferred_element_type=F32).astype(o_ref.dtype)
            return
        o_ref, acc = rest[-2], rest[-1]
        k = pl.program_id(2)
        part = lax.dot_general(a_ref[...], b_ref[...], dims, preferred_element_type=F32)

        @pl.when(k == 0)
        def _():
            acc[...] = part

        @pl.when(jnp.logical_and(k > 0, k < nk - 1))
        def _():
            acc[...] += part

        @pl.when(k == nk - 1)
        def _():
            o_ref[...] = (acc[...] + part).astype(o_ref.dtype)

    return pl.pallas_call(
        body, name=name, out_shape=out_shape, grid=(mo // tmo, no // tno, nk), in_specs=in_specs, out_specs=out_spec,
        scratch_shapes=[pltpu.VMEM((tmo, tno), F32)] if nk > 1 else [],
        compiler_params=_params("parallel", "parallel", "arbitrary"))(*args)


def _ew(fn, ins, out_dtypes, name, dep=None):
    rows, cols = ins[0].shape
    deps = [] if dep is None else [dep]
    tr = rows
    for cand in (512, 256, 128, 64, 32, 16):
        if rows % cand == 0 and cand * cols * 4 <= (1 << 20):
            tr = cand
            break
    spec = pl.BlockSpec((tr, cols), lambda i: (i, 0))
    n_in = len(ins)

    def body(*refs):
        outs = fn(*[r[...] for r in refs[:n_in]])
        for o_ref, o in zip(refs[n_in + len(deps):], outs):
            o_ref[...] = o.astype(o_ref.dtype)

    return pl.pallas_call(
        body, name=name, out_shape=[jax.ShapeDtypeStruct((rows, cols), d) for d in out_dtypes], grid=(rows // tr,),
        in_specs=[spec] * n_in + [pl.BlockSpec(memory_space=pl.ANY)] * len(deps), out_specs=[spec] * len(out_dtypes),
        compiler_params=_params("parallel"))(*ins, *deps)


def _adamw_fn(w, g, m, v):
    m = ADAM_B1 * m + (1.0 - ADAM_B1) * g
    v = ADAM_B2 * v + (1.0 - ADAM_B2) * (g * g)
    m_hat = m / (1.0 - ADAM_B1 ** ADAM_STEP)
    v_hat = v / (1.0 - ADAM_B2 ** ADAM_STEP)
    delta = -ADAM_LR * (m_hat / (jnp.sqrt(v_hat) + ADAM_EPS) + ADAM_WD * w)
    return delta, m, v


def _adamw(w, g, m, v, name, dep=None, with_grad=False):
    shape = w.shape
    cols = shape[-1]
    flat = [t.reshape(-1, cols) for t in (w, g, m, v)]
    fn = (lambda wv, gv, mv, vv: _adamw_fn(wv, gv, mv, vv) + (gv,)) if with_grad else _adamw_fn
    outs = _ew(fn, flat, [F32] * (4 if with_grad else 3), name, dep=dep)
    return [o.reshape(shape) for o in outs]


def _row_tile(s):
    return _tile(s, (256, 128, 64, 32, 16, 8))


def _rms_fwd(x, g, name, dep=None):
    s, d = x.shape
    r = _row_tile(s)
    deps = [] if dep is None else [dep]

    def body(x_ref, g_ref, *rest):
        o_ref = rest[-1]
        xv = x_ref[...]
        o_ref[...] = (xv * lax.rsqrt(jnp.mean(xv * xv, axis=-1, keepdims=True) + EPS) * g_ref[...]).astype(BF16)

    return pl.pallas_call(
        body, name=name, out_shape=jax.ShapeDtypeStruct((s, d), BF16), grid=(s // r,),
        in_specs=[pl.BlockSpec((r, d), lambda i: (i, 0)), pl.BlockSpec((1, d), lambda i: (0, 0))] + [ANY] * len(deps),
        out_specs=pl.BlockSpec((r, d), lambda i: (i, 0)), compiler_params=_params("parallel"))(x, g, *deps)


def _res_rms_fwd(x, y, g, g_next, name):
    s, d = x.shape
    r = _row_tile(s)
    has_next = g_next is not None

    def body(x_ref, y_ref, g_ref, *rest):
        yv = y_ref[...]
        xn = x_ref[...] + yv * lax.rsqrt(jnp.mean(yv * yv, axis=-1, keepdims=True) + EPS) * g_ref[...]
        rest[-2 if has_next else -1][...] = xn
        if has_next:
            rest[-1][...] = (xn * lax.rsqrt(jnp.mean(xn * xn, axis=-1, keepdims=True) + EPS) * rest[0][...]).astype(BF16)

    row = pl.BlockSpec((r, d), lambda i: (i, 0))
    vec = pl.BlockSpec((1, d), lambda i: (0, 0))
    outs = pl.pallas_call(
        body, name=name,
        out_shape=[jax.ShapeDtypeStruct((s, d), F32)] + ([jax.ShapeDtypeStruct((s, d), BF16)] if has_next else []),
        grid=(s // r,), in_specs=[row, row, vec] + ([vec] if has_next else []), out_specs=[row] * (2 if has_next else 1),
        compiler_params=_params("parallel"))(*([x, y, g] + ([g_next] if has_next else [])))
    return outs[0], (outs[1] if has_next else None)


def _rms_bwd(x, g, dy, resid, out_dtype, name, dep=None):
    s, d = x.shape
    r = _row_tile(s)
    has_res = resid is not None
    deps = [] if dep is None else [dep]

    def body(*refs):
        x_ref, g_ref, dy_ref = refs[:3]
        dx_ref, dg_ref = refs[-2:]
        i = pl.program_id(0)
        xv = x_ref[...]
        dyv = dy_ref[...].astype(F32)
        rstd = lax.rsqrt(jnp.mean(xv * xv, axis=-1, keepdims=True) + EPS)
        n = xv * rstd
        dn = dyv * g_ref[...]
        dx = rstd * (dn - n * jnp.mean(dn * n, axis=-1, keepdims=True))
        if has_res:
            dx = dx + refs[3][...]
        dx_ref[...] = dx.astype(dx_ref.dtype)
        part = jnp.sum(dyv * n, axis=0, keepdims=True)

        @pl.when(i == 0)
        def _():
            dg_ref[...] = part

        @pl.when(i > 0)
        def _():
            dg_ref[...] += part

    row = pl.BlockSpec((r, d), lambda i: (i, 0))
    vec = pl.BlockSpec((1, d), lambda i: (0, 0))
    ins = [x, g, dy] + ([resid] if has_res else []) + deps
    return pl.pallas_call(
        body, name=name, out_shape=[jax.ShapeDtypeStruct((s, d), out_dtype), jax.ShapeDtypeStruct((1, d), F32)],
        grid=(s // r,), in_specs=[row, vec, row] + ([row] if has_res else []) + [ANY] * len(deps), out_specs=[row, vec],
        compiler_params=_params("arbitrary"))(*ins)


def _loss(y, t, name):
    s, d = y.shape
    r = _row_tile(s)

    def body(y_ref, t_ref, l_ref, dy_ref):
        i = pl.program_id(0)
        e = y_ref[...] - t_ref[...]
        dy_ref[...] = e * (1.0 / d)
        part = jnp.full((8, 128), 0.5 * jnp.sum(jnp.mean(e * e, axis=-1, keepdims=True)), F32)

        @pl.when(i == 0)
        def _():
            l_ref[...] = part

        @pl.when(i > 0)
        def _():
            l_ref[...] += part

    row = pl.BlockSpec((r, d), lambda i: (i, 0))
    return pl.pallas_call(
        body, name=name, out_shape=[jax.ShapeDtypeStruct((8, 128), F32), jax.ShapeDtypeStruct((s, d), F32)],
        grid=(s // r,), in_specs=[row, row], out_specs=[pl.BlockSpec((8, 128), lambda i: (0, 0)), row],
        compiler_params=_params("arbitrary"))(y, t)


def _rows(xv, a, m, cache):
    r = a % 8
    q = a - r
    if r == 0:
        return xv[q:q + m]
    if r not in cache:
        cache[r] = pltpu.roll(xv, xv.shape[0] - r, 0)
    return cache[r][q:q + m]


def _conv_taps(xv, w, k_w, halo, m, flip):
    cache = {}
    acc = None
    for k in range(k_w):
        a = (k_w - 1 - k) if flip else (halo + k - (k_w - 1))
        term = w[k:k + 1, :] * _rows(xv, a, m, cache)
        acc = term if acc is None else acc + term
    return acc


def _conv_wgrad(dw_ref, dyv, xv, k_w, halo, m):
    cache = {}
    for k in range(k_w):
        xs = _rows(xv, halo + k - (k_w - 1), m, cache)
        dw_ref[pl.ds(k, 1), :] += jnp.sum(dyv * xs, axis=0, keepdims=True)


def _conv_tiles(s, dp, halo):
    r = _tile(s, (256, 128))
    cw = _tile(dp, (256, 128))
    return r, cw, r // halo


A_HALO = 8


def _a_mid_fwd(bcz3, w, name):
    _, s, d = bcz3.shape
    r, cw, rh = _conv_tiles(s, d, A_HALO)
    k_w = w.shape[0]

    def body(m_ref, h_ref, w_ref, o_ref):
        i = pl.program_id(0)
        cz = m_ref[1].astype(F32) * m_ref[2].astype(F32)
        hcz = h_ref[1].astype(F32) * h_ref[2].astype(F32)
        hcz = jnp.where(i == 0, 0.0, hcz)
        xv = jnp.concatenate([hcz, cz], axis=0)
        y = _conv_taps(xv, w_ref[...], k_w, A_HALO, r, False)
        o_ref[...] = (m_ref[0].astype(F32) * y).astype(BF16)

    return pl.pallas_call(
        body, name=name, out_shape=jax.ShapeDtypeStruct((s, d), BF16), grid=(s // r, d // cw),
        in_specs=[pl.BlockSpec((3, r, cw), lambda i, j: (0, i, j)),
                  pl.BlockSpec((3, A_HALO, cw), lambda i, j: (0, jnp.maximum(i * rh - 1, 0), j)),
                  pl.BlockSpec((k_w, cw), lambda i, j: (0, j))],
        out_specs=pl.BlockSpec((r, cw), lambda i, j: (i, j)), compiler_params=_params("parallel", "parallel"))(bcz3, bcz3, w)


def _a_mid_bwd(bcz3, dgated, w, name):
    _, s, d = bcz3.shape
    r, cw, rh = _conv_tiles(s, d, A_HALO)
    k_w = w.shape[0]
    ni = s // r
    last_h = s // A_HALO - 1

    def body(m_ref, hp_ref, hn_ref, dg_ref, dgn_ref, w_ref, o_ref, dw_ref):
        i = pl.program_id(1)
        wv = w_ref[...]
        b = m_ref[0].astype(F32)
        c = m_ref[1].astype(F32)
        z = m_ref[2].astype(F32)
        hcz = jnp.where(i == 0, 0.0, hp_ref[1].astype(F32) * hp_ref[2].astype(F32))
        xv = jnp.concatenate([hcz, c * z], axis=0)
        y = _conv_taps(xv, wv, k_w, A_HALO, r, False)
        dg = dg_ref[...].astype(F32)
        dy = dg * b
        dyn = jnp.where(i == ni - 1, 0.0, dgn_ref[...].astype(F32) * hn_ref[0].astype(F32))
        dcz = _conv_taps(jnp.concatenate([dy, dyn], axis=0), wv, k_w, A_HALO, r, True)
        o_ref[0] = (dg * y).astype(BF16)
        o_ref[1] = (dcz * z).astype(BF16)
        o_ref[2] = (dcz * c).astype(BF16)

        @pl.when(i == 0)
        def _():
            dw_ref[...] = jnp.zeros_like(dw_ref)

        _conv_wgrad(dw_ref, dy, xv, k_w, A_HALO, r)

    return pl.pallas_call(
        body, name=name, out_shape=[jax.ShapeDtypeStruct((3, s, d), BF16), jax.ShapeDtypeStruct((k_w, d), F32)],
        grid=(d // cw, ni),
        in_specs=[pl.BlockSpec((3, r, cw), lambda j, i: (0, i, j)),
                  pl.BlockSpec((3, A_HALO, cw), lambda j, i: (0, jnp.maximum(i * rh - 1, 0), j)),
                  pl.BlockSpec((3, A_HALO, cw), lambda j, i: (0, jnp.minimum((i + 1) * rh, last_h), j)),
                  pl.BlockSpec((r, cw), lambda j, i: (i, j)),
                  pl.BlockSpec((A_HALO, cw), lambda j, i: (jnp.minimum((i + 1) * rh, last_h), j)),
                  pl.BlockSpec((k_w, cw), lambda j, i: (0, j))],
        out_specs=[pl.BlockSpec((3, r, cw), lambda j, i: (0, i, j)), pl.BlockSpec((k_w, cw), lambda j, i: (0, j))],
        compiler_params=_params("parallel", "arbitrary"))(bcz3, bcz3, bcz3, dgated, dgated, w)


C_HALO = 32


def _c_conv_fwd(ag3, w, bias, name):
    _, s, d = ag3.shape
    r, cw, rh = _conv_tiles(s, d, C_HALO)
    k_w = w.shape[0]

    def body(m_ref, h_ref, w_ref, b_ref, o_ref):
        i = pl.program_id(0)
        y1 = m_ref[0].astype(F32) * jax.nn.sigmoid(m_ref[1].astype(F32))
        h1 = jnp.where(i == 0, 0.0, h_ref[0].astype(F32) * jax.nn.sigmoid(h_ref[1].astype(F32)))
        xv = jnp.concatenate([h1, y1], axis=0)
        o_ref[...] = _conv_taps(xv, w_ref[...], k_w, C_HALO, r, False) + b_ref[...]

    return pl.pallas_call(
        body, name=name, out_shape=jax.ShapeDtypeStruct((s, d), F32), grid=(s // r, d // cw),
        in_specs=[pl.BlockSpec((2, r, cw), lambda i, j: (0, i, j)),
                  pl.BlockSpec((2, C_HALO, cw), lambda i, j: (0, jnp.maximum(i * rh - 1, 0), j)),
                  pl.BlockSpec((k_w, cw), lambda i, j: (0, j)), pl.BlockSpec((1, cw), lambda i, j: (0, j))],
        out_specs=pl.BlockSpec((r, cw), lambda i, j: (i, j)),
        compiler_params=_params("parallel", "parallel"))(ag3, ag3, w, bias)


def _c_conv_bwd(ag3, dy2, w, name):
    _, s, d = ag3.shape
    r, cw, rh = _conv_tiles(s, d, C_HALO)
    k_w = w.shape[0]
    ni = s // r
    last_h = s // C_HALO - 1

    def body(m_ref, hp_ref, dy_ref, dyn_ref, w_ref, o_ref, dw_ref, db_ref):
        i = pl.program_id(1)
        wv = w_ref[...]
        a = m_ref[0].astype(F32)
        sg = jax.nn.sigmoid(m_ref[1].astype(F32))
        h1 = jnp.where(i == 0, 0.0, hp_ref[0].astype(F32) * jax.nn.sigmoid(hp_ref[1].astype(F32)))
        xv = jnp.concatenate([h1, a * sg], axis=0)
        dy = dy_ref[...]
        dyn = jnp.where(i == ni - 1, 0.0, dyn_ref[...])
        dy1 = _conv_taps(jnp.concatenate([dy, dyn], axis=0), wv, k_w, C_HALO, r, True)
        o_ref[0] = (dy1 * sg).astype(BF16)
        o_ref[1] = (dy1 * a * sg * (1.0 - sg)).astype(BF16)

        @pl.when(i == 0)
        def _():
            dw_ref[...] = jnp.zeros_like(dw_ref)
            db_ref[...] = jnp.zeros_like(db_ref)

        db_ref[...] += jnp.sum(dy, axis=0, keepdims=True)
        _conv_wgrad(dw_ref, dy, xv, k_w, C_HALO, r)

    return pl.pallas_call(
        body, name=name,
        out_shape=[jax.ShapeDtypeStruct((2, s, d), BF16), jax.ShapeDtypeStruct((k_w, d), F32),
                   jax.ShapeDtypeStruct((1, d), F32)],
        grid=(d // cw, ni),
        in_specs=[pl.BlockSpec((2, r, cw), lambda j, i: (0, i, j)),
                  pl.BlockSpec((2, C_HALO, cw), lambda j, i: (0, jnp.maximum(i * rh - 1, 0), j)),
                  pl.BlockSpec((r, cw), lambda j, i: (i, j)),
                  pl.BlockSpec((C_HALO, cw), lambda j, i: (jnp.minimum((i + 1) * rh, last_h), j)),
                  pl.BlockSpec((k_w, cw), lambda j, i: (0, j))],
        out_specs=[pl.BlockSpec((2, r, cw), lambda j, i: (0, i, j)), pl.BlockSpec((k_w, cw), lambda j, i: (0, j)),
                   pl.BlockSpec((1, cw), lambda j, i: (0, j))],
        compiler_params=_params("parallel", "arbitrary"))(ag3, ag3, dy2, dy2, w)


def _ln_stats(v):
    mu = jnp.mean(v, axis=-1, keepdims=True)
    vc = v - mu
    rstd = lax.rsqrt(jnp.mean(vc * vc, axis=-1, keepdims=True) + EPS)
    return vc * rstd, rstd


def _ln_bwd(dn, g, xh, rstd):
    dxh = dn * g
    return rstd * (dxh - jnp.mean(dxh, axis=-1, keepdims=True) - xh * jnp.mean(dxh * xh, axis=-1, keepdims=True))


def _c_ln_fwd(y2, g, b, name):
    s, d = y2.shape
    r = _row_tile(s)

    def body(y_ref, g_ref, b_ref, o_ref):
        xh, _ = _ln_stats(y_ref[...])
        y3 = xh * g_ref[...] + b_ref[...]
        o_ref[...] = (y3 * jax.nn.sigmoid(y3)).astype(BF16)

    row = pl.BlockSpec((r, d), lambda i: (i, 0))
    vec = pl.BlockSpec((1, d), lambda i: (0, 0))
    return pl.pallas_call(
        body, name=name, out_shape=jax.ShapeDtypeStruct((s, d), BF16), grid=(s // r,), in_specs=[row, vec, vec],
        out_specs=row, compiler_params=_params("parallel"))(y2, g, b)


def _c_ln_bwd(y2, dout, g, b, name):
    s, d = y2.shape
    r = _row_tile(s)

    def body(y_ref, do_ref, g_ref, b_ref, dy_ref, dg_ref, db_ref):
        i = pl.program_id(0)
        xh, rstd = _ln_stats(y_ref[...])
        gv = g_ref[...]
        y3 = xh * gv + b_ref[...]
        sg = jax.nn.sigmoid(y3)
        dy3 = do_ref[...].astype(F32) * (sg + y3 * sg * (1.0 - sg))
        dy_ref[...] = _ln_bwd(dy3, gv, xh, rstd)

        @pl.when(i == 0)
        def _():
            dg_ref[...] = jnp.zeros_like(dg_ref)
            db_ref[...] = jnp.zeros_like(db_ref)

        dg_ref[...] += jnp.sum(dy3 * xh, axis=0, keepdims=True)
        db_ref[...] += jnp.sum(dy3, axis=0, keepdims=True)

    row = pl.BlockSpec((r, d), lambda i: (i, 0))
    vec = pl.BlockSpec((1, d), lambda i: (0, 0))
    return pl.pallas_call(
        body, name=name,
        out_shape=[jax.ShapeDtypeStruct((s, d), F32), jax.ShapeDtypeStruct((1, d), F32), jax.ShapeDtypeStruct((1, d), F32)],
        grid=(s // r,), in_specs=[row, row, vec, vec], out_specs=[row, vec, vec],
        compiler_params=_params("arbitrary"))(y2, dout, g, b)


_GELU_C = 0.7978845608028654
_GELU_A = 0.044715


def _gelu(x):
    return 0.5 * x * (1.0 + jnp.tanh(_GELU_C * (x + _GELU_A * x * x * x)))


def _gelu_grad(x):
    t = jnp.tanh(_GELU_C * (x + _GELU_A * x * x * x))
    return 0.5 * (1.0 + t) + 0.5 * x * (1.0 - t * t) * _GELU_C * (1.0 + 3.0 * _GELU_A * x * x)


def _b_mid_fwd(uv3, vg, vb, ws_m, sbt, name):
    _, s, h = uv3.shape
    g_n, t, _ = ws_m.shape
    gd = h // g_n

    def body(uv_ref, vg_ref, vb_ref, ws_ref, sb_ref, o_ref):
        u = _gelu(uv_ref[0].astype(F32))
        xh, _ = _ln_stats(_gelu(uv_ref[1].astype(F32)))
        vn = (xh * vg_ref[...] + vb_ref[...]).astype(BF16)
        for g in range(g_n):
            sl = slice(g * gd, (g + 1) * gd)
            sv = jnp.dot(ws_ref[g], vn[:, sl], preferred_element_type=F32) + sb_ref[:, g:g + 1]
            o_ref[:, sl] = (u[:, sl] * sv).astype(BF16)

    vec = pl.BlockSpec((1, h), lambda i: (0, 0))
    return pl.pallas_call(
        body, name=name, out_shape=jax.ShapeDtypeStruct((s, h), BF16), grid=(s // t,),
        in_specs=[pl.BlockSpec((2, t, h), lambda i: (0, i, 0)), vec, vec,
                  pl.BlockSpec((g_n, t, t), lambda i: (0, 0, 0)), pl.BlockSpec((t, 128), lambda i: (0, 0))],
        out_specs=pl.BlockSpec((t, h), lambda i: (i, 0)), compiler_params=_params("parallel"))(uv3, vg, vb, ws_m, sbt)


def _b_mid_bwd(uv3, dgated, vg, vb, ws_m, sbt, name):
    _, s, h = uv3.shape
    g_n, t, _ = ws_m.shape
    gd = h // g_n

    def body(uv_ref, dg_ref, vg_ref, vb_ref, ws_ref, sb_ref, o_ref, dvg_ref, dvb_ref, dws_ref, dsb_ref, dvn_ref):
        i = pl.program_id(0)

        @pl.when(i == 0)
        def _():
            dvg_ref[...] = jnp.zeros_like(dvg_ref)
            dvb_ref[...] = jnp.zeros_like(dvb_ref)
            dws_ref[...] = jnp.zeros_like(dws_ref)
            dsb_ref[...] = jnp.zeros_like(dsb_ref)

        upre = uv_ref[0].astype(F32)
        vpre = uv_ref[1].astype(F32)
        u = _gelu(upre)
        xh, rstd = _ln_stats(_gelu(vpre))
        gv = vg_ref[...]
        vn = (xh * gv + vb_ref[...]).astype(BF16)
        causal = lax.broadcasted_iota(jnp.int32, (t, t), 0) >= lax.broadcasted_iota(jnp.int32, (t, t), 1)
        lane = lax.broadcasted_iota(jnp.int32, (t, 128), 1)
        for g in range(g_n):
            sl = slice(g * gd, (g + 1) * gd)
            wsg = ws_ref[g]
            sv = jnp.dot(wsg, vn[:, sl], preferred_element_type=F32) + sb_ref[:, g:g + 1]
            dg = dg_ref[:, sl].astype(F32)
            o_ref[0, :, sl] = (dg * sv * _gelu_grad(upre[:, sl])).astype(BF16)
            dsv = dg * u[:, sl]
            dsvb = dsv.astype(BF16)
            dsb_ref[...] += jnp.where(lane == g, jnp.sum(dsv, axis=1, keepdims=True), 0.0)
            dws = lax.dot_general(dsvb, vn[:, sl], _DIMS["nt"], preferred_element_type=F32)
            dws_ref[g] += jnp.where(causal, dws, 0.0)
            dvn_ref[:, sl] = lax.dot_general(wsg, dsvb, _DIMS["tn"], preferred_element_type=F32)
        dvn = dvn_ref[...]
        dvg_ref[...] += jnp.sum(dvn * xh, axis=0, keepdims=True)
        dvb_ref[...] += jnp.sum(dvn, axis=0, keepdims=True)
        o_ref[1] = (_ln_bwd(dvn, gv, xh, rstd) * _gelu_grad(vpre)).astype(BF16)

    vec = pl.BlockSpec((1, h), lambda i: (0, 0))
    return pl.pallas_call(
        body, name=name,
        out_shape=[jax.ShapeDtypeStruct((2, s, h), BF16), jax.ShapeDtypeStruct((1, h), F32), jax.ShapeDtypeStruct((1, h), F32),
                   jax.ShapeDtypeStruct((g_n, t, t), F32), jax.ShapeDtypeStruct((t, 128), F32)],
        grid=(s // t,),
        in_specs=[pl.BlockSpec((2, t, h), lambda i: (0, i, 0)), pl.BlockSpec((t, h), lambda i: (i, 0)), vec, vec,
                  pl.BlockSpec((g_n, t, t), lambda i: (0, 0, 0)), pl.BlockSpec((t, 128), lambda i: (0, 0))],
        out_specs=[pl.BlockSpec((2, t, h), lambda i: (0, i, 0)), vec, vec,
                   pl.BlockSpec((g_n, t, t), lambda i: (0, 0, 0)), pl.BlockSpec((t, 128), lambda i: (0, 0))],
        scratch_shapes=[pltpu.VMEM((t, h), F32)],
        compiler_params=_params("arbitrary"))(uv3, dgated, vg, vb, ws_m, sbt)


def _softmax_rows(sc):
    e = jnp.exp(sc - jnp.max(sc, axis=-1, keepdims=True))
    return e / jnp.sum(e, axis=-1, keepdims=True)


def _attn_fwd(q, kv3, name):
    s, d = q.shape
    m = kv3.shape[1]
    dh = d // XA_HEADS
    scale = dh ** -0.5
    r = _row_tile(s)

    def body(q_ref, kv_ref, o_ref):
        for hd in range(XA_HEADS):
            sl = slice(hd * dh, (hd + 1) * dh)
            sc = lax.dot_general(q_ref[:, sl], kv_ref[0, :, sl], _DIMS["nt"], preferred_element_type=F32) * scale
            p = _softmax_rows(sc).astype(BF16)
            o_ref[:, sl] = jnp.dot(p, kv_ref[1, :, sl], preferred_element_type=F32).astype(BF16)

    return pl.pallas_call(
        body, name=name, out_shape=jax.ShapeDtypeStruct((s, d), BF16), grid=(s // r,),
        in_specs=[pl.BlockSpec((r, d), lambda i: (i, 0)), pl.BlockSpec((2, m, d), lambda i: (0, 0, 0))],
        out_specs=pl.BlockSpec((r, d), lambda i: (i, 0)), compiler_params=_params("parallel"))(q, kv3)


def _attn_bwd(q, kv3, do, name):
    s, d = q.shape
    m = kv3.shape[1]
    dh = d // XA_HEADS
    scale = dh ** -0.5
    r = _row_tile(s)

    def body(q_ref, kv_ref, do_ref, dq_ref, dkv_ref):
        i = pl.program_id(0)

        @pl.when(i == 0)
        def _():
            dkv_ref[...] = jnp.zeros_like(dkv_ref)

        for hd in range(XA_HEADS):
            sl = slice(hd * dh, (hd + 1) * dh)
            qh = q_ref[:, sl]
            kh = kv_ref[0, :, sl]
            doh = do_ref[:, sl]
            sc = lax.dot_general(qh, kh, _DIMS["nt"], preferred_element_type=F32) * scale
            p = _softmax_rows(sc)
            pb = p.astype(BF16)
            dkv_ref[1, :, sl] += lax.dot_general(pb, doh, _DIMS["tn"], preferred_element_type=F32)
            dp = lax.dot_general(doh, kv_ref[1, :, sl], _DIMS["nt"], preferred_element_type=F32)
            ds = (p * (dp - jnp.sum(dp * p, axis=-1, keepdims=True)) * scale).astype(BF16)
            dq_ref[:, sl] = jnp.dot(ds, kh, preferred_element_type=F32).astype(BF16)
            dkv_ref[0, :, sl] += lax.dot_general(ds, qh, _DIMS["tn"], preferred_element_type=F32)

    row = pl.BlockSpec((r, d), lambda i: (i, 0))
    kvs = pl.BlockSpec((2, m, d), lambda i: (0, 0, 0))
    return pl.pallas_call(
        body, name=name, out_shape=[jax.ShapeDtypeStruct((s, d), BF16), jax.ShapeDtypeStruct((2, m, d), F32)],
        grid=(s // r,), in_specs=[row, kvs, row], out_specs=[row, kvs], compiler_params=_params("arbitrary"))(q, kv3, do)


FFN_COL_TILE = 512


def _ffn_gu_fwd(h, w_gu, name):
    s, d = h.shape
    f = w_gu.shape[2] // 2
    tm = _div_tile(s, MM_OUT_TILE_CAP)
    tn = _div_tile(f, FFN_COL_TILE)
    nj = f // tn

    def body(a_ref, bg_ref, bu_ref, gu_ref, act_ref):
        a = a_ref[...]
        gate = jnp.dot(a, bg_ref[...], preferred_element_type=F32)
        up = jnp.dot(a, bu_ref[...], preferred_element_type=F32)
        gu_ref[0] = gate.astype(BF16)
        gu_ref[1] = up.astype(BF16)
        act_ref[...] = (gate * jax.nn.sigmoid(gate) * up).astype(BF16)

    return pl.pallas_call(
        body, name=name, out_shape=[jax.ShapeDtypeStruct((2, s, f), BF16), jax.ShapeDtypeStruct((s, f), BF16)],
        grid=(s // tm, nj),
        in_specs=[pl.BlockSpec((tm, d), lambda i, j: (i, 0)), pl.BlockSpec((None, d, tn), lambda i, j: (0, 0, j)),
                  pl.BlockSpec((None, d, tn), lambda i, j: (0, 0, j + nj))],
        out_specs=[pl.BlockSpec((2, tm, tn), lambda i, j: (0, i, j)), pl.BlockSpec((tm, tn), lambda i, j: (i, j))],
        compiler_params=_params("parallel", "parallel"))(h, w_gu, w_gu)


def _ffn_down_bwd(dy, w_down, gu3, name):
    s, d = dy.shape
    f = w_down.shape[1]
    tm = _div_tile(s, MM_OUT_TILE_CAP)
    tn = _div_tile(f, FFN_COL_TILE)

    def body(dy_ref, w_ref, gu_ref, o_ref):
        da = lax.dot_general(dy_ref[...], w_ref[...], _DIMS["nt"], preferred_element_type=F32)
        gate = gu_ref[0].astype(F32)
        up = gu_ref[1].astype(F32)
        sg = jax.nn.sigmoid(gate)
        o_ref[0] = (da * up * (sg + gate * sg * (1.0 - sg))).astype(BF16)
        o_ref[1] = (da * gate * sg).astype(BF16)

    return pl.pallas_call(
        body, name=name, out_shape=jax.ShapeDtypeStruct((2, s, f), BF16), grid=(s // tm, f // tn),
        in_specs=[pl.BlockSpec((tm, d), lambda i, j: (i, 0)), pl.BlockSpec((None, tn, d), lambda i, j: (0, j, 0)),
                  pl.BlockSpec((2, tm, tn), lambda i, j: (0, i, j))],
        out_specs=pl.BlockSpec((2, tm, tn), lambda i, j: (0, i, j)),
        compiler_params=_params("parallel", "parallel"))(dy, w_down, gu3)


def _ids():
    x, y, c = lax.axis_index("x"), lax.axis_index("y"), lax.axis_index("c")
    return x, y, c, 2 * x + y


def _chip_peers(x, y):
    return [(d - 1, 2 * (x ^ (d >> 1)) + (y ^ (d & 1)), x ^ (d >> 1), y ^ (d & 1)) for d in (1, 2, 3)]


def _remote(src, dst, ssem, rsem, dev):
    return pltpu.make_async_remote_copy(src_ref=src, dst_ref=dst, send_sem=ssem, recv_sem=rsem, device_id=dev,
                                        device_id_type=MESH)


def _gview(ref, kind, j, cc):
    _, k, n = ref.shape
    if kind == "row":
        return ref.at[:, pl.ds(j * (k // N_CHIPS) + cc * (k // (2 * N_CHIPS)), k // (2 * N_CHIPS)), :]
    return ref.at[:, pl.ds(cc * (k // 2), k // 2), pl.ds(j * (n // N_CHIPS), n // N_CHIPS)]


def _sview(ref, cc):
    r = ref.shape[1]
    return ref.at[:, pl.ds(cc * (r // 2), r // 2), :]


def _comm_call(body, name, ins, out_shapes, n_sems, aliases=None):
    return pl.pallas_call(
        body, name=name, out_shape=out_shapes, in_specs=[HBM] * len(ins), out_specs=[HBM] * len(out_shapes),
        scratch_shapes=[pltpu.SemaphoreType.DMA((n,)) for n in n_sems], input_output_aliases=aliases or {},
        compiler_params=pltpu.CompilerParams(has_side_effects=True))(*ins)


def _mesh_scalars():
    x, y, c = lax.axis_index("x"), lax.axis_index("y"), lax.axis_index("c")
    return jnp.stack([2 * x + y, c]).astype(jnp.int32)


def _slab_rows(rows, cols, itemsize=4):
    best = None
    for cand in range(16, rows + 1, 16):
        if rows % cand == 0 and cand * cols * itemsize <= (2 << 20):
            best = cand
    return best or rows


def _ag_place(shard, layer, kind, name, dep=None):
    deps = [] if dep is None else [dep]
    _, r, n = shard.shape
    full = (1, r * N_CHIPS, n) if kind == "row" else (1, r, n * N_CHIPS)
    tr = _slab_rows(r, n)
    nt = r // tr
    if kind == "row":
        out_spec = pl.BlockSpec((None, tr, n), lambda t, s: (0, s[0] * nt + t, 0))
    else:
        out_spec = pl.BlockSpec((None, tr, n), lambda t, s: (0, t, s[0]))

    def body(s_ref, i_ref, *rest):
        rest[-1][...] = i_ref[...].astype(BF16)

    return pl.pallas_call(
        body, name=name, out_shape=jax.ShapeDtypeStruct(full, BF16),
        grid_spec=pltpu.PrefetchScalarGridSpec(
            num_scalar_prefetch=1, grid=(nt,),
            in_specs=[pl.BlockSpec((None, tr, n), lambda t, s: (layer, t, 0))] + [pl.BlockSpec(memory_space=pl.ANY)] * len(deps),
            out_specs=out_spec),
        compiler_params=_params("parallel"))(_mesh_scalars(), shard, *deps)


SEM = pl.BlockSpec(memory_space=pltpu.SEMAPHORE)
ANY = pl.BlockSpec(memory_space=pl.ANY)
DATAFLOW = pltpu.SideEffectType.DATAFLOW_SIDE_EFFECTING


def _in_hbm(arrs):
    return [pltpu.with_memory_space_constraint(a, pltpu.HBM) for a in arrs]


def _ag_start(bufs, kinds, name, after=None):
    n = len(bufs)
    afters = [] if after is None else [after]
    n_in = n + len(afters)

    def body(*refs):
        ssem, rsem, token = refs[n_in], refs[n_in + 1], refs[-1]
        x, y, c, me = _ids()
        for t in range(n):
            mine = _gview(refs[t], kinds[t], me, c)
            for d, _, px, py in _chip_peers(x, y):
                _remote(mine, mine, ssem.at[3 * t + d], rsem.at[3 * t + d], (px, py, c)).start()
        token[...] = jnp.zeros_like(token)

    outs = pl.pallas_call(
        body, name=name,
        out_shape=(pltpu.SemaphoreType.DMA((3 * n,)), pltpu.SemaphoreType.DMA((3 * n,)),
                   *[pltpu.HBM(b.shape, b.dtype) for b in bufs], jax.ShapeDtypeStruct((8, 128), F32)),
        in_specs=[HBM] * n + [ANY] * len(afters), out_specs=(SEM, SEM, *[HBM] * n, pl.BlockSpec(memory_space=pltpu.VMEM)),
        input_output_aliases={t: 2 + t for t in range(n)},
        compiler_params=pltpu.CompilerParams(has_side_effects=DATAFLOW))(*_in_hbm(bufs), *afters)
    return outs[0], outs[1], list(outs[2:2 + n]), outs[-1]


def _ag_wait(ssem, rsem, bufs, kinds, after, name):
    n = len(bufs)

    def body(*refs):
        ssem_ref, rsem_ref = refs[n], refs[n + 1]
        x, y, c, me = _ids()
        for t in range(n):
            mine = _gview(refs[t], kinds[t], me, c)
            for d, pj, px, py in _chip_peers(x, y):
                theirs = _gview(refs[t], kinds[t], pj, c)
                _remote(mine, mine, ssem_ref.at[3 * t + d], rsem_ref.at[3 * t + d], (px, py, c)).wait_send()
                _remote(theirs, theirs, ssem_ref.at[3 * t + d], rsem_ref.at[3 * t + d], (px, py, c)).wait_recv()

    outs = pl.pallas_call(
        body, name=name, out_shape=[pltpu.HBM(b.shape, b.dtype) for b in bufs],
        in_specs=[HBM] * n + [SEM, SEM, ANY], out_specs=[HBM] * n, input_output_aliases={t: t for t in range(n)},
        compiler_params=pltpu.CompilerParams(has_side_effects=DATAFLOW))(*bufs, ssem, rsem, after)
    return list(outs)


def _ag_forward(bufs, kinds, name):
    n = len(bufs)

    def body(*refs):
        outs = refs[n:2 * n]
        ssem, rsem = refs[2 * n], refs[2 * n + 1]
        x, y, c, _ = _ids()
        sib = (x, y, 1 - c)
        sends = []
        for t in range(n):
            for d, pj, _, _ in _chip_peers(x, y):
                piece = _gview(outs[t], kinds[t], pj, c)
                sends.append(_remote(piece, piece, ssem.at[3 * t + d], rsem.at[3 * t + d], sib))
        for cp in sends:
            cp.start()
        for t in range(n):
            for d, pj, _, _ in _chip_peers(x, y):
                piece = _gview(outs[t], kinds[t], pj, 1 - c)
                _remote(piece, piece, ssem.at[3 * t + d], rsem.at[3 * t + d], sib).wait_recv()
        for cp in sends:
            cp.wait_send()

    return _comm_call(body, name, bufs, [jax.ShapeDtypeStruct(b.shape, b.dtype) for b in bufs], (3 * n, 3 * n),
                      {t: t for t in range(n)})


def _rs1(g_fulls, kinds, name, dep=None):
    n = len(g_fulls)
    outs = []
    for g, kind in zip(g_fulls, kinds):
        l, k, nn = g.shape
        piece = (l, k // (2 * N_CHIPS), nn) if kind == "row" else (l, k // 2, nn // N_CHIPS)
        outs.append(jax.ShapeDtypeStruct((N_CHIPS,) + piece, g.dtype))

    n_in = n + (dep is not None)

    def body(*refs):
        ssem, rsem = refs[n_in + n], refs[n_in + n + 1]
        x, y, c, _ = _ids()
        sends = [_remote(_gview(refs[t], kinds[t], j, 1 - c), refs[n_in + t].at[j], ssem.at[4 * t + j], rsem.at[4 * t + j],
                         (x, y, 1 - c)) for t in range(n) for j in range(N_CHIPS)]
        for cp in sends:
            cp.start()
        for cp in sends:
            cp.wait()

    return _comm_call(body, name, g_fulls + ([] if dep is None else [dep]), outs, (4 * n, 4 * n))


def _rs_add1(g_full, got, kind, name):
    l, k, n = g_full.shape
    _, _, pr, pc = got.shape
    tr = _slab_rows(pr, pc, 2)
    nt = pr // tr
    if kind == "row":
        g_spec = pl.BlockSpec((None, tr, n), lambda j, li, t, s: (li, (2 * j + s[1]) * nt + t, 0))
    else:
        g_spec = pl.BlockSpec((None, tr, pc), lambda j, li, t, s: (li, s[1] * nt + t, j))
    slot = pl.BlockSpec((None, None, tr, pc), lambda j, li, t, s: (j, li, t, 0))

    def body(s_ref, g_ref, got_ref, o_ref):
        o_ref[...] = g_ref[...] + got_ref[...]

    return pl.pallas_call(
        body, name=name, out_shape=jax.ShapeDtypeStruct(got.shape, BF16),
        grid_spec=pltpu.PrefetchScalarGridSpec(num_scalar_prefetch=1, grid=(N_CHIPS, l, nt), in_specs=[g_spec, slot],
                                               out_specs=slot),
        compiler_params=_params("parallel", "parallel", "parallel"))(_mesh_scalars(), g_full, got)


def _rs2_start(ps, name):
    n = len(ps)
    lands = [lax.empty(p.shape, p.dtype) for p in ps]

    def body(*refs):
        ssem, rsem, token = refs[2 * n], refs[2 * n + 1], refs[-1]
        x, y, c, me = _ids()
        for t in range(n):
            for d, pj, px, py in _chip_peers(x, y):
                _remote(refs[t].at[pj], refs[n + t].at[me], ssem.at[3 * t + d], rsem.at[3 * t + d], (px, py, c)).start()
        token[...] = jnp.zeros_like(token)

    outs = pl.pallas_call(
        body, name=name,
        out_shape=(pltpu.SemaphoreType.DMA((3 * n,)), pltpu.SemaphoreType.DMA((3 * n,)),
                   *[pltpu.HBM(p.shape, p.dtype) for p in ps + lands], jax.ShapeDtypeStruct((8, 128), F32)),
        in_specs=[HBM] * (2 * n), out_specs=(SEM, SEM, *[HBM] * (2 * n), pl.BlockSpec(memory_space=pltpu.VMEM)),
        input_output_aliases={t: 2 + t for t in range(2 * n)},
        compiler_params=pltpu.CompilerParams(has_side_effects=DATAFLOW))(*_in_hbm(ps + lands))
    return outs[0], outs[1], list(outs[2:2 + n]), list(outs[2 + n:2 + 2 * n]), outs[-1]


def _rs2_wait(ssem, rsem, ps, lands, after, name):
    n = len(ps)

    def body(*refs):
        ssem_ref, rsem_ref = refs[2 * n], refs[2 * n + 1]
        x, y, c, me = _ids()
        for t in range(n):
            for d, pj, px, py in _chip_peers(x, y):
                _remote(refs[t].at[pj], refs[n + t].at[me], ssem_ref.at[3 * t + d], rsem_ref.at[3 * t + d], (px, py, c)).wait_send()
                _remote(refs[t].at[pj], refs[n + t].at[pj], ssem_ref.at[3 * t + d], rsem_ref.at[3 * t + d], (px, py, c)).wait_recv()

    outs = pl.pallas_call(
        body, name=name, out_shape=[pltpu.HBM(p.shape, p.dtype) for p in ps + lands],
        in_specs=[HBM] * (2 * n) + [SEM, SEM, ANY], out_specs=[HBM] * (2 * n),
        input_output_aliases={t: t for t in range(2 * n)},
        compiler_params=pltpu.CompilerParams(has_side_effects=DATAFLOW))(*ps, *lands, ssem, rsem, after)
    return list(outs[:n]), list(outs[n:])


def _rs_add2(p, got, into, layer, name):
    _, _, pr, pc = p.shape
    tr = _slab_rows(pr, pc)
    nt = pr // tr

    def slot(d):
        return pl.BlockSpec((None, None, tr, pc), lambda t, s: (s[0] ^ d, 0, t, 0))

    def body(s_ref, p_ref, g1_ref, g2_ref, g3_ref, i_ref, o_ref):
        o_ref[...] = (p_ref[...].astype(F32) + g1_ref[...].astype(F32) + g2_ref[...].astype(F32) + g3_ref[...].astype(F32))

    return pl.pallas_call(
        body, name=name, out_shape=jax.ShapeDtypeStruct(into.shape, F32),
        grid_spec=pltpu.PrefetchScalarGridSpec(
            num_scalar_prefetch=1, grid=(nt,), in_specs=[slot(0), slot(1), slot(2), slot(3), HBM],
            out_specs=pl.BlockSpec((None, tr, pc), lambda t, s: (layer, s[1] * nt + t, 0))),
        input_output_aliases={5: 0},
        compiler_params=_params("parallel"))(_mesh_scalars(), p, got, got, got, into)


def _rs3(shards, layers, name):
    n = len(shards)

    def body(*refs):
        outs = refs[n:2 * n]
        token, ssem, rsem = refs[2 * n], refs[2 * n + 1], refs[2 * n + 2]
        x, y, c, _ = _ids()
        sib = (x, y, 1 - c)
        token[...] = jnp.zeros_like(token)

        def half(t, cc):
            return _sview(outs[t].at[pl.ds(layers[t], 1)], cc)

        sends = [_remote(half(t, c), half(t, c), ssem.at[t], rsem.at[t], sib) for t in range(n)]
        for cp in sends:
            cp.start()
        for t in range(n):
            _remote(half(t, 1 - c), half(t, 1 - c), ssem.at[t], rsem.at[t], sib).wait_recv()
        for cp in sends:
            cp.wait_send()

    outs = pl.pallas_call(
        body, name=name, out_shape=[jax.ShapeDtypeStruct(s.shape, s.dtype) for s in shards] + [jax.ShapeDtypeStruct((8, 128), F32)],
        in_specs=[HBM] * n, out_specs=[HBM] * n + [pl.BlockSpec(memory_space=pltpu.VMEM)],
        scratch_shapes=[pltpu.SemaphoreType.DMA((n,)), pltpu.SemaphoreType.DMA((n,))],
        input_output_aliases={t: t for t in range(n)}, compiler_params=pltpu.CompilerParams(has_side_effects=True))(*shards)
    return list(outs[:n]), outs[n]


def _ag_small(sp, name):
    def body(s_ref, o_ref, ssem, rsem, lsem):
        x, y, c, me = _ids()
        local = pltpu.make_async_copy(s_ref, o_ref.at[me], lsem.at[0])
        local.start()
        sends = [_remote(s_ref, o_ref.at[me], ssem.at[d], rsem.at[d], (px, py, c)) for d, _, px, py in _chip_peers(x, y)]
        for cp in sends:
            cp.start()
        for d, pj, px, py in _chip_peers(x, y):
            _remote(s_ref, o_ref.at[pj], ssem.at[d], rsem.at[d], (px, py, c)).wait_recv()
        for cp in sends:
            cp.wait_send()
        local.wait()

    return _comm_call(body, name, [sp], [jax.ShapeDtypeStruct((N_CHIPS,) + sp.shape, sp.dtype)], (3, 3, 1))[0]


def _slot_place(g, name):
    rows, cols = g.shape
    x, y, c = lax.axis_index("x"), lax.axis_index("y"), lax.axis_index("c")
    slot = (4 * x + 2 * y + c).astype(jnp.int32).reshape(1)

    def body(s_ref, i_ref, o_ref):
        o_ref[...] = i_ref[...]

    return pl.pallas_call(
        body, name=name, out_shape=jax.ShapeDtypeStruct((8, rows, cols), g.dtype),
        grid_spec=pltpu.PrefetchScalarGridSpec(
            num_scalar_prefetch=1, grid=(1,), in_specs=[pl.BlockSpec((rows, cols), lambda t, s: (0, 0))],
            out_specs=pl.BlockSpec((None, rows, cols), lambda t, s: (s[0], 0, 0))),
        compiler_params=_params("arbitrary"))(slot, g)


def _gather8_peers(x, y, c):
    return [(d - 1, x ^ (d >> 2), y ^ ((d >> 1) & 1), c ^ (d & 1)) for d in range(1, 8)]


def _gather8_start(buf, name):
    def body(b_ref, ssem, rsem, b_thru, token):
        x, y, c, _ = _ids()
        mine = b_ref.at[4 * x + 2 * y + c]
        for d, px, py, pc in _gather8_peers(x, y, c):
            _remote(mine, mine, ssem.at[d], rsem.at[d], (px, py, pc)).start()
        token[...] = jnp.zeros_like(token)

    outs = pl.pallas_call(
        body, name=name,
        out_shape=(pltpu.SemaphoreType.DMA((7,)), pltpu.SemaphoreType.DMA((7,)), pltpu.HBM(buf.shape, buf.dtype),
                   jax.ShapeDtypeStruct((8, 128), F32)),
        in_specs=[HBM], out_specs=(SEM, SEM, HBM, pl.BlockSpec(memory_space=pltpu.VMEM)), input_output_aliases={0: 2},
        compiler_params=pltpu.CompilerParams(has_side_effects=DATAFLOW))(*_in_hbm([buf]))
    return outs


def _gather8_wait(ssem, rsem, buf, after, name):
    def body(b_ref, ssem_ref, rsem_ref, after_ref, b_out):
        x, y, c, _ = _ids()
        mine = b_ref.at[4 * x + 2 * y + c]
        for d, px, py, pc in _gather8_peers(x, y, c):
            theirs = b_ref.at[4 * px + 2 * py + pc]
            _remote(mine, mine, ssem_ref.at[d], rsem_ref.at[d], (px, py, pc)).wait_send()
            _remote(theirs, theirs, ssem_ref.at[d], rsem_ref.at[d], (px, py, pc)).wait_recv()

    return pl.pallas_call(
        body, name=name, out_shape=pltpu.HBM(buf.shape, buf.dtype), in_specs=[HBM, SEM, SEM, ANY], out_specs=HBM,
        input_output_aliases={0: 0},
        compiler_params=pltpu.CompilerParams(has_side_effects=DATAFLOW))(buf, ssem, rsem, after)


def _sum_slots(a, out_dtype, name):
    n = a.shape[0]
    shape = a.shape[1:]
    cols = shape[-1]
    a3 = a.reshape(n, -1, cols)
    rows = a3.shape[1]
    tr = rows
    for cand in (512, 256, 128, 64, 32, 16):
        if rows % cand == 0 and cand * cols * 4 <= (1 << 20):
            tr = cand
            break

    def body(a_ref, o_ref):
        acc = a_ref[0].astype(F32)
        for j in range(1, n):
            acc = acc + a_ref[j].astype(F32)
        o_ref[...] = acc.astype(o_ref.dtype)

    out = pl.pallas_call(
        body, name=name, out_shape=jax.ShapeDtypeStruct((rows, cols), out_dtype), grid=(rows // tr,),
        in_specs=[pl.BlockSpec((n, tr, cols), lambda i: (0, i, 0))], out_specs=pl.BlockSpec((tr, cols), lambda i: (i, 0)),
        compiler_params=_params("parallel"))(a3)
    return out.reshape(shape)


def kernel(x, mem, mix_norm, xa_norm, xa_wq, xa_wkv, xa_wo, ffn_norm, ffn_w_gu, ffn_w_down, a_w_in, a_conv_w, a_w_out, b_w_in, b_v_g, b_v_b, b_w_s, b_s_bias, b_w_out, c_w_in, c_conv_w, c_conv_b, c_ln_g, c_ln_b, c_w_out, loss_target, m_mix_norm, m_xa_norm, m_xa_wq, m_xa_wkv, m_xa_wo, m_ffn_norm, m_ffn_w_gu, m_ffn_w_down, m_a_w_in, m_a_conv_w, m_a_w_out, m_b_w_in, m_b_v_g, m_b_v_b, m_b_w_s, m_b_s_bias, m_b_w_out, m_c_w_in, m_c_conv_w, m_c_conv_b, m_c_ln_g, m_c_ln_b, m_c_w_out, v_mix_norm, v_xa_norm, v_xa_wq, v_xa_wkv, v_xa_wo, v_ffn_norm, v_ffn_w_gu, v_ffn_w_down, v_a_w_in, v_a_conv_w, v_a_w_out, v_b_w_in, v_b_v_g, v_b_v_b, v_b_w_s, v_b_s_bias, v_b_w_out, v_c_w_in, v_c_conv_w, v_c_conv_b, v_c_ln_g, v_c_ln_b, v_c_w_out):
    given = dict(locals())
    w = {n: given[n] for n in WEIGHTS}
    depth = mix_norm.shape[0]
    s, d = x.shape[1], x.shape[2]
    n_mem = mem.shape[1]
    ds = d // N_CHIPS
    xin = x.reshape(s, d)
    memv = mem.reshape(n_mem, d)
    target = loss_target.reshape(s, d)
    me = 2 * lax.axis_index("x") + lax.axis_index("y")

    wg = {n: [None] * w[n].shape[0] for n in BIG_KINDS}

    def mixer_keys(i):
        return [("abc"[i % 3] + "_w_in", i // 3), ("abc"[i % 3] + "_w_out", i // 3)]

    def rest_keys(i):
        return [("xa_wq", i), ("xa_wkv", i), ("xa_wo", i), ("ffn_w_gu", i), ("ffn_w_down", i)]

    def ag_begin(keys, tag, after):
        kinds = [BIG_KINDS[n] for n, _ in keys]
        ssem, rsem, bufs, token = _ag_start([placed[k] for k in keys], kinds, "ag_start_" + tag, after)
        return keys, kinds, ssem, rsem, bufs, token, tag

    def ag_end(state, after):
        keys, kinds, ssem, rsem, bufs, _, tag = state
        bufs = _ag_forward(_ag_wait(ssem, rsem, bufs, kinds, after, "ag_wait_" + tag), kinds, "ag_fwd_" + tag)
        for (n, l), buf in zip(keys, bufs):
            wg[n][l] = buf

    def pad8(t):
        return jnp.pad(t, ((0, (-t.shape[0]) % 8), (0, 0)))

    small_rows = [w[n].reshape(-1, ds) for n in SMALL_SHARDED]
    counts = [t.shape[0] for t in small_rows]
    gathered = _ag_small(jnp.concatenate([pad8(t) for t in small_rows], axis=0), "ag_small")
    placed = {(n, l): _ag_place(w[n], l, BIG_KINDS[n], f"ag_place_{n}_{l}", dep=gathered) for n, l in mixer_keys(0)}
    gathered = jnp.transpose(gathered, (1, 0, 2)).reshape(-1, d)
    full, off = {}, 0
    for n, cnt in zip(SMALL_SHARDED, counts):
        full[n] = gathered[off:off + cnt].reshape(w[n].shape[:-1] + (d,))
        off += cnt + (-cnt) % 8
    t_chunk = b_w_s.shape[-1]
    tril = jnp.tril(jnp.ones((t_chunk, t_chunk), dtype=bool))

    def vec(a):
        return a.reshape(1, -1)

    def b_params(slot):
        ws_m = jnp.where(tril[None], b_w_s[slot], 0.0).astype(BF16)
        sbt = jnp.zeros((t_chunk, 128), F32).at[:, :b_s_bias.shape[1]].set(b_s_bias[slot].T)
        return vec(b_v_g[slot]), vec(b_v_b[slot]), ws_m, sbt

    saved = []
    xc = xin
    ag_groups = [(mixer_keys(0), "0m"), (rest_keys(0), "0r")] + [(mixer_keys(j) + rest_keys(j), f"{j}") for j in range(1, depth)]
    ag_state = {}

    def ag_begin_group(k, after):
        if k >= len(ag_groups):
            return None
        ag_state[k] = ag_begin(*ag_groups[k], after)
        return ag_state[k][5]

    dep = ag_begin_group(0, None)
    for n, kind in BIG_KINDS.items():
        for l in range(w[n].shape[0]):
            if (n, l) not in placed:
                placed[n, l] = dep = _ag_place(w[n], l, kind, f"ag_place_{n}_{l}", dep=dep)
    ag_end(ag_state[0], dep)
    for i in range(depth):
        kind, slot = i % 3, i // 3
        t = f"{i}"
        dep = ag_begin_group(1, wg[mixer_keys(0)[0][0]][0]) if i == 0 else ag_begin_group(i + 2, xc)
        sv = {"x0": xc}
        if i == 0:
            h = _rms_fwd(xc, vec(full["mix_norm"][i, 0]), "rms_mix_" + t)
        sv["h1"] = h
        if kind == 0:
            pre = _mm("nn", h, wg["a_w_in"], BF16, "a_in_" + t, bl=slot, o_parts=3, dep=dep)
            mid = _a_mid_fwd(pre, full["a_conv_w"][slot], "a_mid_" + t)
            y = _mm("nn", mid, wg["a_w_out"], F32, "a_out_" + t, bl=slot)
        elif kind == 1:
            pre = _mm("nn", h, wg["b_w_in"], BF16, "b_in_" + t, bl=slot, o_parts=2, dep=dep)
            mid = _b_mid_fwd(pre, *b_params(slot), "b_mid_" + t)
            y = _mm("nn", mid, wg["b_w_out"], F32, "b_out_" + t, bl=slot)
        else:
            pre = _mm("nn", h, wg["c_w_in"], BF16, "c_in_" + t, bl=slot, o_parts=2, dep=dep)
            y2 = _c_conv_fwd(pre, full["c_conv_w"][slot], vec(full["c_conv_b"][slot]), "c_conv_" + t)
            sv["cy2"] = y2
            mid = _c_ln_fwd(y2, vec(full["c_ln_g"][slot]), vec(full["c_ln_b"][slot]), "c_ln_" + t)
            y = _mm("nn", mid, wg["c_w_out"], F32, "c_out_" + t, bl=slot)
        sv.update(pre=pre, mid=mid, y1=y)
        xc, h = _res_rms_fwd(xc, y, vec(full["mix_norm"][i, 1]), vec(full["xa_norm"][i, 0]), "res_mix_" + t)

        sv["x1"] = xc
        dep = None
        if i == 0:
            ag_end(ag_state[1], xc)
            dep = ag_begin_group(2, xc)
        mem_n = _rms_fwd(memv, vec(full["xa_norm"][i, 2]), "rms_mem_" + t)
        q = _mm("nn", h, wg["xa_wq"], BF16, "xa_q_" + t, bl=i, dep=dep)
        kv3 = _mm("nn", mem_n, wg["xa_wkv"], BF16, "xa_kv_" + t, bl=i, o_parts=2)
        o = _attn_fwd(q, kv3, "attn_" + t)
        y = _mm("nn", o, wg["xa_wo"], F32, "xa_o_" + t, bl=i)
        sv.update(h2=h, mem_n=mem_n, q=q, kv3=kv3, o=o, y2=y)
        xc, h = _res_rms_fwd(xc, y, vec(full["xa_norm"][i, 1]), vec(full["ffn_norm"][i, 0]), "res_xa_" + t)

        sv["x2"] = xc
        gu3, act = _ffn_gu_fwd(h, wg["ffn_w_gu"][i], "ffn_gu_" + t)
        y = _mm("nn", act, wg["ffn_w_down"], F32, "ffn_down_" + t, bl=i)
        sv.update(h3=h, gu3=gu3, act=act, y3=y)
        xc, h = _res_rms_fwd(xc, y, vec(full["ffn_norm"][i, 1]),
                             vec(full["mix_norm"][i + 1, 0]) if i + 1 < depth else None, "res_ffn_" + t)
        saved.append(sv)
        if i + 2 in ag_state:
            ag_end(ag_state[i + 2], xc)

    loss_blk, dx = _loss(xc, target, "loss")
    loss = lax.psum(loss_blk[0, 0], ("x", "y", "c"))

    gbuf = {}
    gfin = {n: lax.empty(w[n].shape, F32) for n in BIG_KINDS}
    gsmall = {n: [None] * full[n].shape[0] for n in ("mix_norm", "xa_norm", "ffn_norm", "a_conv_w", "c_conv_w", "c_conv_b",
                                                      "c_ln_g", "c_ln_b")}
    grepl = {}

    def wgrad(name, l, a, dy, tag, b_parts=1):
        g2 = _mm("tn", a, dy, BF16, "wg_" + tag, b_parts=b_parts)
        gbuf[name, l] = g2.reshape((1,) + g2.shape)

    def rs_begin(keys, tag, dep):
        kinds = [BIG_KINDS[n] for n, _ in keys]
        gots = _rs1([gbuf[k] for k in keys], kinds, "rs1_" + tag, dep=dep)
        ps = [_rs_add1(gbuf[k], got, kind, f"rs_add1_{k[0]}_{k[1]}") for k, got, kind in zip(keys, gots, kinds)]
        ssem, rsem, ps, lands, token = _rs2_start(ps, "rs2_start_" + tag)
        return keys, ssem, rsem, ps, lands, token, tag

    def rs_end(state, after):
        keys, ssem, rsem, ps, lands, _, tag = state
        ps, lands = _rs2_wait(ssem, rsem, ps, lands, after, "rs2_wait_" + tag)
        for (n, l), p, land in zip(keys, ps, lands):
            gfin[n] = _rs_add2(p, land, gfin[n], l, f"rs_add2_{n}_{l}")
        outs, token = _rs3([gfin[n] for n, _ in keys], [l for _, l in keys], "rs3_" + tag)
        for (n, _), o in zip(keys, outs):
            gfin[n] = o
        return token

    rs_state, rs_token = None, None

    for i in reversed(range(depth)):
        kind, slot = i % 3, i // 3
        t = f"{i}"
        sv = saved[i]
        dy, dg_post = _rms_bwd(sv["y3"], vec(full["ffn_norm"][i, 1]), dx, None, BF16, "rmsb_ffn_post_" + t,
                               dep=None if rs_state is None else rs_state[5])
        wgrad("ffn_w_down", i, sv["act"], dy, "ffn_down_" + t)
        dgu3 = _ffn_down_bwd(dy, wg["ffn_w_down"][i], sv["gu3"], "dg_ffn_down_" + t)
        wgrad("ffn_w_gu", i, sv["h3"], dgu3, "ffn_gu_" + t, b_parts=2)
        dh = _mm("nt", dgu3, wg["ffn_w_gu"], F32, "dg_ffn_gu_" + t, bl=i, a_parts=2)
        dx, dg_pre = _rms_bwd(sv["x2"], vec(full["ffn_norm"][i, 0]), dh, dx, F32, "rmsb_ffn_pre_" + t)
        gsmall["ffn_norm"][i] = jnp.concatenate([dg_pre, dg_post], axis=0)
        if i == 0:
            rs_state_f = rs_begin(rest_keys(0)[3:], "0f", None)
        dy, dg_post = _rms_bwd(sv["y2"], vec(full["xa_norm"][i, 1]), dx, None, BF16, "rmsb_xa_post_" + t,
                               dep=rs_state_f[5] if i == 0 else None)
        wgrad("xa_wo", i, sv["o"], dy, "xa_o_" + t)
        do = _mm("nt", dy, wg["xa_wo"], BF16, "dg_xa_o_" + t, bl=i)
        dq, dkv3 = _attn_bwd(sv["q"], sv["kv3"], do, "attn_b_" + t)
        wgrad("xa_wq", i, sv["h2"], dq, "xa_q_" + t)
        dh = _mm("nt", dq, wg["xa_wq"], F32, "dg_xa_q_" + t, bl=i)
        dkv3 = dkv3.astype(BF16)
        wgrad("xa_wkv", i, sv["mem_n"], dkv3, "xa_kv_" + t, b_parts=2)
        dmem_n = _mm("nt", dkv3, wg["xa_wkv"], F32, "dg_xa_kv_" + t, bl=i, a_parts=2)
        _, dg_mem = _rms_bwd(memv, vec(full["xa_norm"][i, 2]), dmem_n, None, F32, "rmsb_mem_" + t)
        dx, dg_pre = _rms_bwd(sv["x1"], vec(full["xa_norm"][i, 0]), dh, dx, F32, "rmsb_xa_pre_" + t)
        gsmall["xa_norm"][i] = jnp.concatenate([dg_pre, dg_post, dg_mem], axis=0)
        if i == 0:
            rs_token = rs_end(rs_state, dx)
            rs_state = rs_begin(rest_keys(0)[:3], "0x", rs_token)
        dy, dg_post = _rms_bwd(sv["y1"], vec(full["mix_norm"][i, 1]), dx, None, BF16, "rmsb_mix_post_" + t,
                               dep=rs_state[5] if i == 0 else None)
        if kind == 0:
            wgrad("a_w_out", slot, sv["mid"], dy, "a_out_" + t)
            dmid = _mm("nt", dy, wg["a_w_out"], F32, "dg_a_out_" + t, bl=slot)
            dpre, dcw = _a_mid_bwd(sv["pre"], dmid, full["a_conv_w"][slot], "a_mid_b_" + t)
            gsmall["a_conv_w"][slot] = dcw
            wgrad("a_w_in", slot, sv["h1"], dpre, "a_in_" + t, b_parts=3)
            dh = _mm("nt", dpre, wg["a_w_in"], F32, "dg_a_in_" + t, bl=slot, a_parts=3)
        elif kind == 1:
            wgrad("b_w_out", slot, sv["mid"], dy, "b_out_" + t)
            dmid = _mm("nt", dy, wg["b_w_out"], F32, "dg_b_out_" + t, bl=slot)
            dpre, dvg, dvb, dws, dsbt = _b_mid_bwd(sv["pre"], dmid, *b_params(slot), "b_mid_b_" + t)
            grepl[slot] = (dvg, dvb, dws, dsbt[:, :b_s_bias.shape[1]].T)
            wgrad("b_w_in", slot, sv["h1"], dpre, "b_in_" + t, b_parts=2)
            dh = _mm("nt", dpre, wg["b_w_in"], F32, "dg_b_in_" + t, bl=slot, a_parts=2)
        else:
            wgrad("c_w_out", slot, sv["mid"], dy, "c_out_" + t)
            dmid = _mm("nt", dy, wg["c_w_out"], F32, "dg_c_out_" + t, bl=slot)
            dy2, dlg, dlb = _c_ln_bwd(sv["cy2"], dmid, vec(full["c_ln_g"][slot]), vec(full["c_ln_b"][slot]), "c_ln_b_" + t)
            dpre, dcw, dcb = _c_conv_bwd(sv["pre"], dy2, full["c_conv_w"][slot], "c_conv_b_" + t)
            gsmall["c_conv_w"][slot], gsmall["c_conv_b"][slot] = dcw, dcb
            gsmall["c_ln_g"][slot], gsmall["c_ln_b"][slot] = dlg, dlb
            wgrad("c_w_in", slot, sv["h1"], dpre, "c_in_" + t, b_parts=2)
            dh = _mm("nt", dpre, wg["c_w_in"], F32, "dg_c_in_" + t, bl=slot, a_parts=2)
        dx, dg_pre = _rms_bwd(sv["x0"], vec(full["mix_norm"][i, 0]), dh, dx, F32, "rmsb_mix_pre_" + t)
        gsmall["mix_norm"][i] = jnp.concatenate([dg_pre, dg_post], axis=0)
        if i == 0:
            rs_end(rs_state_f, dx)
        if rs_state is not None:
            rs_token = rs_end(rs_state, dx)
        rs_state = rs_begin(mixer_keys(i) + (rest_keys(i) if i > 0 else []), f"{i}" if i > 0 else "0m", rs_token)
    grad_x = dx.reshape(x.shape)

    small_g = [jnp.concatenate(gsmall[n], axis=0).reshape(-1, d) for n in SMALL_SHARDED]
    n_b = b_v_g.shape[0]
    repl_g = [jnp.concatenate([grepl[sl][k] for sl in range(n_b)], axis=0) for k in range(4)]
    repl_rows = []
    for g_arr in repl_g:
        flat = g_arr.reshape(-1)
        flat = jnp.concatenate([flat, jnp.zeros(((-flat.shape[0]) % d,), F32)])
        repl_rows.append(flat.reshape(-1, d))
    packed_g = jnp.concatenate([pad8(t) for t in small_g + repl_rows], axis=0)
    g_ssem, g_rsem, g_buf, g_token = _gather8_start(_slot_place(packed_g, "place_small_grads"), "gather_small_start")

    delta, new_m, new_v, grads = {}, {}, {}, {}
    last_keys = {n for n, _ in mixer_keys(0)}
    deps = [rs_state[5], g_token]
    for n in BIG_KINDS:
        if n not in last_keys:
            delta[n], new_m[n], new_v[n], grads[n] = _adamw(w[n], gfin[n], given["m_" + n], given["v_" + n], "adamw_" + n,
                                                            dep=deps.pop(0), with_grad=True)
            deps.append(delta[n])
    rs_end(rs_state, deps[-1])
    total = _sum_slots(_gather8_wait(g_ssem, g_rsem, g_buf, deps[-1], "gather_small_wait"), F32, "sum_small_grads")
    for n in sorted(last_keys):
        delta[n], new_m[n], new_v[n], grads[n] = _adamw(w[n], gfin[n], given["m_" + n], given["v_" + n], "adamw_" + n,
                                                        with_grad=True)

    off = 0
    for n, cnt in zip(SMALL_SHARDED, counts):
        blk = lax.dynamic_slice_in_dim(total[off:off + cnt], me * ds, ds, axis=1)
        grads[n] = blk.reshape(w[n].shape)
        off += cnt + (-cnt) % 8
    for n, g_arr in zip(SMALL_REPL, repl_g):
        cnt = -(-g_arr.size // d)
        grads[n] = total[off:off + cnt].reshape(-1)[:g_arr.size].reshape(w[n].shape)
        off += cnt + (-cnt) % 8

    for n in WEIGHTS:
        if n not in delta:
            delta[n], new_m[n], new_v[n] = _adamw(w[n], grads[n], given["m_" + n], given["v_" + n], "adamw_" + n)
    return (loss, grad_x, *[grads[n] for n in WEIGHTS], *[delta[n] for n in WEIGHTS], *[new_m[n] for n in WEIGHTS],
            *[new_v[n] for n in WEIGHTS])
```

```python
import functools

import jax
import jax.numpy as jnp
from jax import lax
from jax.experimental import pallas as pl
from jax.experimental.pallas import tpu as pltpu

F32 = jnp.float32
BF16 = jnp.bfloat16
EPS = 1e-6
XA_HEADS = 4
CHUNK = 128
GMLP_GROUPS = 8
ADAM_LR, ADAM_B1, ADAM_B2, ADAM_EPS, ADAM_WD, ADAM_STEP = 0.001, 0.9, 0.999, 1e-08, 0.01, 10
VMEM_LIMIT_V7X = 48 * 1024 * 1024
HBM = pl.BlockSpec(memory_space=pltpu.HBM)
MESH = pl.DeviceIdType.MESH
N_CHIPS = 4
BIG_KINDS = {"xa_wq": "row", "xa_wkv": "col", "xa_wo": "row", "ffn_w_gu": "col", "ffn_w_down": "row",
             "a_w_in": "col", "a_w_out": "row", "b_w_in": "col", "b_w_out": "row", "c_w_in": "col", "c_w_out": "row"}
SMALL_SHARDED = ["mix_norm", "xa_norm", "ffn_norm", "a_conv_w", "c_conv_w", "c_conv_b", "c_ln_g", "c_ln_b"]
SMALL_REPL = ["b_v_g", "b_v_b", "b_w_s", "b_s_bias"]
WEIGHTS = ["mix_norm", "xa_norm", "xa_wq", "xa_wkv", "xa_wo", "ffn_norm", "ffn_w_gu", "ffn_w_down", "a_w_in", "a_conv_w",
           "a_w_out", "b_w_in", "b_v_g", "b_v_b", "b_w_s", "b_s_bias", "b_w_out", "c_w_in", "c_conv_w", "c_conv_b",
           "c_ln_g", "c_ln_b", "c_w_out"]


def _params(*sem):
    return pltpu.CompilerParams(dimension_semantics=sem, vmem_limit_bytes=VMEM_LIMIT_V7X)


def _tile(n, cands=(1024, 512, 256, 128)):
    for c in cands:
        if n % c == 0:
            return c
    return n


def _div_tile(n, cap):
    best = None
    for t in range(128, min(n, cap) + 1, 128):
        if n % t == 0:
            best = t
    return best or n


MM_OUT_TILE_CAP = 1408
MM_K_TILE_CAP = 2816

_DIMS = {"nn": (((1,), (0,)), ((), ())), "nt": (((1,), (1,)), ((), ())), "tn": (((0,), (0,)), ((), ()))}


def _mm(mode, a, b, out_dtype, name, *, bl=None, a_parts=1, b_parts=1, o_parts=1, dep=None):
    if isinstance(b, list):
        b, bl = b[bl], 0
    bshape = b.shape[1:] if bl is not None else b.shape
    if mode == "nn":
        mo, c = a.shape
        no = bshape[1]
    elif mode == "nt":
        mo, c = (a.shape[1], a.shape[0] * a.shape[2]) if a_parts > 1 else a.shape
        no = bshape[0]
    else:
        c, mo = a.shape
        no = b.shape[0] * b.shape[2] if b_parts > 1 else bshape[1]
    tmo = _div_tile(mo, MM_OUT_TILE_CAP)
    tno = _div_tile(no // max(o_parts, b_parts), MM_OUT_TILE_CAP)
    tc = _div_tile(c // a_parts, MM_K_TILE_CAP)
    nk = c // tc
    nkp = nk // a_parts
    njp = (no // tno) // max(o_parts, b_parts)
    lead = (None,) if bl is not None else ()
    lidx = (bl,) if bl is not None else ()

    if mode == "nn":
        a_spec = pl.BlockSpec((tmo, tc), lambda i, j, k: (i, k))
        b_spec = pl.BlockSpec(lead + (tc, tno), lambda i, j, k: lidx + (k, j))
    elif mode == "nt":
        if a_parts > 1:
            a_spec = pl.BlockSpec((None, tmo, tc), lambda i, j, k: (k // nkp, i, k % nkp))
        else:
            a_spec = pl.BlockSpec((tmo, tc), lambda i, j, k: (i, k))
        b_spec = pl.BlockSpec(lead + (tno, tc), lambda i, j, k: lidx + (j, k))
    else:
        a_spec = pl.BlockSpec((tc, tmo), lambda i, j, k: (k, i))
        if b_parts > 1:
            b_spec = pl.BlockSpec((None, tc, tno), lambda i, j, k: (j // njp, k, j % njp))
        else:
            b_spec = pl.BlockSpec((tc, tno), lambda i, j, k: (k, j))

    in_specs = [a_spec, b_spec]
    args = [a, b]
    if dep is not None:
        in_specs.append(pl.BlockSpec(memory_space=pl.ANY))
        args.append(dep)
    if o_parts > 1:
        out_shape = jax.ShapeDtypeStruct((o_parts, mo, no // o_parts), out_dtype)
        out_spec = pl.BlockSpec((None, tmo, tno), lambda i, j, k: (j // njp, i, j % njp))
    else:
        out_shape = jax.ShapeDtypeStruct((mo, no), out_dtype)
        out_spec = pl.BlockSpec((tmo, tno), lambda i, j, k: (i, j))
    dims = _DIMS[mode]

    def body(a_ref, b_ref, *rest):
        if nk == 1:
            o_ref = rest[-1]
            o_ref[...] = lax.dot_general(a_ref[...], b_ref[...], dims, preferred_element_type=F32).astype(o_ref.dtype)
            return
        o_ref, acc = rest[-2], rest[-1]
        k = pl.program_id(2)
        part = lax.dot_general(a_ref[...], b_ref[...], dims, preferred_element_type=F32)

        @pl.when(k == 0)
        def _():
            acc[...] = part

        @pl.when(jnp.logical_and(k > 0, k < nk - 1))
        def _():
            acc[...] += part

        @pl.when(k == nk - 1)
        def _():
            o_ref[...] = (acc[...] + part).astype(o_ref.dtype)

    return pl.pallas_call(
        body, name=name, out_shape=out_shape, grid=(mo // tmo, no // tno, nk), in_specs=in_specs, out_specs=out_spec,
        scratch_shapes=[pltpu.VMEM((tmo, tno), F32)] if nk > 1 else [],
        compiler_params=_params("parallel", "parallel", "arbitrary"))(*args)


def _ew(fn, ins, out_dtypes, name, dep=None):
    rows, cols = ins[0].shape
    deps = [] if dep is None else [dep]
    tr = rows
    for cand in (512, 256, 128, 64, 32, 16):
        if rows % cand == 0 and cand * cols * 4 <= (1 << 20):
            tr = cand
            break
    spec = pl.BlockSpec((tr, cols), lambda i: (i, 0))
    n_in = len(ins)

    def body(*refs):
        outs = fn(*[r[...] for r in refs[:n_in]])
        for o_ref, o in zip(refs[n_in + len(deps):], outs):
            o_ref[...] = o.astype(o_ref.dtype)

    return pl.pallas_call(
        body, name=name, out_shape=[jax.ShapeDtypeStruct((rows, cols), d) for d in out_dtypes], grid=(rows // tr,),
        in_specs=[spec] * n_in + [pl.BlockSpec(memory_space=pl.ANY)] * len(deps), out_specs=[spec] * len(out_dtypes),
        compiler_params=_params("parallel"))(*ins, *deps)


def _adamw_fn(w, g, m, v):
    m = ADAM_B1 * m + (1.0 - ADAM_B1) * g
    v = ADAM_B2 * v + (1.0 - ADAM_B2) * (g * g)
    m_hat = m / (1.0 - ADAM_B1 ** ADAM_STEP)
    v_hat = v / (1.0 - ADAM_B2 ** ADAM_STEP)
    delta = -ADAM_LR * (m_hat / (jnp.sqrt(v_hat) + ADAM_EPS) + ADAM_WD * w)
    return delta, m, v


def _adamw(w, g, m, v, name, dep=None, with_grad=False):
    shape = w.shape
    cols = shape[-1]
    flat = [t.reshape(-1, cols) for t in (w, g, m, v)]
    fn = (lambda wv, gv, mv, vv: _adamw_fn(wv, gv, mv, vv) + (gv,)) if with_grad else _adamw_fn
    outs = _ew(fn, flat, [F32] * (4 if with_grad else 3), name, dep=dep)
    return [o.reshape(shape) for o in outs]


def _row_tile(s):
    return _tile(s, (256, 128, 64, 32, 16, 8))


def _rms_fwd(x, g, name, dep=None):
    s, d = x.shape
    r = _row_tile(s)
    deps = [] if dep is None else [dep]

    def body(x_ref, g_ref, *rest):
        o_ref = rest[-1]
        xv = x_ref[...]
        o_ref[...] = (xv * lax.rsqrt(jnp.mean(xv * xv, axis=-1, keepdims=True) + EPS) * g_ref[...]).astype(BF16)

    return pl.pallas_call(
        body, name=name, out_shape=jax.ShapeDtypeStruct((s, d), BF16), grid=(s // r,),
        in_specs=[pl.BlockSpec((r, d), lambda i: (i, 0)), pl.BlockSpec((1, d), lambda i: (0, 0))] + [ANY] * len(deps),
        out_specs=pl.BlockSpec((r, d), lambda i: (i, 0)), compiler_params=_params("parallel"))(x, g, *deps)


def _res_rms_fwd(x, y, g, g_next, name):
    s, d = x.shape
    r = _row_tile(s)
    has_next = g_next is not None

    def body(x_ref, y_ref, g_ref, *rest):
        yv = y_ref[...]
        xn = x_ref[...] + yv * lax.rsqrt(jnp.mean(yv * yv, axis=-1, keepdims=True) + EPS) * g_ref[...]
        rest[-2 if has_next else -1][...] = xn
        if has_next:
            rest[-1][...] = (xn * lax.rsqrt(jnp.mean(xn * xn, axis=-1, keepdims=True) + EPS) * rest[0][...]).astype(BF16)

    row = pl.BlockSpec((r, d), lambda i: (i, 0))
    vec = pl.BlockSpec((1, d), lambda i: (0, 0))
    outs = pl.pallas_call(
        body, name=name,
        out_shape=[jax.ShapeDtypeStruct((s, d), F32)] + ([jax.ShapeDtypeStruct((s, d), BF16)] if has_next else []),
        grid=(s // r,), in_specs=[row, row, vec] + ([vec] if has_next else []), out_specs=[row] * (2 if has_next else 1),
        compiler_params=_params("parallel"))(*([x, y, g] + ([g_next] if has_next else [])))
    return outs[0], (outs[1] if has_next else None)


def _rms_bwd(x, g, dy, resid, out_dtype, name, dep=None):
    s, d = x.shape
    r = _row_tile(s)
    has_res = resid is not None
    deps = [] if dep is None else [dep]

    def body(*refs):
        x_ref, g_ref, dy_ref = refs[:3]
        dx_ref, dg_ref = refs[-2:]
        i = pl.program_id(0)
        xv = x_ref[...]
        dyv = dy_ref[...].astype(F32)
        rstd = lax.rsqrt(jnp.mean(xv * xv, axis=-1, keepdims=True) + EPS)
        n = xv * rstd
        dn = dyv * g_ref[...]
        dx = rstd * (dn - n * jnp.mean(dn * n, axis=-1, keepdims=True))
        if has_res:
            dx = dx + refs[3][...]
        dx_ref[...] = dx.astype(dx_ref.dtype)
        part = jnp.sum(dyv * n, axis=0, keepdims=True)

        @pl.when(i == 0)
        def _():
            dg_ref[...] = part

        @pl.when(i > 0)
        def _():
            dg_ref[...] += part

    row = pl.BlockSpec((r, d), lambda i: (i, 0))
    vec = pl.BlockSpec((1, d), lambda i: (0, 0))
    ins = [x, g, dy] + ([resid] if has_res else []) + deps
    return pl.pallas_call(
        body, name=name, out_shape=[jax.ShapeDtypeStruct((s, d), out_dtype), jax.ShapeDtypeStruct((1, d), F32)],
        grid=(s // r,), in_specs=[row, vec, row] + ([row] if has_res else []) + [ANY] * len(deps), out_specs=[row, vec],
        compiler_params=_params("arbitrary"))(*ins)


def _loss(y, t, name):
    s, d = y.shape
    r = _row_tile(s)

    def body(y_ref, t_ref, l_ref, dy_ref):
        i = pl.program_id(0)
        e = y_ref[...] - t_ref[...]
        dy_ref[...] = e * (1.0 / d)
        part = jnp.full((8, 128), 0.5 * jnp.sum(jnp.mean(e * e, axis=-1, keepdims=True)), F32)

        @pl.when(i == 0)
        def _():
            l_ref[...] = part

        @pl.when(i > 0)
        def _():
            l_ref[...] += part

    row = pl.BlockSpec((r, d), lambda i: (i, 0))
    return pl.pallas_call(
        body, name=name, out_shape=[jax.ShapeDtypeStruct((8, 128), F32), jax.ShapeDtypeStruct((s, d), F32)],
        grid=(s // r,), in_specs=[row, row], out_specs=[pl.BlockSpec((8, 128), lambda i: (0, 0)), row],
        compiler_params=_params("arbitrary"))(y, t)


def _rows(xv, a, m, cache):
    r = a % 8
    q = a - r
    if r == 0:
        return xv[q:q + m]
    if r not in cache:
        cache[r] = pltpu.roll(xv, xv.shape[0] - r, 0)
    return cache[r][q:q + m]


def _conv_taps(xv, w, k_w, halo, m, flip):
    cache = {}
    acc = None
    for k in range(k_w):
        a = (k_w - 1 - k) if flip else (halo + k - (k_w - 1))
        term = w[k:k + 1, :] * _rows(xv, a, m, cache)
        acc = term if acc is None else acc + term
    return acc


def _conv_wgrad(dw_ref, dyv, xv, k_w, halo, m):
    cache = {}
    for k in range(k_w):
        xs = _rows(xv, halo + k - (k_w - 1), m, cache)
        dw_ref[pl.ds(k, 1), :] += jnp.sum(dyv * xs, axis=0, keepdims=True)


def _conv_tiles(s, dp, halo):
    r = _tile(s, (256, 128))
    cw = _tile(dp, (256, 128))
    return r, cw, r // halo


A_HALO = 8


def _a_mid_fwd(bcz3, w, name):
    _, s, d = bcz3.shape
    r, cw, rh = _conv_tiles(s, d, A_HALO)
    k_w = w.shape[0]

    def body(m_ref, h_ref, w_ref, o_ref):
        i = pl.program_id(0)
        cz = m_ref[1].astype(F32) * m_ref[2].astype(F32)
        hcz = h_ref[1].astype(F32) * h_ref[2].astype(F32)
        hcz = jnp.where(i == 0, 0.0, hcz)
        xv = jnp.concatenate([hcz, cz], axis=0)
        y = _conv_taps(xv, w_ref[...], k_w, A_HALO, r, False)
        o_ref[...] = (m_ref[0].astype(F32) * y).astype(BF16)

    return pl.pallas_call(
        body, name=name, out_shape=jax.ShapeDtypeStruct((s, d), BF16), grid=(s // r, d // cw),
        in_specs=[pl.BlockSpec((3, r, cw), lambda i, j: (0, i, j)),
                  pl.BlockSpec((3, A_HALO, cw), lambda i, j: (0, jnp.maximum(i * rh - 1, 0), j)),
                  pl.BlockSpec((k_w, cw), lambda i, j: (0, j))],
        out_specs=pl.BlockSpec((r, cw), lambda i, j: (i, j)), compiler_params=_params("parallel", "parallel"))(bcz3, bcz3, w)


def _a_mid_bwd(bcz3, dgated, w, name):
    _, s, d = bcz3.shape
    r, cw, rh = _conv_tiles(s, d, A_HALO)
    k_w = w.shape[0]
    ni = s // r
    last_h = s // A_HALO - 1

    def body(m_ref, hp_ref, hn_ref, dg_ref, dgn_ref, w_ref, o_ref, dw_ref):
        i = pl.program_id(1)
        wv = w_ref[...]
        b = m_ref[0].astype(F32)
        c = m_ref[1].astype(F32)
        z = m_ref[2].astype(F32)
        hcz = jnp.where(i == 0, 0.0, hp_ref[1].astype(F32) * hp_ref[2].astype(F32))
        xv = jnp.concatenate([hcz, c * z], axis=0)
        y = _conv_taps(xv, wv, k_w, A_HALO, r, False)
        dg = dg_ref[...].astype(F32)
        dy = dg * b
        dyn = jnp.where(i == ni - 1, 0.0, dgn_ref[...].astype(F32) * hn_ref[0].astype(F32))
        dcz = _conv_taps(jnp.concatenate([dy, dyn], axis=0), wv, k_w, A_HALO, r, True)
        o_ref[0] = (dg * y).astype(BF16)
        o_ref[1] = (dcz * z).astype(BF16)
        o_ref[2] = (dcz * c).astype(BF16)

        @pl.when(i == 0)
        def _():
            dw_ref[...] = jnp.zeros_like(dw_ref)

        _conv_wgrad(dw_ref, dy, xv, k_w, A_HALO, r)

    return pl.pallas_call(
        body, name=name, out_shape=[jax.ShapeDtypeStruct((3, s, d), BF16), jax.ShapeDtypeStruct((k_w, d), F32)],
        grid=(d // cw, ni),
        in_specs=[pl.BlockSpec((3, r, cw), lambda j, i: (0, i, j)),
                  pl.BlockSpec((3, A_HALO, cw), lambda j, i: (0, jnp.maximum(i * rh - 1, 0), j)),
                  pl.BlockSpec((3, A_HALO, cw), lambda j, i: (0, jnp.minimum((i + 1) * rh, last_h), j)),
                  pl.BlockSpec((r, cw), lambda j, i: (i, j)),
                  pl.BlockSpec((A_HALO, cw), lambda j, i: (jnp.minimum((i + 1) * rh, last_h), j)),
                  pl.BlockSpec((k_w, cw), lambda j, i: (0, j))],
        out_specs=[pl.BlockSpec((3, r, cw), lambda j, i: (0, i, j)), pl.BlockSpec((k_w, cw), lambda j, i: (0, j))],
        compiler_params=_params("parallel", "arbitrary"))(bcz3, bcz3, bcz3, dgated, dgated, w)


C_HALO = 32


def _c_conv_fwd(ag3, w, bias, name):
    _, s, d = ag3.shape
    r, cw, rh = _conv_tiles(s, d, C_HALO)
    k_w = w.shape[0]

    def body(m_ref, h_ref, w_ref, b_ref, o_ref):
        i = pl.program_id(0)
        y1 = m_ref[0].astype(F32) * jax.nn.sigmoid(m_ref[1].astype(F32))
        h1 = jnp.where(i == 0, 0.0, h_ref[0].astype(F32) * jax.nn.sigmoid(h_ref[1].astype(F32)))
        xv = jnp.concatenate([h1, y1], axis=0)
        o_ref[...] = _conv_taps(xv, w_ref[...], k_w, C_HALO, r, False) + b_ref[...]

    return pl.pallas_call(
        body, name=name, out_shape=jax.ShapeDtypeStruct((s, d), F32), grid=(s // r, d // cw),
        in_specs=[pl.BlockSpec((2, r, cw), lambda i, j: (0, i, j)),
                  pl.BlockSpec((2, C_HALO, cw), lambda i, j: (0, jnp.maximum(i * rh - 1, 0), j)),
                  pl.BlockSpec((k_w, cw), lambda i, j: (0, j)), pl.BlockSpec((1, cw), lambda i, j: (0, j))],
        out_specs=pl.BlockSpec((r, cw), lambda i, j: (i, j)),
        compiler_params=_params("parallel", "parallel"))(ag3, ag3, w, bias)


def _c_conv_bwd(ag3, dy2, w, name):
    _, s, d = ag3.shape
    r, cw, rh = _conv_tiles(s, d, C_HALO)
    k_w = w.shape[0]
    ni = s // r
    last_h = s // C_HALO - 1

    def body(m_ref, hp_ref, dy_ref, dyn_ref, w_ref, o_ref, dw_ref, db_ref):
        i = pl.program_id(1)
        wv = w_ref[...]
        a = m_ref[0].astype(F32)
        sg = jax.nn.sigmoid(m_ref[1].astype(F32))
        h1 = jnp.where(i == 0, 0.0, hp_ref[0].astype(F32) * jax.nn.sigmoid(hp_ref[1].astype(F32)))
        xv = jnp.concatenate([h1, a * sg], axis=0)
        dy = dy_ref[...]
        dyn = jnp.where(i == ni - 1, 0.0, dyn_ref[...])
        dy1 = _conv_taps(jnp.concatenate([dy, dyn], axis=0), wv, k_w, C_HALO, r, True)
        o_ref[0] = (dy1 * sg).astype(BF16)
        o_ref[1] = (dy1 * a * sg * (1.0 - sg)).astype(BF16)

        @pl.when(i == 0)
        def _():
            dw_ref[...] = jnp.zeros_like(dw_ref)
            db_ref[...] = jnp.zeros_like(db_ref)

        db_ref[...] += jnp.sum(dy, axis=0, keepdims=True)
        _conv_wgrad(dw_ref, dy, xv, k_w, C_HALO, r)

    return pl.pallas_call(
        body, name=name,
        out_shape=[jax.ShapeDtypeStruct((2, s, d), BF16), jax.ShapeDtypeStruct((k_w, d), F32),
                   jax.ShapeDtypeStruct((1, d), F32)],
        grid=(d // cw, ni),
        in_specs=[pl.BlockSpec((2, r, cw), lambda j, i: (0, i, j)),
                  pl.BlockSpec((2, C_HALO, cw), lambda j, i: (0, jnp.maximum(i * rh - 1, 0), j)),
                  pl.BlockSpec((r, cw), lambda j, i: (i, j)),
                  pl.BlockSpec((C_HALO, cw), lambda j, i: (jnp.minimum((i + 1) * rh, last_h), j)),
                  pl.BlockSpec((k_w, cw), lambda j, i: (0, j))],
        out_specs=[pl.BlockSpec((2, r, cw), lambda j, i: (0, i, j)), pl.BlockSpec((k_w, cw), lambda j, i: (0, j)),
                   pl.BlockSpec((1, cw), lambda j, i: (0, j))],
        compiler_params=_params("parallel", "arbitrary"))(ag3, ag3, dy2, dy2, w)


def _ln_stats(v):
    mu = jnp.mean(v, axis=-1, keepdims=True)
    vc = v - mu
    rstd = lax.rsqrt(jnp.mean(vc * vc, axis=-1, keepdims=True) + EPS)
    return vc * rstd, rstd


def _ln_bwd(dn, g, xh, rstd):
    dxh = dn * g
    return rstd * (dxh - jnp.mean(dxh, axis=-1, keepdims=True) - xh * jnp.mean(dxh * xh, axis=-1, keepdims=True))


def _c_ln_fwd(y2, g, b, name):
    s, d = y2.shape
    r = _row_tile(s)

    def body(y_ref, g_ref, b_ref, o_ref):
        xh, _ = _ln_stats(y_ref[...])
        y3 = xh * g_ref[...] + b_ref[...]
        o_ref[...] = (y3 * jax.nn.sigmoid(y3)).astype(BF16)

    row = pl.BlockSpec((r, d), lambda i: (i, 0))
    vec = pl.BlockSpec((1, d), lambda i: (0, 0))
    return pl.pallas_call(
        body, name=name, out_shape=jax.ShapeDtypeStruct((s, d), BF16), grid=(s // r,), in_specs=[row, vec, vec],
        out_specs=row, compiler_params=_params("parallel"))(y2, g, b)


def _c_ln_bwd(y2, dout, g, b, name):
    s, d = y2.shape
    r = _row_tile(s)

    def body(y_ref, do_ref, g_ref, b_ref, dy_ref, dg_ref, db_ref):
        i = pl.program_id(0)
        xh, rstd = _ln_stats(y_ref[...])
        gv = g_ref[...]
        y3 = xh * gv + b_ref[...]
        sg = jax.nn.sigmoid(y3)
        dy3 = do_ref[...].astype(F32) * (sg + y3 * sg * (1.0 - sg))
        dy_ref[...] = _ln_bwd(dy3, gv, xh, rstd)

        @pl.when(i == 0)
        def _():
            dg_ref[...] = jnp.zeros_like(dg_ref)
            db_ref[...] = jnp.zeros_like(db_ref)

        dg_ref[...] += jnp.sum(dy3 * xh, axis=0, keepdims=True)
        db_ref[...] += jnp.sum(dy3, axis=0, keepdims=True)

    row = pl.BlockSpec((r, d), lambda i: (i, 0))
    vec = pl.BlockSpec((1, d), lambda i: (0, 0))
    return pl.pallas_call(
        body, name=name,
        out_shape=[jax.ShapeDtypeStruct((s, d), F32), jax.ShapeDtypeStruct((1, d), F32), jax.ShapeDtypeStruct((1, d), F32)],
        grid=(s // r,), in_specs=[row, row, vec, vec], out_specs=[row, vec, vec],
        compiler_params=_params("arbitrary"))(y2, dout, g, b)


_GELU_C = 0.7978845608028654
_GELU_A = 0.044715


def _gelu(x):
    return 0.5 * x * (1.0 + jnp.tanh(_GELU_C * (x + _GELU_A * x * x * x)))


def _gelu_grad(x):
    t = jnp.tanh(_GELU_C * (x + _GELU_A * x * x * x))
    return 0.5 * (1.0 + t) + 0.5 * x * (1.0 - t * t) * _GELU_C * (1.0 + 3.0 * _GELU_A * x * x)


def _b_mid_fwd(uv3, vg, vb, ws_m, sbt, name):
    _, s, h = uv3.shape
    g_n, t, _ = ws_m.shape
    gd = h // g_n

    def body(uv_ref, vg_ref, vb_ref, ws_ref, sb_ref, o_ref):
        u = _gelu(uv_ref[0].astype(F32))
        xh, _ = _ln_stats(_gelu(uv_ref[1].astype(F32)))
        vn = (xh * vg_ref[...] + vb_ref[...]).astype(BF16)
        for g in range(g_n):
            sl = slice(g * gd, (g + 1) * gd)
            sv = jnp.dot(ws_ref[g], vn[:, sl], preferred_element_type=F32) + sb_ref[:, g:g + 1]
            o_ref[:, sl] = (u[:, sl] * sv).astype(BF16)

    vec = pl.BlockSpec((1, h), lambda i: (0, 0))
    return pl.pallas_call(
        body, name=name, out_shape=jax.ShapeDtypeStruct((s, h), BF16), grid=(s // t,),
        in_specs=[pl.BlockSpec((2, t, h), lambda i: (0, i, 0)), vec, vec,
                  pl.BlockSpec((g_n, t, t), lambda i: (0, 0, 0)), pl.BlockSpec((t, 128), lambda i: (0, 0))],
        out_specs=pl.BlockSpec((t, h), lambda i: (i, 0)), compiler_params=_params("parallel"))(uv3, vg, vb, ws_m, sbt)


def _b_mid_bwd(uv3, dgated, vg, vb, ws_m, sbt, name):
    _, s, h = uv3.shape
    g_n, t, _ = ws_m.shape
    gd = h // g_n

    def body(uv_ref, dg_ref, vg_ref, vb_ref, ws_ref, sb_ref, o_ref, dvg_ref, dvb_ref, dws_ref, dsb_ref, dvn_ref):
        i = pl.program_id(0)

        @pl.when(i == 0)
        def _():
            dvg_ref[...] = jnp.zeros_like(dvg_ref)
            dvb_ref[...] = jnp.zeros_like(dvb_ref)
            dws_ref[...] = jnp.zeros_like(dws_ref)
            dsb_ref[...] = jnp.zeros_like(dsb_ref)

        upre = uv_ref[0].astype(F32)
        vpre = uv_ref[1].astype(F32)
        u = _gelu(upre)
        xh, rstd = _ln_stats(_gelu(vpre))
        gv = vg_ref[...]
        vn = (xh * gv + vb_ref[...]).astype(BF16)
        causal = lax.broadcasted_iota(jnp.int32, (t, t), 0) >= lax.broadcasted_iota(jnp.int32, (t, t), 1)
        lane = lax.broadcasted_iota(jnp.int32, (t, 128), 1)
        for g in range(g_n):
            sl = slice(g * gd, (g + 1) * gd)
            wsg = ws_ref[g]
            sv = jnp.dot(wsg, vn[:, sl], preferred_element_type=F32) + sb_ref[:, g:g + 1]
            dg = dg_ref[:, sl].astype(F32)
            o_ref[0, :, sl] = (dg * sv * _gelu_grad(upre[:, sl])).astype(BF16)
            dsv = dg * u[:, sl]
            dsvb = dsv.astype(BF16)
            dsb_ref[...] += jnp.where(lane == g, jnp.sum(dsv, axis=1, keepdims=True), 0.0)
            dws = lax.dot_general(dsvb, vn[:, sl], _DIMS["nt"], preferred_element_type=F32)
            dws_ref[g] += jnp.where(causal, dws, 0.0)
            dvn_ref[:, sl] = lax.dot_general(wsg, dsvb, _DIMS["tn"], preferred_element_type=F32)
        dvn = dvn_ref[...]
        dvg_ref[...] += jnp.sum(dvn * xh, axis=0, keepdims=True)
        dvb_ref[...] += jnp.sum(dvn, axis=0, keepdims=True)
        o_ref[1] = (_ln_bwd(dvn, gv, xh, rstd) * _gelu_grad(vpre)).astype(BF16)

    vec = pl.BlockSpec((1, h), lambda i: (0, 0))
    return pl.pallas_call(
        body, name=name,
        out_shape=[jax.ShapeDtypeStruct((2, s, h), BF16), jax.ShapeDtypeStruct((1, h), F32), jax.ShapeDtypeStruct((1, h), F32),
                   jax.ShapeDtypeStruct((g_n, t, t), F32), jax.ShapeDtypeStruct((t, 128), F32)],
        grid=(s // t,),
        in_specs=[pl.BlockSpec((2, t, h), lambda i: (0, i, 0)), pl.BlockSpec((t, h), lambda i: (i, 0)), vec, vec,
                  pl.BlockSpec((g_n, t, t), lambda i: (0, 0, 0)), pl.BlockSpec((t, 128), lambda i: (0, 0))],
        out_specs=[pl.BlockSpec((2, t, h), lambda i: (0, i, 0)), vec, vec,
                   pl.BlockSpec((g_n, t, t), lambda i: (0, 0, 0)), pl.BlockSpec((t, 128), lambda i: (0, 0))],
        scratch_shapes=[pltpu.VMEM((t, h), F32)],
        compiler_params=_params("arbitrary"))(uv3, dgated, vg, vb, ws_m, sbt)


def _softmax_rows(sc):
    e = jnp.exp(sc - jnp.max(sc, axis=-1, keepdims=True))
    return e / jnp.sum(e, axis=-1, keepdims=True)


def _attn_fwd(q, kv3, name):
    s, d = q.shape
    m = kv3.shape[1]
    dh = d // XA_HEADS
    scale = dh ** -0.5
    r = _row_tile(s)

    def body(q_ref, kv_ref, o_ref):
        for hd in range(XA_HEADS):
            sl = slice(hd * dh, (hd + 1) * dh)
            sc = lax.dot_general(q_ref[:, sl], kv_ref[0, :, sl], _DIMS["nt"], preferred_element_type=F32) * scale
            p = _softmax_rows(sc).astype(BF16)
            o_ref[:, sl] = jnp.dot(p, kv_ref[1, :, sl], preferred_element_type=F32).astype(BF16)

    return pl.pallas_call(
        body, name=name, out_shape=jax.ShapeDtypeStruct((s, d), BF16), grid=(s // r,),
        in_specs=[pl.BlockSpec((r, d), lambda i: (i, 0)), pl.BlockSpec((2, m, d), lambda i: (0, 0, 0))],
        out_specs=pl.BlockSpec((r, d), lambda i: (i, 0)), compiler_params=_params("parallel"))(q, kv3)


def _attn_bwd(q, kv3, do, name):
    s, d = q.shape
    m = kv3.shape[1]
    dh = d // XA_HEADS
    scale = dh ** -0.5
    r = _row_tile(s)

    def body(q_ref, kv_ref, do_ref, dq_ref, dkv_ref):
        i = pl.program_id(0)

        @pl.when(i == 0)
        def _():
            dkv_ref[...] = jnp.zeros_like(dkv_ref)

        for hd in range(XA_HEADS):
            sl = slice(hd * dh, (hd + 1) * dh)
            qh = q_ref[:, sl]
            kh = kv_ref[0, :, sl]
            doh = do_ref[:, sl]
            sc = lax.dot_general(qh, kh, _DIMS["nt"], preferred_element_type=F32) * scale
            p = _softmax_rows(sc)
            pb = p.astype(BF16)
            dkv_ref[1, :, sl] += lax.dot_general(pb, doh, _DIMS["tn"], preferred_element_type=F32)
            dp = lax.dot_general(doh, kv_ref[1, :, sl], _DIMS["nt"], preferred_element_type=F32)
            ds = (p * (dp - jnp.sum(dp * p, axis=-1, keepdims=True)) * scale).astype(BF16)
            dq_ref[:, sl] = jnp.dot(ds, kh, preferred_element_type=F32).astype(BF16)
            dkv_ref[0, :, sl] += lax.dot_general(ds, qh, _DIMS["tn"], preferred_element_type=F32)

    row = pl.BlockSpec((r, d), lambda i: (i, 0))
    kvs = pl.BlockSpec((2, m, d), lambda i: (0, 0, 0))
    return pl.pallas_call(
        body, name=name, out_shape=[jax.ShapeDtypeStruct((s, d), BF16), jax.ShapeDtypeStruct((2, m, d), F32)],
        grid=(s // r,), in_specs=[row, kvs, row], out_specs=[row, kvs], compiler_params=_params("arbitrary"))(q, kv3, do)


FFN_COL_TILE = 512


def _ffn_gu_fwd(h, w_gu, name):
    s, d = h.shape
    f = w_gu.shape[2] // 2
    tm = _div_tile(s, MM_OUT_TILE_CAP)
    tn = _div_tile(f, FFN_COL_TILE)
    nj = f // tn

    def body(a_ref, bg_ref, bu_ref, gu_ref, act_ref):
        a = a_ref[...]
        gate = jnp.dot(a, bg_ref[...], preferred_element_type=F32)
        up = jnp.dot(a, bu_ref[...], preferred_element_type=F32)
        gu_ref[0] = gate.astype(BF16)
        gu_ref[1] = up.astype(BF16)
        act_ref[...] = (gate * jax.nn.sigmoid(gate) * up).astype(BF16)

    return pl.pallas_call(
        body, name=name, out_shape=[jax.ShapeDtypeStruct((2, s, f), BF16), jax.ShapeDtypeStruct((s, f), BF16)],
        grid=(s // tm, nj),
        in_specs=[pl.BlockSpec((tm, d), lambda i, j: (i, 0)), pl.BlockSpec((None, d, tn), lambda i, j: (0, 0, j)),
                  pl.BlockSpec((None, d, tn), lambda i, j: (0, 0, j + nj))],
        out_specs=[pl.BlockSpec((2, tm, tn), lambda i, j: (0, i, j)), pl.BlockSpec((tm, tn), lambda i, j: (i, j))],
        compiler_params=_params("parallel", "parallel"))(h, w_gu, w_gu)


def _ffn_down_bwd(dy, w_down, gu3, name):
    s, d = dy.shape
    f = w_down.shape[1]
    tm = _div_tile(s, MM_OUT_TILE_CAP)
    tn = _div_tile(f, FFN_COL_TILE)

    def body(dy_ref, w_ref, gu_ref, o_ref):
        da = lax.dot_general(dy_ref[...], w_ref[...], _DIMS["nt"], preferred_element_type=F32)
        gate = gu_ref[0].astype(F32)
        up = gu_ref[1].astype(F32)
        sg = jax.nn.sigmoid(gate)
        o_ref[0] = (da * up * (sg + gate * sg * (1.0 - sg))).astype(BF16)
        o_ref[1] = (da * gate * sg).astype(BF16)

    return pl.pallas_call(
        body, name=name, out_shape=jax.ShapeDtypeStruct((2, s, f), BF16), grid=(s // tm, f // tn),
        in_specs=[pl.BlockSpec((tm, d), lambda i, j: (i, 0)), pl.BlockSpec((None, tn, d), lambda i, j: (0, j, 0)),
                  pl.BlockSpec((2, tm, tn), lambda i, j: (0, i, j))],
        out_specs=pl.BlockSpec((2, tm, tn), lambda i, j: (0, i, j)),
        compiler_params=_params("parallel", "parallel"))(dy, w_down, gu3)


def _ids():
    x, y, c = lax.axis_index("x"), lax.axis_index("y"), lax.axis_index("c")
    return x, y, c, 2 * x + y


def _chip_peers(x, y):
    return [(d - 1, 2 * (x ^ (d >> 1)) + (y ^ (d & 1)), x ^ (d >> 1), y ^ (d & 1)) for d in (1, 2, 3)]


def _remote(src, dst, ssem, rsem, dev):
    return pltpu.make_async_remote_copy(src_ref=src, dst_ref=dst, send_sem=ssem, recv_sem=rsem, device_id=dev,
                                        device_id_type=MESH)


def _gview(ref, kind, j, cc):
    _, k, n = ref.shape
    if kind == "row":
        return ref.at[:, pl.ds(j * (k // N_CHIPS) + cc * (k // (2 * N_CHIPS)), k // (2 * N_CHIPS)), :]
    return ref.at[:, pl.ds(cc * (k // 2), k // 2), pl.ds(j * (n // N_CHIPS), n // N_CHIPS)]


def _sview(ref, cc):
    r = ref.shape[1]
    return ref.at[:, pl.ds(cc * (r // 2), r // 2), :]


def _comm_call(body, name, ins, out_shapes, n_sems, aliases=None):
    return pl.pallas_call(
        body, name=name, out_shape=out_shapes, in_specs=[HBM] * len(ins), out_specs=[HBM] * len(out_shapes),
        scratch_shapes=[pltpu.SemaphoreType.DMA((n,)) for n in n_sems], input_output_aliases=aliases or {},
        compiler_params=pltpu.CompilerParams(has_side_effects=True))(*ins)


def _mesh_scalars():
    x, y, c = lax.axis_index("x"), lax.axis_index("y"), lax.axis_index("c")
    return jnp.stack([2 * x + y, c]).astype(jnp.int32)


def _slab_rows(rows, cols, itemsize=4):
    best = None
    for cand in range(16, rows + 1, 16):
        if rows % cand == 0 and cand * cols * itemsize <= (2 << 20):
            best = cand
    return best or rows


def _ag_place(shard, layer, kind, name, dep=None):
    deps = [] if dep is None else [dep]
    _, r, n = shard.shape
    full = (1, r * N_CHIPS, n) if kind == "row" else (1, r, n * N_CHIPS)
    tr = _slab_rows(r, n)
    nt = r // tr
    if kind == "row":
        out_spec = pl.BlockSpec((None, tr, n), lambda t, s: (0, s[0] * nt + t, 0))
    else:
        out_spec = pl.BlockSpec((None, tr, n), lambda t, s: (0, t, s[0]))

    def body(s_ref, i_ref, *rest):
        rest[-1][...] = i_ref[...].astype(BF16)

    return pl.pallas_call(
        body, name=name, out_shape=jax.ShapeDtypeStruct(full, BF16),
        grid_spec=pltpu.PrefetchScalarGridSpec(
            num_scalar_prefetch=1, grid=(nt,),
            in_specs=[pl.BlockSpec((None, tr, n), lambda t, s: (layer, t, 0))] + [pl.BlockSpec(memory_space=pl.ANY)] * len(deps),
            out_specs=out_spec),
        compiler_params=_params("parallel"))(_mesh_scalars(), shard, *deps)


SEM = pl.BlockSpec(memory_space=pltpu.SEMAPHORE)
ANY = pl.BlockSpec(memory_space=pl.ANY)
DATAFLOW = pltpu.SideEffectType.DATAFLOW_SIDE_EFFECTING


def _in_hbm(arrs):
    return [pltpu.with_memory_space_constraint(a, pltpu.HBM) for a in arrs]


def _ag_start(bufs, kinds, name, after=None):
    n = len(bufs)
    afters = [] if after is None else [after]
    n_in = n + len(afters)

    def body(*refs):
        ssem, rsem, token = refs[n_in], refs[n_in + 1], refs[-1]
        x, y, c, me = _ids()
        for t in range(n):
            mine = _gview(refs[t], kinds[t], me, c)
            for d, _, px, py in _chip_peers(x, y):
                _remote(mine, mine, ssem.at[3 * t + d], rsem.at[3 * t + d], (px, py, c)).start()
        token[...] = jnp.zeros_like(token)

    outs = pl.pallas_call(
        body, name=name,
        out_shape=(pltpu.SemaphoreType.DMA((3 * n,)), pltpu.SemaphoreType.DMA((3 * n,)),
                   *[pltpu.HBM(b.shape, b.dtype) for b in bufs], jax.ShapeDtypeStruct((8, 128), F32)),
        in_specs=[HBM] * n + [ANY] * len(afters), out_specs=(SEM, SEM, *[HBM] * n, pl.BlockSpec(memory_space=pltpu.VMEM)),
        input_output_aliases={t: 2 + t for t in range(n)},
        compiler_params=pltpu.CompilerParams(has_side_effects=DATAFLOW))(*_in_hbm(bufs), *afters)
    return outs[0], outs[1], list(outs[2:2 + n]), outs[-1]


def _ag_wait(ssem, rsem, bufs, kinds, after, name):
    n = len(bufs)

    def body(*refs):
        ssem_ref, rsem_ref = refs[n], refs[n + 1]
        x, y, c, me = _ids()
        for t in range(n):
            mine = _gview(refs[t], kinds[t], me, c)
            for d, pj, px, py in _chip_peers(x, y):
                theirs = _gview(refs[t], kinds[t], pj, c)
                _remote(mine, mine, ssem_ref.at[3 * t + d], rsem_ref.at[3 * t + d], (px, py, c)).wait_send()
                _remote(theirs, theirs, ssem_ref.at[3 * t + d], rsem_ref.at[3 * t + d], (px, py, c)).wait_recv()

    outs = pl.pallas_call(
        body, name=name, out_shape=[pltpu.HBM(b.shape, b.dtype) for b in bufs],
        in_specs=[HBM] * n + [SEM, SEM, ANY], out_specs=[HBM] * n, input_output_aliases={t: t for t in range(n)},
        compiler_params=pltpu.CompilerParams(has_side_effects=DATAFLOW))(*bufs, ssem, rsem, after)
    return list(outs)


def _ag_forward(bufs, kinds, name):
    n = len(bufs)

    def body(*refs):
        outs = refs[n:2 * n]
        ssem, rsem = refs[2 * n], refs[2 * n + 1]
        x, y, c, _ = _ids()
        sib = (x, y, 1 - c)
        sends = []
        for t in range(n):
            for d, pj, _, _ in _chip_peers(x, y):
                piece = _gview(outs[t], kinds[t], pj, c)
                sends.append(_remote(piece, piece, ssem.at[3 * t + d], rsem.at[3 * t + d], sib))
        for cp in sends:
            cp.start()
        for t in range(n):
            for d, pj, _, _ in _chip_peers(x, y):
                piece = _gview(outs[t], kinds[t], pj, 1 - c)
                _remote(piece, piece, ssem.at[3 * t + d], rsem.at[3 * t + d], sib).wait_recv()
        for cp in sends:
            cp.wait_send()

    return _comm_call(body, name, bufs, [jax.ShapeDtypeStruct(b.shape, b.dtype) for b in bufs], (3 * n, 3 * n),
                      {t: t for t in range(n)})


def _rs1(g_fulls, kinds, name, dep=None):
    n = len(g_fulls)
    outs = []
    for g, kind in zip(g_fulls, kinds):
        l, k, nn = g.shape
        piece = (l, k // (2 * N_CHIPS), nn) if kind == "row" else (l, k // 2, nn // N_CHIPS)
        outs.append(jax.ShapeDtypeStruct((N_CHIPS,) + piece, g.dtype))

    n_in = n + (dep is not None)

    def body(*refs):
        ssem, rsem = refs[n_in + n], refs[n_in + n + 1]
        x, y, c, _ = _ids()
        sends = [_remote(_gview(refs[t], kinds[t], j, 1 - c), refs[n_in + t].at[j], ssem.at[4 * t + j], rsem.at[4 * t + j],
                         (x, y, 1 - c)) for t in range(n) for j in range(N_CHIPS)]
        for cp in sends:
            cp.start()
        for cp in sends:
            cp.wait()

    return _comm_call(body, name, g_fulls + ([] if dep is None else [dep]), outs, (4 * n, 4 * n))


def _rs1_pieces(g_fulls, kinds):
    shapes = []
    for g, kind in zip(g_fulls, kinds):
        l, k, nn = g.shape
        shapes.append((N_CHIPS,) + ((l, k // (2 * N_CHIPS), nn) if kind == "row" else (l, k // 2, nn // N_CHIPS)))
    return shapes


def _rs1_start(g_fulls, kinds, name, after=None):
    n = len(g_fulls)
    lands = [lax.empty(s, g.dtype) for s, g in zip(_rs1_pieces(g_fulls, kinds), g_fulls)]
    afters = [] if after is None else [after]
    n_in = 2 * n + len(afters)

    def body(*refs):
        ssem, rsem, token = refs[n_in], refs[n_in + 1], refs[-1]
        x, y, c, _ = _ids()
        for t in range(n):
            for j in range(N_CHIPS):
                _remote(_gview(refs[t], kinds[t], j, 1 - c), refs[n + t].at[j], ssem.at[4 * t + j], rsem.at[4 * t + j],
                        (x, y, 1 - c)).start()
        token[...] = jnp.zeros_like(token)

    outs = pl.pallas_call(
        body, name=name,
        out_shape=(pltpu.SemaphoreType.DMA((4 * n,)), pltpu.SemaphoreType.DMA((4 * n,)),
                   *[pltpu.HBM(a.shape, a.dtype) for a in g_fulls + lands], jax.ShapeDtypeStruct((8, 128), F32)),
        in_specs=[HBM] * (2 * n) + [ANY] * len(afters),
        out_specs=(SEM, SEM, *[HBM] * (2 * n), pl.BlockSpec(memory_space=pltpu.VMEM)),
        input_output_aliases={t: 2 + t for t in range(2 * n)},
        compiler_params=pltpu.CompilerParams(has_side_effects=DATAFLOW))(*_in_hbm(g_fulls + lands), *afters)
    return outs[0], outs[1], list(outs[2:2 + n]), list(outs[2 + n:2 + 2 * n]), outs[-1]


def _rs1_wait(ssem, rsem, g_fulls, lands, kinds, after, name):
    n = len(g_fulls)

    def body(*refs):
        ssem_ref, rsem_ref = refs[2 * n], refs[2 * n + 1]
        x, y, c, _ = _ids()
        for t in range(n):
            for j in range(N_CHIPS):
                _remote(_gview(refs[t], kinds[t], j, 1 - c), refs[n + t].at[j], ssem_ref.at[4 * t + j], rsem_ref.at[4 * t + j],
                        (x, y, 1 - c)).wait()

    outs = pl.pallas_call(
        body, name=name, out_shape=[pltpu.HBM(a.shape, a.dtype) for a in g_fulls + lands],
        in_specs=[HBM] * (2 * n) + [SEM, SEM, ANY], out_specs=[HBM] * (2 * n),
        input_output_aliases={t: t for t in range(2 * n)},
        compiler_params=pltpu.CompilerParams(has_side_effects=DATAFLOW))(*g_fulls, *lands, ssem, rsem, after)
    return list(outs[:n]), list(outs[n:])


def _rs_add1(g_full, got, kind, name):
    l, k, n = g_full.shape
    _, _, pr, pc = got.shape
    tr = _slab_rows(pr, pc, 2)
    nt = pr // tr
    if kind == "row":
        g_spec = pl.BlockSpec((None, tr, n), lambda j, li, t, s: (li, (2 * j + s[1]) * nt + t, 0))
    else:
        g_spec = pl.BlockSpec((None, tr, pc), lambda j, li, t, s: (li, s[1] * nt + t, j))
    slot = pl.BlockSpec((None, None, tr, pc), lambda j, li, t, s: (j, li, t, 0))

    def body(s_ref, g_ref, got_ref, o_ref):
        o_ref[...] = g_ref[...] + got_ref[...]

    return pl.pallas_call(
        body, name=name, out_shape=jax.ShapeDtypeStruct(got.shape, BF16),
        grid_spec=pltpu.PrefetchScalarGridSpec(num_scalar_prefetch=1, grid=(N_CHIPS, l, nt), in_specs=[g_spec, slot],
                                               out_specs=slot),
        compiler_params=_params("parallel", "parallel", "parallel"))(_mesh_scalars(), g_full, got)


def _rs2_start(ps, name):
    n = len(ps)
    lands = [lax.empty(p.shape, p.dtype) for p in ps]

    def body(*refs):
        ssem, rsem, token = refs[2 * n], refs[2 * n + 1], refs[-1]
        x, y, c, me = _ids()
        for t in range(n):
            for d, pj, px, py in _chip_peers(x, y):
                _remote(refs[t].at[pj], refs[n + t].at[me], ssem.at[3 * t + d], rsem.at[3 * t + d], (px, py, c)).start()
        token[...] = jnp.zeros_like(token)

    outs = pl.pallas_call(
        body, name=name,
        out_shape=(pltpu.SemaphoreType.DMA((3 * n,)), pltpu.SemaphoreType.DMA((3 * n,)),
                   *[pltpu.HBM(p.shape, p.dtype) for p in ps + lands], jax.ShapeDtypeStruct((8, 128), F32)),
        in_specs=[HBM] * (2 * n), out_specs=(SEM, SEM, *[HBM] * (2 * n), pl.BlockSpec(memory_space=pltpu.VMEM)),
        input_output_aliases={t: 2 + t for t in range(2 * n)},
        compiler_params=pltpu.CompilerParams(has_side_effects=DATAFLOW))(*_in_hbm(ps + lands))
    return outs[0], outs[1], list(outs[2:2 + n]), list(outs[2 + n:2 + 2 * n]), outs[-1]


def _rs2_wait(ssem, rsem, ps, lands, after, name):
    n = len(ps)

    def body(*refs):
        ssem_ref, rsem_ref = refs[2 * n], refs[2 * n + 1]
        x, y, c, me = _ids()
        for t in range(n):
            for d, pj, px, py in _chip_peers(x, y):
                _remote(refs[t].at[pj], refs[n + t].at[me], ssem_ref.at[3 * t + d], rsem_ref.at[3 * t + d], (px, py, c)).wait_send()
                _remote(refs[t].at[pj], refs[n + t].at[pj], ssem_ref.at[3 * t + d], rsem_ref.at[3 * t + d], (px, py, c)).wait_recv()

    outs = pl.pallas_call(
        body, name=name, out_shape=[pltpu.HBM(p.shape, p.dtype) for p in ps + lands],
        in_specs=[HBM] * (2 * n) + [SEM, SEM, ANY], out_specs=[HBM] * (2 * n),
        input_output_aliases={t: t for t in range(2 * n)},
        compiler_params=pltpu.CompilerParams(has_side_effects=DATAFLOW))(*ps, *lands, ssem, rsem, after)
    return list(outs[:n]), list(outs[n:])


def _rs_add2(p, got, into, layer, name):
    _, _, pr, pc = p.shape
    tr = _slab_rows(pr, pc)
    nt = pr // tr

    def slot(d):
        return pl.BlockSpec((None, None, tr, pc), lambda t, s: (s[0] ^ d, 0, t, 0))

    def body(s_ref, p_ref, g1_ref, g2_ref, g3_ref, i_ref, o_ref):
        o_ref[...] = (p_ref[...].astype(F32) + g1_ref[...].astype(F32) + g2_ref[...].astype(F32) + g3_ref[...].astype(F32))

    return pl.pallas_call(
        body, name=name, out_shape=jax.ShapeDtypeStruct(into.shape, F32),
        grid_spec=pltpu.PrefetchScalarGridSpec(
            num_scalar_prefetch=1, grid=(nt,), in_specs=[slot(0), slot(1), slot(2), slot(3), HBM],
            out_specs=pl.BlockSpec((None, tr, pc), lambda t, s: (layer, s[1] * nt + t, 0))),
        input_output_aliases={5: 0},
        compiler_params=_params("parallel"))(_mesh_scalars(), p, got, got, got, into)


def _rs3(shards, layers, name):
    n = len(shards)

    def body(*refs):
        outs = refs[n:2 * n]
        token, ssem, rsem = refs[2 * n], refs[2 * n + 1], refs[2 * n + 2]
        x, y, c, _ = _ids()
        sib = (x, y, 1 - c)
        token[...] = jnp.zeros_like(token)

        def half(t, cc):
            return _sview(outs[t].at[pl.ds(layers[t], 1)], cc)

        sends = [_remote(half(t, c), half(t, c), ssem.at[t], rsem.at[t], sib) for t in range(n)]
        for cp in sends:
            cp.start()
        for t in range(n):
            _remote(half(t, 1 - c), half(t, 1 - c), ssem.at[t], rsem.at[t], sib).wait_recv()
        for cp in sends:
            cp.wait_send()

    outs = pl.pallas_call(
        body, name=name, out_shape=[jax.ShapeDtypeStruct(s.shape, s.dtype) for s in shards] + [jax.ShapeDtypeStruct((8, 128), F32)],
        in_specs=[HBM] * n, out_specs=[HBM] * n + [pl.BlockSpec(memory_space=pltpu.VMEM)],
        scratch_shapes=[pltpu.SemaphoreType.DMA((n,)), pltpu.SemaphoreType.DMA((n,))],
        input_output_aliases={t: t for t in range(n)}, compiler_params=pltpu.CompilerParams(has_side_effects=True))(*shards)
    return list(outs[:n]), outs[n]


def _ag_small(sp, name):
    def body(s_ref, o_ref, ssem, rsem, lsem):
        x, y, c, me = _ids()
        local = pltpu.make_async_copy(s_ref, o_ref.at[me], lsem.at[0])
        local.start()
        sends = [_remote(s_ref, o_ref.at[me], ssem.at[d], rsem.at[d], (px, py, c)) for d, _, px, py in _chip_peers(x, y)]
        for cp in sends:
            cp.start()
        for d, pj, px, py in _chip_peers(x, y):
            _remote(s_ref, o_ref.at[pj], ssem.at[d], rsem.at[d], (px, py, c)).wait_recv()
        for cp in sends:
            cp.wait_send()
        local.wait()

    return _comm_call(body, name, [sp], [jax.ShapeDtypeStruct((N_CHIPS,) + sp.shape, sp.dtype)], (3, 3, 1))[0]


def _slot_place(g, name):
    rows, cols = g.shape
    x, y, c = lax.axis_index("x"), lax.axis_index("y"), lax.axis_index("c")
    slot = (4 * x + 2 * y + c).astype(jnp.int32).reshape(1)

    def body(s_ref, i_ref, o_ref):
        o_ref[...] = i_ref[...]

    return pl.pallas_call(
        body, name=name, out_shape=jax.ShapeDtypeStruct((8, rows, cols), g.dtype),
        grid_spec=pltpu.PrefetchScalarGridSpec(
            num_scalar_prefetch=1, grid=(1,), in_specs=[pl.BlockSpec((rows, cols), lambda t, s: (0, 0))],
            out_specs=pl.BlockSpec((None, rows, cols), lambda t, s: (s[0], 0, 0))),
        compiler_params=_params("arbitrary"))(slot, g)


def _gather8_peers(x, y, c):
    return [(d - 1, x ^ (d >> 2), y ^ ((d >> 1) & 1), c ^ (d & 1)) for d in range(1, 8)]


def _gather8_start(buf, name):
    def body(b_ref, ssem, rsem, b_thru, token):
        x, y, c, _ = _ids()
        mine = b_ref.at[4 * x + 2 * y + c]
        for d, px, py, pc in _gather8_peers(x, y, c):
            _remote(mine, mine, ssem.at[d], rsem.at[d], (px, py, pc)).start()
        token[...] = jnp.zeros_like(token)

    outs = pl.pallas_call(
        body, name=name,
        out_shape=(pltpu.SemaphoreType.DMA((7,)), pltpu.SemaphoreType.DMA((7,)), pltpu.HBM(buf.shape, buf.dtype),
                   jax.ShapeDtypeStruct((8, 128), F32)),
        in_specs=[HBM], out_specs=(SEM, SEM, HBM, pl.BlockSpec(memory_space=pltpu.VMEM)), input_output_aliases={0: 2},
        compiler_params=pltpu.CompilerParams(has_side_effects=DATAFLOW))(*_in_hbm([buf]))
    return outs


def _gather8_wait(ssem, rsem, buf, after, name):
    def body(b_ref, ssem_ref, rsem_ref, after_ref, b_out):
        x, y, c, _ = _ids()
        mine = b_ref.at[4 * x + 2 * y + c]
        for d, px, py, pc in _gather8_peers(x, y, c):
            theirs = b_ref.at[4 * px + 2 * py + pc]
            _remote(mine, mine, ssem_ref.at[d], rsem_ref.at[d], (px, py, pc)).wait_send()
            _remote(theirs, theirs, ssem_ref.at[d], rsem_ref.at[d], (px, py, pc)).wait_recv()

    return pl.pallas_call(
        body, name=name, out_shape=pltpu.HBM(buf.shape, buf.dtype), in_specs=[HBM, SEM, SEM, ANY], out_specs=HBM,
        input_output_aliases={0: 0},
        compiler_params=pltpu.CompilerParams(has_side_effects=DATAFLOW))(buf, ssem, rsem, after)


def _sum_slots(a, out_dtype, name):
    n = a.shape[0]
    shape = a.shape[1:]
    cols = shape[-1]
    a3 = a.reshape(n, -1, cols)
    rows = a3.shape[1]
    tr = rows
    for cand in (512, 256, 128, 64, 32, 16):
        if rows % cand == 0 and cand * cols * 4 <= (1 << 20):
            tr = cand
            break

    def body(a_ref, o_ref):
        acc = a_ref[0].astype(F32)
        for j in range(1, n):
            acc = acc + a_ref[j].astype(F32)
        o_ref[...] = acc.astype(o_ref.dtype)

    out = pl.pallas_call(
        body, name=name, out_shape=jax.ShapeDtypeStruct((rows, cols), out_dtype), grid=(rows // tr,),
        in_specs=[pl.BlockSpec((n, tr, cols), lambda i: (0, i, 0))], out_specs=pl.BlockSpec((tr, cols), lambda i: (i, 0)),
        compiler_params=_params("parallel"))(a3)
    return out.reshape(shape)


def kernel(x, mem, mix_norm, xa_norm, xa_wq, xa_wkv, xa_wo, ffn_norm, ffn_w_gu, ffn_w_down, a_w_in, a_conv_w, a_w_out, b_w_in, b_v_g, b_v_b, b_w_s, b_s_bias, b_w_out, c_w_in, c_conv_w, c_conv_b, c_ln_g, c_ln_b, c_w_out, loss_target, m_mix_norm, m_xa_norm, m_xa_wq, m_xa_wkv, m_xa_wo, m_ffn_norm, m_ffn_w_gu, m_ffn_w_down, m_a_w_in, m_a_conv_w, m_a_w_out, m_b_w_in, m_b_v_g, m_b_v_b, m_b_w_s, m_b_s_bias, m_b_w_out, m_c_w_in, m_c_conv_w, m_c_conv_b, m_c_ln_g, m_c_ln_b, m_c_w_out, v_mix_norm, v_xa_norm, v_xa_wq, v_xa_wkv, v_xa_wo, v_ffn_norm, v_ffn_w_gu, v_ffn_w_down, v_a_w_in, v_a_conv_w, v_a_w_out, v_b_w_in, v_b_v_g, v_b_v_b, v_b_w_s, v_b_s_bias, v_b_w_out, v_c_w_in, v_c_conv_w, v_c_conv_b, v_c_ln_g, v_c_ln_b, v_c_w_out):
    given = dict(locals())
    w = {n: given[n] for n in WEIGHTS}
    depth = mix_norm.shape[0]
    s, d = x.shape[1], x.shape[2]
    n_mem = mem.shape[1]
    ds = d // N_CHIPS
    xin = x.reshape(s, d)
    memv = mem.reshape(n_mem, d)
    target = loss_target.reshape(s, d)
    me = 2 * lax.axis_index("x") + lax.axis_index("y")

    wg = {n: [None] * w[n].shape[0] for n in BIG_KINDS}

    def mixer_keys(i):
        return [("abc"[i % 3] + "_w_in", i // 3), ("abc"[i % 3] + "_w_out", i // 3)]

    def rest_keys(i):
        return [("xa_wq", i), ("xa_wkv", i), ("xa_wo", i), ("ffn_w_gu", i), ("ffn_w_down", i)]

    def ag_begin(keys, tag, after):
        kinds = [BIG_KINDS[n] for n, _ in keys]
        ssem, rsem, bufs, token = _ag_start([placed[k] for k in keys], kinds, "ag_start_" + tag, after)
        return keys, kinds, ssem, rsem, bufs, token, tag

    def ag_end(state, after):
        keys, kinds, ssem, rsem, bufs, _, tag = state
        bufs = _ag_forward(_ag_wait(ssem, rsem, bufs, kinds, after, "ag_wait_" + tag), kinds, "ag_fwd_" + tag)
        for (n, l), buf in zip(keys, bufs):
            wg[n][l] = buf

    def pad8(t):
        return jnp.pad(t, ((0, (-t.shape[0]) % 8), (0, 0)))

    small_rows = [w[n].reshape(-1, ds) for n in SMALL_SHARDED]
    counts = [t.shape[0] for t in small_rows]
    gathered = _ag_small(jnp.concatenate([pad8(t) for t in small_rows], axis=0), "ag_small")
    placed = {(n, l): _ag_place(w[n], l, BIG_KINDS[n], f"ag_place_{n}_{l}", dep=gathered) for n, l in mixer_keys(0)}
    gathered = jnp.transpose(gathered, (1, 0, 2)).reshape(-1, d)
    full, off = {}, 0
    for n, cnt in zip(SMALL_SHARDED, counts):
        full[n] = gathered[off:off + cnt].reshape(w[n].shape[:-1] + (d,))
        off += cnt + (-cnt) % 8
    t_chunk = b_w_s.shape[-1]
    tril = jnp.tril(jnp.ones((t_chunk, t_chunk), dtype=bool))

    def vec(a):
        return a.reshape(1, -1)

    def b_params(slot):
        ws_m = jnp.where(tril[None], b_w_s[slot], 0.0).astype(BF16)
        sbt = jnp.zeros((t_chunk, 128), F32).at[:, :b_s_bias.shape[1]].set(b_s_bias[slot].T)
        return vec(b_v_g[slot]), vec(b_v_b[slot]), ws_m, sbt

    saved = []
    xc = xin
    ag_groups = [(mixer_keys(0), "0m"), (rest_keys(0), "0r")] + [(mixer_keys(j) + rest_keys(j), f"{j}") for j in range(1, depth)]
    ag_state = {}

    def ag_begin_group(k, after):
        if k >= len(ag_groups):
            return None
        ag_state[k] = ag_begin(*ag_groups[k], after)
        return ag_state[k][5]

    dep = ag_begin_group(0, None)
    for n, kind in BIG_KINDS.items():
        for l in range(w[n].shape[0]):
            if (n, l) not in placed:
                placed[n, l] = dep = _ag_place(w[n], l, kind, f"ag_place_{n}_{l}", dep=dep)
    ag_end(ag_state[0], dep)
    for i in range(depth):
        kind, slot = i % 3, i // 3
        t = f"{i}"
        dep = ag_begin_group(1, wg[mixer_keys(0)[0][0]][0]) if i == 0 else ag_begin_group(i + 2, xc)
        sv = {"x0": xc}
        if i == 0:
            h = _rms_fwd(xc, vec(full["mix_norm"][i, 0]), "rms_mix_" + t)
        sv["h1"] = h
        if kind == 0:
            pre = _mm("nn", h, wg["a_w_in"], BF16, "a_in_" + t, bl=slot, o_parts=3, dep=dep)
            mid = _a_mid_fwd(pre, full["a_conv_w"][slot], "a_mid_" + t)
            y = _mm("nn", mid, wg["a_w_out"], F32, "a_out_" + t, bl=slot)
        elif kind == 1:
            pre = _mm("nn", h, wg["b_w_in"], BF16, "b_in_" + t, bl=slot, o_parts=2, dep=dep)
            mid = _b_mid_fwd(pre, *b_params(slot), "b_mid_" + t)
            y = _mm("nn", mid, wg["b_w_out"], F32, "b_out_" + t, bl=slot)
        else:
            pre = _mm("nn", h, wg["c_w_in"], BF16, "c_in_" + t, bl=slot, o_parts=2, dep=dep)
            y2 = _c_conv_fwd(pre, full["c_conv_w"][slot], vec(full["c_conv_b"][slot]), "c_conv_" + t)
            sv["cy2"] = y2
            mid = _c_ln_fwd(y2, vec(full["c_ln_g"][slot]), vec(full["c_ln_b"][slot]), "c_ln_" + t)
            y = _mm("nn", mid, wg["c_w_out"], F32, "c_out_" + t, bl=slot)
        sv.update(pre=pre, mid=mid, y1=y)
        xc, h = _res_rms_fwd(xc, y, vec(full["mix_norm"][i, 1]), vec(full["xa_norm"][i, 0]), "res_mix_" + t)

        sv["x1"] = xc
        dep = None
        if i == 0:
            ag_end(ag_state[1], xc)
            dep = ag_begin_group(2, xc)
        mem_n = _rms_fwd(memv, vec(full["xa_norm"][i, 2]), "rms_mem_" + t)
        q = _mm("nn", h, wg["xa_wq"], BF16, "xa_q_" + t, bl=i, dep=dep)
        kv3 = _mm("nn", mem_n, wg["xa_wkv"], BF16, "xa_kv_" + t, bl=i, o_parts=2)
        o = _attn_fwd(q, kv3, "attn_" + t)
        y = _mm("nn", o, wg["xa_wo"], F32, "xa_o_" + t, bl=i)
        sv.update(h2=h, mem_n=mem_n, q=q, kv3=kv3, o=o, y2=y)
        xc, h = _res_rms_fwd(xc, y, vec(full["xa_norm"][i, 1]), vec(full["ffn_norm"][i, 0]), "res_xa_" + t)

        sv["x2"] = xc
        gu3, act = _ffn_gu_fwd(h, wg["ffn_w_gu"][i], "ffn_gu_" + t)
        y = _mm("nn", act, wg["ffn_w_down"], F32, "ffn_down_" + t, bl=i)
        sv.update(h3=h, gu3=gu3, act=act, y3=y)
        xc, h = _res_rms_fwd(xc, y, vec(full["ffn_norm"][i, 1]),
                             vec(full["mix_norm"][i + 1, 0]) if i + 1 < depth else None, "res_ffn_" + t)
        saved.append(sv)
        if i + 2 in ag_state:
            ag_end(ag_state[i + 2], xc)

    loss_blk, dx = _loss(xc, target, "loss")
    loss = lax.psum(loss_blk[0, 0], ("x", "y", "c"))

    gbuf = {}
    gfin = {n: lax.empty(w[n].shape, F32) for n in BIG_KINDS}
    gsmall = {n: [None] * full[n].shape[0] for n in ("mix_norm", "xa_norm", "ffn_norm", "a_conv_w", "c_conv_w", "c_conv_b",
                                                      "c_ln_g", "c_ln_b")}
    grepl = {}

    def wgrad(name, l, a, dy, tag, b_parts=1):
        g2 = _mm("tn", a, dy, BF16, "wg_" + tag, b_parts=b_parts)
        gbuf[name, l] = g2.reshape((1,) + g2.shape)

    def rs_begin(keys, tag, dep):
        kinds = [BIG_KINDS[n] for n, _ in keys]
        gots = _rs1([gbuf[k] for k in keys], kinds, "rs1_" + tag, dep=dep)
        ps = [_rs_add1(gbuf[k], got, kind, f"rs_add1_{k[0]}_{k[1]}") for k, got, kind in zip(keys, gots, kinds)]
        ssem, rsem, ps, lands, token = _rs2_start(ps, "rs2_start_" + tag)
        return keys, ssem, rsem, ps, lands, token, tag

    def rs_end(state, after):
        keys, ssem, rsem, ps, lands, _, tag = state
        ps, lands = _rs2_wait(ssem, rsem, ps, lands, after, "rs2_wait_" + tag)
        for (n, l), p, land in zip(keys, ps, lands):
            gfin[n] = _rs_add2(p, land, gfin[n], l, f"rs_add2_{n}_{l}")
        outs, token = _rs3([gfin[n] for n, _ in keys], [l for _, l in keys], "rs3_" + tag)
        for (n, _), o in zip(keys, outs):
            gfin[n] = o
        return token

    def rs1_begin(keys, tag, after):
        kinds = [BIG_KINDS[n] for n, _ in keys]
        ssem, rsem, gs, lands, token = _rs1_start([gbuf[k] for k in keys], kinds, "rs1_start_" + tag, after)
        return keys, kinds, ssem, rsem, gs, lands, token, tag

    def rs_begin_after_rs1(state, after):
        keys, kinds, ssem, rsem, gs, lands, _, tag = state
        gs, gots = _rs1_wait(ssem, rsem, gs, lands, kinds, after, "rs1_wait_" + tag)
        ps = [_rs_add1(g, got, kind, f"rs_add1_{k[0]}_{k[1]}") for k, g, got, kind in zip(keys, gs, gots, kinds)]
        ssem, rsem, ps, lands, token = _rs2_start(ps, "rs2_start_" + tag)
        return keys, ssem, rsem, ps, lands, token, tag

    rs_state, rs_token, rs1_state = None, None, None

    for i in reversed(range(depth)):
        kind, slot = i % 3, i // 3
        t = f"{i}"
        sv = saved[i]
        dep = rs1_state[6] if rs1_state is not None else (None if rs_state is None else rs_state[5])
        dy, dg_post = _rms_bwd(sv["y3"], vec(full["ffn_norm"][i, 1]), dx, None, BF16, "rmsb_ffn_post_" + t, dep=dep)
        wgrad("ffn_w_down", i, sv["act"], dy, "ffn_down_" + t)
        dgu3 = _ffn_down_bwd(dy, wg["ffn_w_down"][i], sv["gu3"], "dg_ffn_down_" + t)
        wgrad("ffn_w_gu", i, sv["h3"], dgu3, "ffn_gu_" + t, b_parts=2)
        dh = _mm("nt", dgu3, wg["ffn_w_gu"], BF16, "dg_ffn_gu_" + t, bl=i, a_parts=2)
        dx, dg_pre = _rms_bwd(sv["x2"], vec(full["ffn_norm"][i, 0]), dh, dx, F32, "rmsb_ffn_pre_" + t)
        gsmall["ffn_norm"][i] = jnp.concatenate([dg_pre, dg_post], axis=0)
        dep = None
        if rs1_state is not None:
            rs_state, rs1_state = rs_begin_after_rs1(rs1_state, dx), None
            dep = rs_state[5]
        if i == 0:
            rs_state_f = rs_begin(rest_keys(0)[3:], "0f", None)
            dep = rs_state_f[5]
        dy, dg_post = _rms_bwd(sv["y2"], vec(full["xa_norm"][i, 1]), dx, None, BF16, "rmsb_xa_post_" + t, dep=dep)
        wgrad("xa_wo", i, sv["o"], dy, "xa_o_" + t)
        do = _mm("nt", dy, wg["xa_wo"], BF16, "dg_xa_o_" + t, bl=i)
        dq, dkv3 = _attn_bwd(sv["q"], sv["kv3"], do, "attn_b_" + t)
        wgrad("xa_wq", i, sv["h2"], dq, "xa_q_" + t)
        dh = _mm("nt", dq, wg["xa_wq"], BF16, "dg_xa_q_" + t, bl=i)
        dkv3 = dkv3.astype(BF16)
        wgrad("xa_wkv", i, sv["mem_n"], dkv3, "xa_kv_" + t, b_parts=2)
        dmem_n = _mm("nt", dkv3, wg["xa_wkv"], F32, "dg_xa_kv_" + t, bl=i, a_parts=2)
        _, dg_mem = _rms_bwd(memv, vec(full["xa_norm"][i, 2]), dmem_n, None, F32, "rmsb_mem_" + t)
        dx, dg_pre = _rms_bwd(sv["x1"], vec(full["xa_norm"][i, 0]), dh, dx, F32, "rmsb_xa_pre_" + t)
        gsmall["xa_norm"][i] = jnp.concatenate([dg_pre, dg_post, dg_mem], axis=0)
        if i == 0:
            rs_token = rs_end(rs_state, dx)
            rs_state = rs_begin(rest_keys(0)[:3], "0x", rs_token)
        dy, dg_post = _rms_bwd(sv["y1"], vec(full["mix_norm"][i, 1]), dx, None, BF16, "rmsb_mix_post_" + t,
                               dep=rs_state[5] if i == 0 else None)
        if kind == 0:
            wgrad("a_w_out", slot, sv["mid"], dy, "a_out_" + t)
            dmid = _mm("nt", dy, wg["a_w_out"], F32, "dg_a_out_" + t, bl=slot)
            dpre, dcw = _a_mid_bwd(sv["pre"], dmid, full["a_conv_w"][slot], "a_mid_b_" + t)
            gsmall["a_conv_w"][slot] = dcw
            wgrad("a_w_in", slot, sv["h1"], dpre, "a_in_" + t, b_parts=3)
            dh = _mm("nt", dpre, wg["a_w_in"], BF16, "dg_a_in_" + t, bl=slot, a_parts=3)
        elif kind == 1:
            wgrad("b_w_out", slot, sv["mid"], dy, "b_out_" + t)
            dmid = _mm("nt", dy, wg["b_w_out"], F32, "dg_b_out_" + t, bl=slot)
            dpre, dvg, dvb, dws, dsbt = _b_mid_bwd(sv["pre"], dmid, *b_params(slot), "b_mid_b_" + t)
            grepl[slot] = (dvg, dvb, dws, dsbt[:, :b_s_bias.shape[1]].T)
            wgrad("b_w_in", slot, sv["h1"], dpre, "b_in_" + t, b_parts=2)
            dh = _mm("nt", dpre, wg["b_w_in"], BF16, "dg_b_in_" + t, bl=slot, a_parts=2)
        else:
            wgrad("c_w_out", slot, sv["mid"], dy, "c_out_" + t)
            dmid = _mm("nt", dy, wg["c_w_out"], F32, "dg_c_out_" + t, bl=slot)
            dy2, dlg, dlb = _c_ln_bwd(sv["cy2"], dmid, vec(full["c_ln_g"][slot]), vec(full["c_ln_b"][slot]), "c_ln_b_" + t)
            dpre, dcw, dcb = _c_conv_bwd(sv["pre"], dy2, full["c_conv_w"][slot], "c_conv_b_" + t)
            gsmall["c_conv_w"][slot], gsmall["c_conv_b"][slot] = dcw, dcb
            gsmall["c_ln_g"][slot], gsmall["c_ln_b"][slot] = dlg, dlb
            wgrad("c_w_in", slot, sv["h1"], dpre, "c_in_" + t, b_parts=2)
            dh = _mm("nt", dpre, wg["c_w_in"], BF16, "dg_c_in_" + t, bl=slot, a_parts=2)
        dx, dg_pre = _rms_bwd(sv["x0"], vec(full["mix_norm"][i, 0]), dh, dx, F32, "rmsb_mix_pre_" + t)
        gsmall["mix_norm"][i] = jnp.concatenate([dg_pre, dg_post], axis=0)
        if i == 0:
            rs_end(rs_state_f, dx)
        if rs_state is not None:
            rs_token = rs_end(rs_state, dx)
        if i == depth - 1:
            rs_state, rs1_state = None, rs1_begin(mixer_keys(i) + rest_keys(i), f"{i}", rs_token if rs_token is not None else dx)
        else:
            rs_state = rs_begin(mixer_keys(i) + (rest_keys(i) if i > 0 else []), f"{i}" if i > 0 else "0m", rs_token)
    grad_x = dx.reshape(x.shape)

    small_g = [jnp.concatenate(gsmall[n], axis=0).reshape(-1, d) for n in SMALL_SHARDED]
    n_b = b_v_g.shape[0]
    repl_g = [jnp.concatenate([grepl[sl][k] for sl in range(n_b)], axis=0) for k in range(4)]
    repl_rows = []
    for g_arr in repl_g:
        flat = g_arr.reshape(-1)
        flat = jnp.concatenate([flat, jnp.zeros(((-flat.shape[0]) % d,), F32)])
        repl_rows.append(flat.reshape(-1, d))
    packed_g = jnp.concatenate([pad8(t) for t in small_g + repl_rows], axis=0)
    g_ssem, g_rsem, g_buf, g_token = _gather8_start(_slot_place(packed_g, "place_small_grads"), "gather_small_start")

    delta, new_m, new_v, grads = {}, {}, {}, {}
    last_keys = {n for n, _ in mixer_keys(0)}
    deps = [rs_state[5], g_token]
    for n in BIG_KINDS:
        if n not in last_keys:
            delta[n], new_m[n], new_v[n], grads[n] = _adamw(w[n], gfin[n], given["m_" + n], given["v_" + n], "adamw_" + n,
                                                            dep=deps.pop(0), with_grad=True)
            deps.append(delta[n])
    rs_end(rs_state, deps[-1])
    total = _sum_slots(_gather8_wait(g_ssem, g_rsem, g_buf, deps[-1], "gather_small_wait"), F32, "sum_small_grads")
    for n in sorted(last_keys):
        delta[n], new_m[n], new_v[n], grads[n] = _adamw(w[n], gfin[n], given["m_" + n], given["v_" + n], "adamw_" + n,
                                                        with_grad=True)

    off = 0
    for n, cnt in zip(SMALL_SHARDED, counts):
        blk = lax.dynamic_slice_in_dim(total[off:off + cnt], me * ds, ds, axis=1)
        grads[n] = blk.reshape(w[n].shape)
        off += cnt + (-cnt) % 8
    for n, g_arr in zip(SMALL_REPL, repl_g):
        cnt = -(-g_arr.size // d)
        grads[n] = total[off:off + cnt].reshape(-1)[:g_arr.size].reshape(w[n].shape)
        off += cnt + (-cnt) % 8

    for n in WEIGHTS:
        if n not in delta:
            delta[n], new_m[n], new_v[n] = _adamw(w[n], grads[n], given["m_" + n], given["v_" + n], "adamw_" + n)
    return (loss, grad_x, *[grads[n] for n in WEIGHTS], *[delta[n] for n in WEIGHTS], *[new_m[n] for n in WEIGHTS],
            *[new_v[n] for n in WEIGHTS])
```

```python
import functools

import jax
import jax.numpy as jnp
from jax import lax
from jax.experimental import pallas as pl
from jax.experimental.pallas import tpu as pltpu

F32 = jnp.float32
BF16 = jnp.bfloat16
EPS = 1e-6
XA_HEADS = 4
CHUNK = 128
GMLP_GROUPS = 8
ADAM_LR, ADAM_B1, ADAM_B2, ADAM_EPS, ADAM_WD, ADAM_STEP = 0.001, 0.9, 0.999, 1e-08, 0.01, 10
VMEM_LIMIT_V7X = 48 * 1024 * 1024
HBM = pl.BlockSpec(memory_space=pltpu.HBM)
MESH = pl.DeviceIdType.MESH
N_CHIPS = 4
BIG_KINDS = {"xa_wq": "row", "xa_wkv": "col", "xa_wo": "row", "ffn_w_gu": "col", "ffn_w_down": "row",
             "a_w_in": "col", "a_w_out": "row", "b_w_in": "col", "b_w_out": "row", "c_w_in": "col", "c_w_out": "row"}
SMALL_SHARDED = ["mix_norm", "xa_norm", "ffn_norm", "a_conv_w", "c_conv_w", "c_conv_b", "c_ln_g", "c_ln_b"]
SMALL_REPL = ["b_v_g", "b_v_b", "b_w_s", "b_s_bias"]
WEIGHTS = ["mix_norm", "xa_norm", "xa_wq", "xa_wkv", "xa_wo", "ffn_norm", "ffn_w_gu", "ffn_w_down", "a_w_in", "a_conv_w",
           "a_w_out", "b_w_in", "b_v_g", "b_v_b", "b_w_s", "b_s_bias", "b_w_out", "c_w_in", "c_conv_w", "c_conv_b",
           "c_ln_g", "c_ln_b", "c_w_out"]


def _params(*sem):
    return pltpu.CompilerParams(dimension_semantics=sem, vmem_limit_bytes=VMEM_LIMIT_V7X)


def _tile(n, cands=(1024, 512, 256, 128)):
    for c in cands:
        if n % c == 0:
            return c
    return n


def _div_tile(n, cap):
    best = None
    for t in range(128, min(n, cap) + 1, 128):
        if n % t == 0:
            best = t
    return best or n


MM_OUT_TILE_CAP = 1408
MM_K_TILE_CAP = 2816
TN_OUT_ROWS_CAP, TN_OUT_COLS_CAP, TN_K_TILE_CAP = 512, 1024, 4096

_DIMS = {"nn": (((1,), (0,)), ((), ())), "nt": (((1,), (1,)), ((), ())), "tn": (((0,), (0,)), ((), ()))}


def _mm(mode, a, b, out_dtype, name, *, bl=None, a_parts=1, b_parts=1, o_parts=1, dep=None):
    if isinstance(b, list):
        b, bl = b[bl], 0
    bshape = b.shape[1:] if bl is not None else b.shape
    if mode == "nn":
        mo, c = a.shape
        no = bshape[1]
    elif mode == "nt":
        mo, c = (a.shape[1], a.shape[0] * a.shape[2]) if a_parts > 1 else a.shape
        no = bshape[0]
    else:
        c, mo = a.shape
        no = b.shape[0] * b.shape[2] if b_parts > 1 else bshape[1]
    if mode == "tn":
        tmo = _div_tile(mo, TN_OUT_ROWS_CAP)
        tno = _div_tile(no // b_parts, TN_OUT_COLS_CAP)
        tc = _div_tile(c, TN_K_TILE_CAP)
    else:
        tmo = _div_tile(mo, MM_OUT_TILE_CAP)
        tno = _div_tile(no // max(o_parts, b_parts), MM_OUT_TILE_CAP)
        tc = _div_tile(c // a_parts, MM_K_TILE_CAP)
    nk = c // tc
    nkp = nk // a_parts
    njp = (no // tno) // max(o_parts, b_parts)
    lead = (None,) if bl is not None else ()
    lidx = (bl,) if bl is not None else ()

    if mode == "nn":
        a_spec = pl.BlockSpec((tmo, tc), lambda i, j, k: (i, k))
        b_spec = pl.BlockSpec(lead + (tc, tno), lambda i, j, k: lidx + (k, j))
    elif mode == "nt":
        if a_parts > 1:
            a_spec = pl.BlockSpec((None, tmo, tc), lambda i, j, k: (k // nkp, i, k % nkp))
        else:
            a_spec = pl.BlockSpec((tmo, tc), lambda i, j, k: (i, k))
        b_spec = pl.BlockSpec(lead + (tno, tc), lambda i, j, k: lidx + (j, k))
    else:
        a_spec = pl.BlockSpec((tc, tmo), lambda i, j, k: (k, i))
        if b_parts > 1:
            b_spec = pl.BlockSpec((None, tc, tno), lambda i, j, k: (j // njp, k, j % njp))
        else:
            b_spec = pl.BlockSpec((tc, tno), lambda i, j, k: (k, j))

    in_specs = [a_spec, b_spec]
    args = [a, b]
    if dep is not None:
        in_specs.append(pl.BlockSpec(memory_space=pl.ANY))
        args.append(dep)
    if o_parts > 1:
        out_shape = jax.ShapeDtypeStruct((o_parts, mo, no // o_parts), out_dtype)
        out_spec = pl.BlockSpec((None, tmo, tno), lambda i, j, k: (j // njp, i, j % njp))
    else:
        out_shape = jax.ShapeDtypeStruct((mo, no), out_dtype)
        out_spec = pl.BlockSpec((tmo, tno), lambda i, j, k: (i, j))
    dims = _DIMS[mode]

    def body(a_ref, b_ref, *rest):
        if nk == 1:
            o_ref = rest[-1]
            o_ref[...] = lax.dot_general(a_ref[...], b_ref[...], dims, preferred_element_type=F32).astype(o_ref.dtype)
            return
        o_ref, acc = rest[-2], rest[-1]
        k = pl.program_id(2)
        part = lax.dot_general(a_ref[...], b_ref[...], dims, preferred_element_type=F32)

        @pl.when(k == 0)
        def _():
            acc[...] = part

        @pl.when(jnp.logical_and(k > 0, k < nk - 1))
        def _():
            acc[...] += part

        @pl.when(k == nk - 1)
        def _():
            o_ref[...] = (acc[...] + part).astype(o_ref.dtype)

    return pl.pallas_call(
        body, name=name, out_shape=out_shape, grid=(mo // tmo, no // tno, nk), in_specs=in_specs, out_specs=out_spec,
        scratch_shapes=[pltpu.VMEM((tmo, tno), F32)] if nk > 1 else [],
        compiler_params=_params("parallel", "parallel", "arbitrary"))(*args)


def _ew(fn, ins, out_dtypes, name, dep=None):
    rows, cols = ins[0].shape
    deps = [] if dep is None else [dep]
    tr = rows
    for cand in (512, 256, 128, 64, 32, 16):
        if rows % cand == 0 and cand * cols * 4 <= (1 << 20):
            tr = cand
            break
    spec = pl.BlockSpec((tr, cols), lambda i: (i, 0))
    n_in = len(ins)

    def body(*refs):
        outs = fn(*[r[...] for r in refs[:n_in]])
        for o_ref, o in zip(refs[n_in + len(deps):], outs):
            o_ref[...] = o.astype(o_ref.dtype)

    return pl.pallas_call(
        body, name=name, out_shape=[jax.ShapeDtypeStruct((rows, cols), d) for d in out_dtypes], grid=(rows // tr,),
        in_specs=[spec] * n_in + [pl.BlockSpec(memory_space=pl.ANY)] * len(deps), out_specs=[spec] * len(out_dtypes),
        compiler_params=_params("parallel"))(*ins, *deps)


def _adamw_fn(w, g, m, v):
    m = ADAM_B1 * m + (1.0 - ADAM_B1) * g
    v = ADAM_B2 * v + (1.0 - ADAM_B2) * (g * g)
    m_hat = m / (1.0 - ADAM_B1 ** ADAM_STEP)
    v_hat = v / (1.0 - ADAM_B2 ** ADAM_STEP)
    delta = -ADAM_LR * (m_hat / (jnp.sqrt(v_hat) + ADAM_EPS) + ADAM_WD * w)
    return delta, m, v


def _adamw(w, g, m, v, name, dep=None, with_grad=False):
    shape = w.shape
    cols = shape[-1]
    flat = [t.reshape(-1, cols) for t in (w, g, m, v)]
    fn = (lambda wv, gv, mv, vv: _adamw_fn(wv, gv, mv, vv) + (gv,)) if with_grad else _adamw_fn
    outs = _ew(fn, flat, [F32] * (4 if with_grad else 3), name, dep=dep)
    return [o.reshape(shape) for o in outs]


def _row_tile(s):
    return _tile(s, (256, 128, 64, 32, 16, 8))


def _rms_fwd(x, g, name, dep=None):
    s, d = x.shape
    r = _row_tile(s)
    deps = [] if dep is None else [dep]

    def body(x_ref, g_ref, *rest):
        o_ref = rest[-1]
        xv = x_ref[...]
        o_ref[...] = (xv * lax.rsqrt(jnp.mean(xv * xv, axis=-1, keepdims=True) + EPS) * g_ref[...]).astype(BF16)

    return pl.pallas_call(
        body, name=name, out_shape=jax.ShapeDtypeStruct((s, d), BF16), grid=(s // r,),
        in_specs=[pl.BlockSpec((r, d), lambda i: (i, 0)), pl.BlockSpec((1, d), lambda i: (0, 0))] + [ANY] * len(deps),
        out_specs=pl.BlockSpec((r, d), lambda i: (i, 0)), compiler_params=_params("parallel"))(x, g, *deps)


def _res_rms_fwd(x, y, g, g_next, name):
    s, d = x.shape
    r = _row_tile(s)
    has_next = g_next is not None

    def body(x_ref, y_ref, g_ref, *rest):
        yv = y_ref[...]
        xn = x_ref[...] + yv * lax.rsqrt(jnp.mean(yv * yv, axis=-1, keepdims=True) + EPS) * g_ref[...]
        rest[-2 if has_next else -1][...] = xn
        if has_next:
            rest[-1][...] = (xn * lax.rsqrt(jnp.mean(xn * xn, axis=-1, keepdims=True) + EPS) * rest[0][...]).astype(BF16)

    row = pl.BlockSpec((r, d), lambda i: (i, 0))
    vec = pl.BlockSpec((1, d), lambda i: (0, 0))
    outs = pl.pallas_call(
        body, name=name,
        out_shape=[jax.ShapeDtypeStruct((s, d), F32)] + ([jax.ShapeDtypeStruct((s, d), BF16)] if has_next else []),
        grid=(s // r,), in_specs=[row, row, vec] + ([vec] if has_next else []), out_specs=[row] * (2 if has_next else 1),
        compiler_params=_params("parallel"))(*([x, y, g] + ([g_next] if has_next else [])))
    return outs[0], (outs[1] if has_next else None)


def _rms_bwd(x, g, dy, resid, out_dtype, name, dep=None):
    s, d = x.shape
    r = _row_tile(s)
    has_res = resid is not None
    deps = [] if dep is None else [dep]

    def body(*refs):
        x_ref, g_ref, dy_ref = refs[:3]
        dx_ref, dg_ref = refs[-2:]
        i = pl.program_id(0)
        xv = x_ref[...]
        dyv = dy_ref[...].astype(F32)
        rstd = lax.rsqrt(jnp.mean(xv * xv, axis=-1, keepdims=True) + EPS)
        n = xv * rstd
        dn = dyv * g_ref[...]
        dx = rstd * (dn - n * jnp.mean(dn * n, axis=-1, keepdims=True))
        if has_res:
            dx = dx + refs[3][...]
        dx_ref[...] = dx.astype(dx_ref.dtype)
        part = jnp.sum(dyv * n, axis=0, keepdims=True)

        @pl.when(i == 0)
        def _():
            dg_ref[...] = part

        @pl.when(i > 0)
        def _():
            dg_ref[...] += part

    row = pl.BlockSpec((r, d), lambda i: (i, 0))
    vec = pl.BlockSpec((1, d), lambda i: (0, 0))
    ins = [x, g, dy] + ([resid] if has_res else []) + deps
    return pl.pallas_call(
        body, name=name, out_shape=[jax.ShapeDtypeStruct((s, d), out_dtype), jax.ShapeDtypeStruct((1, d), F32)],
        grid=(s // r,), in_specs=[row, vec, row] + ([row] if has_res else []) + [ANY] * len(deps), out_specs=[row, vec],
        compiler_params=_params("arbitrary"))(*ins)


def _loss(y, t, name):
    s, d = y.shape
    r = _row_tile(s)

    def body(y_ref, t_ref, l_ref, dy_ref):
        i = pl.program_id(0)
        e = y_ref[...] - t_ref[...]
        dy_ref[...] = e * (1.0 / d)
        part = jnp.full((8, 128), 0.5 * jnp.sum(jnp.mean(e * e, axis=-1, keepdims=True)), F32)

        @pl.when(i == 0)
        def _():
            l_ref[...] = part

        @pl.when(i > 0)
        def _():
            l_ref[...] += part

    row = pl.BlockSpec((r, d), lambda i: (i, 0))
    return pl.pallas_call(
        body, name=name, out_shape=[jax.ShapeDtypeStruct((8, 128), F32), jax.ShapeDtypeStruct((s, d), F32)],
        grid=(s // r,), in_specs=[row, row], out_specs=[pl.BlockSpec((8, 128), lambda i: (0, 0)), row],
        compiler_params=_params("arbitrary"))(y, t)


def _rows(xv, a, m, cache):
    r = a % 8
    q = a - r
    if r == 0:
        return xv[q:q + m]
    if r not in cache:
        cache[r] = pltpu.roll(xv, xv.shape[0] - r, 0)
    return cache[r][q:q + m]


def _conv_taps(xv, w, k_w, halo, m, flip):
    cache = {}
    acc = None
    for k in range(k_w):
        a = (k_w - 1 - k) if flip else (halo + k - (k_w - 1))
        term = w[k:k + 1, :] * _rows(xv, a, m, cache)
        acc = term if acc is None else acc + term
    return acc


def _conv_wgrad(dw_ref, dyv, xv, k_w, halo, m):
    cache = {}
    for k in range(k_w):
        xs = _rows(xv, halo + k - (k_w - 1), m, cache)
        dw_ref[pl.ds(k, 1), :] += jnp.sum(dyv * xs, axis=0, keepdims=True)


def _conv_tiles(s, dp, halo):
    r = _tile(s, (256, 128))
    cw = _tile(dp, (256, 128))
    return r, cw, r // halo


A_HALO = 8


def _a_mid_fwd(bcz3, w, name):
    _, s, d = bcz3.shape
    r, cw, rh = _conv_tiles(s, d, A_HALO)
    k_w = w.shape[0]

    def body(m_ref, h_ref, w_ref, o_ref):
        i = pl.program_id(0)
        cz = m_ref[1].astype(F32) * m_ref[2].astype(F32)
        hcz = h_ref[1].astype(F32) * h_ref[2].astype(F32)
        hcz = jnp.where(i == 0, 0.0, hcz)
        xv = jnp.concatenate([hcz, cz], axis=0)
        y = _conv_taps(xv, w_ref[...], k_w, A_HALO, r, False)
        o_ref[...] = (m_ref[0].astype(F32) * y).astype(BF16)

    return pl.pallas_call(
        body, name=name, out_shape=jax.ShapeDtypeStruct((s, d), BF16), grid=(s // r, d // cw),
        in_specs=[pl.BlockSpec((3, r, cw), lambda i, j: (0, i, j)),
                  pl.BlockSpec((3, A_HALO, cw), lambda i, j: (0, jnp.maximum(i * rh - 1, 0), j)),
                  pl.BlockSpec((k_w, cw), lambda i, j: (0, j))],
        out_specs=pl.BlockSpec((r, cw), lambda i, j: (i, j)), compiler_params=_params("parallel", "parallel"))(bcz3, bcz3, w)


def _a_mid_bwd(bcz3, dgated, w, name):
    _, s, d = bcz3.shape
    r, cw, rh = _conv_tiles(s, d, A_HALO)
    k_w = w.shape[0]
    ni = s // r
    last_h = s // A_HALO - 1

    def body(m_ref, hp_ref, hn_ref, dg_ref, dgn_ref, w_ref, o_ref, dw_ref):
        i = pl.program_id(1)
        wv = w_ref[...]
        b = m_ref[0].astype(F32)
        c = m_ref[1].astype(F32)
        z = m_ref[2].astype(F32)
        hcz = jnp.where(i == 0, 0.0, hp_ref[1].astype(F32) * hp_ref[2].astype(F32))
        xv = jnp.concatenate([hcz, c * z], axis=0)
        y = _conv_taps(xv, wv, k_w, A_HALO, r, False)
        dg = dg_ref[...].astype(F32)
        dy = dg * b
        dyn = jnp.where(i == ni - 1, 0.0, dgn_ref[...].astype(F32) * hn_ref[0].astype(F32))
        dcz = _conv_taps(jnp.concatenate([dy, dyn], axis=0), wv, k_w, A_HALO, r, True)
        o_ref[0] = (dg * y).astype(BF16)
        o_ref[1] = (dcz * z).astype(BF16)
        o_ref[2] = (dcz * c).astype(BF16)

        @pl.when(i == 0)
        def _():
            dw_ref[...] = jnp.zeros_like(dw_ref)

        _conv_wgrad(dw_ref, dy, xv, k_w, A_HALO, r)

    return pl.pallas_call(
        body, name=name, out_shape=[jax.ShapeDtypeStruct((3, s, d), BF16), jax.ShapeDtypeStruct((k_w, d), F32)],
        grid=(d // cw, ni),
        in_specs=[pl.BlockSpec((3, r, cw), lambda j, i: (0, i, j)),
                  pl.BlockSpec((3, A_HALO, cw), lambda j, i: (0, jnp.maximum(i * rh - 1, 0), j)),
                  pl.BlockSpec((3, A_HALO, cw), lambda j, i: (0, jnp.minimum((i + 1) * rh, last_h), j)),
                  pl.BlockSpec((r, cw), lambda j, i: (i, j)),
                  pl.BlockSpec((A_HALO, cw), lambda j, i: (jnp.minimum((i + 1) * rh, last_h), j)),
                  pl.BlockSpec((k_w, cw), lambda j, i: (0, j))],
        out_specs=[pl.BlockSpec((3, r, cw), lambda j, i: (0, i, j)), pl.BlockSpec((k_w, cw), lambda j, i: (0, j))],
        compiler_params=_params("parallel", "arbitrary"))(bcz3, bcz3, bcz3, dgated, dgated, w)


C_HALO = 32


def _c_conv_fwd(ag3, w, bias, name):
    _, s, d = ag3.shape
    r, cw, rh = _conv_tiles(s, d, C_HALO)
    k_w = w.shape[0]

    def body(m_ref, h_ref, w_ref, b_ref, o_ref):
        i = pl.program_id(0)
        y1 = m_ref[0].astype(F32) * jax.nn.sigmoid(m_ref[1].astype(F32))
        h1 = jnp.where(i == 0, 0.0, h_ref[0].astype(F32) * jax.nn.sigmoid(h_ref[1].astype(F32)))
        xv = jnp.concatenate([h1, y1], axis=0)
        o_ref[...] = _conv_taps(xv, w_ref[...], k_w, C_HALO, r, False) + b_ref[...]

    return pl.pallas_call(
        body, name=name, out_shape=jax.ShapeDtypeStruct((s, d), F32), grid=(s // r, d // cw),
        in_specs=[pl.BlockSpec((2, r, cw), lambda i, j: (0, i, j)),
                  pl.BlockSpec((2, C_HALO, cw), lambda i, j: (0, jnp.maximum(i * rh - 1, 0), j)),
                  pl.BlockSpec((k_w, cw), lambda i, j: (0, j)), pl.BlockSpec((1, cw), lambda i, j: (0, j))],
        out_specs=pl.BlockSpec((r, cw), lambda i, j: (i, j)),
        compiler_params=_params("parallel", "parallel"))(ag3, ag3, w, bias)


def _c_conv_bwd(ag3, dy2, w, name):
    _, s, d = ag3.shape
    r, cw, rh = _conv_tiles(s, d, C_HALO)
    k_w = w.shape[0]
    ni = s // r
    last_h = s // C_HALO - 1

    def body(m_ref, hp_ref, dy_ref, dyn_ref, w_ref, o_ref, dw_ref, db_ref):
        i = pl.program_id(1)
        wv = w_ref[...]
        a = m_ref[0].astype(F32)
        sg = jax.nn.sigmoid(m_ref[1].astype(F32))
        h1 = jnp.where(i == 0, 0.0, hp_ref[0].astype(F32) * jax.nn.sigmoid(hp_ref[1].astype(F32)))
        xv = jnp.concatenate([h1, a * sg], axis=0)
        dy = dy_ref[...]
        dyn = jnp.where(i == ni - 1, 0.0, dyn_ref[...])
        dy1 = _conv_taps(jnp.concatenate([dy, dyn], axis=0), wv, k_w, C_HALO, r, True)
        o_ref[0] = (dy1 * sg).astype(BF16)
        o_ref[1] = (dy1 * a * sg * (1.0 - sg)).astype(BF16)

        @pl.when(i == 0)
        def _():
            dw_ref[...] = jnp.zeros_like(dw_ref)
            db_ref[...] = jnp.zeros_like(db_ref)

        db_ref[...] += jnp.sum(dy, axis=0, keepdims=True)
        _conv_wgrad(dw_ref, dy, xv, k_w, C_HALO, r)

    return pl.pallas_call(
        body, name=name,
        out_shape=[jax.ShapeDtypeStruct((2, s, d), BF16), jax.ShapeDtypeStruct((k_w, d), F32),
                   jax.ShapeDtypeStruct((1, d), F32)],
        grid=(d // cw, ni),
        in_specs=[pl.BlockSpec((2, r, cw), lambda j, i: (0, i, j)),
                  pl.BlockSpec((2, C_HALO, cw), lambda j, i: (0, jnp.maximum(i * rh - 1, 0), j)),
                  pl.BlockSpec((r, cw), lambda j, i: (i, j)),
                  pl.BlockSpec((C_HALO, cw), lambda j, i: (jnp.minimum((i + 1) * rh, last_h), j)),
                  pl.BlockSpec((k_w, cw), lambda j, i: (0, j))],
        out_specs=[pl.BlockSpec((2, r, cw), lambda j, i: (0, i, j)), pl.BlockSpec((k_w, cw), lambda j, i: (0, j)),
                   pl.BlockSpec((1, cw), lambda j, i: (0, j))],
        compiler_params=_params("parallel", "arbitrary"))(ag3, ag3, dy2, dy2, w)


def _ln_stats(v):
    mu = jnp.mean(v, axis=-1, keepdims=True)
    vc = v - mu
    rstd = lax.rsqrt(jnp.mean(vc * vc, axis=-1, keepdims=True) + EPS)
    return vc * rstd, rstd


def _ln_bwd(dn, g, xh, rstd):
    dxh = dn * g
    return rstd * (dxh - jnp.mean(dxh, axis=-1, keepdims=True) - xh * jnp.mean(dxh * xh, axis=-1, keepdims=True))


def _c_ln_fwd(y2, g, b, name):
    s, d = y2.shape
    r = _row_tile(s)

    def body(y_ref, g_ref, b_ref, o_ref):
        xh, _ = _ln_stats(y_ref[...])
        y3 = xh * g_ref[...] + b_ref[...]
        o_ref[...] = (y3 * jax.nn.sigmoid(y3)).astype(BF16)

    row = pl.BlockSpec((r, d), lambda i: (i, 0))
    vec = pl.BlockSpec((1, d), lambda i: (0, 0))
    return pl.pallas_call(
        body, name=name, out_shape=jax.ShapeDtypeStruct((s, d), BF16), grid=(s // r,), in_specs=[row, vec, vec],
        out_specs=row, compiler_params=_params("parallel"))(y2, g, b)


def _c_ln_bwd(y2, dout, g, b, name):
    s, d = y2.shape
    r = _row_tile(s)

    def body(y_ref, do_ref, g_ref, b_ref, dy_ref, dg_ref, db_ref):
        i = pl.program_id(0)
        xh, rstd = _ln_stats(y_ref[...])
        gv = g_ref[...]
        y3 = xh * gv + b_ref[...]
        sg = jax.nn.sigmoid(y3)
        dy3 = do_ref[...].astype(F32) * (sg + y3 * sg * (1.0 - sg))
        dy_ref[...] = _ln_bwd(dy3, gv, xh, rstd)

        @pl.when(i == 0)
        def _():
            dg_ref[...] = jnp.zeros_like(dg_ref)
            db_ref[...] = jnp.zeros_like(db_ref)

        dg_ref[...] += jnp.sum(dy3 * xh, axis=0, keepdims=True)
        db_ref[...] += jnp.sum(dy3, axis=0, keepdims=True)

    row = pl.BlockSpec((r, d), lambda i: (i, 0))
    vec = pl.BlockSpec((1, d), lambda i: (0, 0))
    return pl.pallas_call(
        body, name=name,
        out_shape=[jax.ShapeDtypeStruct((s, d), F32), jax.ShapeDtypeStruct((1, d), F32), jax.ShapeDtypeStruct((1, d), F32)],
        grid=(s // r,), in_specs=[row, row, vec, vec], out_specs=[row, vec, vec],
        compiler_params=_params("arbitrary"))(y2, dout, g, b)


_GELU_C = 0.7978845608028654
_GELU_A = 0.044715


def _gelu(x):
    return 0.5 * x * (1.0 + jnp.tanh(_GELU_C * (x + _GELU_A * x * x * x)))


def _gelu_grad(x):
    t = jnp.tanh(_GELU_C * (x + _GELU_A * x * x * x))
    return 0.5 * (1.0 + t) + 0.5 * x * (1.0 - t * t) * _GELU_C * (1.0 + 3.0 * _GELU_A * x * x)


def _b_mid_fwd(uv3, vg, vb, ws_m, sbt, name):
    _, s, h = uv3.shape
    g_n, t, _ = ws_m.shape
    gd = h // g_n

    def body(uv_ref, vg_ref, vb_ref, ws_ref, sb_ref, o_ref):
        u = _gelu(uv_ref[0].astype(F32))
        xh, _ = _ln_stats(_gelu(uv_ref[1].astype(F32)))
        vn = (xh * vg_ref[...] + vb_ref[...]).astype(BF16)
        for g in range(g_n):
            sl = slice(g * gd, (g + 1) * gd)
            sv = jnp.dot(ws_ref[g], vn[:, sl], preferred_element_type=F32) + sb_ref[:, g:g + 1]
            o_ref[:, sl] = (u[:, sl] * sv).astype(BF16)

    vec = pl.BlockSpec((1, h), lambda i: (0, 0))
    return pl.pallas_call(
        body, name=name, out_shape=jax.ShapeDtypeStruct((s, h), BF16), grid=(s // t,),
        in_specs=[pl.BlockSpec((2, t, h), lambda i: (0, i, 0)), vec, vec,
                  pl.BlockSpec((g_n, t, t), lambda i: (0, 0, 0)), pl.BlockSpec((t, 128), lambda i: (0, 0))],
        out_specs=pl.BlockSpec((t, h), lambda i: (i, 0)), compiler_params=_params("parallel"))(uv3, vg, vb, ws_m, sbt)


def _b_mid_bwd(uv3, dgated, vg, vb, ws_m, sbt, name):
    _, s, h = uv3.shape
    g_n, t, _ = ws_m.shape
    gd = h // g_n

    def body(uv_ref, dg_ref, vg_ref, vb_ref, ws_ref, sb_ref, o_ref, dvg_ref, dvb_ref, dws_ref, dsb_ref, dvn_ref):
        i = pl.program_id(0)

        @pl.when(i == 0)
        def _():
            dvg_ref[...] = jnp.zeros_like(dvg_ref)
            dvb_ref[...] = jnp.zeros_like(dvb_ref)
            dws_ref[...] = jnp.zeros_like(dws_ref)
            dsb_ref[...] = jnp.zeros_like(dsb_ref)

        upre = uv_ref[0].astype(F32)
        vpre = uv_ref[1].astype(F32)
        u = _gelu(upre)
        xh, rstd = _ln_stats(_gelu(vpre))
        gv = vg_ref[...]
        vn = (xh * gv + vb_ref[...]).astype(BF16)
        causal = lax.broadcasted_iota(jnp.int32, (t, t), 0) >= lax.broadcasted_iota(jnp.int32, (t, t), 1)
        lane = lax.broadcasted_iota(jnp.int32, (t, 128), 1)
        for g in range(g_n):
            sl = slice(g * gd, (g + 1) * gd)
            wsg = ws_ref[g]
            sv = jnp.dot(wsg, vn[:, sl], preferred_element_type=F32) + sb_ref[:, g:g + 1]
            dg = dg_ref[:, sl].astype(F32)
            o_ref[0, :, sl] = (dg * sv * _gelu_grad(upre[:, sl])).astype(BF16)
            dsv = dg * u[:, sl]
            dsvb = dsv.astype(BF16)
            dsb_ref[...] += jnp.where(lane == g, jnp.sum(dsv, axis=1, keepdims=True), 0.0)
            dws = lax.dot_general(dsvb, vn[:, sl], _DIMS["nt"], preferred_element_type=F32)
            dws_ref[g] += jnp.where(causal, dws, 0.0)
            dvn_ref[:, sl] = lax.dot_general(wsg, dsvb, _DIMS["tn"], preferred_element_type=F32)
        dvn = dvn_ref[...]
        dvg_ref[...] += jnp.sum(dvn * xh, axis=0, keepdims=True)
        dvb_ref[...] += jnp.sum(dvn, axis=0, keepdims=True)
        o_ref[1] = (_ln_bwd(dvn, gv, xh, rstd) * _gelu_grad(vpre)).astype(BF16)

    vec = pl.BlockSpec((1, h), lambda i: (0, 0))
    return pl.pallas_call(
        body, name=name,
        out_shape=[jax.ShapeDtypeStruct((2, s, h), BF16), jax.ShapeDtypeStruct((1, h), F32), jax.ShapeDtypeStruct((1, h), F32),
                   jax.ShapeDtypeStruct((g_n, t, t), F32), jax.ShapeDtypeStruct((t, 128), F32)],
        grid=(s // t,),
        in_specs=[pl.BlockSpec((2, t, h), lambda i: (0, i, 0)), pl.BlockSpec((t, h), lambda i: (i, 0)), vec, vec,
                  pl.BlockSpec((g_n, t, t), lambda i: (0, 0, 0)), pl.BlockSpec((t, 128), lambda i: (0, 0))],
        out_specs=[pl.BlockSpec((2, t, h), lambda i: (0, i, 0)), vec, vec,
                   pl.BlockSpec((g_n, t, t), lambda i: (0, 0, 0)), pl.BlockSpec((t, 128), lambda i: (0, 0))],
        scratch_shapes=[pltpu.VMEM((t, h), F32)],
        compiler_params=_params("arbitrary"))(uv3, dgated, vg, vb, ws_m, sbt)


def _softmax_rows(sc):
    e = jnp.exp(sc - jnp.max(sc, axis=-1, keepdims=True))
    return e / jnp.sum(e, axis=-1, keepdims=True)


def _attn_fwd(q, kv3, name):
    s, d = q.shape
    m = kv3.shape[1]
    dh = d // XA_HEADS
    scale = dh ** -0.5
    r = _row_tile(s)

    def body(q_ref, kv_ref, o_ref):
        for hd in range(XA_HEADS):
            sl = slice(hd * dh, (hd + 1) * dh)
            sc = lax.dot_general(q_ref[:, sl], kv_ref[0, :, sl], _DIMS["nt"], preferred_element_type=F32) * scale
            p = _softmax_rows(sc).astype(BF16)
            o_ref[:, sl] = jnp.dot(p, kv_ref[1, :, sl], preferred_element_type=F32).astype(BF16)

    return pl.pallas_call(
        body, name=name, out_shape=jax.ShapeDtypeStruct((s, d), BF16), grid=(s // r,),
        in_specs=[pl.BlockSpec((r, d), lambda i: (i, 0)), pl.BlockSpec((2, m, d), lambda i: (0, 0, 0))],
        out_specs=pl.BlockSpec((r, d), lambda i: (i, 0)), compiler_params=_params("parallel"))(q, kv3)


def _attn_bwd(q, kv3, do, name):
    s, d = q.shape
    m = kv3.shape[1]
    dh = d // XA_HEADS
    scale = dh ** -0.5
    r = _row_tile(s)

    def body(q_ref, kv_ref, do_ref, dq_ref, dkv_ref):
        i = pl.program_id(0)

        @pl.when(i == 0)
        def _():
            dkv_ref[...] = jnp.zeros_like(dkv_ref)

        for hd in range(XA_HEADS):
            sl = slice(hd * dh, (hd + 1) * dh)
            qh = q_ref[:, sl]
            kh = kv_ref[0, :, sl]
            doh = do_ref[:, sl]
            sc = lax.dot_general(qh, kh, _DIMS["nt"], preferred_element_type=F32) * scale
            p = _softmax_rows(sc)
            pb = p.astype(BF16)
            dkv_ref[1, :, sl] += lax.dot_general(pb, doh, _DIMS["tn"], preferred_element_type=F32)
            dp = lax.dot_general(doh, kv_ref[1, :, sl], _DIMS["nt"], preferred_element_type=F32)
            ds = (p * (dp - jnp.sum(dp * p, axis=-1, keepdims=True)) * scale).astype(BF16)
            dq_ref[:, sl] = jnp.dot(ds, kh, preferred_element_type=F32).astype(BF16)
            dkv_ref[0, :, sl] += lax.dot_general(ds, qh, _DIMS["tn"], preferred_element_type=F32)

    row = pl.BlockSpec((r, d), lambda i: (i, 0))
    kvs = pl.BlockSpec((2, m, d), lambda i: (0, 0, 0))
    return pl.pallas_call(
        body, name=name, out_shape=[jax.ShapeDtypeStruct((s, d), BF16), jax.ShapeDtypeStruct((2, m, d), F32)],
        grid=(s // r,), in_specs=[row, kvs, row], out_specs=[row, kvs], compiler_params=_params("arbitrary"))(q, kv3, do)


FFN_COL_TILE = 512


def _ffn_gu_fwd(h, w_gu, name):
    s, d = h.shape
    f = w_gu.shape[2] // 2
    tm = _div_tile(s, MM_OUT_TILE_CAP)
    tn = _div_tile(f, FFN_COL_TILE)
    nj = f // tn

    def body(a_ref, bg_ref, bu_ref, gu_ref, act_ref):
        a = a_ref[...]
        gate = jnp.dot(a, bg_ref[...], preferred_element_type=F32)
        up = jnp.dot(a, bu_ref[...], preferred_element_type=F32)
        gu_ref[0] = gate.astype(BF16)
        gu_ref[1] = up.astype(BF16)
        act_ref[...] = (gate * jax.nn.sigmoid(gate) * up).astype(BF16)

    return pl.pallas_call(
        body, name=name, out_shape=[jax.ShapeDtypeStruct((2, s, f), BF16), jax.ShapeDtypeStruct((s, f), BF16)],
        grid=(s // tm, nj),
        in_specs=[pl.BlockSpec((tm, d), lambda i, j: (i, 0)), pl.BlockSpec((None, d, tn), lambda i, j: (0, 0, j)),
                  pl.BlockSpec((None, d, tn), lambda i, j: (0, 0, j + nj))],
        out_specs=[pl.BlockSpec((2, tm, tn), lambda i, j: (0, i, j)), pl.BlockSpec((tm, tn), lambda i, j: (i, j))],
        compiler_params=_params("parallel", "parallel"))(h, w_gu, w_gu)


def _ffn_down_bwd(dy, w_down, gu3, name):
    s, d = dy.shape
    f = w_down.shape[1]
    tm = _div_tile(s, MM_OUT_TILE_CAP)
    tn = _div_tile(f, FFN_COL_TILE)

    def body(dy_ref, w_ref, gu_ref, o_ref):
        da = lax.dot_general(dy_ref[...], w_ref[...], _DIMS["nt"], preferred_element_type=F32)
        gate = gu_ref[0].astype(F32)
        up = gu_ref[1].astype(F32)
        sg = jax.nn.sigmoid(gate)
        o_ref[0] = (da * up * (sg + gate * sg * (1.0 - sg))).astype(BF16)
        o_ref[1] = (da * gate * sg).astype(BF16)

    return pl.pallas_call(
        body, name=name, out_shape=jax.ShapeDtypeStruct((2, s, f), BF16), grid=(s // tm, f // tn),
        in_specs=[pl.BlockSpec((tm, d), lambda i, j: (i, 0)), pl.BlockSpec((None, tn, d), lambda i, j: (0, j, 0)),
                  pl.BlockSpec((2, tm, tn), lambda i, j: (0, i, j))],
        out_specs=pl.BlockSpec((2, tm, tn), lambda i, j: (0, i, j)),
        compiler_params=_params("parallel", "parallel"))(dy, w_down, gu3)


def _ids():
    x, y, c = lax.axis_index("x"), lax.axis_index("y"), lax.axis_index("c")
    return x, y, c, 2 * x + y


def _chip_peers(x, y):
    return [(d - 1, 2 * (x ^ (d >> 1)) + (y ^ (d & 1)), x ^ (d >> 1), y ^ (d & 1)) for d in (1, 2, 3)]


def _remote(src, dst, ssem, rsem, dev):
    return pltpu.make_async_remote_copy(src_ref=src, dst_ref=dst, send_sem=ssem, recv_sem=rsem, device_id=dev,
                                        device_id_type=MESH)


def _gview(ref, kind, j, cc):
    _, k, n = ref.shape
    if kind == "row":
        return ref.at[:, pl.ds(j * (k // N_CHIPS) + cc * (k // (2 * N_CHIPS)), k // (2 * N_CHIPS)), :]
    return ref.at[:, pl.ds(cc * (k // 2), k // 2), pl.ds(j * (n // N_CHIPS), n // N_CHIPS)]


def _sview(ref, cc):
    r = ref.shape[1]
    return ref.at[:, pl.ds(cc * (r // 2), r // 2), :]


def _comm_call(body, name, ins, out_shapes, n_sems, aliases=None):
    return pl.pallas_call(
        body, name=name, out_shape=out_shapes, in_specs=[HBM] * len(ins), out_specs=[HBM] * len(out_shapes),
        scratch_shapes=[pltpu.SemaphoreType.DMA((n,)) for n in n_sems], input_output_aliases=aliases or {},
        compiler_params=pltpu.CompilerParams(has_side_effects=True))(*ins)


def _mesh_scalars():
    x, y, c = lax.axis_index("x"), lax.axis_index("y"), lax.axis_index("c")
    return jnp.stack([2 * x + y, c]).astype(jnp.int32)


def _slab_rows(rows, cols, itemsize=4):
    best = None
    for cand in range(16, rows + 1, 16):
        if rows % cand == 0 and cand * cols * itemsize <= (2 << 20):
            best = cand
    return best or rows


def _ag_place(shard, layer, kind, name, dep=None):
    deps = [] if dep is None else [dep]
    _, r, n = shard.shape
    full = (1, r * N_CHIPS, n) if kind == "row" else (1, r, n * N_CHIPS)
    tr = _slab_rows(r, n)
    nt = r // tr
    if kind == "row":
        out_spec = pl.BlockSpec((None, tr, n), lambda t, s: (0, s[0] * nt + t, 0))
    else:
        out_spec = pl.BlockSpec((None, tr, n), lambda t, s: (0, t, s[0]))

    def body(s_ref, i_ref, *rest):
        rest[-1][...] = i_ref[...].astype(BF16)

    return pl.pallas_call(
        body, name=name, out_shape=jax.ShapeDtypeStruct(full, BF16),
        grid_spec=pltpu.PrefetchScalarGridSpec(
            num_scalar_prefetch=1, grid=(nt,),
            in_specs=[pl.BlockSpec((None, tr, n), lambda t, s: (layer, t, 0))] + [pl.BlockSpec(memory_space=pl.ANY)] * len(deps),
            out_specs=out_spec),
        compiler_params=_params("parallel"))(_mesh_scalars(), shard, *deps)


SEM = pl.BlockSpec(memory_space=pltpu.SEMAPHORE)
ANY = pl.BlockSpec(memory_space=pl.ANY)
DATAFLOW = pltpu.SideEffectType.DATAFLOW_SIDE_EFFECTING


def _in_hbm(arrs):
    return [pltpu.with_memory_space_constraint(a, pltpu.HBM) for a in arrs]


def _ag_start(bufs, kinds, name, after=None):
    n = len(bufs)
    afters = [] if after is None else [after]
    n_in = n + len(afters)

    def body(*refs):
        ssem, rsem, token = refs[n_in], refs[n_in + 1], refs[-1]
        x, y, c, me = _ids()
        for t in range(n):
            mine = _gview(refs[t], kinds[t], me, c)
            for d, _, px, py in _chip_peers(x, y):
                _remote(mine, mine, ssem.at[3 * t + d], rsem.at[3 * t + d], (px, py, c)).start()
        token[...] = jnp.zeros_like(token)

    outs = pl.pallas_call(
        body, name=name,
        out_shape=(pltpu.SemaphoreType.DMA((3 * n,)), pltpu.SemaphoreType.DMA((3 * n,)),
                   *[pltpu.HBM(b.shape, b.dtype) for b in bufs], jax.ShapeDtypeStruct((8, 128), F32)),
        in_specs=[HBM] * n + [ANY] * len(afters), out_specs=(SEM, SEM, *[HBM] * n, pl.BlockSpec(memory_space=pltpu.VMEM)),
        input_output_aliases={t: 2 + t for t in range(n)},
        compiler_params=pltpu.CompilerParams(has_side_effects=DATAFLOW))(*_in_hbm(bufs), *afters)
    return outs[0], outs[1], list(outs[2:2 + n]), outs[-1]


def _ag_wait(ssem, rsem, bufs, kinds, after, name):
    n = len(bufs)

    def body(*refs):
        ssem_ref, rsem_ref = refs[n], refs[n + 1]
        x, y, c, me = _ids()
        for t in range(n):
            mine = _gview(refs[t], kinds[t], me, c)
            for d, pj, px, py in _chip_peers(x, y):
                theirs = _gview(refs[t], kinds[t], pj, c)
                _remote(mine, mine, ssem_ref.at[3 * t + d], rsem_ref.at[3 * t + d], (px, py, c)).wait_send()
                _remote(theirs, theirs, ssem_ref.at[3 * t + d], rsem_ref.at[3 * t + d], (px, py, c)).wait_recv()

    outs = pl.pallas_call(
        body, name=name, out_shape=[pltpu.HBM(b.shape, b.dtype) for b in bufs],
        in_specs=[HBM] * n + [SEM, SEM, ANY], out_specs=[HBM] * n, input_output_aliases={t: t for t in range(n)},
        compiler_params=pltpu.CompilerParams(has_side_effects=DATAFLOW))(*bufs, ssem, rsem, after)
    return list(outs)


def _ag_forward(bufs, kinds, name):
    n = len(bufs)

    def body(*refs):
        outs = refs[n:2 * n]
        ssem, rsem = refs[2 * n], refs[2 * n + 1]
        x, y, c, _ = _ids()
        sib = (x, y, 1 - c)
        sends = []
        for t in range(n):
            for d, pj, _, _ in _chip_peers(x, y):
                piece = _gview(outs[t], kinds[t], pj, c)
                sends.append(_remote(piece, piece, ssem.at[3 * t + d], rsem.at[3 * t + d], sib))
        for cp in sends:
            cp.start()
        for t in range(n):
            for d, pj, _, _ in _chip_peers(x, y):
                piece = _gview(outs[t], kinds[t], pj, 1 - c)
                _remote(piece, piece, ssem.at[3 * t + d], rsem.at[3 * t + d], sib).wait_recv()
        for cp in sends:
            cp.wait_send()

    return _comm_call(body, name, bufs, [jax.ShapeDtypeStruct(b.shape, b.dtype) for b in bufs], (3 * n, 3 * n),
                      {t: t for t in range(n)})


def _rs1(g_fulls, kinds, name, dep=None):
    n = len(g_fulls)
    outs = []
    for g, kind in zip(g_fulls, kinds):
        l, k, nn = g.shape
        piece = (l, k // (2 * N_CHIPS), nn) if kind == "row" else (l, k // 2, nn // N_CHIPS)
        outs.append(jax.ShapeDtypeStruct((N_CHIPS,) + piece, g.dtype))

    n_in = n + (dep is not None)

    def body(*refs):
        ssem, rsem = refs[n_in + n], refs[n_in + n + 1]
        x, y, c, _ = _ids()
        sends = [_remote(_gview(refs[t], kinds[t], j, 1 - c), refs[n_in + t].at[j], ssem.at[4 * t + j], rsem.at[4 * t + j],
                         (x, y, 1 - c)) for t in range(n) for j in range(N_CHIPS)]
        for cp in sends:
            cp.start()
        for cp in sends:
            cp.wait()

    return _comm_call(body, name, g_fulls + ([] if dep is None else [dep]), outs, (4 * n, 4 * n))


def _rs1_pieces(g_fulls, kinds):
    shapes = []
    for g, kind in zip(g_fulls, kinds):
        l, k, nn = g.shape
        shapes.append((N_CHIPS,) + ((l, k // (2 * N_CHIPS), nn) if kind == "row" else (l, k // 2, nn // N_CHIPS)))
    return shapes


def _rs1_start(g_fulls, kinds, name, after=None):
    n = len(g_fulls)
    lands = [lax.empty(s, g.dtype) for s, g in zip(_rs1_pieces(g_fulls, kinds), g_fulls)]
    afters = [] if after is None else [after]
    n_in = 2 * n + len(afters)

    def body(*refs):
        ssem, rsem, token = refs[n_in], refs[n_in + 1], refs[-1]
        x, y, c, _ = _ids()
        for t in range(n):
            for j in range(N_CHIPS):
                _remote(_gview(refs[t], kinds[t], j, 1 - c), refs[n + t].at[j], ssem.at[4 * t + j], rsem.at[4 * t + j],
                        (x, y, 1 - c)).start()
        token[...] = jnp.zeros_like(token)

    outs = pl.pallas_call(
        body, name=name,
        out_shape=(pltpu.SemaphoreType.DMA((4 * n,)), pltpu.SemaphoreType.DMA((4 * n,)),
                   *[pltpu.HBM(a.shape, a.dtype) for a in g_fulls + lands], jax.ShapeDtypeStruct((8, 128), F32)),
        in_specs=[HBM] * (2 * n) + [ANY] * len(afters),
        out_specs=(SEM, SEM, *[HBM] * (2 * n), pl.BlockSpec(memory_space=pltpu.VMEM)),
        input_output_aliases={t: 2 + t for t in range(2 * n)},
        compiler_params=pltpu.CompilerParams(has_side_effects=DATAFLOW))(*_in_hbm(g_fulls + lands), *afters)
    return outs[0], outs[1], list(outs[2:2 + n]), list(outs[2 + n:2 + 2 * n]), outs[-1]


def _rs1_wait(ssem, rsem, g_fulls, lands, kinds, after, name):
    n = len(g_fulls)

    def body(*refs):
        ssem_ref, rsem_ref = refs[2 * n], refs[2 * n + 1]
        x, y, c, _ = _ids()
        for t in range(n):
            for j in range(N_CHIPS):
                _remote(_gview(refs[t], kinds[t], j, 1 - c), refs[n + t].at[j], ssem_ref.at[4 * t + j], rsem_ref.at[4 * t + j],
                        (x, y, 1 - c)).wait()

    outs = pl.pallas_call(
        body, name=name, out_shape=[pltpu.HBM(a.shape, a.dtype) for a in g_fulls + lands],
        in_specs=[HBM] * (2 * n) + [SEM, SEM, ANY], out_specs=[HBM] * (2 * n),
        input_output_aliases={t: t for t in range(2 * n)},
        compiler_params=pltpu.CompilerParams(has_side_effects=DATAFLOW))(*g_fulls, *lands, ssem, rsem, after)
    return list(outs[:n]), list(outs[n:])


def _rs_add1(g_full, got, kind, name):
    l, k, n = g_full.shape
    _, _, pr, pc = got.shape
    tr = _slab_rows(pr, pc, 2)
    nt = pr // tr
    if kind == "row":
        g_spec = pl.BlockSpec((None, tr, n), lambda j, li, t, s: (li, (2 * j + s[1]) * nt + t, 0))
    else:
        g_spec = pl.BlockSpec((None, tr, pc), lambda j, li, t, s: (li, s[1] * nt + t, j))
    slot = pl.BlockSpec((None, None, tr, pc), lambda j, li, t, s: (j, li, t, 0))

    def body(s_ref, g_ref, got_ref, o_ref):
        o_ref[...] = g_ref[...] + got_ref[...]

    return pl.pallas_call(
        body, name=name, out_shape=jax.ShapeDtypeStruct(got.shape, BF16),
        grid_spec=pltpu.PrefetchScalarGridSpec(num_scalar_prefetch=1, grid=(N_CHIPS, l, nt), in_specs=[g_spec, slot],
                                               out_specs=slot),
        compiler_params=_params("parallel", "parallel", "parallel"))(_mesh_scalars(), g_full, got)


def _rs2_start(ps, name):
    n = len(ps)
    lands = [lax.empty(p.shape, p.dtype) for p in ps]

    def body(*refs):
        ssem, rsem, token = refs[2 * n], refs[2 * n + 1], refs[-1]
        x, y, c, me = _ids()
        for t in range(n):
            for d, pj, px, py in _chip_peers(x, y):
                _remote(refs[t].at[pj], refs[n + t].at[me], ssem.at[3 * t + d], rsem.at[3 * t + d], (px, py, c)).start()
        token[...] = jnp.zeros_like(token)

    outs = pl.pallas_call(
        body, name=name,
        out_shape=(pltpu.SemaphoreType.DMA((3 * n,)), pltpu.SemaphoreType.DMA((3 * n,)),
                   *[pltpu.HBM(p.shape, p.dtype) for p in ps + lands], jax.ShapeDtypeStruct((8, 128), F32)),
        in_specs=[HBM] * (2 * n), out_specs=(SEM, SEM, *[HBM] * (2 * n), pl.BlockSpec(memory_space=pltpu.VMEM)),
        input_output_aliases={t: 2 + t for t in range(2 * n)},
        compiler_params=pltpu.CompilerParams(has_side_effects=DATAFLOW))(*_in_hbm(ps + lands))
    return outs[0], outs[1], list(outs[2:2 + n]), list(outs[2 + n:2 + 2 * n]), outs[-1]


def _rs2_wait(ssem, rsem, ps, lands, after, name):
    n = len(ps)

    def body(*refs):
        ssem_ref, rsem_ref = refs[2 * n], refs[2 * n + 1]
        x, y, c, me = _ids()
        for t in range(n):
            for d, pj, px, py in _chip_peers(x, y):
                _remote(refs[t].at[pj], refs[n + t].at[me], ssem_ref.at[3 * t + d], rsem_ref.at[3 * t + d], (px, py, c)).wait_send()
                _remote(refs[t].at[pj], refs[n + t].at[pj], ssem_ref.at[3 * t + d], rsem_ref.at[3 * t + d], (px, py, c)).wait_recv()

    outs = pl.pallas_call(
        body, name=name, out_shape=[pltpu.HBM(p.shape, p.dtype) for p in ps + lands],
        in_specs=[HBM] * (2 * n) + [SEM, SEM, ANY], out_specs=[HBM] * (2 * n),
        input_output_aliases={t: t for t in range(2 * n)},
        compiler_params=pltpu.CompilerParams(has_side_effects=DATAFLOW))(*ps, *lands, ssem, rsem, after)
    return list(outs[:n]), list(outs[n:])


def _rs_add2(p, got, into, layer, name):
    _, _, pr, pc = p.shape
    tr = _slab_rows(pr, pc)
    nt = pr // tr

    def slot(d):
        return pl.BlockSpec((None, None, tr, pc), lambda t, s: (s[0] ^ d, 0, t, 0))

    def body(s_ref, p_ref, g1_ref, g2_ref, g3_ref, i_ref, o_ref):
        o_ref[...] = (p_ref[...].astype(F32) + g1_ref[...].astype(F32) + g2_ref[...].astype(F32) + g3_ref[...].astype(F32))

    return pl.pallas_call(
        body, name=name, out_shape=jax.ShapeDtypeStruct(into.shape, F32),
        grid_spec=pltpu.PrefetchScalarGridSpec(
            num_scalar_prefetch=1, grid=(nt,), in_specs=[slot(0), slot(1), slot(2), slot(3), HBM],
            out_specs=pl.BlockSpec((None, tr, pc), lambda t, s: (layer, s[1] * nt + t, 0))),
        input_output_aliases={5: 0},
        compiler_params=_params("parallel"))(_mesh_scalars(), p, got, got, got, into)


def _rs3(shards, layers, name):
    n = len(shards)

    def body(*refs):
        outs = refs[n:2 * n]
        token, ssem, rsem = refs[2 * n], refs[2 * n + 1], refs[2 * n + 2]
        x, y, c, _ = _ids()
        sib = (x, y, 1 - c)
        token[...] = jnp.zeros_like(token)

        def half(t, cc):
            return _sview(outs[t].at[pl.ds(layers[t], 1)], cc)

        sends = [_remote(half(t, c), half(t, c), ssem.at[t], rsem.at[t], sib) for t in range(n)]
        for cp in sends:
            cp.start()
        for t in range(n):
            _remote(half(t, 1 - c), half(t, 1 - c), ssem.at[t], rsem.at[t], sib).wait_recv()
        for cp in sends:
            cp.wait_send()

    outs = pl.pallas_call(
        body, name=name, out_shape=[jax.ShapeDtypeStruct(s.shape, s.dtype) for s in shards] + [jax.ShapeDtypeStruct((8, 128), F32)],
        in_specs=[HBM] * n, out_specs=[HBM] * n + [pl.BlockSpec(memory_space=pltpu.VMEM)],
        scratch_shapes=[pltpu.SemaphoreType.DMA((n,)), pltpu.SemaphoreType.DMA((n,))],
        input_output_aliases={t: t for t in range(n)}, compiler_params=pltpu.CompilerParams(has_side_effects=True))(*shards)
    return list(outs[:n]), outs[n]


def _ag_small(sp, name):
    def body(s_ref, o_ref, ssem, rsem, lsem):
        x, y, c, me = _ids()
        local = pltpu.make_async_copy(s_ref, o_ref.at[me], lsem.at[0])
        local.start()
        sends = [_remote(s_ref, o_ref.at[me], ssem.at[d], rsem.at[d], (px, py, c)) for d, _, px, py in _chip_peers(x, y)]
        for cp in sends:
            cp.start()
        for d, pj, px, py in _chip_peers(x, y):
            _remote(s_ref, o_ref.at[pj], ssem.at[d], rsem.at[d], (px, py, c)).wait_recv()
        for cp in sends:
            cp.wait_send()
        local.wait()

    return _comm_call(body, name, [sp], [jax.ShapeDtypeStruct((N_CHIPS,) + sp.shape, sp.dtype)], (3, 3, 1))[0]


def _slot_place(g, name):
    rows, cols = g.shape
    x, y, c = lax.axis_index("x"), lax.axis_index("y"), lax.axis_index("c")
    slot = (4 * x + 2 * y + c).astype(jnp.int32).reshape(1)

    def body(s_ref, i_ref, o_ref):
        o_ref[...] = i_ref[...]

    return pl.pallas_call(
        body, name=name, out_shape=jax.ShapeDtypeStruct((8, rows, cols), g.dtype),
        grid_spec=pltpu.PrefetchScalarGridSpec(
            num_scalar_prefetch=1, grid=(1,), in_specs=[pl.BlockSpec((rows, cols), lambda t, s: (0, 0))],
            out_specs=pl.BlockSpec((None, rows, cols), lambda t, s: (s[0], 0, 0))),
        compiler_params=_params("arbitrary"))(slot, g)


def _gather8_peers(x, y, c):
    return [(d - 1, x ^ (d >> 2), y ^ ((d >> 1) & 1), c ^ (d & 1)) for d in range(1, 8)]


def _gather8_start(buf, name):
    def body(b_ref, ssem, rsem, b_thru, token):
        x, y, c, _ = _ids()
        mine = b_ref.at[4 * x + 2 * y + c]
        for d, px, py, pc in _gather8_peers(x, y, c):
            _remote(mine, mine, ssem.at[d], rsem.at[d], (px, py, pc)).start()
        token[...] = jnp.zeros_like(token)

    outs = pl.pallas_call(
        body, name=name,
        out_shape=(pltpu.SemaphoreType.DMA((7,)), pltpu.SemaphoreType.DMA((7,)), pltpu.HBM(buf.shape, buf.dtype),
                   jax.ShapeDtypeStruct((8, 128), F32)),
        in_specs=[HBM], out_specs=(SEM, SEM, HBM, pl.BlockSpec(memory_space=pltpu.VMEM)), input_output_aliases={0: 2},
        compiler_params=pltpu.CompilerParams(has_side_effects=DATAFLOW))(*_in_hbm([buf]))
    return outs


def _gather8_wait(ssem, rsem, buf, after, name):
    def body(b_ref, ssem_ref, rsem_ref, after_ref, b_out):
        x, y, c, _ = _ids()
        mine = b_ref.at[4 * x + 2 * y + c]
        for d, px, py, pc in _gather8_peers(x, y, c):
            theirs = b_ref.at[4 * px + 2 * py + pc]
            _remote(mine, mine, ssem_ref.at[d], rsem_ref.at[d], (px, py, pc)).wait_send()
            _remote(theirs, theirs, ssem_ref.at[d], rsem_ref.at[d], (px, py, pc)).wait_recv()

    return pl.pallas_call(
        body, name=name, out_shape=pltpu.HBM(buf.shape, buf.dtype), in_specs=[HBM, SEM, SEM, ANY], out_specs=HBM,
        input_output_aliases={0: 0},
        compiler_params=pltpu.CompilerParams(has_side_effects=DATAFLOW))(buf, ssem, rsem, after)


def _sum_slots(a, out_dtype, name):
    n = a.shape[0]
    shape = a.shape[1:]
    cols = shape[-1]
    a3 = a.reshape(n, -1, cols)
    rows = a3.shape[1]
    tr = rows
    for cand in (512, 256, 128, 64, 32, 16):
        if rows % cand == 0 and cand * cols * 4 <= (1 << 20):
            tr = cand
            break

    def body(a_ref, o_ref):
        acc = a_ref[0].astype(F32)
        for j in range(1, n):
            acc = acc + a_ref[j].astype(F32)
        o_ref[...] = acc.astype(o_ref.dtype)

    out = pl.pallas_call(
        body, name=name, out_shape=jax.ShapeDtypeStruct((rows, cols), out_dtype), grid=(rows // tr,),
        in_specs=[pl.BlockSpec((n, tr, cols), lambda i: (0, i, 0))], out_specs=pl.BlockSpec((tr, cols), lambda i: (i, 0)),
        compiler_params=_params("parallel"))(a3)
    return out.reshape(shape)


def kernel(x, mem, mix_norm, xa_norm, xa_wq, xa_wkv, xa_wo, ffn_norm, ffn_w_gu, ffn_w_down, a_w_in, a_conv_w, a_w_out, b_w_in, b_v_g, b_v_b, b_w_s, b_s_bias, b_w_out, c_w_in, c_conv_w, c_conv_b, c_ln_g, c_ln_b, c_w_out, loss_target, m_mix_norm, m_xa_norm, m_xa_wq, m_xa_wkv, m_xa_wo, m_ffn_norm, m_ffn_w_gu, m_ffn_w_down, m_a_w_in, m_a_conv_w, m_a_w_out, m_b_w_in, m_b_v_g, m_b_v_b, m_b_w_s, m_b_s_bias, m_b_w_out, m_c_w_in, m_c_conv_w, m_c_conv_b, m_c_ln_g, m_c_ln_b, m_c_w_out, v_mix_norm, v_xa_norm, v_xa_wq, v_xa_wkv, v_xa_wo, v_ffn_norm, v_ffn_w_gu, v_ffn_w_down, v_a_w_in, v_a_conv_w, v_a_w_out, v_b_w_in, v_b_v_g, v_b_v_b, v_b_w_s, v_b_s_bias, v_b_w_out, v_c_w_in, v_c_conv_w, v_c_conv_b, v_c_ln_g, v_c_ln_b, v_c_w_out):
    given = dict(locals())
    w = {n: given[n] for n in WEIGHTS}
    depth = mix_norm.shape[0]
    s, d = x.shape[1], x.shape[2]
    n_mem = mem.shape[1]
    ds = d // N_CHIPS
    xin = x.reshape(s, d)
    memv = mem.reshape(n_mem, d)
    target = loss_target.reshape(s, d)
    me = 2 * lax.axis_index("x") + lax.axis_index("y")

    wg = {n: [None] * w[n].shape[0] for n in BIG_KINDS}

    def mixer_keys(i):
        return [("abc"[i % 3] + "_w_in", i // 3), ("abc"[i % 3] + "_w_out", i // 3)]

    def rest_keys(i):
        return [("xa_wq", i), ("xa_wkv", i), ("xa_wo", i), ("ffn_w_gu", i), ("ffn_w_down", i)]

    def ag_begin(keys, tag, after):
        kinds = [BIG_KINDS[n] for n, _ in keys]
        ssem, rsem, bufs, token = _ag_start([placed[k] for k in keys], kinds, "ag_start_" + tag, after)
        return keys, kinds, ssem, rsem, bufs, token, tag

    def ag_end(state, after):
        keys, kinds, ssem, rsem, bufs, _, tag = state
        bufs = _ag_forward(_ag_wait(ssem, rsem, bufs, kinds, after, "ag_wait_" + tag), kinds, "ag_fwd_" + tag)
        for (n, l), buf in zip(keys, bufs):
            wg[n][l] = buf

    def pad8(t):
        return jnp.pad(t, ((0, (-t.shape[0]) % 8), (0, 0)))

    small_rows = [w[n].reshape(-1, ds) for n in SMALL_SHARDED]
    counts = [t.shape[0] for t in small_rows]
    gathered = _ag_small(jnp.concatenate([pad8(t) for t in small_rows], axis=0), "ag_small")
    placed = {(n, l): _ag_place(w[n], l, BIG_KINDS[n], f"ag_place_{n}_{l}", dep=gathered) for n, l in mixer_keys(0)}
    gathered = jnp.transpose(gathered, (1, 0, 2)).reshape(-1, d)
    full, off = {}, 0
    for n, cnt in zip(SMALL_SHARDED, counts):
        full[n] = gathered[off:off + cnt].reshape(w[n].shape[:-1] + (d,))
        off += cnt + (-cnt) % 8
    t_chunk = b_w_s.shape[-1]
    tril = jnp.tril(jnp.ones((t_chunk, t_chunk), dtype=bool))

    def vec(a):
        return a.reshape(1, -1)

    def b_params(slot):
        ws_m = jnp.where(tril[None], b_w_s[slot], 0.0).astype(BF16)
        sbt = jnp.zeros((t_chunk, 128), F32).at[:, :b_s_bias.shape[1]].set(b_s_bias[slot].T)
        return vec(b_v_g[slot]), vec(b_v_b[slot]), ws_m, sbt

    saved = []
    xc = xin
    ag_groups = [(mixer_keys(0), "0m"), (rest_keys(0), "0r")] + [(mixer_keys(j) + rest_keys(j), f"{j}") for j in range(1, depth)]
    ag_state = {}

    def ag_begin_group(k, after):
        if k >= len(ag_groups):
            return None
        ag_state[k] = ag_begin(*ag_groups[k], after)
        return ag_state[k][5]

    dep = ag_begin_group(0, None)
    for n, kind in BIG_KINDS.items():
        for l in range(w[n].shape[0]):
            if (n, l) not in placed:
                placed[n, l] = dep = _ag_place(w[n], l, kind, f"ag_place_{n}_{l}", dep=dep)
    ag_end(ag_state[0], dep)
    for i in range(depth):
        kind, slot = i % 3, i // 3
        t = f"{i}"
        dep = ag_begin_group(1, wg[mixer_keys(0)[0][0]][0]) if i == 0 else ag_begin_group(i + 2, xc)
        sv = {"x0": xc}
        if i == 0:
            h = _rms_fwd(xc, vec(full["mix_norm"][i, 0]), "rms_mix_" + t)
        sv["h1"] = h
        if kind == 0:
            pre = _mm("nn", h, wg["a_w_in"], BF16, "a_in_" + t, bl=slot, o_parts=3, dep=dep)
            mid = _a_mid_fwd(pre, full["a_conv_w"][slot], "a_mid_" + t)
            y = _mm("nn", mid, wg["a_w_out"], F32, "a_out_" + t, bl=slot)
        elif kind == 1:
            pre = _mm("nn", h, wg["b_w_in"], BF16, "b_in_" + t, bl=slot, o_parts=2, dep=dep)
            mid = _b_mid_fwd(pre, *b_params(slot), "b_mid_" + t)
            y = _mm("nn", mid, wg["b_w_out"], F32, "b_out_" + t, bl=slot)
        else:
            pre = _mm("nn", h, wg["c_w_in"], BF16, "c_in_" + t, bl=slot, o_parts=2, dep=dep)
            y2 = _c_conv_fwd(pre, full["c_conv_w"][slot], vec(full["c_conv_b"][slot]), "c_conv_" + t)
            sv["cy2"] = y2
            mid = _c_ln_fwd(y2, vec(full["c_ln_g"][slot]), vec(full["c_ln_b"][slot]), "c_ln_" + t)
            y = _mm("nn", mid, wg["c_w_out"], F32, "c_out_" + t, bl=slot)
        sv.update(pre=pre, mid=mid, y1=y)
        xc, h = _res_rms_fwd(xc, y, vec(full["mix_norm"][i, 1]), vec(full["xa_norm"][i, 0]), "res_mix_" + t)

        sv["x1"] = xc
        dep = None
        if i == 0:
            ag_end(ag_state[1], xc)
            dep = ag_begin_group(2, xc)
        mem_n = _rms_fwd(memv, vec(full["xa_norm"][i, 2]), "rms_mem_" + t)
        q = _mm("nn", h, wg["xa_wq"], BF16, "xa_q_" + t, bl=i, dep=dep)
        kv3 = _mm("nn", mem_n, wg["xa_wkv"], BF16, "xa_kv_" + t, bl=i, o_parts=2)
        o = _attn_fwd(q, kv3, "attn_" + t)
        y = _mm("nn", o, wg["xa_wo"], F32, "xa_o_" + t, bl=i)
        sv.update(h2=h, mem_n=mem_n, q=q, kv3=kv3, o=o, y2=y)
        xc, h = _res_rms_fwd(xc, y, vec(full["xa_norm"][i, 1]), vec(full["ffn_norm"][i, 0]), "res_xa_" + t)

        sv["x2"] = xc
        gu3, act = _ffn_gu_fwd(h, wg["ffn_w_gu"][i], "ffn_gu_" + t)
        y = _mm("nn", act, wg["ffn_w_down"], F32, "ffn_down_" + t, bl=i)
        sv.update(h3=h, gu3=gu3, act=act, y3=y)
        xc, h = _res_rms_fwd(xc, y, vec(full["ffn_norm"][i, 1]),
                             vec(full["mix_norm"][i + 1, 0]) if i + 1 < depth else None, "res_ffn_" + t)
        saved.append(sv)
        if i + 2 in ag_state:
            ag_end(ag_state[i + 2], xc)

    loss_blk, dx = _loss(xc, target, "loss")
    loss = lax.psum(loss_blk[0, 0], ("x", "y", "c"))

    gbuf = {}
    gfin = {n: lax.empty(w[n].shape, F32) for n in BIG_KINDS}
    gsmall = {n: [None] * full[n].shape[0] for n in ("mix_norm", "xa_norm", "ffn_norm", "a_conv_w", "c_conv_w", "c_conv_b",
                                                      "c_ln_g", "c_ln_b")}
    grepl = {}

    def wgrad(name, l, a, dy, tag, b_parts=1):
        g2 = _mm("tn", a, dy, BF16, "wg_" + tag, b_parts=b_parts)
        gbuf[name, l] = g2.reshape((1,) + g2.shape)

    def rs_begin(keys, tag, dep):
        kinds = [BIG_KINDS[n] for n, _ in keys]
        gots = _rs1([gbuf[k] for k in keys], kinds, "rs1_" + tag, dep=dep)
        ps = [_rs_add1(gbuf[k], got, kind, f"rs_add1_{k[0]}_{k[1]}") for k, got, kind in zip(keys, gots, kinds)]
        ssem, rsem, ps, lands, token = _rs2_start(ps, "rs2_start_" + tag)
        return keys, ssem, rsem, ps, lands, token, tag

    def rs_end(state, after):
        keys, ssem, rsem, ps, lands, _, tag = state
        ps, lands = _rs2_wait(ssem, rsem, ps, lands, after, "rs2_wait_" + tag)
        for (n, l), p, land in zip(keys, ps, lands):
            gfin[n] = _rs_add2(p, land, gfin[n], l, f"rs_add2_{n}_{l}")
        outs, token = _rs3([gfin[n] for n, _ in keys], [l for _, l in keys], "rs3_" + tag)
        for (n, _), o in zip(keys, outs):
            gfin[n] = o
        return token

    def rs1_begin(keys, tag, after):
        kinds = [BIG_KINDS[n] for n, _ in keys]
        ssem, rsem, gs, lands, token = _rs1_start([gbuf[k] for k in keys], kinds, "rs1_start_" + tag, after)
        return keys, kinds, ssem, rsem, gs, lands, token, tag

    def rs_begin_after_rs1(state, after):
        keys, kinds, ssem, rsem, gs, lands, _, tag = state
        gs, gots = _rs1_wait(ssem, rsem, gs, lands, kinds, after, "rs1_wait_" + tag)
        ps = [_rs_add1(g, got, kind, f"rs_add1_{k[0]}_{k[1]}") for k, g, got, kind in zip(keys, gs, gots, kinds)]
        ssem, rsem, ps, lands, token = _rs2_start(ps, "rs2_start_" + tag)
        return keys, ssem, rsem, ps, lands, token, tag

    rs_state, rs_token, rs1_state = None, None, None

    for i in reversed(range(depth)):
        kind, slot = i % 3, i // 3
        t = f"{i}"
        sv = saved[i]
        dep = rs1_state[6] if rs1_state is not None else (None if rs_state is None else rs_state[5])
        dy, dg_post = _rms_bwd(sv["y3"], vec(full["ffn_norm"][i, 1]), dx, None, BF16, "rmsb_ffn_post_" + t, dep=dep)
        wgrad("ffn_w_down", i, sv["act"], dy, "ffn_down_" + t)
        dgu3 = _ffn_down_bwd(dy, wg["ffn_w_down"][i], sv["gu3"], "dg_ffn_down_" + t)
        wgrad("ffn_w_gu", i, sv["h3"], dgu3, "ffn_gu_" + t, b_parts=2)
        dh = _mm("nt", dgu3, wg["ffn_w_gu"], BF16, "dg_ffn_gu_" + t, bl=i, a_parts=2)
        dx, dg_pre = _rms_bwd(sv["x2"], vec(full["ffn_norm"][i, 0]), dh, dx, F32, "rmsb_ffn_pre_" + t)
        gsmall["ffn_norm"][i] = jnp.concatenate([dg_pre, dg_post], axis=0)
        dep = None
        if rs1_state is not None:
            rs_state, rs1_state = rs_begin_after_rs1(rs1_state, dx), None
            dep = rs_state[5]
        if i == 0:
            rs_state_f = rs_begin(rest_keys(0)[3:], "0f", None)
            dep = rs_state_f[5]
        dy, dg_post = _rms_bwd(sv["y2"], vec(full["xa_norm"][i, 1]), dx, None, BF16, "rmsb_xa_post_" + t, dep=dep)
        wgrad("xa_wo", i, sv["o"], dy, "xa_o_" + t)
        do = _mm("nt", dy, wg["xa_wo"], BF16, "dg_xa_o_" + t, bl=i)
        dq, dkv3 = _attn_bwd(sv["q"], sv["kv3"], do, "attn_b_" + t)
        wgrad("xa_wq", i, sv["h2"], dq, "xa_q_" + t)
        dh = _mm("nt", dq, wg["xa_wq"], BF16, "dg_xa_q_" + t, bl=i)
        dkv3 = dkv3.astype(BF16)
        wgrad("xa_wkv", i, sv["mem_n"], dkv3, "xa_kv_" + t, b_parts=2)
        dmem_n = _mm("nt", dkv3, wg["xa_wkv"], F32, "dg_xa_kv_" + t, bl=i, a_parts=2)
        _, dg_mem = _rms_bwd(memv, vec(full["xa_norm"][i, 2]), dmem_n, None, F32, "rmsb_mem_" + t)
        dx, dg_pre = _rms_bwd(sv["x1"], vec(full["xa_norm"][i, 0]), dh, dx, F32, "rmsb_xa_pre_" + t)
        gsmall["xa_norm"][i] = jnp.concatenate([dg_pre, dg_post, dg_mem], axis=0)
        if i == 0:
            rs_token = rs_end(rs_state, dx)
            rs_state = rs_begin(rest_keys(0)[:3], "0x", rs_token)
        dy, dg_post = _rms_bwd(sv["y1"], vec(full["mix_norm"][i, 1]), dx, None, BF16, "rmsb_mix_post_" + t,
                               dep=rs_state[5] if i == 0 else None)
        if kind == 0:
            wgrad("a_w_out", slot, sv["mid"], dy, "a_out_" + t)
            dmid = _mm("nt", dy, wg["a_w_out"], F32, "dg_a_out_" + t, bl=slot)
            dpre, dcw = _a_mid_bwd(sv["pre"], dmid, full["a_conv_w"][slot], "a_mid_b_" + t)
            gsmall["a_conv_w"][slot] = dcw
            wgrad("a_w_in", slot, sv["h1"], dpre, "a_in_" + t, b_parts=3)
            dh = _mm("nt", dpre, wg["a_w_in"], BF16, "dg_a_in_" + t, bl=slot, a_parts=3)
        elif kind == 1:
            wgrad("b_w_out", slot, sv["mid"], dy, "b_out_" + t)
            dmid = _mm("nt", dy, wg["b_w_out"], F32, "dg_b_out_" + t, bl=slot)
            dpre, dvg, dvb, dws, dsbt = _b_mid_bwd(sv["pre"], dmid, *b_params(slot), "b_mid_b_" + t)
            grepl[slot] = (dvg, dvb, dws, dsbt[:, :b_s_bias.shape[1]].T)
            wgrad("b_w_in", slot, sv["h1"], dpre, "b_in_" + t, b_parts=2)
            dh = _mm("nt", dpre, wg["b_w_in"], BF16, "dg_b_in_" + t, bl=slot, a_parts=2)
        else:
            wgrad("c_w_out", slot, sv["mid"], dy, "c_out_" + t)
            dmid = _mm("nt", dy, wg["c_w_out"], F32, "dg_c_out_" + t, bl=slot)
            dy2, dlg, dlb = _c_ln_bwd(sv["cy2"], dmid, vec(full["c_ln_g"][slot]), vec(full["c_ln_b"][slot]), "c_ln_b_" + t)
            dpre, dcw, dcb = _c_conv_bwd(sv["pre"], dy2, full["c_conv_w"][slot], "c_conv_b_" + t)
            gsmall["c_conv_w"][slot], gsmall["c_conv_b"][slot] = dcw, dcb
            gsmall["c_ln_g"][slot], gsmall["c_ln_b"][slot] = dlg, dlb
            wgrad("c_w_in", slot, sv["h1"], dpre, "c_in_" + t, b_parts=2)
            dh = _mm("nt", dpre, wg["c_w_in"], BF16, "dg_c_in_" + t, bl=slot, a_parts=2)
        dx, dg_pre = _rms_bwd(sv["x0"], vec(full["mix_norm"][i, 0]), dh, dx, F32, "rmsb_mix_pre_" + t)
        gsmall["mix_norm"][i] = jnp.concatenate([dg_pre, dg_post], axis=0)
        if i == 0:
            rs_end(rs_state_f, dx)
        if rs_state is not None:
            rs_token = rs_end(rs_state, dx)
        if i == depth - 1:
            rs_state, rs1_state = None, rs1_begin(mixer_keys(i) + rest_keys(i), f"{i}", rs_token if rs_token is not None else dx)
        else:
            rs_state = rs_begin(mixer_keys(i) + (rest_keys(i) if i > 0 else []), f"{i}" if i > 0 else "0m", rs_token)
    grad_x = dx.reshape(x.shape)

    small_g = [jnp.concatenate(gsmall[n], axis=0).reshape(-1, d) for n in SMALL_SHARDED]
    n_b = b_v_g.shape[0]
    repl_g = [jnp.concatenate([grepl[sl][k] for sl in range(n_b)], axis=0) for k in range(4)]
    repl_rows = []
    for g_arr in repl_g:
        flat = g_arr.reshape(-1)
        flat = jnp.concatenate([flat, jnp.zeros(((-flat.shape[0]) % d,), F32)])
        repl_rows.append(flat.reshape(-1, d))
    packed_g = jnp.concatenate([pad8(t) for t in small_g + repl_rows], axis=0)
    g_ssem, g_rsem, g_buf, g_token = _gather8_start(_slot_place(packed_g, "place_small_grads"), "gather_small_start")

    delta, new_m, new_v, grads = {}, {}, {}, {}
    last_keys = {n for n, _ in mixer_keys(0)}
    deps = [rs_state[5], g_token]
    for n in BIG_KINDS:
        if n not in last_keys:
            delta[n], new_m[n], new_v[n], grads[n] = _adamw(w[n], gfin[n], given["m_" + n], given["v_" + n], "adamw_" + n,
                                                            dep=deps.pop(0), with_grad=True)
            deps.append(delta[n])
    rs_end(rs_state, deps[-1])
    total = _sum_slots(_gather8_wait(g_ssem, g_rsem, g_buf, deps[-1], "gather_small_wait"), F32, "sum_small_grads")
    for n in sorted(last_keys):
        delta[n], new_m[n], new_v[n], grads[n] = _adamw(w[n], gfin[n], given["m_" + n], given["v_" + n], "adamw_" + n,
                                                        with_grad=True)

    off = 0
    for n, cnt in zip(SMALL_SHARDED, counts):
        blk = lax.dynamic_slice_in_dim(total[off:off + cnt], me * ds, ds, axis=1)
        grads[n] = blk.reshape(w[n].shape)
        off += cnt + (-cnt) % 8
    for n, g_arr in zip(SMALL_REPL, repl_g):
        cnt = -(-g_arr.size // d)
        grads[n] = total[off:off + cnt].reshape(-1)[:g_arr.size].reshape(w[n].shape)
        off += cnt + (-cnt) % 8

    for n in WEIGHTS:
        if n not in delta:
            delta[n], new_m[n], new_v[n] = _adamw(w[n], grads[n], given["m_" + n], given["v_" + n], "adamw_" + n)
    return (loss, grad_x, *[grads[n] for n in WEIGHTS], *[delta[n] for n in WEIGHTS], *[new_m[n] for n in WEIGHTS],
            *[new_v[n] for n in WEIGHTS])
```

```python
import functools

import jax
import jax.numpy as jnp
from jax import lax
from jax.experimental import pallas as pl
from jax.experimental.pallas import tpu as pltpu

F32 = jnp.float32
BF16 = jnp.bfloat16
EPS = 1e-6
XA_HEADS = 4
CHUNK = 128
GMLP_GROUPS = 8
ADAM_LR, ADAM_B1, ADAM_B2, ADAM_EPS, ADAM_WD, ADAM_STEP = 0.001, 0.9, 0.999, 1e-08, 0.01, 10
VMEM_LIMIT_V7X = 48 * 1024 * 1024
HBM = pl.BlockSpec(memory_space=pltpu.HBM)
MESH = pl.DeviceIdType.MESH
N_CHIPS = 4
BIG_KINDS = {"xa_wq": "row", "xa_wkv": "col", "xa_wo": "row", "ffn_w_gu": "col", "ffn_w_down": "row",
             "a_w_in": "col", "a_w_out": "row", "b_w_in": "col", "b_w_out": "row", "c_w_in": "col", "c_w_out": "row"}
SMALL_SHARDED = ["mix_norm", "xa_norm", "ffn_norm", "a_conv_w", "c_conv_w", "c_conv_b", "c_ln_g", "c_ln_b"]
SMALL_REPL = ["b_v_g", "b_v_b", "b_w_s", "b_s_bias"]
WEIGHTS = ["mix_norm", "xa_norm", "xa_wq", "xa_wkv", "xa_wo", "ffn_norm", "ffn_w_gu", "ffn_w_down", "a_w_in", "a_conv_w",
           "a_w_out", "b_w_in", "b_v_g", "b_v_b", "b_w_s", "b_s_bias", "b_w_out", "c_w_in", "c_conv_w", "c_conv_b",
           "c_ln_g", "c_ln_b", "c_w_out"]


def _params(*sem):
    return pltpu.CompilerParams(dimension_semantics=sem, vmem_limit_bytes=VMEM_LIMIT_V7X)


def _tile(n, cands=(1024, 512, 256, 128)):
    for c in cands:
        if n % c == 0:
            return c
    return n


def _div_tile(n, cap):
    best = None
    for t in range(128, min(n, cap) + 1, 128):
        if n % t == 0:
            best = t
    return best or n


MM_OUT_TILE_CAP = 1408
MM_K_TILE_CAP = 2816
TN_OUT_ROWS_CAP, TN_OUT_COLS_CAP, TN_K_TILE_CAP = 512, 1024, 4096

_DIMS = {"nn": (((1,), (0,)), ((), ())), "nt": (((1,), (1,)), ((), ())), "tn": (((0,), (0,)), ((), ()))}


def _mm(mode, a, b, out_dtype, name, *, bl=None, a_parts=1, b_parts=1, o_parts=1, dep=None):
    if isinstance(b, list):
        b, bl = b[bl], 0
    bshape = b.shape[1:] if bl is not None else b.shape
    if mode == "nn":
        mo, c = a.shape
        no = bshape[1]
    elif mode == "nt":
        mo, c = (a.shape[1], a.shape[0] * a.shape[2]) if a_parts > 1 else a.shape
        no = bshape[0]
    else:
        c, mo = a.shape
        no = b.shape[0] * b.shape[2] if b_parts > 1 else bshape[1]
    if mode == "tn":
        tmo = _div_tile(mo, TN_OUT_ROWS_CAP)
        tno = _div_tile(no // b_parts, TN_OUT_COLS_CAP)
        tc = _div_tile(c, TN_K_TILE_CAP)
    else:
        tmo = _div_tile(mo, MM_OUT_TILE_CAP)
        tno = _div_tile(no // max(o_parts, b_parts), MM_OUT_TILE_CAP)
        tc = _div_tile(c // a_parts, MM_K_TILE_CAP)
    nk = c // tc
    nkp = nk // a_parts
    njp = (no // tno) // max(o_parts, b_parts)
    lead = (None,) if bl is not None else ()
    lidx = (bl,) if bl is not None else ()

    if mode == "nn":
        a_spec = pl.BlockSpec((tmo, tc), lambda i, j, k: (i, k))
        b_spec = pl.BlockSpec(lead + (tc, tno), lambda i, j, k: lidx + (k, j))
    elif mode == "nt":
        if a_parts > 1:
            a_spec = pl.BlockSpec((None, tmo, tc), lambda i, j, k: (k // nkp, i, k % nkp))
        else:
            a_spec = pl.BlockSpec((tmo, tc), lambda i, j, k: (i, k))
        b_spec = pl.BlockSpec(lead + (tno, tc), lambda i, j, k: lidx + (j, k))
    else:
        a_spec = pl.BlockSpec((tc, tmo), lambda i, j, k: (k, i))
        if b_parts > 1:
            b_spec = pl.BlockSpec((None, tc, tno), lambda i, j, k: (j // njp, k, j % njp))
        else:
            b_spec = pl.BlockSpec((tc, tno), lambda i, j, k: (k, j))

    in_specs = [a_spec, b_spec]
    args = [a, b]
    if dep is not None:
        in_specs.append(pl.BlockSpec(memory_space=pl.ANY))
        args.append(dep)
    if o_parts > 1:
        out_shape = jax.ShapeDtypeStruct((o_parts, mo, no // o_parts), out_dtype)
        out_spec = pl.BlockSpec((None, tmo, tno), lambda i, j, k: (j // njp, i, j % njp))
    else:
        out_shape = jax.ShapeDtypeStruct((mo, no), out_dtype)
        out_spec = pl.BlockSpec((tmo, tno), lambda i, j, k: (i, j))
    dims = _DIMS[mode]

    def body(a_ref, b_ref, *rest):
        if nk == 1:
            o_ref = rest[-1]
            o_ref[...] = lax.dot_general(a_ref[...], b_ref[...], dims, preferred_element_type=F32).astype(o_ref.dtype)
            return
        o_ref, acc = rest[-2], rest[-1]
        k = pl.program_id(2)
        part = lax.dot_general(a_ref[...], b_ref[...], dims, preferred_element_type=F32)

        @pl.when(k == 0)
        def _():
            acc[...] = part

        @pl.when(jnp.logical_and(k > 0, k < nk - 1))
        def _():
            acc[...] += part

        @pl.when(k == nk - 1)
        def _():
            o_ref[...] = (acc[...] + part).astype(o_ref.dtype)

    return pl.pallas_call(
        body, name=name, out_shape=out_shape, grid=(mo // tmo, no // tno, nk), in_specs=in_specs, out_specs=out_spec,
        scratch_shapes=[pltpu.VMEM((tmo, tno), F32)] if nk > 1 else [],
        compiler_params=_params("parallel", "parallel", "arbitrary"))(*args)


def _ew(fn, ins, out_dtypes, name, dep=None):
    rows, cols = ins[0].shape
    deps = [] if dep is None else [dep]
    tr = rows
    for cand in (512, 256, 128, 64, 32, 16):
        if rows % cand == 0 and cand * cols * 4 <= (1 << 20):
            tr = cand
            break
    spec = pl.BlockSpec((tr, cols), lambda i: (i, 0))
    n_in = len(ins)

    def body(*refs):
        outs = fn(*[r[...] for r in refs[:n_in]])
        for o_ref, o in zip(refs[n_in + len(deps):], outs):
            o_ref[...] = o.astype(o_ref.dtype)

    return pl.pallas_call(
        body, name=name, out_shape=[jax.ShapeDtypeStruct((rows, cols), d) for d in out_dtypes], grid=(rows // tr,),
        in_specs=[spec] * n_in + [pl.BlockSpec(memory_space=pl.ANY)] * len(deps), out_specs=[spec] * len(out_dtypes),
        compiler_params=_params("parallel"))(*ins, *deps)


def _adamw_fn(w, g, m, v):
    m = ADAM_B1 * m + (1.0 - ADAM_B1) * g
    v = ADAM_B2 * v + (1.0 - ADAM_B2) * (g * g)
    m_hat = m / (1.0 - ADAM_B1 ** ADAM_STEP)
    v_hat = v / (1.0 - ADAM_B2 ** ADAM_STEP)
    delta = -ADAM_LR * (m_hat / (jnp.sqrt(v_hat) + ADAM_EPS) + ADAM_WD * w)
    return delta, m, v


def _adamw(w, g, m, v, name, dep=None, with_grad=False):
    shape = w.shape
    cols = shape[-1]
    flat = [t.reshape(-1, cols) for t in (w, g, m, v)]
    fn = (lambda wv, gv, mv, vv: _adamw_fn(wv, gv, mv, vv) + (gv,)) if with_grad else _adamw_fn
    outs = _ew(fn, flat, [F32] * (4 if with_grad else 3), name, dep=dep)
    return [o.reshape(shape) for o in outs]


def _row_tile(s):
    return _tile(s, (256, 128, 64, 32, 16, 8))


def _rms_fwd(x, g, name, dep=None):
    s, d = x.shape
    r = _row_tile(s)
    deps = [] if dep is None else [dep]

    def body(x_ref, g_ref, *rest):
        o_ref = rest[-1]
        xv = x_ref[...]
        o_ref[...] = (xv * lax.rsqrt(jnp.mean(xv * xv, axis=-1, keepdims=True) + EPS) * g_ref[...]).astype(BF16)

    return pl.pallas_call(
        body, name=name, out_shape=jax.ShapeDtypeStruct((s, d), BF16), grid=(s // r,),
        in_specs=[pl.BlockSpec((r, d), lambda i: (i, 0)), pl.BlockSpec((1, d), lambda i: (0, 0))] + [ANY] * len(deps),
        out_specs=pl.BlockSpec((r, d), lambda i: (i, 0)), compiler_params=_params("parallel"))(x, g, *deps)


def _res_rms_fwd(x, y, g, g_next, name):
    s, d = x.shape
    r = _row_tile(s)
    has_next = g_next is not None

    def body(x_ref, y_ref, g_ref, *rest):
        yv = y_ref[...].astype(F32)
        xn =x_ref[...] + yv * lax.rsqrt(jnp.mean(yv * yv, axis=-1, keepdims=True) + EPS) * g_ref[...]
        rest[-2 if has_next else -1][...] = xn
        if has_next:
            rest[-1][...] = (xn * lax.rsqrt(jnp.mean(xn * xn, axis=-1, keepdims=True) + EPS) * rest[0][...]).astype(BF16)

    row = pl.BlockSpec((r, d), lambda i: (i, 0))
    vec = pl.BlockSpec((1, d), lambda i: (0, 0))
    outs = pl.pallas_call(
        body, name=name,
        out_shape=[jax.ShapeDtypeStruct((s, d), F32)] + ([jax.ShapeDtypeStruct((s, d), BF16)] if has_next else []),
        grid=(s // r,), in_specs=[row, row, vec] + ([vec] if has_next else []), out_specs=[row] * (2 if has_next else 1),
        compiler_params=_params("parallel"))(*([x, y, g] + ([g_next] if has_next else [])))
    return outs[0], (outs[1] if has_next else None)


def _rms_bwd(x, g, dy, resid, out_dtype, name, dep=None):
    s, d = x.shape
    r = _row_tile(s)
    has_res = resid is not None
    deps = [] if dep is None else [dep]

    def body(*refs):
        x_ref, g_ref, dy_ref = refs[:3]
        dx_ref, dg_ref = refs[-2:]
        i = pl.program_id(0)
        xv = x_ref[...].astype(F32)
        dyv = dy_ref[...].astype(F32)
        rstd = lax.rsqrt(jnp.mean(xv * xv, axis=-1, keepdims=True) + EPS)
        n = xv * rstd
        dn = dyv * g_ref[...]
        dx = rstd * (dn - n * jnp.mean(dn * n, axis=-1, keepdims=True))
        if has_res:
            dx = dx + refs[3][...]
        dx_ref[...] = dx.astype(dx_ref.dtype)
        part = jnp.sum(dyv * n, axis=0, keepdims=True)

        @pl.when(i == 0)
        def _():
            dg_ref[...] = part

        @pl.when(i > 0)
        def _():
            dg_ref[...] += part

    row = pl.BlockSpec((r, d), lambda i: (i, 0))
    vec = pl.BlockSpec((1, d), lambda i: (0, 0))
    ins = [x, g, dy] + ([resid] if has_res else []) + deps
    return pl.pallas_call(
        body, name=name, out_shape=[jax.ShapeDtypeStruct((s, d), out_dtype), jax.ShapeDtypeStruct((1, d), F32)],
        grid=(s // r,), in_specs=[row, vec, row] + ([row] if has_res else []) + [ANY] * len(deps), out_specs=[row, vec],
        compiler_params=_params("arbitrary"))(*ins)


def _loss(y, t, name):
    s, d = y.shape
    r = _row_tile(s)

    def body(y_ref, t_ref, l_ref, dy_ref):
        i = pl.program_id(0)
        e = y_ref[...] - t_ref[...]
        dy_ref[...] = e * (1.0 / d)
        part = jnp.full((8, 128), 0.5 * jnp.sum(jnp.mean(e * e, axis=-1, keepdims=True)), F32)

        @pl.when(i == 0)
        def _():
            l_ref[...] = part

        @pl.when(i > 0)
        def _():
            l_ref[...] += part

    row = pl.BlockSpec((r, d), lambda i: (i, 0))
    return pl.pallas_call(
        body, name=name, out_shape=[jax.ShapeDtypeStruct((8, 128), F32), jax.ShapeDtypeStruct((s, d), F32)],
        grid=(s // r,), in_specs=[row, row], out_specs=[pl.BlockSpec((8, 128), lambda i: (0, 0)), row],
        compiler_params=_params("arbitrary"))(y, t)


def _rows(xv, a, m, cache):
    r = a % 8
    q = a - r
    if r == 0:
        return xv[q:q + m]
    if r not in cache:
        cache[r] = pltpu.roll(xv, xv.shape[0] - r, 0)
    return cache[r][q:q + m]


def _conv_taps(xv, w, k_w, halo, m, flip):
    cache = {}
    acc = None
    for k in range(k_w):
        a = (k_w - 1 - k) if flip else (halo + k - (k_w - 1))
        term = w[k:k + 1, :] * _rows(xv, a, m, cache)
        acc = term if acc is None else acc + term
    return acc


def _conv_wgrad(dw_ref, dyv, xv, k_w, halo, m):
    cache = {}
    for k in range(k_w):
        xs = _rows(xv, halo + k - (k_w - 1), m, cache)
        dw_ref[pl.ds(k, 1), :] += jnp.sum(dyv * xs, axis=0, keepdims=True)


def _conv_tiles(s, dp, halo):
    r = _tile(s, (256, 128))
    cw = _tile(dp, (256, 128))
    return r, cw, r // halo


A_HALO = 8


def _a_mid_fwd(bcz3, w, name):
    _, s, d = bcz3.shape
    r, cw, rh = _conv_tiles(s, d, A_HALO)
    k_w = w.shape[0]

    def body(m_ref, h_ref, w_ref, o_ref):
        i = pl.program_id(0)
        cz = m_ref[1].astype(F32) * m_ref[2].astype(F32)
        hcz = h_ref[1].astype(F32) * h_ref[2].astype(F32)
        hcz = jnp.where(i == 0, 0.0, hcz)
        xv = jnp.concatenate([hcz, cz], axis=0)
        y = _conv_taps(xv, w_ref[...], k_w, A_HALO, r, False)
        o_ref[...] = (m_ref[0].astype(F32) * y).astype(BF16)

    return pl.pallas_call(
        body, name=name, out_shape=jax.ShapeDtypeStruct((s, d), BF16), grid=(s // r, d // cw),
        in_specs=[pl.BlockSpec((3, r, cw), lambda i, j: (0, i, j)),
                  pl.BlockSpec((3, A_HALO, cw), lambda i, j: (0, jnp.maximum(i * rh - 1, 0), j)),
                  pl.BlockSpec((k_w, cw), lambda i, j: (0, j))],
        out_specs=pl.BlockSpec((r, cw), lambda i, j: (i, j)), compiler_params=_params("parallel", "parallel"))(bcz3, bcz3, w)


def _a_mid_bwd(bcz3, dgated, w, name):
    _, s, d = bcz3.shape
    r, cw, rh = _conv_tiles(s, d, A_HALO)
    k_w = w.shape[0]
    ni = s // r
    last_h = s // A_HALO - 1

    def body(m_ref, hp_ref, hn_ref, dg_ref, dgn_ref, w_ref, o_ref, dw_ref):
        i = pl.program_id(1)
        wv = w_ref[...]
        b = m_ref[0].astype(F32)
        c = m_ref[1].astype(F32)
        z = m_ref[2].astype(F32)
        hcz = jnp.where(i == 0, 0.0, hp_ref[1].astype(F32) * hp_ref[2].astype(F32))
        xv = jnp.concatenate([hcz, c * z], axis=0)
        y = _conv_taps(xv, wv, k_w, A_HALO, r, False)
        dg = dg_ref[...].astype(F32)
        dy = dg * b
        dyn = jnp.where(i == ni - 1, 0.0, dgn_ref[...].astype(F32) * hn_ref[0].astype(F32))
        dcz = _conv_taps(jnp.concatenate([dy, dyn], axis=0), wv, k_w, A_HALO, r, True)
        o_ref[0] = (dg * y).astype(BF16)
        o_ref[1] = (dcz * z).astype(BF16)
        o_ref[2] = (dcz * c).astype(BF16)

        @pl.when(i == 0)
        def _():
            dw_ref[...] = jnp.zeros_like(dw_ref)

        _conv_wgrad(dw_ref, dy, xv, k_w, A_HALO, r)

    return pl.pallas_call(
        body, name=name, out_shape=[jax.ShapeDtypeStruct((3, s, d), BF16), jax.ShapeDtypeStruct((k_w, d), F32)],
        grid=(d // cw, ni),
        in_specs=[pl.BlockSpec((3, r, cw), lambda j, i: (0, i, j)),
                  pl.BlockSpec((3, A_HALO, cw), lambda j, i: (0, jnp.maximum(i * rh - 1, 0), j)),
                  pl.BlockSpec((3, A_HALO, cw), lambda j, i: (0, jnp.minimum((i + 1) * rh, last_h), j)),
                  pl.BlockSpec((r, cw), lambda j, i: (i, j)),
                  pl.BlockSpec((A_HALO, cw), lambda j, i: (jnp.minimum((i + 1) * rh, last_h), j)),
                  pl.BlockSpec((k_w, cw), lambda j, i: (0, j))],
        out_specs=[pl.BlockSpec((3, r, cw), lambda j, i: (0, i, j)), pl.BlockSpec((k_w, cw), lambda j, i: (0, j))],
        compiler_params=_params("parallel", "arbitrary"))(bcz3, bcz3, bcz3, dgated, dgated, w)


C_HALO = 32


def _c_conv_fwd(ag3, w, bias, name):
    _, s, d = ag3.shape
    r, cw, rh = _conv_tiles(s, d, C_HALO)
    k_w = w.shape[0]

    def body(m_ref, h_ref, w_ref, b_ref, o_ref):
        i = pl.program_id(0)
        y1 = m_ref[0].astype(F32) * jax.nn.sigmoid(m_ref[1].astype(F32))
        h1 = jnp.where(i == 0, 0.0, h_ref[0].astype(F32) * jax.nn.sigmoid(h_ref[1].astype(F32)))
        xv = jnp.concatenate([h1, y1], axis=0)
        o_ref[...] = _conv_taps(xv, w_ref[...], k_w, C_HALO, r, False) + b_ref[...]

    return pl.pallas_call(
        body, name=name, out_shape=jax.ShapeDtypeStruct((s, d), F32), grid=(s // r, d // cw),
        in_specs=[pl.BlockSpec((2, r, cw), lambda i, j: (0, i, j)),
                  pl.BlockSpec((2, C_HALO, cw), lambda i, j: (0, jnp.maximum(i * rh - 1, 0), j)),
                  pl.BlockSpec((k_w, cw), lambda i, j: (0, j)), pl.BlockSpec((1, cw), lambda i, j: (0, j))],
        out_specs=pl.BlockSpec((r, cw), lambda i, j: (i, j)),
        compiler_params=_params("parallel", "parallel"))(ag3, ag3, w, bias)


def _c_conv_bwd(ag3, dy2, w, name):
    _, s, d = ag3.shape
    r, cw, rh = _conv_tiles(s, d, C_HALO)
    k_w = w.shape[0]
    ni = s // r
    last_h = s // C_HALO - 1

    def body(m_ref, hp_ref, dy_ref, dyn_ref, w_ref, o_ref, dw_ref, db_ref):
        i = pl.program_id(1)
        wv = w_ref[...]
        a = m_ref[0].astype(F32)
        sg = jax.nn.sigmoid(m_ref[1].astype(F32))
        h1 = jnp.where(i == 0, 0.0, hp_ref[0].astype(F32) * jax.nn.sigmoid(hp_ref[1].astype(F32)))
        xv = jnp.concatenate([h1, a * sg], axis=0)
        dy = dy_ref[...]
        dyn = jnp.where(i == ni - 1, 0.0, dyn_ref[...])
        dy1 = _conv_taps(jnp.concatenate([dy, dyn], axis=0), wv, k_w, C_HALO, r, True)
        o_ref[0] = (dy1 * sg).astype(BF16)
        o_ref[1] = (dy1 * a * sg * (1.0 - sg)).astype(BF16)

        @pl.when(i == 0)
        def _():
            dw_ref[...] = jnp.zeros_like(dw_ref)
            db_ref[...] = jnp.zeros_like(db_ref)

        db_ref[...] += jnp.sum(dy, axis=0, keepdims=True)
        _conv_wgrad(dw_ref, dy, xv, k_w, C_HALO, r)

    return pl.pallas_call(
        body, name=name,
        out_shape=[jax.ShapeDtypeStruct((2, s, d), BF16), jax.ShapeDtypeStruct((k_w, d), F32),
                   jax.ShapeDtypeStruct((1, d), F32)],
        grid=(d // cw, ni),
        in_specs=[pl.BlockSpec((2, r, cw), lambda j, i: (0, i, j)),
                  pl.BlockSpec((2, C_HALO, cw), lambda j, i: (0, jnp.maximum(i * rh - 1, 0), j)),
                  pl.BlockSpec((r, cw), lambda j, i: (i, j)),
                  pl.BlockSpec((C_HALO, cw), lambda j, i: (jnp.minimum((i + 1) * rh, last_h), j)),
                  pl.BlockSpec((k_w, cw), lambda j, i: (0, j))],
        out_specs=[pl.BlockSpec((2, r, cw), lambda j, i: (0, i, j)), pl.BlockSpec((k_w, cw), lambda j, i: (0, j)),
                   pl.BlockSpec((1, cw), lambda j, i: (0, j))],
        compiler_params=_params("parallel", "arbitrary"))(ag3, ag3, dy2, dy2, w)


def _ln_stats(v):
    mu = jnp.mean(v, axis=-1, keepdims=True)
    vc = v - mu
    rstd = lax.rsqrt(jnp.mean(vc * vc, axis=-1, keepdims=True) + EPS)
    return vc * rstd, rstd


def _ln_bwd(dn, g, xh, rstd):
    dxh = dn * g
    return rstd * (dxh - jnp.mean(dxh, axis=-1, keepdims=True) - xh * jnp.mean(dxh * xh, axis=-1, keepdims=True))


def _c_ln_fwd(y2, g, b, name):
    s, d = y2.shape
    r = _row_tile(s)

    def body(y_ref, g_ref, b_ref, o_ref):
        xh, _ = _ln_stats(y_ref[...])
        y3 = xh * g_ref[...] + b_ref[...]
        o_ref[...] = (y3 * jax.nn.sigmoid(y3)).astype(BF16)

    row = pl.BlockSpec((r, d), lambda i: (i, 0))
    vec = pl.BlockSpec((1, d), lambda i: (0, 0))
    return pl.pallas_call(
        body, name=name, out_shape=jax.ShapeDtypeStruct((s, d), BF16), grid=(s // r,), in_specs=[row, vec, vec],
        out_specs=row, compiler_params=_params("parallel"))(y2, g, b)


def _c_ln_bwd(y2, dout, g, b, name):
    s, d = y2.shape
    r = _row_tile(s)

    def body(y_ref, do_ref, g_ref, b_ref, dy_ref, dg_ref, db_ref):
        i = pl.program_id(0)
        xh, rstd = _ln_stats(y_ref[...])
        gv = g_ref[...]
        y3 = xh * gv + b_ref[...]
        sg = jax.nn.sigmoid(y3)
        dy3 = do_ref[...].astype(F32) * (sg + y3 * sg * (1.0 - sg))
        dy_ref[...] = _ln_bwd(dy3, gv, xh, rstd)

        @pl.when(i == 0)
        def _():
            dg_ref[...] = jnp.zeros_like(dg_ref)
            db_ref[...] = jnp.zeros_like(db_ref)

        dg_ref[...] += jnp.sum(dy3 * xh, axis=0, keepdims=True)
        db_ref[...] += jnp.sum(dy3, axis=0, keepdims=True)

    row = pl.BlockSpec((r, d), lambda i: (i, 0))
    vec = pl.BlockSpec((1, d), lambda i: (0, 0))
    return pl.pallas_call(
        body, name=name,
        out_shape=[jax.ShapeDtypeStruct((s, d), F32), jax.ShapeDtypeStruct((1, d), F32), jax.ShapeDtypeStruct((1, d), F32)],
        grid=(s // r,), in_specs=[row, row, vec, vec], out_specs=[row, vec, vec],
        compiler_params=_params("arbitrary"))(y2, dout, g, b)


_GELU_C = 0.7978845608028654
_GELU_A = 0.044715


def _gelu(x):
    return 0.5 * x * (1.0 + jnp.tanh(_GELU_C * (x + _GELU_A * x * x * x)))


def _gelu_grad(x):
    t = jnp.tanh(_GELU_C * (x + _GELU_A * x * x * x))
    return 0.5 * (1.0 + t) + 0.5 * x * (1.0 - t * t) * _GELU_C * (1.0 + 3.0 * _GELU_A * x * x)


def _b_mid_fwd(uv3, vg, vb, ws_m, sbt, name):
    _, s, h = uv3.shape
    g_n, t, _ = ws_m.shape
    gd = h // g_n

    def body(uv_ref, vg_ref, vb_ref, ws_ref, sb_ref, o_ref):
        u = _gelu(uv_ref[0].astype(F32))
        xh, _ = _ln_stats(_gelu(uv_ref[1].astype(F32)))
        vn = (xh * vg_ref[...] + vb_ref[...]).astype(BF16)
        for g in range(g_n):
            sl = slice(g * gd, (g + 1) * gd)
            sv = jnp.dot(ws_ref[g], vn[:, sl], preferred_element_type=F32) + sb_ref[:, g:g + 1]
            o_ref[:, sl] = (u[:, sl] * sv).astype(BF16)

    vec = pl.BlockSpec((1, h), lambda i: (0, 0))
    return pl.pallas_call(
        body, name=name, out_shape=jax.ShapeDtypeStruct((s, h), BF16), grid=(s // t,),
        in_specs=[pl.BlockSpec((2, t, h), lambda i: (0, i, 0)), vec, vec,
                  pl.BlockSpec((g_n, t, t), lambda i: (0, 0, 0)), pl.BlockSpec((t, 128), lambda i: (0, 0))],
        out_specs=pl.BlockSpec((t, h), lambda i: (i, 0)), compiler_params=_params("parallel"))(uv3, vg, vb, ws_m, sbt)


def _b_mid_bwd(uv3, dgated, vg, vb, ws_m, sbt, name):
    _, s, h = uv3.shape
    g_n, t, _ = ws_m.shape
    gd = h // g_n

    def body(uv_ref, dg_ref, vg_ref, vb_ref, ws_ref, sb_ref, o_ref, dvg_ref, dvb_ref, dws_ref, dsb_ref, dvn_ref):
        i = pl.program_id(0)

        @pl.when(i == 0)
        def _():
            dvg_ref[...] = jnp.zeros_like(dvg_ref)
            dvb_ref[...] = jnp.zeros_like(dvb_ref)
            dws_ref[...] = jnp.zeros_like(dws_ref)
            dsb_ref[...] = jnp.zeros_like(dsb_ref)

        upre = uv_ref[0].astype(F32)
        vpre = uv_ref[1].astype(F32)
        u = _gelu(upre)
        xh, rstd = _ln_stats(_gelu(vpre))
        gv = vg_ref[...]
        vn = (xh * gv + vb_ref[...]).astype(BF16)
        causal = lax.broadcasted_iota(jnp.int32, (t, t), 0) >= lax.broadcasted_iota(jnp.int32, (t, t), 1)
        lane = lax.broadcasted_iota(jnp.int32, (t, 128), 1)
        for g in range(g_n):
            sl = slice(g * gd, (g + 1) * gd)
            wsg = ws_ref[g]
            sv = jnp.dot(wsg, vn[:, sl], preferred_element_type=F32) + sb_ref[:, g:g + 1]
            dg = dg_ref[:, sl].astype(F32)
            o_ref[0, :, sl] = (dg * sv * _gelu_grad(upre[:, sl])).astype(BF16)
            dsv = dg * u[:, sl]
            dsvb = dsv.astype(BF16)
            dsb_ref[...] += jnp.where(lane == g, jnp.sum(dsv, axis=1, keepdims=True), 0.0)
            dws = lax.dot_general(dsvb, vn[:, sl], _DIMS["nt"], preferred_element_type=F32)
            dws_ref[g] += jnp.where(causal, dws, 0.0)
            dvn_ref[:, sl] = lax.dot_general(wsg, dsvb, _DIMS["tn"], preferred_element_type=F32)
        dvn = dvn_ref[...]
        dvg_ref[...] += jnp.sum(dvn * xh, axis=0, keepdims=True)
        dvb_ref[...] += jnp.sum(dvn, axis=0, keepdims=True)
        o_ref[1] = (_ln_bwd(dvn, gv, xh, rstd) * _gelu_grad(vpre)).astype(BF16)

    vec = pl.BlockSpec((1, h), lambda i: (0, 0))
    return pl.pallas_call(
        body, name=name,
        out_shape=[jax.ShapeDtypeStruct((2, s, h), BF16), jax.ShapeDtypeStruct((1, h), F32), jax.ShapeDtypeStruct((1, h), F32),
                   jax.ShapeDtypeStruct((g_n, t, t), F32), jax.ShapeDtypeStruct((t, 128), F32)],
        grid=(s // t,),
        in_specs=[pl.BlockSpec((2, t, h), lambda i: (0, i, 0)), pl.BlockSpec((t, h), lambda i: (i, 0)), vec, vec,
                  pl.BlockSpec((g_n, t, t), lambda i: (0, 0, 0)), pl.BlockSpec((t, 128), lambda i: (0, 0))],
        out_specs=[pl.BlockSpec((2, t, h), lambda i: (0, i, 0)), vec, vec,
                   pl.BlockSpec((g_n, t, t), lambda i: (0, 0, 0)), pl.BlockSpec((t, 128), lambda i: (0, 0))],
        scratch_shapes=[pltpu.VMEM((t, h), F32)],
        compiler_params=_params("arbitrary"))(uv3, dgated, vg, vb, ws_m, sbt)


def _softmax_rows(sc):
    e = jnp.exp(sc - jnp.max(sc, axis=-1, keepdims=True))
    return e / jnp.sum(e, axis=-1, keepdims=True)


def _attn_fwd(q, kv3, name):
    s, d = q.shape
    m = kv3.shape[1]
    dh = d // XA_HEADS
    scale = dh ** -0.5
    r = _row_tile(s)

    def body(q_ref, kv_ref, o_ref):
        for hd in range(XA_HEADS):
            sl = slice(hd * dh, (hd + 1) * dh)
            sc = lax.dot_general(q_ref[:, sl], kv_ref[0, :, sl], _DIMS["nt"], preferred_element_type=F32) * scale
            p = _softmax_rows(sc).astype(BF16)
            o_ref[:, sl] = jnp.dot(p, kv_ref[1, :, sl], preferred_element_type=F32).astype(BF16)

    return pl.pallas_call(
        body, name=name, out_shape=jax.ShapeDtypeStruct((s, d), BF16), grid=(s // r,),
        in_specs=[pl.BlockSpec((r, d), lambda i: (i, 0)), pl.BlockSpec((2, m, d), lambda i: (0, 0, 0))],
        out_specs=pl.BlockSpec((r, d), lambda i: (i, 0)), compiler_params=_params("parallel"))(q, kv3)


def _attn_bwd(q, kv3, do, name):
    s, d = q.shape
    m = kv3.shape[1]
    dh = d // XA_HEADS
    scale = dh ** -0.5
    r = _row_tile(s)

    def body(q_ref, kv_ref, do_ref, dq_ref, dkv_ref):
        i = pl.program_id(0)

        @pl.when(i == 0)
        def _():
            dkv_ref[...] = jnp.zeros_like(dkv_ref)

        for hd in range(XA_HEADS):
            sl = slice(hd * dh, (hd + 1) * dh)
            qh = q_ref[:, sl]
            kh = kv_ref[0, :, sl]
            doh = do_ref[:, sl]
            sc = lax.dot_general(qh, kh, _DIMS["nt"], preferred_element_type=F32) * scale
            p = _softmax_rows(sc)
            pb = p.astype(BF16)
            dkv_ref[1, :, sl] += lax.dot_general(pb, doh, _DIMS["tn"], preferred_element_type=F32)
            dp = lax.dot_general(doh, kv_ref[1, :, sl], _DIMS["nt"], preferred_element_type=F32)
            ds = (p * (dp - jnp.sum(dp * p, axis=-1, keepdims=True)) * scale).astype(BF16)
            dq_ref[:, sl] = jnp.dot(ds, kh, preferred_element_type=F32).astype(BF16)
            dkv_ref[0, :, sl] += lax.dot_general(ds, qh, _DIMS["tn"], preferred_element_type=F32)

    row = pl.BlockSpec((r, d), lambda i: (i, 0))
    kvs = pl.BlockSpec((2, m, d), lambda i: (0, 0, 0))
    return pl.pallas_call(
        body, name=name, out_shape=[jax.ShapeDtypeStruct((s, d), BF16), jax.ShapeDtypeStruct((2, m, d), F32)],
        grid=(s // r,), in_specs=[row, kvs, row], out_specs=[row, kvs], compiler_params=_params("arbitrary"))(q, kv3, do)


FFN_COL_TILE = 512


def _ffn_gu_fwd(h, w_gu, name):
    s, d = h.shape
    f = w_gu.shape[2] // 2
    tm = _div_tile(s, MM_OUT_TILE_CAP)
    tn = _div_tile(f, FFN_COL_TILE)
    nj = f // tn

    def body(a_ref, bg_ref, bu_ref, gu_ref, act_ref):
        a = a_ref[...]
        gate = jnp.dot(a, bg_ref[...], preferred_element_type=F32)
        up = jnp.dot(a, bu_ref[...], preferred_element_type=F32)
        gu_ref[0] = gate.astype(BF16)
        gu_ref[1] = up.astype(BF16)
        act_ref[...] = (gate * jax.nn.sigmoid(gate) * up).astype(BF16)

    return pl.pallas_call(
        body, name=name, out_shape=[jax.ShapeDtypeStruct((2, s, f), BF16), jax.ShapeDtypeStruct((s, f), BF16)],
        grid=(s // tm, nj),
        in_specs=[pl.BlockSpec((tm, d), lambda i, j: (i, 0)), pl.BlockSpec((None, d, tn), lambda i, j: (0, 0, j)),
                  pl.BlockSpec((None, d, tn), lambda i, j: (0, 0, j + nj))],
        out_specs=[pl.BlockSpec((2, tm, tn), lambda i, j: (0, i, j)), pl.BlockSpec((tm, tn), lambda i, j: (i, j))],
        compiler_params=_params("parallel", "parallel"))(h, w_gu, w_gu)


def _ffn_down_bwd(dy, w_down, gu3, name):
    s, d = dy.shape
    f = w_down.shape[1]
    tm = _div_tile(s, MM_OUT_TILE_CAP)
    tn = _div_tile(f, FFN_COL_TILE)

    def body(dy_ref, w_ref, gu_ref, o_ref):
        da = lax.dot_general(dy_ref[...], w_ref[...], _DIMS["nt"], preferred_element_type=F32)
        gate = gu_ref[0].astype(F32)
        up = gu_ref[1].astype(F32)
        sg = jax.nn.sigmoid(gate)
        o_ref[0] = (da * up * (sg + gate * sg * (1.0 - sg))).astype(BF16)
        o_ref[1] = (da * gate * sg).astype(BF16)

    return pl.pallas_call(
        body, name=name, out_shape=jax.ShapeDtypeStruct((2, s, f), BF16), grid=(s // tm, f // tn),
        in_specs=[pl.BlockSpec((tm, d), lambda i, j: (i, 0)), pl.BlockSpec((None, tn, d), lambda i, j: (0, j, 0)),
                  pl.BlockSpec((2, tm, tn), lambda i, j: (0, i, j))],
        out_specs=pl.BlockSpec((2, tm, tn), lambda i, j: (0, i, j)),
        compiler_params=_params("parallel", "parallel"))(dy, w_down, gu3)


def _ids():
    x, y, c = lax.axis_index("x"), lax.axis_index("y"), lax.axis_index("c")
    return x, y, c, 2 * x + y


def _chip_peers(x, y):
    return [(d - 1, 2 * (x ^ (d >> 1)) + (y ^ (d & 1)), x ^ (d >> 1), y ^ (d & 1)) for d in (1, 2, 3)]


def _remote(src, dst, ssem, rsem, dev):
    return pltpu.make_async_remote_copy(src_ref=src, dst_ref=dst, send_sem=ssem, recv_sem=rsem, device_id=dev,
                                        device_id_type=MESH)


def _gview(ref, kind, j, cc):
    _, k, n = ref.shape
    if kind == "row":
        return ref.at[:, pl.ds(j * (k // N_CHIPS) + cc * (k // (2 * N_CHIPS)), k // (2 * N_CHIPS)), :]
    return ref.at[:, pl.ds(cc * (k // 2), k // 2), pl.ds(j * (n // N_CHIPS), n // N_CHIPS)]


def _sview(ref, cc):
    r = ref.shape[1]
    return ref.at[:, pl.ds(cc * (r // 2), r // 2), :]


def _comm_call(body, name, ins, out_shapes, n_sems, aliases=None):
    return pl.pallas_call(
        body, name=name, out_shape=out_shapes, in_specs=[HBM] * len(ins), out_specs=[HBM] * len(out_shapes),
        scratch_shapes=[pltpu.SemaphoreType.DMA((n,)) for n in n_sems], input_output_aliases=aliases or {},
        compiler_params=pltpu.CompilerParams(has_side_effects=True))(*ins)


def _mesh_scalars():
    x, y, c = lax.axis_index("x"), lax.axis_index("y"), lax.axis_index("c")
    return jnp.stack([2 * x + y, c]).astype(jnp.int32)


def _slab_rows(rows, cols, itemsize=4):
    best = None
    for cand in range(16, rows + 1, 16):
        if rows % cand == 0 and cand * cols * itemsize <= (2 << 20):
            best = cand
    return best or rows


def _ag_place(shard, layer, kind, name, dep=None):
    deps = [] if dep is None else [dep]
    _, r, n = shard.shape
    full = (1, r * N_CHIPS, n) if kind == "row" else (1, r, n * N_CHIPS)
    tr = _slab_rows(r, n)
    nt = r // tr
    if kind == "row":
        out_spec = pl.BlockSpec((None, tr, n), lambda t, s: (0, s[0] * nt + t, 0))
    else:
        out_spec = pl.BlockSpec((None, tr, n), lambda t, s: (0, t, s[0]))

    def body(s_ref, i_ref, *rest):
        rest[-1][...] = i_ref[...].astype(BF16)

    return pl.pallas_call(
        body, name=name, out_shape=jax.ShapeDtypeStruct(full, BF16),
        grid_spec=pltpu.PrefetchScalarGridSpec(
            num_scalar_prefetch=1, grid=(nt,),
            in_specs=[pl.BlockSpec((None, tr, n), lambda t, s: (layer, t, 0))] + [pl.BlockSpec(memory_space=pl.ANY)] * len(deps),
            out_specs=out_spec),
        compiler_params=_params("parallel"))(_mesh_scalars(), shard, *deps)


SEM = pl.BlockSpec(memory_space=pltpu.SEMAPHORE)
ANY = pl.BlockSpec(memory_space=pl.ANY)
DATAFLOW = pltpu.SideEffectType.DATAFLOW_SIDE_EFFECTING


def _in_hbm(arrs):
    return [pltpu.with_memory_space_constraint(a, pltpu.HBM) for a in arrs]


def _ag_start(bufs, kinds, name, after=None):
    n = len(bufs)
    afters = [] if after is None else [after]
    n_in = n + len(afters)

    def body(*refs):
        ssem, rsem, token = refs[n_in], refs[n_in + 1], refs[-1]
        x, y, c, me = _ids()
        for t in range(n):
            mine = _gview(refs[t], kinds[t], me, c)
            for d, _, px, py in _chip_peers(x, y):
                _remote(mine, mine, ssem.at[3 * t + d], rsem.at[3 * t + d], (px, py, c)).start()
        token[...] = jnp.zeros_like(token)

    outs = pl.pallas_call(
        body, name=name,
        out_shape=(pltpu.SemaphoreType.DMA((3 * n,)), pltpu.SemaphoreType.DMA((3 * n,)),
                   *[pltpu.HBM(b.shape, b.dtype) for b in bufs], jax.ShapeDtypeStruct((8, 128), F32)),
        in_specs=[HBM] * n + [ANY] * len(afters), out_specs=(SEM, SEM, *[HBM] * n, pl.BlockSpec(memory_space=pltpu.VMEM)),
        input_output_aliases={t: 2 + t for t in range(n)},
        compiler_params=pltpu.CompilerParams(has_side_effects=DATAFLOW))(*_in_hbm(bufs), *afters)
    return outs[0], outs[1], list(outs[2:2 + n]), outs[-1]


def _ag_wait(ssem, rsem, bufs, kinds, after, name):
    n = len(bufs)

    def body(*refs):
        ssem_ref, rsem_ref = refs[n], refs[n + 1]
        x, y, c, me = _ids()
        for t in range(n):
            mine = _gview(refs[t], kinds[t], me, c)
            for d, pj, px, py in _chip_peers(x, y):
                theirs = _gview(refs[t], kinds[t], pj, c)
                _remote(mine, mine, ssem_ref.at[3 * t + d], rsem_ref.at[3 * t + d], (px, py, c)).wait_send()
                _remote(theirs, theirs, ssem_ref.at[3 * t + d], rsem_ref.at[3 * t + d], (px, py, c)).wait_recv()

    outs = pl.pallas_call(
        body, name=name, out_shape=[pltpu.HBM(b.shape, b.dtype) for b in bufs],
        in_specs=[HBM] * n + [SEM, SEM, ANY], out_specs=[HBM] * n, input_output_aliases={t: t for t in range(n)},
        compiler_params=pltpu.CompilerParams(has_side_effects=DATAFLOW))(*bufs, ssem, rsem, after)
    return list(outs)


def _ag_forward(bufs, kinds, name):
    n = len(bufs)

    def body(*refs):
        outs = refs[n:2 * n]
        ssem, rsem = refs[2 * n], refs[2 * n + 1]
        x, y, c, _ = _ids()
        sib = (x, y, 1 - c)
        sends = []
        for t in range(n):
            for d, pj, _, _ in _chip_peers(x, y):
                piece = _gview(outs[t], kinds[t], pj, c)
                sends.append(_remote(piece, piece, ssem.at[3 * t + d], rsem.at[3 * t + d], sib))
        for cp in sends:
            cp.start()
        for t in range(n):
            for d, pj, _, _ in _chip_peers(x, y):
                piece = _gview(outs[t], kinds[t], pj, 1 - c)
                _remote(piece, piece, ssem.at[3 * t + d], rsem.at[3 * t + d], sib).wait_recv()
        for cp in sends:
            cp.wait_send()

    return _comm_call(body, name, bufs, [jax.ShapeDtypeStruct(b.shape, b.dtype) for b in bufs], (3 * n, 3 * n),
                      {t: t for t in range(n)})


def _rs1(g_fulls, kinds, name, dep=None):
    n = len(g_fulls)
    outs = []
    for g, kind in zip(g_fulls, kinds):
        l, k, nn = g.shape
        piece = (l, k // (2 * N_CHIPS), nn) if kind == "row" else (l, k // 2, nn // N_CHIPS)
        outs.append(jax.ShapeDtypeStruct((N_CHIPS,) + piece, g.dtype))

    n_in = n + (dep is not None)

    def body(*refs):
        ssem, rsem = refs[n_in + n], refs[n_in + n + 1]
        x, y, c, _ = _ids()
        sends = [_remote(_gview(refs[t], kinds[t], j, 1 - c), refs[n_in + t].at[j], ssem.at[4 * t + j], rsem.at[4 * t + j],
                         (x, y, 1 - c)) for t in range(n) for j in range(N_CHIPS)]
        for cp in sends:
            cp.start()
        for cp in sends:
            cp.wait()

    return _comm_call(body, name, g_fulls + ([] if dep is None else [dep]), outs, (4 * n, 4 * n))


def _rs1_pieces(g_fulls, kinds):
    shapes = []
    for g, kind in zip(g_fulls, kinds):
        l, k, nn = g.shape
        shapes.append((N_CHIPS,) + ((l, k // (2 * N_CHIPS), nn) if kind == "row" else (l, k // 2, nn // N_CHIPS)))
    return shapes


def _rs1_start(g_fulls, kinds, name, after=None):
    n = len(g_fulls)
    lands = [lax.empty(s, g.dtype) for s, g in zip(_rs1_pieces(g_fulls, kinds), g_fulls)]
    afters = [] if after is None else [after]
    n_in = 2 * n + len(afters)

    def body(*refs):
        ssem, rsem, token = refs[n_in], refs[n_in + 1], refs[-1]
        x, y, c, _ = _ids()
        for t in range(n):
            for j in range(N_CHIPS):
                _remote(_gview(refs[t], kinds[t], j, 1 - c), refs[n + t].at[j], ssem.at[4 * t + j], rsem.at[4 * t + j],
                        (x, y, 1 - c)).start()
        token[...] = jnp.zeros_like(token)

    outs = pl.pallas_call(
        body, name=name,
        out_shape=(pltpu.SemaphoreType.DMA((4 * n,)), pltpu.SemaphoreType.DMA((4 * n,)),
                   *[pltpu.HBM(a.shape, a.dtype) for a in g_fulls + lands], jax.ShapeDtypeStruct((8, 128), F32)),
        in_specs=[HBM] * (2 * n) + [ANY] * len(afters),
        out_specs=(SEM, SEM, *[HBM] * (2 * n), pl.BlockSpec(memory_space=pltpu.VMEM)),
        input_output_aliases={t: 2 + t for t in range(2 * n)},
        compiler_params=pltpu.CompilerParams(has_side_effects=DATAFLOW))(*_in_hbm(g_fulls + lands), *afters)
    return outs[0], outs[1], list(outs[2:2 + n]), list(outs[2 + n:2 + 2 * n]), outs[-1]


def _rs1_wait(ssem, rsem, g_fulls, lands, kinds, after, name):
    n = len(g_fulls)

    def body(*refs):
        ssem_ref, rsem_ref = refs[2 * n], refs[2 * n + 1]
        x, y, c, _ = _ids()
        for t in range(n):
            for j in range(N_CHIPS):
                _remote(_gview(refs[t], kinds[t], j, 1 - c), refs[n + t].at[j], ssem_ref.at[4 * t + j], rsem_ref.at[4 * t + j],
                        (x, y, 1 - c)).wait()

    outs = pl.pallas_call(
        body, name=name, out_shape=[pltpu.HBM(a.shape, a.dtype) for a in g_fulls + lands],
        in_specs=[HBM] * (2 * n) + [SEM, SEM, ANY], out_specs=[HBM] * (2 * n),
        input_output_aliases={t: t for t in range(2 * n)},
        compiler_params=pltpu.CompilerParams(has_side_effects=DATAFLOW))(*g_fulls, *lands, ssem, rsem, after)
    return list(outs[:n]), list(outs[n:])


def _rs_add1(g_full, got, kind, name):
    l, k, n = g_full.shape
    _, _, pr, pc = got.shape
    tr = _slab_rows(pr, pc, 2)
    nt = pr // tr
    if kind == "row":
        g_spec = pl.BlockSpec((None, tr, n), lambda j, li, t, s: (li, (2 * j + s[1]) * nt + t, 0))
    else:
        g_spec = pl.BlockSpec((None, tr, pc), lambda j, li, t, s: (li, s[1] * nt + t, j))
    slot = pl.BlockSpec((None, None, tr, pc), lambda j, li, t, s: (j, li, t, 0))

    def body(s_ref, g_ref, got_ref, o_ref):
        o_ref[...] = g_ref[...] + got_ref[...]

    return pl.pallas_call(
        body, name=name, out_shape=jax.ShapeDtypeStruct(got.shape, BF16),
        grid_spec=pltpu.PrefetchScalarGridSpec(num_scalar_prefetch=1, grid=(N_CHIPS, l, nt), in_specs=[g_spec, slot],
                                               out_specs=slot),
        compiler_params=_params("parallel", "parallel", "parallel"))(_mesh_scalars(), g_full, got)


def _rs2_start(ps, name):
    n = len(ps)
    lands = [lax.empty(p.shape, p.dtype) for p in ps]

    def body(*refs):
        ssem, rsem, token = refs[2 * n], refs[2 * n + 1], refs[-1]
        x, y, c, me = _ids()
        for t in range(n):
            for d, pj, px, py in _chip_peers(x, y):
                _remote(refs[t].at[pj], refs[n + t].at[me], ssem.at[3 * t + d], rsem.at[3 * t + d], (px, py, c)).start()
        token[...] = jnp.zeros_like(token)

    outs = pl.pallas_call(
        body, name=name,
        out_shape=(pltpu.SemaphoreType.DMA((3 * n,)), pltpu.SemaphoreType.DMA((3 * n,)),
                   *[pltpu.HBM(p.shape, p.dtype) for p in ps + lands], jax.ShapeDtypeStruct((8, 128), F32)),
        in_specs=[HBM] * (2 * n), out_specs=(SEM, SEM, *[HBM] * (2 * n), pl.BlockSpec(memory_space=pltpu.VMEM)),
        input_output_aliases={t: 2 + t for t in range(2 * n)},
        compiler_params=pltpu.CompilerParams(has_side_effects=DATAFLOW))(*_in_hbm(ps + lands))
    return outs[0], outs[1], list(outs[2:2 + n]), list(outs[2 + n:2 + 2 * n]), outs[-1]


def _rs2_wait(ssem, rsem, ps, lands, after, name):
    n = len(ps)

    def body(*refs):
        ssem_ref, rsem_ref = refs[2 * n], refs[2 * n + 1]
        x, y, c, me = _ids()
        for t in range(n):
            for d, pj, px, py in _chip_peers(x, y):
                _remote(refs[t].at[pj], refs[n + t].at[me], ssem_ref.at[3 * t + d], rsem_ref.at[3 * t + d], (px, py, c)).wait_send()
                _remote(refs[t].at[pj], refs[n + t].at[pj], ssem_ref.at[3 * t + d], rsem_ref.at[3 * t + d], (px, py, c)).wait_recv()

    outs = pl.pallas_call(
        body, name=name, out_shape=[pltpu.HBM(p.shape, p.dtype) for p in ps + lands],
        in_specs=[HBM] * (2 * n) + [SEM, SEM, ANY], out_specs=[HBM] * (2 * n),
        input_output_aliases={t: t for t in range(2 * n)},
        compiler_params=pltpu.CompilerParams(has_side_effects=DATAFLOW))(*ps, *lands, ssem, rsem, after)
    return list(outs[:n]), list(outs[n:])


def _rs_add2(p, got, into, layer, name):
    _, _, pr, pc = p.shape
    tr = _slab_rows(pr, pc)
    nt = pr // tr

    def slot(d):
        return pl.BlockSpec((None, None, tr, pc), lambda t, s: (s[0] ^ d, 0, t, 0))

    def body(s_ref, p_ref, g1_ref, g2_ref, g3_ref, i_ref, o_ref):
        o_ref[...] = (p_ref[...].astype(F32) + g1_ref[...].astype(F32) + g2_ref[...].astype(F32) + g3_ref[...].astype(F32))

    return pl.pallas_call(
        body, name=name, out_shape=jax.ShapeDtypeStruct(into.shape, F32),
        grid_spec=pltpu.PrefetchScalarGridSpec(
            num_scalar_prefetch=1, grid=(nt,), in_specs=[slot(0), slot(1), slot(2), slot(3), HBM],
            out_specs=pl.BlockSpec((None, tr, pc), lambda t, s: (layer, s[1] * nt + t, 0))),
        input_output_aliases={5: 0},
        compiler_params=_params("parallel"))(_mesh_scalars(), p, got, got, got, into)


def _rs3(shards, layers, name):
    n = len(shards)

    def body(*refs):
        outs = refs[n:2 * n]
        token, ssem, rsem = refs[2 * n], refs[2 * n + 1], refs[2 * n + 2]
        x, y, c, _ = _ids()
        sib = (x, y, 1 - c)
        token[...] = jnp.zeros_like(token)

        def half(t, cc):
            return _sview(outs[t].at[pl.ds(layers[t], 1)], cc)

        sends = [_remote(half(t, c), half(t, c), ssem.at[t], rsem.at[t], sib) for t in range(n)]
        for cp in sends:
            cp.start()
        for t in range(n):
            _remote(half(t, 1 - c), half(t, 1 - c), ssem.at[t], rsem.at[t], sib).wait_recv()
        for cp in sends:
            cp.wait_send()

    outs = pl.pallas_call(
        body, name=name, out_shape=[jax.ShapeDtypeStruct(s.shape, s.dtype) for s in shards] + [jax.ShapeDtypeStruct((8, 128), F32)],
        in_specs=[HBM] * n, out_specs=[HBM] * n + [pl.BlockSpec(memory_space=pltpu.VMEM)],
        scratch_shapes=[pltpu.SemaphoreType.DMA((n,)), pltpu.SemaphoreType.DMA((n,))],
        input_output_aliases={t: t for t in range(n)}, compiler_params=pltpu.CompilerParams(has_side_effects=True))(*shards)
    return list(outs[:n]), outs[n]


def _ag_small(sp, name):
    def body(s_ref, o_ref, ssem, rsem, lsem):
        x, y, c, me = _ids()
        local = pltpu.make_async_copy(s_ref, o_ref.at[me], lsem.at[0])
        local.start()
        sends = [_remote(s_ref, o_ref.at[me], ssem.at[d], rsem.at[d], (px, py, c)) for d, _, px, py in _chip_peers(x, y)]
        for cp in sends:
            cp.start()
        for d, pj, px, py in _chip_peers(x, y):
            _remote(s_ref, o_ref.at[pj], ssem.at[d], rsem.at[d], (px, py, c)).wait_recv()
        for cp in sends:
            cp.wait_send()
        local.wait()

    return _comm_call(body, name, [sp], [jax.ShapeDtypeStruct((N_CHIPS,) + sp.shape, sp.dtype)], (3, 3, 1))[0]


def _slot_place(g, name):
    rows, cols = g.shape
    x, y, c = lax.axis_index("x"), lax.axis_index("y"), lax.axis_index("c")
    slot = (4 * x + 2 * y + c).astype(jnp.int32).reshape(1)

    def body(s_ref, i_ref, o_ref):
        o_ref[...] = i_ref[...]

    return pl.pallas_call(
        body, name=name, out_shape=jax.ShapeDtypeStruct((8, rows, cols), g.dtype),
        grid_spec=pltpu.PrefetchScalarGridSpec(
            num_scalar_prefetch=1, grid=(1,), in_specs=[pl.BlockSpec((rows, cols), lambda t, s: (0, 0))],
            out_specs=pl.BlockSpec((None, rows, cols), lambda t, s: (s[0], 0, 0))),
        compiler_params=_params("arbitrary"))(slot, g)


def _gather8_peers(x, y, c):
    return [(d - 1, x ^ (d >> 2), y ^ ((d >> 1) & 1), c ^ (d & 1)) for d in range(1, 8)]


def _gather8_start(buf, name):
    def body(b_ref, ssem, rsem, b_thru, token):
        x, y, c, _ = _ids()
        mine = b_ref.at[4 * x + 2 * y + c]
        for d, px, py, pc in _gather8_peers(x, y, c):
            _remote(mine, mine, ssem.at[d], rsem.at[d], (px, py, pc)).start()
        token[...] = jnp.zeros_like(token)

    outs = pl.pallas_call(
        body, name=name,
        out_shape=(pltpu.SemaphoreType.DMA((7,)), pltpu.SemaphoreType.DMA((7,)), pltpu.HBM(buf.shape, buf.dtype),
                   jax.ShapeDtypeStruct((8, 128), F32)),
        in_specs=[HBM], out_specs=(SEM, SEM, HBM, pl.BlockSpec(memory_space=pltpu.VMEM)), input_output_aliases={0: 2},
        compiler_params=pltpu.CompilerParams(has_side_effects=DATAFLOW))(*_in_hbm([buf]))
    return outs


def _gather8_wait(ssem, rsem, buf, after, name):
    def body(b_ref, ssem_ref, rsem_ref, after_ref, b_out):
        x, y, c, _ = _ids()
        mine = b_ref.at[4 * x + 2 * y + c]
        for d, px, py, pc in _gather8_peers(x, y, c):
            theirs = b_ref.at[4 * px + 2 * py + pc]
            _remote(mine, mine, ssem_ref.at[d], rsem_ref.at[d], (px, py, pc)).wait_send()
            _remote(theirs, theirs, ssem_ref.at[d], rsem_ref.at[d], (px, py, pc)).wait_recv()

    return pl.pallas_call(
        body, name=name, out_shape=pltpu.HBM(buf.shape, buf.dtype), in_specs=[HBM, SEM, SEM, ANY], out_specs=HBM,
        input_output_aliases={0: 0},
        compiler_params=pltpu.CompilerParams(has_side_effects=DATAFLOW))(buf, ssem, rsem, after)


def _sum_slots(a, out_dtype, name):
    n = a.shape[0]
    shape = a.shape[1:]
    cols = shape[-1]
    a3 = a.reshape(n, -1, cols)
    rows = a3.shape[1]
    tr = rows
    for cand in (512, 256, 128, 64, 32, 16):
        if rows % cand == 0 and cand * cols * 4 <= (1 << 20):
            tr = cand
            break

    def body(a_ref, o_ref):
        acc = a_ref[0].astype(F32)
        for j in range(1, n):
            acc = acc + a_ref[j].astype(F32)
        o_ref[...] = acc.astype(o_ref.dtype)

    out = pl.pallas_call(
        body, name=name, out_shape=jax.ShapeDtypeStruct((rows, cols), out_dtype), grid=(rows // tr,),
        in_specs=[pl.BlockSpec((n, tr, cols), lambda i: (0, i, 0))], out_specs=pl.BlockSpec((tr, cols), lambda i: (i, 0)),
        compiler_params=_params("parallel"))(a3)
    return out.reshape(shape)


def kernel(x, mem, mix_norm, xa_norm, xa_wq, xa_wkv, xa_wo, ffn_norm, ffn_w_gu, ffn_w_down, a_w_in, a_conv_w, a_w_out, b_w_in, b_v_g, b_v_b, b_w_s, b_s_bias, b_w_out, c_w_in, c_conv_w, c_conv_b, c_ln_g, c_ln_b, c_w_out, loss_target, m_mix_norm, m_xa_norm, m_xa_wq, m_xa_wkv, m_xa_wo, m_ffn_norm, m_ffn_w_gu, m_ffn_w_down, m_a_w_in, m_a_conv_w, m_a_w_out, m_b_w_in, m_b_v_g, m_b_v_b, m_b_w_s, m_b_s_bias, m_b_w_out, m_c_w_in, m_c_conv_w, m_c_conv_b, m_c_ln_g, m_c_ln_b, m_c_w_out, v_mix_norm, v_xa_norm, v_xa_wq, v_xa_wkv, v_xa_wo, v_ffn_norm, v_ffn_w_gu, v_ffn_w_down, v_a_w_in, v_a_conv_w, v_a_w_out, v_b_w_in, v_b_v_g, v_b_v_b, v_b_w_s, v_b_s_bias, v_b_w_out, v_c_w_in, v_c_conv_w, v_c_conv_b, v_c_ln_g, v_c_ln_b, v_c_w_out):
    given = dict(locals())
    w = {n: given[n] for n in WEIGHTS}
    depth = mix_norm.shape[0]
    s, d = x.shape[1], x.shape[2]
    n_mem = mem.shape[1]
    ds = d // N_CHIPS
    xin = x.reshape(s, d)
    memv = mem.reshape(n_mem, d)
    target = loss_target.reshape(s, d)
    me = 2 * lax.axis_index("x") + lax.axis_index("y")

    wg = {n: [None] * w[n].shape[0] for n in BIG_KINDS}

    def mixer_keys(i):
        return [("abc"[i % 3] + "_w_in", i // 3), ("abc"[i % 3] + "_w_out", i // 3)]

    def rest_keys(i):
        return [("xa_wq", i), ("xa_wkv", i), ("xa_wo", i), ("ffn_w_gu", i), ("ffn_w_down", i)]

    def ag_begin(keys, tag, after):
        kinds = [BIG_KINDS[n] for n, _ in keys]
        ssem, rsem, bufs, token = _ag_start([placed[k] for k in keys], kinds, "ag_start_" + tag, after)
        return keys, kinds, ssem, rsem, bufs, token, tag

    def ag_end(state, after):
        keys, kinds, ssem, rsem, bufs, _, tag = state
        bufs = _ag_forward(_ag_wait(ssem, rsem, bufs, kinds, after, "ag_wait_" + tag), kinds, "ag_fwd_" + tag)
        for (n, l), buf in zip(keys, bufs):
            wg[n][l] = buf

    def pad8(t):
        return jnp.pad(t, ((0, (-t.shape[0]) % 8), (0, 0)))

    small_rows = [w[n].reshape(-1, ds) for n in SMALL_SHARDED]
    counts = [t.shape[0] for t in small_rows]
    gathered = _ag_small(jnp.concatenate([pad8(t) for t in small_rows], axis=0), "ag_small")
    placed = {(n, l): _ag_place(w[n], l, BIG_KINDS[n], f"ag_place_{n}_{l}", dep=gathered) for n, l in mixer_keys(0)}
    gathered = jnp.transpose(gathered, (1, 0, 2)).reshape(-1, d)
    full, off = {}, 0
    for n, cnt in zip(SMALL_SHARDED, counts):
        full[n] = gathered[off:off + cnt].reshape(w[n].shape[:-1] + (d,))
        off += cnt + (-cnt) % 8
    t_chunk = b_w_s.shape[-1]
    tril = jnp.tril(jnp.ones((t_chunk, t_chunk), dtype=bool))

    def vec(a):
        return a.reshape(1, -1)

    def b_params(slot):
        ws_m = jnp.where(tril[None], b_w_s[slot], 0.0).astype(BF16)
        sbt = jnp.zeros((t_chunk, 128), F32).at[:, :b_s_bias.shape[1]].set(b_s_bias[slot].T)
        return vec(b_v_g[slot]), vec(b_v_b[slot]), ws_m, sbt

    saved = []
    xc = xin
    ag_groups = [(mixer_keys(0), "0m"), (rest_keys(0), "0r")] + [(mixer_keys(j) + rest_keys(j), f"{j}") for j in range(1, depth)]
    ag_state = {}

    def ag_begin_group(k, after):
        if k >= len(ag_groups):
            return None
        ag_state[k] = ag_begin(*ag_groups[k], after)
        return ag_state[k][5]

    dep = ag_begin_group(0, None)
    for n, kind in BIG_KINDS.items():
        for l in range(w[n].shape[0]):
            if (n, l) not in placed:
                placed[n, l] = dep = _ag_place(w[n], l, kind, f"ag_place_{n}_{l}", dep=dep)
    ag_end(ag_state[0], dep)
    for i in range(depth):
        kind, slot = i % 3, i // 3
        t = f"{i}"
        dep = ag_begin_group(1, wg[mixer_keys(0)[0][0]][0]) if i == 0 else ag_begin_group(i + 2, xc)
        sv = {"x0": xc}
        if i == 0:
            h = _rms_fwd(xc, vec(full["mix_norm"][i, 0]), "rms_mix_" + t)
        sv["h1"] = h
        if kind == 0:
            pre = _mm("nn", h, wg["a_w_in"], BF16, "a_in_" + t, bl=slot, o_parts=3, dep=dep)
            mid = _a_mid_fwd(pre, full["a_conv_w"][slot], "a_mid_" + t)
            y = _mm("nn", mid, wg["a_w_out"], BF16, "a_out_" + t, bl=slot)
        elif kind == 1:
            pre = _mm("nn", h, wg["b_w_in"], BF16, "b_in_" + t, bl=slot, o_parts=2, dep=dep)
            mid = _b_mid_fwd(pre, *b_params(slot), "b_mid_" + t)
            y = _mm("nn", mid, wg["b_w_out"], BF16, "b_out_" + t, bl=slot)
        else:
            pre = _mm("nn", h, wg["c_w_in"], BF16, "c_in_" + t, bl=slot, o_parts=2, dep=dep)
            y2 = _c_conv_fwd(pre, full["c_conv_w"][slot], vec(full["c_conv_b"][slot]), "c_conv_" + t)
            sv["cy2"] = y2
            mid = _c_ln_fwd(y2, vec(full["c_ln_g"][slot]), vec(full["c_ln_b"][slot]), "c_ln_" + t)
            y = _mm("nn", mid, wg["c_w_out"], BF16, "c_out_" + t, bl=slot)
        sv.update(pre=pre, mid=mid, y1=y)
        xc, h = _res_rms_fwd(xc, y, vec(full["mix_norm"][i, 1]), vec(full["xa_norm"][i, 0]), "res_mix_" + t)

        sv["x1"] = xc
        dep = None
        if i == 0:
            ag_end(ag_state[1], xc)
            dep = ag_begin_group(2, xc)
        mem_n = _rms_fwd(memv, vec(full["xa_norm"][i, 2]), "rms_mem_" + t)
        q = _mm("nn", h, wg["xa_wq"], BF16, "xa_q_" + t, bl=i, dep=dep)
        kv3 = _mm("nn", mem_n, wg["xa_wkv"], BF16, "xa_kv_" + t, bl=i, o_parts=2)
        o = _attn_fwd(q, kv3, "attn_" + t)
        y = _mm("nn", o, wg["xa_wo"], BF16, "xa_o_" + t, bl=i)
        sv.update(h2=h, mem_n=mem_n, q=q, kv3=kv3, o=o, y2=y)
        xc, h = _res_rms_fwd(xc, y, vec(full["xa_norm"][i, 1]), vec(full["ffn_norm"][i, 0]), "res_xa_" + t)

        sv["x2"] = xc
        gu3, act = _ffn_gu_fwd(h, wg["ffn_w_gu"][i], "ffn_gu_" + t)
        y = _mm("nn", act, wg["ffn_w_down"], BF16, "ffn_down_" + t, bl=i)
        sv.update(h3=h, gu3=gu3, act=act, y3=y)
        xc, h = _res_rms_fwd(xc, y, vec(full["ffn_norm"][i, 1]),
                             vec(full["mix_norm"][i + 1, 0]) if i + 1 < depth else None, "res_ffn_" + t)
        saved.append(sv)
        if i + 2 in ag_state:
            ag_end(ag_state[i + 2], xc)

    loss_blk, dx = _loss(xc, target, "loss")
    loss = lax.psum(loss_blk[0, 0], ("x", "y", "c"))

    gbuf = {}
    gfin = {n: lax.empty(w[n].shape, F32) for n in BIG_KINDS}
    gsmall = {n: [None] * full[n].shape[0] for n in ("mix_norm", "xa_norm", "ffn_norm", "a_conv_w", "c_conv_w", "c_conv_b",
                                                      "c_ln_g", "c_ln_b")}
    grepl = {}

    def wgrad(name, l, a, dy, tag, b_parts=1):
        g2 = _mm("tn", a, dy, BF16, "wg_" + tag, b_parts=b_parts)
        gbuf[name, l] = g2.reshape((1,) + g2.shape)

    def rs_begin(keys, tag, dep):
        kinds = [BIG_KINDS[n] for n, _ in keys]
        gots = _rs1([gbuf[k] for k in keys], kinds, "rs1_" + tag, dep=dep)
        ps = [_rs_add1(gbuf[k], got, kind, f"rs_add1_{k[0]}_{k[1]}") for k, got, kind in zip(keys, gots, kinds)]
        ssem, rsem, ps, lands, token = _rs2_start(ps, "rs2_start_" + tag)
        return keys, ssem, rsem, ps, lands, token, tag

    def rs_end(state, after):
        keys, ssem, rsem, ps, lands, _, tag = state
        ps, lands = _rs2_wait(ssem, rsem, ps, lands, after, "rs2_wait_" + tag)
        for (n, l), p, land in zip(keys, ps, lands):
            gfin[n] = _rs_add2(p, land, gfin[n], l, f"rs_add2_{n}_{l}")
        outs, token = _rs3([gfin[n] for n, _ in keys], [l for _, l in keys], "rs3_" + tag)
        for (n, _), o in zip(keys, outs):
            gfin[n] = o
        return token

    def rs1_begin(keys, tag, after):
        kinds = [BIG_KINDS[n] for n, _ in keys]
        ssem, rsem, gs, lands, token = _rs1_start([gbuf[k] for k in keys], kinds, "rs1_start_" + tag, after)
        return keys, kinds, ssem, rsem, gs, lands, token, tag

    def rs_begin_after_rs1(state, after):
        keys, kinds, ssem, rsem, gs, lands, _, tag = state
        gs, gots = _rs1_wait(ssem, rsem, gs, lands, kinds, after, "rs1_wait_" + tag)
        ps = [_rs_add1(g, got, kind, f"rs_add1_{k[0]}_{k[1]}") for k, g, got, kind in zip(keys, gs, gots, kinds)]
        ssem, rsem, ps, lands, token = _rs2_start(ps, "rs2_start_" + tag)
        return keys, ssem, rsem, ps, lands, token, tag

    rs_state, rs_token, rs1_state = None, None, None

    for i in reversed(range(depth)):
        kind, slot = i % 3, i // 3
        t = f"{i}"
        sv = saved[i]
        dep = rs1_state[6] if rs1_state is not None else (None if rs_state is None else rs_state[5])
        dy, dg_post = _rms_bwd(sv["y3"], vec(full["ffn_norm"][i, 1]), dx, None, BF16, "rmsb_ffn_post_" + t, dep=dep)
        wgrad("ffn_w_down", i, sv["act"], dy, "ffn_down_" + t)
        dgu3 = _ffn_down_bwd(dy, wg["ffn_w_down"][i], sv["gu3"], "dg_ffn_down_" + t)
        wgrad("ffn_w_gu", i, sv["h3"], dgu3, "ffn_gu_" + t, b_parts=2)
        dh = _mm("nt", dgu3, wg["ffn_w_gu"], BF16, "dg_ffn_gu_" + t, bl=i, a_parts=2)
        dx, dg_pre = _rms_bwd(sv["x2"], vec(full["ffn_norm"][i, 0]), dh, dx, F32, "rmsb_ffn_pre_" + t)
        gsmall["ffn_norm"][i] = jnp.concatenate([dg_pre, dg_post], axis=0)
        dep = None
        if rs1_state is not None:
            rs_state, rs1_state = rs_begin_after_rs1(rs1_state, dx), None
            dep = rs_state[5]
        if i == 0:
            rs_state_f = rs_begin(rest_keys(0)[3:], "0f", None)
            dep = rs_state_f[5]
        dy, dg_post = _rms_bwd(sv["y2"], vec(full["xa_norm"][i, 1]), dx, None, BF16, "rmsb_xa_post_" + t, dep=dep)
        wgrad("xa_wo", i, sv["o"], dy, "xa_o_" + t)
        do = _mm("nt", dy, wg["xa_wo"], BF16, "dg_xa_o_" + t, bl=i)
        dq, dkv3 = _attn_bwd(sv["q"], sv["kv3"], do, "attn_b_" + t)
        wgrad("xa_wq", i, sv["h2"], dq, "xa_q_" + t)
        dh = _mm("nt", dq, wg["xa_wq"], BF16, "dg_xa_q_" + t, bl=i)
        dkv3 = dkv3.astype(BF16)
        wgrad("xa_wkv", i, sv["mem_n"], dkv3, "xa_kv_" + t, b_parts=2)
        dmem_n = _mm("nt", dkv3, wg["xa_wkv"], F32, "dg_xa_kv_" + t, bl=i, a_parts=2)
        _, dg_mem = _rms_bwd(memv, vec(full["xa_norm"][i, 2]), dmem_n, None, F32, "rmsb_mem_" + t)
        dx, dg_pre = _rms_bwd(sv["x1"], vec(full["xa_norm"][i, 0]), dh, dx, F32, "rmsb_xa_pre_" + t)
        gsmall["xa_norm"][i] = jnp.concatenate([dg_pre, dg_post, dg_mem], axis=0)
        if i == 0:
            rs_token = rs_end(rs_state, dx)
            rs_state = rs_begin(rest_keys(0)[:3], "0x", rs_token)
        dy, dg_post = _rms_bwd(sv["y1"], vec(full["mix_norm"][i, 1]), dx, None, BF16, "rmsb_mix_post_" + t,
                               dep=rs_state[5] if i == 0 else None)
        if kind == 0:
            wgrad("a_w_out", slot, sv["mid"], dy, "a_out_" + t)
            dmid = _mm("nt", dy, wg["a_w_out"], F32, "dg_a_out_" + t, bl=slot)
            dpre, dcw = _a_mid_bwd(sv["pre"], dmid, full["a_conv_w"][slot], "a_mid_b_" + t)
            gsmall["a_conv_w"][slot] = dcw
            wgrad("a_w_in", slot, sv["h1"], dpre, "a_in_" + t, b_parts=3)
            dh = _mm("nt", dpre, wg["a_w_in"], BF16, "dg_a_in_" + t, bl=slot, a_parts=3)
        elif kind == 1:
            wgrad("b_w_out", slot, sv["mid"], dy, "b_out_" + t)
            dmid = _mm("nt", dy, wg["b_w_out"], F32, "dg_b_out_" + t, bl=slot)
            dpre, dvg, dvb, dws, dsbt = _b_mid_bwd(sv["pre"], dmid, *b_params(slot), "b_mid_b_" + t)
            grepl[slot] = (dvg, dvb, dws, dsbt[:, :b_s_bias.shape[1]].T)
            wgrad("b_w_in", slot, sv["h1"], dpre, "b_in_" + t, b_parts=2)
            dh = _mm("nt", dpre, wg["b_w_in"], BF16, "dg_b_in_" + t, bl=slot, a_parts=2)
        else:
            wgrad("c_w_out", slot, sv["mid"], dy, "c_out_" + t)
            dmid = _mm("nt", dy, wg["c_w_out"], F32, "dg_c_out_" + t, bl=slot)
            dy2, dlg, dlb = _c_ln_bwd(sv["cy2"], dmid, vec(full["c_ln_g"][slot]), vec(full["c_ln_b"][slot]), "c_ln_b_" + t)
            dpre, dcw, dcb = _c_conv_bwd(sv["pre"], dy2, full["c_conv_w"][slot], "c_conv_b_" + t)
            gsmall["c_conv_w"][slot], gsmall["c_conv_b"][slot] = dcw, dcb
            gsmall["c_ln_g"][slot], gsmall["c_ln_b"][slot] = dlg, dlb
            wgrad("c_w_in", slot, sv["h1"], dpre, "c_in_" + t, b_parts=2)
            dh = _mm("nt", dpre, wg["c_w_in"], BF16, "dg_c_in_" + t, bl=slot, a_parts=2)
        dx, dg_pre = _rms_bwd(sv["x0"], vec(full["mix_norm"][i, 0]), dh, dx, F32, "rmsb_mix_pre_" + t)
        gsmall["mix_norm"][i] = jnp.concatenate([dg_pre, dg_post], axis=0)
        if i == 0:
            rs_end(rs_state_f, dx)
        if rs_state is not None:
            rs_token = rs_end(rs_state, dx)
        if i == depth - 1:
            rs_state, rs1_state = None, rs1_begin(mixer_keys(i) + rest_keys(i), f"{i}", rs_token if rs_token is not None else dx)
        else:
            rs_state = rs_begin(mixer_keys(i) + (rest_keys(i) if i > 0 else []), f"{i}" if i > 0 else "0m", rs_token)
    grad_x = dx.reshape(x.shape)

    small_g = [jnp.concatenate(gsmall[n], axis=0).reshape(-1, d) for n in SMALL_SHARDED]
    n_b = b_v_g.shape[0]
    repl_g = [jnp.concatenate([grepl[sl][k] for sl in range(n_b)], axis=0) for k in range(4)]
    repl_rows = []
    for g_arr in repl_g:
        flat = g_arr.reshape(-1)
        flat = jnp.concatenate([flat, jnp.zeros(((-flat.shape[0]) % d,), F32)])
        repl_rows.append(flat.reshape(-1, d))
    packed_g = jnp.concatenate([pad8(t) for t in small_g + repl_rows], axis=0)
    g_ssem, g_rsem, g_buf, g_token = _gather8_start(_slot_place(packed_g, "place_small_grads"), "gather_small_start")

    delta, new_m, new_v, grads = {}, {}, {}, {}
    last_keys = {n for n, _ in mixer_keys(0)}
    deps = [rs_state[5], g_token]
    for n in BIG_KINDS:
        if n not in last_keys:
            delta[n], new_m[n], new_v[n], grads[n] = _adamw(w[n], gfin[n], given["m_" + n], given["v_" + n], "adamw_" + n,
                                                            dep=deps.pop(0), with_grad=True)
            deps.append(delta[n])
    rs_end(rs_state, deps[-1])
    total = _sum_slots(_gather8_wait(g_ssem, g_rsem, g_buf, deps[-1], "gather_small_wait"), F32, "sum_small_grads")
    for n in sorted(last_keys):
        delta[n], new_m[n], new_v[n], grads[n] = _adamw(w[n], gfin[n], given["m_" + n], given["v_" + n], "adamw_" + n,
                                                        with_grad=True)

    off = 0
    for n, cnt in zip(SMALL_SHARDED, counts):
        blk = lax.dynamic_slice_in_dim(total[off:off + cnt], me * ds, ds, axis=1)
        grads[n] = blk.reshape(w[n].shape)
        off += cnt + (-cnt) % 8
    for n, g_arr in zip(SMALL_REPL, repl_g):
        cnt = -(-g_arr.size // d)
        grads[n] = total[off:off + cnt].reshape(-1)[:g_arr.size].reshape(w[n].shape)
        off += cnt + (-cnt) % 8

    for n in WEIGHTS:
        if n not in delta:
            delta[n], new_m[n], new_v[n] = _adamw(w[n], grads[n], given["m_" + n], given["v_" + n], "adamw_" + n)
    return (loss, grad_x, *[grads[n] for n in WEIGHTS], *[delta[n] for n in WEIGHTS], *[new_m[n] for n in WEIGHTS],
            *[new_v[n] for n in WEIGHTS])
```
